```python
import jax, jax.numpy as jnp
from jax import lax
import numpy as np

D_MODEL = 1024
BATCH = 8
SEQ = 2048
DEPTH = 1
DEC_BATCH = 128
DEC_SEQ = 1
PAST_LEN = 16384
PAGE_SIZE = 128

D_A = D_MODEL
CHUNK = 128
N_GROUPS_A = D_A // CHUNK
GD_A = D_A // N_GROUPS_A
D_B = D_MODEL
POOL_WINDOWS = (2, 4, 8, 16)
N_GROUPS_B = len(POOL_WINDOWS)
GD_B = D_B // N_GROUPS_B
POOL_BUF = max(POOL_WINDOWS) - 1
PLE_DIM = 256
EPS = 1e-6
SPLITS = (D_A, 2 * D_A, 3 * D_A, 3 * D_A + D_B, 3 * D_A + 2 * D_B, 3 * D_A + 2 * D_B + D_MODEL)
D_IN = 3 * D_A + 2 * D_B + 2 * D_MODEL

kernel_name = "gated_chunk_gmlp_pool_hybrid_step"


def _rmsnorm(x, g):
    x32 = x.astype(jnp.float32)
    r = x32 * lax.rsqrt(jnp.mean(x32 * x32, axis=-1, keepdims=True) + EPS)
    return (r * g.astype(jnp.float32)).astype(x.dtype)


def _layernorm(x, g, b):
    x32 = x.astype(jnp.float32)
    mu = jnp.mean(x32, axis=-1, keepdims=True)
    xc = x32 - mu
    r = xc * lax.rsqrt(jnp.mean(xc * xc, axis=-1, keepdims=True) + EPS)
    return (r * g.astype(jnp.float32) + b.astype(jnp.float32)).astype(x.dtype)


def _chunk_spatial_mix(vn, w_s, b_s):
    bn, L, _ = vn.shape
    pad = (-L) % CHUNK
    vp = jnp.pad(vn, ((0, 0), (0, pad), (0, 0)))
    nc = (L + pad) // CHUNK
    vp = vp.reshape(bn, nc, CHUNK, N_GROUPS_A, GD_A)
    w = jnp.tril(w_s)
    s = jnp.einsum('hts,bcshd->bcthd', w, vp) + jnp.transpose(b_s)[None, None, :, :, None]
    return s.reshape(bn, nc * CHUNK, D_A)[:, :L]


def _pool_branch(xb, prev, pos0, w_lin, scale):
    bn, L, _ = xb.shape
    ext = jnp.concatenate([prev, xb], axis=1)
    e32 = ext.astype(jnp.float32)
    cs = jnp.concatenate([jnp.zeros_like(e32[:, :1]), jnp.cumsum(e32, axis=1)], axis=1)
    pos = pos0 + jnp.arange(L, dtype=jnp.int32)
    x32 = xb.astype(jnp.float32)
    outs = []
    for g, w in enumerate(POOL_WINDOWS):
        sl = slice(g * GD_B, (g + 1) * GD_B)
        win_sum = (cs[:, POOL_BUF + 1:POOL_BUF + 1 + L, sl]
                   - cs[:, POOL_BUF + 1 - w:POOL_BUF + 1 - w + L, sl])
        cnt = jnp.minimum(pos + 1, w).astype(jnp.float32)[None, :, None]
        outs.append(win_sum / cnt - x32[..., sl])
    pooled = jnp.stack(outs, axis=2).astype(xb.dtype)
    mixed = jnp.einsum('blgc,gcd->blgd', pooled, w_lin).reshape(bn, L, D_B) * scale
    return mixed, ext[:, -POOL_BUF:]


def _layer(h, p, prev_pool, pos0, pre_g, w_in, ln_g, ln_b, w_s, b_s, w_pool, pool_scale,
           w_pa, w_pb, w_out, post_g, w_ple, w_pg, ple_in_g, ple_out_g):
    L = h.shape[1]
    xn = _rmsnorm(h, pre_g)
    proj = jnp.einsum('bld,de->ble', xn, w_in)
    u, v, z_a, x_b, z_b, g_a, g_b = jnp.split(proj, SPLITS, axis=-1)
    u = jax.nn.gelu(u)
    vn = _layernorm(jax.nn.gelu(v), ln_g, ln_b)
    y_a = u * _chunk_spatial_mix(vn, w_s, b_s) * jax.nn.silu(z_a)
    pooled, new_pool = _pool_branch(x_b, prev_pool, pos0, w_pool, pool_scale)
    y_b = pooled * jax.nn.silu(z_b)
    m = (jax.nn.sigmoid(g_a) * jnp.einsum('ble,ed->bld', y_a, w_pa)
         + jax.nn.sigmoid(g_b) * jnp.einsum('ble,ed->bld', y_b, w_pb))
    h = h + _rmsnorm(jnp.einsum('bld,de->ble', m, w_out), post_g)
    e = jnp.einsum('blp,pd->bld', p, w_ple)
    gate = jax.nn.sigmoid(jnp.einsum('bld,de->ble', _rmsnorm(h, ple_in_g), w_pg))
    h = h + _rmsnorm(gate * e, ple_out_g)
    start = ((L - 1) // CHUNK) * CHUNK
    return h, vn[:, start:], new_pool


def setup_inputs(seed: int = 0) -> dict:
    key = jax.random.key(seed)
    ks = jax.random.split(key, 32)
    f32 = jnp.float32
    nrm = lambda k, shape, s: jax.random.normal(k, shape, f32) * s
    return {
        "x_prompt": nrm(ks[0], (BATCH, SEQ, D_MODEL), 1.0),
        "x_sample": nrm(ks[1], (DEC_BATCH, DEC_SEQ, D_MODEL), 1.0),
        "state_pool": nrm(ks[2], (DEPTH, DEC_BATCH, POOL_BUF, D_B), 1.0),
        "p_prompt": nrm(ks[3], (DEPTH, BATCH, SEQ, PLE_DIM), 1.0),
        "p_sample": nrm(ks[4], (DEPTH, DEC_BATCH, DEC_SEQ, PLE_DIM), 1.0),
        "pre_g": 1.0 + nrm(ks[5], (DEPTH, D_MODEL), 0.02),
        "w_in": nrm(ks[6], (DEPTH, D_MODEL, D_IN), D_MODEL ** -0.5),
        "ln_g": 1.0 + nrm(ks[7], (DEPTH, D_A), 0.02),
        "ln_b": nrm(ks[8], (DEPTH, D_A), 0.02),
        "w_s": nrm(ks[9], (DEPTH, N_GROUPS_A, CHUNK, CHUNK), CHUNK ** -0.5),
        "b_s": 1.0 + nrm(ks[10], (DEPTH, N_GROUPS_A, CHUNK), 0.02),
        "w_pool": nrm(ks[11], (DEPTH, N_GROUPS_B, GD_B, GD_B), GD_B ** -0.5),
        "pool_scale": 1.0 + nrm(ks[12], (DEPTH, D_B), 0.02),
        "w_pa": nrm(ks[13], (DEPTH, D_A, D_MODEL), D_A ** -0.5),
        "w_pb": nrm(ks[14], (DEPTH, D_B, D_MODEL), D_B ** -0.5),
        "w_out": nrm(ks[15], (DEPTH, D_MODEL, D_MODEL), D_MODEL ** -0.5),
        "post_g": 1.0 + nrm(ks[16], (DEPTH, D_MODEL), 0.02),
        "w_ple": nrm(ks[17], (DEPTH, PLE_DIM, D_MODEL), PLE_DIM ** -0.5),
        "w_pg": nrm(ks[18], (DEPTH, D_MODEL, D_MODEL), D_MODEL ** -0.5),
        "ple_in_g": 1.0 + nrm(ks[19], (DEPTH, D_MODEL), 0.02),
        "ple_out_g": 1.0 + nrm(ks[20], (DEPTH, D_MODEL), 0.02),
    }


def reference(x_prompt, x_sample, state_pool, p_prompt, p_sample, pre_g, w_in, ln_g, ln_b,
              w_s, b_s, w_pool, pool_scale, w_pa, w_pb, w_out, post_g, w_ple, w_pg,
              ple_in_g, ple_out_g):
    hp, hs = x_prompt, x_sample
    pv_list, pp_list, sv_list, sp_list = [], [], [], []
    for i in range(DEPTH):
        lw = (pre_g[i], w_in[i], ln_g[i], ln_b[i], w_s[i], b_s[i], w_pool[i], pool_scale[i],
              w_pa[i], w_pb[i], w_out[i], post_g[i], w_ple[i], w_pg[i], ple_in_g[i], ple_out_g[i])
        zero_prev = jnp.zeros((hp.shape[0], POOL_BUF, D_B), hp.dtype)
        hp, pv, pp = _layer(hp, p_prompt[i], zero_prev, 0, *lw)
        hs, sv, sp = _layer(hs, p_sample[i], state_pool[i], PAST_LEN, *lw)
        pv_list.append(pv); pp_list.append(pp); sv_list.append(sv); sp_list.append(sp)
    prompt_chunk_v = jnp.stack(pv_list)
    prompt_pool = jnp.stack(pp_list)
    sample_chunk_v = jnp.stack(sv_list)
    sample_pool = jnp.stack(sp_list)
    return (hp, hs, prompt_chunk_v, prompt_pool, sample_chunk_v, sample_pool)
```

```python
import functools

import jax
import jax.numpy as jnp
from jax import lax
from jax.experimental import pallas as pl
from jax.experimental.pallas import tpu as pltpu

D_MODEL = 1024
CHUNK = 128
N_HEADS = D_MODEL // CHUNK
POOL_WINDOWS = (2, 4, 8, 16)
POOL_GROUP = D_MODEL // len(POOL_WINDOWS)
POOL_BUF = max(POOL_WINDOWS) - 1
PLE_DIM = 256
PAST_LEN = 16384
EPS = 1e-6
COL_U, COL_V, COL_ZA, COL_XB, COL_ZB, COL_GA, COL_GB = range(7)
VEC_PRE_G, VEC_LN_G, VEC_LN_B, VEC_POOL_SCALE, VEC_POST_G, VEC_PLE_IN_G, VEC_PLE_OUT_G = range(7)
N_VEC_ROWS = 8

SUBLANES = 8
TAIL = 16
TILE_M = 256
VMEM_LIMIT_BYTES = 56 * 1024 * 1024

_BF16 = jnp.bfloat16
_F32 = jnp.float32
_GELU_C = 0.7978845608028654


def _dot(a, b):
    return jnp.dot(a, b, preferred_element_type=_F32)


def _rmsnorm(x, g):
    return x * lax.rsqrt(jnp.mean(x * x, axis=-1, keepdims=True) + EPS) * g


def _layernorm(x, g, b):
    xc = x - jnp.mean(x, axis=-1, keepdims=True)
    return xc * lax.rsqrt(jnp.mean(xc * xc, axis=-1, keepdims=True) + EPS) * g + b


def _gelu_tanh(x):
    return 0.5 * x * (1.0 + jnp.tanh(_GELU_C * (x + 0.044715 * (x * x * x))))


def _sigmoid(x):
    return 0.5 * jnp.tanh(0.5 * x) + 0.5


def _silu(x):
    return x * _sigmoid(x)


def _vec(vecs_ref, row):
    return vecs_ref[row:row + 1, :]


def _in_proj(xn, w_in_ref, col):
    return _dot(xn, w_in_ref[:, col * D_MODEL:(col + 1) * D_MODEL])


def _pool_mix(pooled, w_pool_ref, vecs_ref):
    parts = []
    for g in range(len(POOL_WINDOWS)):
        q = pooled[:, g * POOL_GROUP:(g + 1) * POOL_GROUP].astype(_BF16)
        parts.append(_dot(q, w_pool_ref[g]))
    return jnp.concatenate(parts, axis=1) * _vec(vecs_ref, VEC_POOL_SCALE)


def _merge_and_embed(h, p, xn, y_a, y_b, w_in_ref, w_pa_ref, w_pb_ref, w_out_ref, w_ple_ref,
                     w_pg_ref, vecs_ref):
    g_a = _in_proj(xn, w_in_ref, COL_GA)
    g_b = _in_proj(xn, w_in_ref, COL_GB)
    m = (_sigmoid(g_a) * _dot(y_a.astype(_BF16), w_pa_ref[...])
         + _sigmoid(g_b) * _dot(y_b.astype(_BF16), w_pb_ref[...]))
    h = h + _rmsnorm(_dot(m.astype(_BF16), w_out_ref[...]), _vec(vecs_ref, VEC_POST_G))
    e = _dot(p.astype(_BF16), w_ple_ref[...])
    hn = _rmsnorm(h, _vec(vecs_ref, VEC_PLE_IN_G)).astype(_BF16)
    gate = _sigmoid(_dot(hn, w_pg_ref[...]))
    return h + _rmsnorm(gate * e, _vec(vecs_ref, VEC_PLE_OUT_G))


def _prompt_kernel(x_ref, p_ref, w_in_ref, w_s_ref, bs_t_ref, w_pool_ref, w_pa_ref, w_pb_ref,
                   w_out_ref, w_ple_ref, w_pg_ref, vecs_ref,
                   y_ref, vn_ref, pool_ref, ext_ref, *, tile_m):
    j = pl.program_id(1)

    @pl.when(j == 0)
    def _():
        ext_ref[0:TAIL, :] = jnp.zeros((TAIL, D_MODEL), _F32)

    x = x_ref[0]
    xn = _rmsnorm(x, _vec(vecs_ref, VEC_PRE_G)).astype(_BF16)

    u = _gelu_tanh(_in_proj(xn, w_in_ref, COL_U))
    vn = _layernorm(_gelu_tanh(_in_proj(xn, w_in_ref, COL_V)),
                    _vec(vecs_ref, VEC_LN_G), _vec(vecs_ref, VEC_LN_B))
    vn_ref[0] = vn[tile_m - CHUNK:, :]
    vn_b = vn.astype(_BF16)
    causal = (lax.broadcasted_iota(jnp.int32, (CHUNK, CHUNK), 0)
              >= lax.broadcasted_iota(jnp.int32, (CHUNK, CHUNK), 1))
    w_tril = [jnp.where(causal, w_s_ref[hd], 0.0).astype(_BF16) for hd in range(N_HEADS)]
    bias = [bs_t_ref[:, hd:hd + 1] for hd in range(N_HEADS)]
    s_rows = []
    for c in range(tile_m // CHUNK):
        heads = []
        for hd in range(N_HEADS):
            blk = vn_b[c * CHUNK:(c + 1) * CHUNK, hd * CHUNK:(hd + 1) * CHUNK]
            heads.append(_dot(w_tril[hd], blk) + bias[hd])
        s_rows.append(jnp.concatenate(heads, axis=1))
    s = jnp.concatenate(s_rows, axis=0)
    y_a = u * s * _silu(_in_proj(xn, w_in_ref, COL_ZA))

    x_b = _in_proj(xn, w_in_ref, COL_XB)
    ext_ref[TAIL:TAIL + tile_m, :] = x_b
    pos = j * tile_m + lax.broadcasted_iota(jnp.int32, (tile_m, 1), 0)
    groups = []
    for g, w in enumerate(POOL_WINDOWS):
        lo, hi = g * POOL_GROUP, (g + 1) * POOL_GROUP
        cur = x_b[:, lo:hi]
        win = cur
        for k in range(1, w):
            win = win + ext_ref[TAIL - k:TAIL - k + tile_m, lo:hi]
        cnt = jnp.minimum(pos + 1, w).astype(_F32)
        groups.append(win / cnt - cur)
    pooled = jnp.concatenate(groups, axis=1)
    pool_ref[0] = ext_ref[TAIL + tile_m - POOL_BUF:TAIL + tile_m, :]
    ext_ref[0:TAIL, :] = ext_ref[tile_m:tile_m + TAIL, :]
    y_b = _pool_mix(pooled, w_pool_ref, vecs_ref) * _silu(_in_proj(xn, w_in_ref, COL_ZB))

    y_ref[0] = _merge_and_embed(x, p_ref[0], xn, y_a, y_b, w_in_ref, w_pa_ref, w_pb_ref,
                                w_out_ref, w_ple_ref, w_pg_ref, vecs_ref)


def _sample_kernel(x_ref, p_ref, state_ref, w_in_ref, w_s_ref, bs_t_ref, w_pool_ref, w_pa_ref,
                   w_pb_ref, w_out_ref, w_ple_ref, w_pg_ref, vecs_ref,
                   y_ref, vn_ref, pool_ref, *, pos0):
    x = x_ref[...]
    xn = _rmsnorm(x, _vec(vecs_ref, VEC_PRE_G)).astype(_BF16)

    u = _gelu_tanh(_in_proj(xn, w_in_ref, COL_U))
    vn = _layernorm(_gelu_tanh(_in_proj(xn, w_in_ref, COL_V)),
                    _vec(vecs_ref, VEC_LN_G), _vec(vecs_ref, VEC_LN_B))
    vn_ref[...] = vn
    heads = []
    for hd in range(N_HEADS):
        tap = w_s_ref[hd, 0:1, 0:1]
        heads.append(vn[:, hd * CHUNK:(hd + 1) * CHUNK] * tap + bs_t_ref[0:1, hd:hd + 1])
    s = jnp.concatenate(heads, axis=1)
    y_a = u * s * _silu(_in_proj(xn, w_in_ref, COL_ZA))

    x_b = _in_proj(xn, w_in_ref, COL_XB)
    groups = []
    for g, w in enumerate(POOL_WINDOWS):
        lo, hi = g * POOL_GROUP, (g + 1) * POOL_GROUP
        cur = x_b[:, lo:hi]
        win = cur
        for k in range(1, w):
            row = POOL_BUF - k
            win = win + state_ref[:, row * D_MODEL + lo:row * D_MODEL + hi]
        groups.append(win / float(min(pos0 + 1, w)) - cur)
    pooled = jnp.concatenate(groups, axis=1)
    pool_ref[:, 0:(POOL_BUF - 1) * D_MODEL] = state_ref[:, D_MODEL:POOL_BUF * D_MODEL]
    pool_ref[:, (POOL_BUF - 1) * D_MODEL:] = x_b
    y_b = _pool_mix(pooled, w_pool_ref, vecs_ref) * _silu(_in_proj(xn, w_in_ref, COL_ZB))

    y_ref[...] = _merge_and_embed(x, p_ref[...], xn, y_a, y_b, w_in_ref, w_pa_ref, w_pb_ref,
                                  w_out_ref, w_ple_ref, w_pg_ref, vecs_ref)


def _resident(shape):
    zeros = (0,) * len(shape)
    return pl.BlockSpec(shape, lambda *_: zeros, pipeline_mode=pl.Buffered(1))


def _prompt_layer(h, p, weights, tile_m):
    batch, seq, _ = h.shape
    assert seq % tile_m == 0 and tile_m % CHUNK == 0 and seq >= POOL_BUF
    row_map = lambda b, j: (b, j, 0)
    per_prompt = lambda b, j: (b, 0, 0)
    return pl.pallas_call(
        functools.partial(_prompt_kernel, tile_m=tile_m),
        grid=(batch, seq // tile_m),
        in_specs=[pl.BlockSpec((1, tile_m, D_MODEL), row_map),
                  pl.BlockSpec((1, tile_m, PLE_DIM), row_map)]
                 + [_resident(w.shape) for w in weights],
        out_specs=[pl.BlockSpec((1, tile_m, D_MODEL), row_map),
                   pl.BlockSpec((1, CHUNK, D_MODEL), per_prompt),
                   pl.BlockSpec((1, POOL_BUF, D_MODEL), per_prompt)],
        out_shape=[jax.ShapeDtypeStruct((batch, seq, D_MODEL), _F32),
                   jax.ShapeDtypeStruct((batch, CHUNK, D_MODEL), _F32),
                   jax.ShapeDtypeStruct((batch, POOL_BUF, D_MODEL), _F32)],
        scratch_shapes=[pltpu.VMEM((TAIL + tile_m, D_MODEL), _F32)],
        compiler_params=pltpu.CompilerParams(
            dimension_semantics=("arbitrary", "arbitrary"),
            vmem_limit_bytes=VMEM_LIMIT_BYTES),
        name="prompt_layer",
    )(h, p, *weights)


def _sample_layer(h, p, state, weights, pos0):
    rows = h.shape[0]
    return pl.pallas_call(
        functools.partial(_sample_kernel, pos0=pos0),
        out_shape=[jax.ShapeDtypeStruct((rows, D_MODEL), _F32),
                   jax.ShapeDtypeStruct((rows, D_MODEL), _F32),
                   jax.ShapeDtypeStruct((rows, POOL_BUF * D_MODEL), _F32)],
        compiler_params=pltpu.CompilerParams(vmem_limit_bytes=VMEM_LIMIT_BYTES),
        name="sample_layer",
    )(h, p, state, *weights)


def kernel(x_prompt, x_sample, state_pool, p_prompt, p_sample, pre_g, w_in, ln_g, ln_b, w_s, b_s,
           w_pool, pool_scale, w_pa, w_pb, w_out, post_g, w_ple, w_pg, ple_in_g, ple_out_g):
    depth = w_in.shape[0]
    dec_batch, dec_seq, _ = x_sample.shape
    assert dec_seq == 1 and x_prompt.shape[2] == D_MODEL
    hp = x_prompt
    hs = x_sample.reshape(dec_batch, D_MODEL)
    pv, pp, sv, sp = [], [], [], []
    for i in range(depth):
        vecs = jnp.stack([pre_g[i], ln_g[i], ln_b[i], pool_scale[i], post_g[i], ple_in_g[i],
                          ple_out_g[i]] + [jnp.zeros_like(pre_g[i])] * (N_VEC_ROWS - 7))
        weights = (w_in[i].astype(_BF16), w_s[i], b_s[i].T, w_pool[i].astype(_BF16),
                   w_pa[i].astype(_BF16), w_pb[i].astype(_BF16), w_out[i].astype(_BF16),
                   w_ple[i].astype(_BF16), w_pg[i].astype(_BF16), vecs)
        hp, v, pool = _prompt_layer(hp, p_prompt[i], weights, TILE_M)
        pv.append(v)
        pp.append(pool)
        hs, v, pool = _sample_layer(hs, p_sample[i].reshape(dec_batch, PLE_DIM),
                                    state_pool[i].reshape(dec_batch, POOL_BUF * D_MODEL),
                                    weights, PAST_LEN)
        sv.append(v.reshape(dec_batch, 1, D_MODEL))
        sp.append(pool.reshape(dec_batch, POOL_BUF, D_MODEL))
    return (hp, hs.reshape(dec_batch, 1, D_MODEL), jnp.stack(pv), jnp.stack(pp), jnp.stack(sv),
            jnp.stack(sp))
```

```python
import functools

import jax
import jax.numpy as jnp
from jax import lax
from jax.experimental import pallas as pl
from jax.experimental.pallas import tpu as pltpu

D_MODEL = 1024
CHUNK = 128
N_HEADS = D_MODEL // CHUNK
POOL_WINDOWS = (2, 4, 8, 16)
POOL_GROUP = D_MODEL // len(POOL_WINDOWS)
POOL_BUF = max(POOL_WINDOWS) - 1
PLE_DIM = 256
PAST_LEN = 16384
EPS = 1e-6
COL_U, COL_V, COL_ZA, COL_XB, COL_ZB, COL_GA, COL_GB = range(7)
VEC_PRE_G, VEC_LN_G, VEC_LN_B, VEC_POOL_SCALE, VEC_POST_G, VEC_PLE_IN_G, VEC_PLE_OUT_G = range(7)
N_VEC_ROWS = 8

LANES = 128
N_LANE_BLOCKS = D_MODEL // LANES
TAIL = 16
TILE_M = 256
VMEM_LIMIT_BYTES = 56 * 1024 * 1024

_BF16 = jnp.bfloat16
_F32 = jnp.float32
_GELU_C = 0.7978845608028654


def _dot(a, b):
    return jnp.dot(a, b, preferred_element_type=_F32)


def _rmsnorm(x, g):
    return x * lax.rsqrt(jnp.mean(x * x, axis=-1, keepdims=True) + EPS) * g


def _layernorm(x, g, b):
    xc = x - jnp.mean(x, axis=-1, keepdims=True)
    return xc * lax.rsqrt(jnp.mean(xc * xc, axis=-1, keepdims=True) + EPS) * g + b


def _gelu_tanh(x):
    return 0.5 * x * (1.0 + jnp.tanh(_GELU_C * (x + 0.044715 * (x * x * x))))


def _sigmoid(x):
    return 0.5 * jnp.tanh(0.5 * x) + 0.5


def _silu(x):
    return x * _sigmoid(x)


def _vec(vecs_ref, row):
    return vecs_ref[row:row + 1, :]


def _in_proj(xn, w_in_ref, col):
    return _dot(xn, w_in_ref[:, col * D_MODEL:(col + 1) * D_MODEL])


def _pool_mix(pooled, w_pool_ref, vecs_ref):
    parts = []
    for g in range(len(POOL_WINDOWS)):
        q = pooled[:, g * POOL_GROUP:(g + 1) * POOL_GROUP].astype(_BF16)
        parts.append(_dot(q, w_pool_ref[g]))
    return jnp.concatenate(parts, axis=1) * _vec(vecs_ref, VEC_POOL_SCALE)


def _merge_and_embed(h, p, xn, y_a, y_b, w_in_ref, w_pa_ref, w_pb_ref, w_out_ref, w_ple_ref,
                     w_pg_ref, vecs_ref):
    g_a = _in_proj(xn, w_in_ref, COL_GA)
    g_b = _in_proj(xn, w_in_ref, COL_GB)
    m = (_sigmoid(g_a) * _dot(y_a.astype(_BF16), w_pa_ref[...])
         + _sigmoid(g_b) * _dot(y_b.astype(_BF16), w_pb_ref[...]))
    h = h + _rmsnorm(_dot(m.astype(_BF16), w_out_ref[...]), _vec(vecs_ref, VEC_POST_G))
    e = _dot(p.astype(_BF16), w_ple_ref[...])
    hn = _rmsnorm(h, _vec(vecs_ref, VEC_PLE_IN_G)).astype(_BF16)
    gate = _sigmoid(_dot(hn, w_pg_ref[...]))
    return h + _rmsnorm(gate * e, _vec(vecs_ref, VEC_PLE_OUT_G))


def _prompt_kernel(x_ref, p_ref, w_in_ref, w_s_ref, bs_t_ref, w_pool_ref, w_pa_ref, w_pb_ref,
                   w_out_ref, w_ple_ref, w_pg_ref, vecs_ref,
                   y_ref, vn_ref, pool_ref, ext_ref, *, tile_m):
    j = pl.program_id(1)

    @pl.when(j == 0)
    def _():
        ext_ref[:, 0:TAIL, :] = jnp.zeros((N_LANE_BLOCKS, TAIL, LANES), _F32)

    x = x_ref[0]
    xn = _rmsnorm(x, _vec(vecs_ref, VEC_PRE_G)).astype(_BF16)

    u = _gelu_tanh(_in_proj(xn, w_in_ref, COL_U))
    vn = _layernorm(_gelu_tanh(_in_proj(xn, w_in_ref, COL_V)),
                    _vec(vecs_ref, VEC_LN_G), _vec(vecs_ref, VEC_LN_B))
    vn_ref[0] = vn[tile_m - CHUNK:, :]
    vn_b = vn.astype(_BF16)
    causal = (lax.broadcasted_iota(jnp.int32, (CHUNK, CHUNK), 0)
              >= lax.broadcasted_iota(jnp.int32, (CHUNK, CHUNK), 1))
    w_tril = [jnp.where(causal, w_s_ref[hd], 0.0).astype(_BF16) for hd in range(N_HEADS)]
    bias = [bs_t_ref[:, hd:hd + 1] for hd in range(N_HEADS)]
    s_rows = []
    for c in range(tile_m // CHUNK):
        heads = []
        for hd in range(N_HEADS):
            blk = vn_b[c * CHUNK:(c + 1) * CHUNK, hd * CHUNK:(hd + 1) * CHUNK]
            heads.append(_dot(w_tril[hd], blk) + bias[hd])
        s_rows.append(jnp.concatenate(heads, axis=1))
    s = jnp.concatenate(s_rows, axis=0)
    y_a = u * s * _silu(_in_proj(xn, w_in_ref, COL_ZA))

    x_b = _in_proj(xn, w_in_ref, COL_XB)
    for c in range(N_LANE_BLOCKS):
        ext_ref[c, TAIL:TAIL + tile_m, :] = x_b[:, c * LANES:(c + 1) * LANES]
    pos = j * tile_m + lax.broadcasted_iota(jnp.int32, (tile_m, 1), 0)
    blocks = []
    for c in range(N_LANE_BLOCKS):
        w = POOL_WINDOWS[c * LANES // POOL_GROUP]
        cur = x_b[:, c * LANES:(c + 1) * LANES]
        win = cur
        for k in range(1, w):
            win = win + ext_ref[c, TAIL - k:TAIL - k + tile_m, :]
        cnt = jnp.minimum(pos + 1, w).astype(_F32)
        blocks.append(win / cnt - cur)
    pooled = jnp.concatenate(blocks, axis=1)
    for c in range(N_LANE_BLOCKS):
        pool_ref[0, :, c * LANES:(c + 1) * LANES] = (
            ext_ref[c, TAIL + tile_m - POOL_BUF:TAIL + tile_m, :])
        ext_ref[c, 0:TAIL, :] = ext_ref[c, tile_m:tile_m + TAIL, :]
    y_b = _pool_mix(pooled, w_pool_ref, vecs_ref) * _silu(_in_proj(xn, w_in_ref, COL_ZB))

    y_ref[0] = _merge_and_embed(x, p_ref[0], xn, y_a, y_b, w_in_ref, w_pa_ref, w_pb_ref,
                                w_out_ref, w_ple_ref, w_pg_ref, vecs_ref)


def _sample_kernel(x_ref, p_ref, state_ref, w_in_ref, w_s_ref, bs_t_ref, w_pool_ref, w_pa_ref,
                   w_pb_ref, w_out_ref, w_ple_ref, w_pg_ref, vecs_ref,
                   y_ref, vn_ref, pool_ref, *, pos0):
    x = x_ref[...]
    xn = _rmsnorm(x, _vec(vecs_ref, VEC_PRE_G)).astype(_BF16)

    u = _gelu_tanh(_in_proj(xn, w_in_ref, COL_U))
    vn = _layernorm(_gelu_tanh(_in_proj(xn, w_in_ref, COL_V)),
                    _vec(vecs_ref, VEC_LN_G), _vec(vecs_ref, VEC_LN_B))
    vn_ref[...] = vn
    heads = []
    for hd in range(N_HEADS):
        tap = w_s_ref[hd, 0:1, 0:1]
        heads.append(vn[:, hd * CHUNK:(hd + 1) * CHUNK] * tap + bs_t_ref[0:1, hd:hd + 1])
    s = jnp.concatenate(heads, axis=1)
    y_a = u * s * _silu(_in_proj(xn, w_in_ref, COL_ZA))

    x_b = _in_proj(xn, w_in_ref, COL_XB)
    groups = []
    for g, w in enumerate(POOL_WINDOWS):
        lo, hi = g * POOL_GROUP, (g + 1) * POOL_GROUP
        cur = x_b[:, lo:hi]
        win = cur
        for k in range(1, w):
            win = win + state_ref[POOL_BUF - k, :, lo:hi]
        groups.append(win / float(min(pos0 + 1, w)) - cur)
    pooled = jnp.concatenate(groups, axis=1)
    pool_ref[0:POOL_BUF - 1] = state_ref[1:POOL_BUF]
    pool_ref[POOL_BUF - 1] = x_b
    y_b = _pool_mix(pooled, w_pool_ref, vecs_ref) * _silu(_in_proj(xn, w_in_ref, COL_ZB))

    y_ref[...] = _merge_and_embed(x, p_ref[...], xn, y_a, y_b, w_in_ref, w_pa_ref, w_pb_ref,
                                  w_out_ref, w_ple_ref, w_pg_ref, vecs_ref)


def _resident(shape):
    zeros = (0,) * len(shape)
    return pl.BlockSpec(shape, lambda *_: zeros, pipeline_mode=pl.Buffered(1))


def _prompt_layer(h, p, weights, tile_m):
    batch, seq, _ = h.shape
    assert seq % tile_m == 0 and tile_m % CHUNK == 0 and seq >= POOL_BUF
    row_map = lambda b, j: (b, j, 0)
    per_prompt = lambda b, j: (b, 0, 0)
    return pl.pallas_call(
        functools.partial(_prompt_kernel, tile_m=tile_m),
        grid=(batch, seq // tile_m),
        in_specs=[pl.BlockSpec((1, tile_m, D_MODEL), row_map),
                  pl.BlockSpec((1, tile_m, PLE_DIM), row_map)]
                 + [_resident(w.shape) for w in weights],
        out_specs=[pl.BlockSpec((1, tile_m, D_MODEL), row_map),
                   pl.BlockSpec((1, CHUNK, D_MODEL), per_prompt),
                   pl.BlockSpec((1, POOL_BUF, D_MODEL), per_prompt)],
        out_shape=[jax.ShapeDtypeStruct((batch, seq, D_MODEL), _F32),
                   jax.ShapeDtypeStruct((batch, CHUNK, D_MODEL), _F32),
                   jax.ShapeDtypeStruct((batch, POOL_BUF, D_MODEL), _F32)],
        scratch_shapes=[pltpu.VMEM((N_LANE_BLOCKS, TAIL + tile_m, LANES), _F32)],
        compiler_params=pltpu.CompilerParams(
            dimension_semantics=("arbitrary", "arbitrary"),
            vmem_limit_bytes=VMEM_LIMIT_BYTES),
        name="prompt_layer",
    )(h, p, *weights)


def _sample_layer(h, p, state, weights, pos0):
    rows = h.shape[0]
    return pl.pallas_call(
        functools.partial(_sample_kernel, pos0=pos0),
        out_shape=[jax.ShapeDtypeStruct((rows, D_MODEL), _F32),
                   jax.ShapeDtypeStruct((rows, D_MODEL), _F32),
                   jax.ShapeDtypeStruct((POOL_BUF, rows, D_MODEL), _F32)],
        compiler_params=pltpu.CompilerParams(vmem_limit_bytes=VMEM_LIMIT_BYTES),
        name="sample_layer",
    )(h, p, state, *weights)


def kernel(x_prompt, x_sample, state_pool, p_prompt, p_sample, pre_g, w_in, ln_g, ln_b, w_s, b_s,
           w_pool, pool_scale, w_pa, w_pb, w_out, post_g, w_ple, w_pg, ple_in_g, ple_out_g):
    depth = w_in.shape[0]
    dec_batch, dec_seq, _ = x_sample.shape
    assert dec_seq == 1 and x_prompt.shape[2] == D_MODEL
    hp = x_prompt
    hs = x_sample.reshape(dec_batch, D_MODEL)
    pv, pp, sv, sp = [], [], [], []
    for i in range(depth):
        vecs = jnp.stack([pre_g[i], ln_g[i], ln_b[i], pool_scale[i], post_g[i], ple_in_g[i],
                          ple_out_g[i]] + [jnp.zeros_like(pre_g[i])] * (N_VEC_ROWS - 7))
        weights = (w_in[i].astype(_BF16), w_s[i], b_s[i].T, w_pool[i].astype(_BF16),
                   w_pa[i].astype(_BF16), w_pb[i].astype(_BF16), w_out[i].astype(_BF16),
                   w_ple[i].astype(_BF16), w_pg[i].astype(_BF16), vecs)
        hp, v, pool = _prompt_layer(hp, p_prompt[i], weights, TILE_M)
        pv.append(v)
        pp.append(pool)
        hs, v, pool = _sample_layer(hs, p_sample[i].reshape(dec_batch, PLE_DIM),
                                    jnp.swapaxes(state_pool[i], 0, 1), weights, PAST_LEN)
        sv.append(v.reshape(dec_batch, 1, D_MODEL))
        sp.append(jnp.swapaxes(pool, 0, 1))
    return (hp, hs.reshape(dec_batch, 1, D_MODEL), jnp.stack(pv), jnp.stack(pp), jnp.stack(sv),
            jnp.stack(sp))
```

```python
import functools

import jax
import jax.numpy as jnp
from jax import lax
from jax.experimental import pallas as pl
from jax.experimental.pallas import tpu as pltpu

D_MODEL = 1024
CHUNK = 128
N_HEADS = D_MODEL // CHUNK
POOL_WINDOWS = (2, 4, 8, 16)
POOL_GROUP = D_MODEL // len(POOL_WINDOWS)
POOL_BUF = max(POOL_WINDOWS) - 1
PLE_DIM = 256
PAST_LEN = 16384
EPS = 1e-6
COL_U, COL_V, COL_ZA, COL_XB, COL_ZB, COL_GA, COL_GB = range(7)
VEC_PRE_G, VEC_LN_G, VEC_LN_B, VEC_POOL_SCALE, VEC_POST_G, VEC_PLE_IN_G, VEC_PLE_OUT_G = range(7)
N_VEC_ROWS = 8

LANES = 128
N_LANE_BLOCKS = D_MODEL // LANES
TAIL = 16
TILE_M = 256
VMEM_LIMIT_BYTES = 56 * 1024 * 1024

_BF16 = jnp.bfloat16
_F32 = jnp.float32
_GELU_C = 0.7978845608028654


def _dot(a, b):
    return jnp.dot(a, b, preferred_element_type=_F32)


def _rmsnorm(x, g):
    return x * lax.rsqrt(jnp.mean(x * x, axis=-1, keepdims=True) + EPS) * g


def _layernorm(x, g, b):
    xc = x - jnp.mean(x, axis=-1, keepdims=True)
    return xc * lax.rsqrt(jnp.mean(xc * xc, axis=-1, keepdims=True) + EPS) * g + b


def _gelu_tanh(x):
    return 0.5 * x * (1.0 + jnp.tanh(_GELU_C * (x + 0.044715 * (x * x * x))))


def _sigmoid(x):
    return 0.5 * jnp.tanh(0.5 * x) + 0.5


def _silu(x):
    return x * _sigmoid(x)


def _vec(vecs_ref, row):
    return vecs_ref[row:row + 1, :]


def _in_proj(xn, w_in_ref, col):
    return _dot(xn, w_in_ref[:, col * D_MODEL:(col + 1) * D_MODEL])


def _pool_mix(pooled, w_pool_ref, vecs_ref):
    parts = []
    for g in range(len(POOL_WINDOWS)):
        q = pooled[:, g * POOL_GROUP:(g + 1) * POOL_GROUP].astype(_BF16)
        parts.append(_dot(q, w_pool_ref[g]))
    return jnp.concatenate(parts, axis=1) * _vec(vecs_ref, VEC_POOL_SCALE)


def _merge_and_embed(h, p, xn, y_a, y_b, w_in_ref, w_pa_ref, w_pb_ref, w_out_ref, w_ple_ref,
                     w_pg_ref, vecs_ref):
    g_a = _in_proj(xn, w_in_ref, COL_GA)
    g_b = _in_proj(xn, w_in_ref, COL_GB)
    m = (_sigmoid(g_a) * _dot(y_a.astype(_BF16), w_pa_ref[...])
         + _sigmoid(g_b) * _dot(y_b.astype(_BF16), w_pb_ref[...]))
    h = h + _rmsnorm(_dot(m.astype(_BF16), w_out_ref[...]), _vec(vecs_ref, VEC_POST_G))
    e = _dot(p.astype(_BF16), w_ple_ref[...])
    hn = _rmsnorm(h, _vec(vecs_ref, VEC_PLE_IN_G)).astype(_BF16)
    gate = _sigmoid(_dot(hn, w_pg_ref[...]))
    return h + _rmsnorm(gate * e, _vec(vecs_ref, VEC_PLE_OUT_G))


def _prompt_kernel(x_ref, x_lag_ref, p_lag_ref, w_in_ref, w_s_ref, bs_t_ref, w_pool_ref, w_pa_ref,
                   w_pb_ref, w_out_ref, w_ple_ref, w_pg_ref, vecs_ref,
                   y_ref, vn_ref, pool_ref, ext_ref, xn_c, ya_c, yb_c, *, tile_m, tiles_per_prompt,
                   n_tiles):
    s_id = pl.program_id(0)
    j = lax.rem(jnp.minimum(s_id, n_tiles - 1), tiles_per_prompt)

    @pl.when(s_id == 0)
    def _():
        xn_c[...] = jnp.zeros(xn_c.shape, xn_c.dtype)
        ya_c[...] = jnp.zeros(ya_c.shape, ya_c.dtype)
        yb_c[...] = jnp.zeros(yb_c.shape, yb_c.dtype)

    @pl.when(j == 0)
    def _():
        ext_ref[:, 0:TAIL, :] = jnp.zeros((N_LANE_BLOCKS, TAIL, LANES), _F32)

    vec = functools.partial(_vec, vecs_ref)
    xn_prev = xn_c[...]
    g_a = _in_proj(xn_prev, w_in_ref, COL_GA)
    g_b = _in_proj(xn_prev, w_in_ref, COL_GB)
    x = x_ref[0]
    xn = _rmsnorm(x, vec(VEC_PRE_G)).astype(_BF16)
    xn_c[...] = xn
    u_raw = _in_proj(xn, w_in_ref, COL_U)
    v_raw = _in_proj(xn, w_in_ref, COL_V)
    gate_a = _sigmoid(g_a)
    gate_b = _sigmoid(g_b)
    proj_a = _dot(ya_c[...], w_pa_ref[...])
    proj_b = _dot(yb_c[...], w_pb_ref[...])
    u = _gelu_tanh(u_raw)
    vn = _layernorm(_gelu_tanh(v_raw), vec(VEC_LN_G), vec(VEC_LN_B))
    vn_ref[0] = vn[tile_m - CHUNK:, :]
    vn_b = vn.astype(_BF16)
    za_raw = _in_proj(xn, w_in_ref, COL_ZA)
    x_b = _in_proj(xn, w_in_ref, COL_XB)
    m = (gate_a * proj_a + gate_b * proj_b).astype(_BF16)

    causal =(lax.broadcasted_iota(jnp.int32, (CHUNK, CHUNK), 0)
              >= lax.broadcasted_iota(jnp.int32, (CHUNK, CHUNK), 1))
    w_tril = [jnp.where(causal, w_s_ref[hd], 0.0).astype(_BF16) for hd in range(N_HEADS)]
    bias = [bs_t_ref[:, hd:hd + 1] for hd in range(N_HEADS)]
    s_rows = []
    for c in range(tile_m // CHUNK):
        heads = []
        for hd in range(N_HEADS):
            blk = vn_b[c * CHUNK:(c + 1) * CHUNK, hd * CHUNK:(hd + 1) * CHUNK]
            heads.append(_dot(w_tril[hd], blk) + bias[hd])
        s_rows.append(jnp.concatenate(heads, axis=1))
    s = jnp.concatenate(s_rows, axis=0)
    o = _dot(m, w_out_ref[...])
    ya_c[...] = (u * s * _silu(za_raw)).astype(_BF16)

    for c in range(N_LANE_BLOCKS):
        ext_ref[c, TAIL:TAIL + tile_m, :] = x_b[:, c * LANES:(c + 1) * LANES]
    pos = j * tile_m + lax.broadcasted_iota(jnp.int32, (tile_m, 1), 0)
    blocks = []
    for c in range(N_LANE_BLOCKS):
        w = POOL_WINDOWS[c * LANES // POOL_GROUP]
        cur = x_b[:, c * LANES:(c + 1) * LANES]
        win = cur
        for k in range(1, w):
            win = win + ext_ref[c, TAIL - k:TAIL - k + tile_m, :]
        cnt = jnp.minimum(pos + 1, w).astype(_F32)
        blocks.append(win / cnt - cur)
    pooled = jnp.concatenate(blocks, axis=1)
    for c in range(N_LANE_BLOCKS):
        pool_ref[0, :, c * LANES:(c + 1) * LANES] = (
            ext_ref[c, TAIL + tile_m - POOL_BUF:TAIL + tile_m, :])
        ext_ref[c, 0:TAIL, :] = ext_ref[c, tile_m:tile_m + TAIL, :]
    h = x_lag_ref[0] + _rmsnorm(o, vec(VEC_POST_G))
    hn = _rmsnorm(h, vec(VEC_PLE_IN_G)).astype(_BF16)
    e = _dot(p_lag_ref[0].astype(_BF16), w_ple_ref[...])
    gate_raw = _dot(hn, w_pg_ref[...])
    mixed = _pool_mix(pooled, w_pool_ref, vecs_ref)
    zb_raw = _in_proj(xn, w_in_ref, COL_ZB)
    y_ref[0] = h + _rmsnorm(_sigmoid(gate_raw) * e, vec(VEC_PLE_OUT_G))
    yb_c[...] = (mixed * _silu(zb_raw)).astype(_BF16)


def _sample_kernel(x_ref, p_ref, state_ref, w_in_ref, w_s_ref, bs_t_ref, w_pool_ref, w_pa_ref,
                   w_pb_ref, w_out_ref, w_ple_ref, w_pg_ref, vecs_ref,
                   y_ref, vn_ref, pool_ref, *, pos0):
    x = x_ref[...]
    xn = _rmsnorm(x, _vec(vecs_ref, VEC_PRE_G)).astype(_BF16)

    u = _gelu_tanh(_in_proj(xn, w_in_ref, COL_U))
    vn = _layernorm(_gelu_tanh(_in_proj(xn, w_in_ref, COL_V)),
                    _vec(vecs_ref, VEC_LN_G), _vec(vecs_ref, VEC_LN_B))
    vn_ref[...] = vn
    heads = []
    for hd in range(N_HEADS):
        tap = w_s_ref[hd, 0:1, 0:1]
        heads.append(vn[:, hd * CHUNK:(hd + 1) * CHUNK] * tap + bs_t_ref[0:1, hd:hd + 1])
    s = jnp.concatenate(heads, axis=1)
    y_a = u * s * _silu(_in_proj(xn, w_in_ref, COL_ZA))

    x_b = _in_proj(xn, w_in_ref, COL_XB)
    groups = []
    for g, w in enumerate(POOL_WINDOWS):
        lo, hi = g * POOL_GROUP, (g + 1) * POOL_GROUP
        cur = x_b[:, lo:hi]
        win = cur
        for k in range(1, w):
            win = win + state_ref[POOL_BUF - k, :, lo:hi]
        groups.append(win / float(min(pos0 + 1, w)) - cur)
    pooled = jnp.concatenate(groups, axis=1)
    pool_ref[0:POOL_BUF - 1] = state_ref[1:POOL_BUF]
    pool_ref[POOL_BUF - 1] = x_b
    y_b = _pool_mix(pooled, w_pool_ref, vecs_ref) * _silu(_in_proj(xn, w_in_ref, COL_ZB))

    y_ref[...] = _merge_and_embed(x, p_ref[...], xn, y_a, y_b, w_in_ref, w_pa_ref, w_pb_ref,
                                  w_out_ref, w_ple_ref, w_pg_ref, vecs_ref)


def _resident(shape):
    zeros = (0,) * len(shape)
    return pl.BlockSpec(shape, lambda *_: zeros, pipeline_mode=pl.Buffered(1))


def _prompt_layer(h, p, weights, tile_m):
    batch, seq, _ = h.shape
    assert seq % tile_m == 0 and tile_m % CHUNK == 0 and seq >= POOL_BUF
    tiles_per_prompt = seq // tile_m
    n_tiles = batch * tiles_per_prompt

    def front_tile(s):
        t = jnp.minimum(s, n_tiles - 1)
        return t // tiles_per_prompt, t % tiles_per_prompt, 0

    def back_tile(s):
        t = jnp.maximum(s - 1, 0)
        return t // tiles_per_prompt, t % tiles_per_prompt, 0

    per_prompt = lambda s: (jnp.minimum(s, n_tiles - 1) // tiles_per_prompt, 0, 0)
    return pl.pallas_call(
        functools.partial(_prompt_kernel, tile_m=tile_m, tiles_per_prompt=tiles_per_prompt,
                          n_tiles=n_tiles),
        grid=(n_tiles + 1,),
        in_specs=[pl.BlockSpec((1, tile_m, D_MODEL), front_tile),
                  pl.BlockSpec((1, tile_m, D_MODEL), back_tile),
                  pl.BlockSpec((1, tile_m, PLE_DIM), back_tile)]
                 + [_resident(w.shape) for w in weights],
        out_specs=[pl.BlockSpec((1, tile_m, D_MODEL), back_tile),
                   pl.BlockSpec((1, CHUNK, D_MODEL), per_prompt),
                   pl.BlockSpec((1, POOL_BUF, D_MODEL), per_prompt)],
        out_shape=[jax.ShapeDtypeStruct((batch, seq, D_MODEL), _F32),
                   jax.ShapeDtypeStruct((batch, CHUNK, D_MODEL), _F32),
                   jax.ShapeDtypeStruct((batch, POOL_BUF, D_MODEL), _F32)],
        scratch_shapes=[pltpu.VMEM((N_LANE_BLOCKS, TAIL + tile_m, LANES), _F32),
                        pltpu.VMEM((tile_m, D_MODEL), _BF16),
                        pltpu.VMEM((tile_m, D_MODEL), _BF16),
                        pltpu.VMEM((tile_m, D_MODEL), _BF16)],
        compiler_params=pltpu.CompilerParams(
            dimension_semantics=("arbitrary",),
            vmem_limit_bytes=VMEM_LIMIT_BYTES),
        name="prompt_layer",
    )(h, h, p, *weights)


def _sample_layer(h, p, state, weights, pos0):
    rows = h.shape[0]
    return pl.pallas_call(
        functools.partial(_sample_kernel, pos0=pos0),
        out_shape=[jax.ShapeDtypeStruct((rows, D_MODEL), _F32),
                   jax.ShapeDtypeStruct((rows, D_MODEL), _F32),
                   jax.ShapeDtypeStruct((POOL_BUF, rows, D_MODEL), _F32)],
        compiler_params=pltpu.CompilerParams(vmem_limit_bytes=VMEM_LIMIT_BYTES),
        name="sample_layer",
    )(h, p, state, *weights)


def kernel(x_prompt, x_sample, state_pool, p_prompt, p_sample, pre_g, w_in, ln_g, ln_b, w_s, b_s,
           w_pool, pool_scale, w_pa, w_pb, w_out, post_g, w_ple, w_pg, ple_in_g, ple_out_g):
    depth = w_in.shape[0]
    dec_batch, dec_seq, _ = x_sample.shape
    assert dec_seq == 1 and x_prompt.shape[2] == D_MODEL
    hp = x_prompt
    hs = x_sample.reshape(dec_batch, D_MODEL)
    pv, pp, sv, sp = [], [], [], []
    for i in range(depth):
        vecs = jnp.stack([pre_g[i], ln_g[i], ln_b[i], pool_scale[i], post_g[i], ple_in_g[i],
                          ple_out_g[i]] + [jnp.zeros_like(pre_g[i])] * (N_VEC_ROWS - 7))
        weights = (w_in[i].astype(_BF16), w_s[i], b_s[i].T, w_pool[i].astype(_BF16),
                   w_pa[i].astype(_BF16), w_pb[i].astype(_BF16), w_out[i].astype(_BF16),
                   w_ple[i].astype(_BF16), w_pg[i].astype(_BF16), vecs)
        hp, v, pool = _prompt_layer(hp, p_prompt[i], weights, TILE_M)
        pv.append(v)
        pp.append(pool)
        hs, v, pool = _sample_layer(hs, p_sample[i].reshape(dec_batch, PLE_DIM),
                                    jnp.swapaxes(state_pool[i], 0, 1), weights, PAST_LEN)
        sv.append(v.reshape(dec_batch, 1, D_MODEL))
        sp.append(jnp.swapaxes(pool, 0, 1))
    return (hp, hs.reshape(dec_batch, 1, D_MODEL), jnp.stack(pv), jnp.stack(pp), jnp.stack(sv),
            jnp.stack(sp))
```

```python
import functools

import jax
import jax.numpy as jnp
from jax import lax
from jax.experimental import pallas as pl
from jax.experimental.pallas import tpu as pltpu

D_MODEL = 1024
CHUNK = 128
N_HEADS = D_MODEL // CHUNK
POOL_WINDOWS = (2, 4, 8, 16)
POOL_GROUP = D_MODEL // len(POOL_WINDOWS)
POOL_BUF = max(POOL_WINDOWS) - 1
PLE_DIM = 256
PAST_LEN = 16384
EPS = 1e-6
COL_U, COL_V, COL_ZA, COL_XB, COL_ZB, COL_GA, COL_GB = range(7)
VEC_PRE_G, VEC_LN_G, VEC_LN_B, VEC_POOL_SCALE, VEC_POST_G, VEC_PLE_IN_G, VEC_PLE_OUT_G = range(7)
N_VEC_ROWS = 8

LANES = 128
N_LANE_BLOCKS = D_MODEL // LANES
BF16_ROWS = 16
TAIL = 16
PACK_ROWS = 256
PACK_ROWS_WIDE = 128
TILE_M = 256
VMEM_LIMIT_BYTES = 56 * 1024 * 1024

_BF16 = jnp.bfloat16
_F32 = jnp.float32
_GELU_C = 0.7978845608028654


def _dot(a, b):
    return jnp.dot(a, b, preferred_element_type=_F32)


def _rmsnorm(x, g):
    return x * lax.rsqrt(jnp.mean(x * x, axis=-1, keepdims=True) + EPS) * g


def _layernorm(x, g, b):
    xc = x - jnp.mean(x, axis=-1, keepdims=True)
    return xc * lax.rsqrt(jnp.mean(xc * xc, axis=-1, keepdims=True) + EPS) * g + b


def _gelu_tanh(x):
    return 0.5 * x * (1.0 + jnp.tanh(_GELU_C * (x + 0.044715 * (x * x * x))))


def _sigmoid(x):
    return 0.5 * jnp.tanh(0.5 * x) + 0.5


def _silu(x):
    return x * _sigmoid(x)


def _unpack(ref, rows=slice(None), cols=slice(None)):
    return pltpu.bitcast(ref[rows, cols], _BF16)


def _vec(vecs_ref, row):
    return vecs_ref[row:row + 1, :]


def _in_proj(xn, w_in_ref, col):
    return _dot(xn, _unpack(w_in_ref, cols=slice(col * D_MODEL, (col + 1) * D_MODEL)))


def _pool_mix(pooled, w_pool_ref, vecs_ref):
    parts = []
    for g in range(len(POOL_WINDOWS)):
        q = pooled[:, g * POOL_GROUP:(g + 1) * POOL_GROUP].astype(_BF16)
        rows = slice(g * POOL_GROUP // 2, (g + 1) * POOL_GROUP // 2)
        parts.append(_dot(q, _unpack(w_pool_ref, rows=rows)))
    return jnp.concatenate(parts, axis=1) * _vec(vecs_ref, VEC_POOL_SCALE)


def _merge_and_embed(h, p, xn, y_a, y_b, w_in_ref, w_pa_ref, w_pb_ref, w_out_ref, w_ple_ref,
                     w_pg_ref, vecs_ref):
    g_a = _in_proj(xn, w_in_ref, COL_GA)
    g_b = _in_proj(xn, w_in_ref, COL_GB)
    m = (_sigmoid(g_a) * _dot(y_a.astype(_BF16), _unpack(w_pa_ref))
         + _sigmoid(g_b) * _dot(y_b.astype(_BF16), _unpack(w_pb_ref)))
    h = h + _rmsnorm(_dot(m.astype(_BF16), _unpack(w_out_ref)), _vec(vecs_ref, VEC_POST_G))
    e = _dot(p.astype(_BF16), _unpack(w_ple_ref))
    hn = _rmsnorm(h, _vec(vecs_ref, VEC_PLE_IN_G)).astype(_BF16)
    gate = _sigmoid(_dot(hn, _unpack(w_pg_ref)))
    return h + _rmsnorm(gate * e, _vec(vecs_ref, VEC_PLE_OUT_G))


def _prompt_kernel(x_ref, x_lag_ref, p_lag_ref, w_in_ref, w_s_ref, bs_t_ref, w_pool_ref, w_pa_ref,
                   w_pb_ref, w_out_ref, w_ple_ref, w_pg_ref, vecs_ref,
                   y_ref, vn_ref, pool_ref, ext_ref, xn_c, ya_c, yb_c, *, tile_m, tiles_per_prompt,
                   n_tiles):
    s_id = pl.program_id(0)
    j = lax.rem(jnp.minimum(s_id, n_tiles - 1), tiles_per_prompt)

    @pl.when(s_id == 0)
    def _():
        xn_c[...] = jnp.zeros(xn_c.shape, xn_c.dtype)
        ya_c[...] = jnp.zeros(ya_c.shape, ya_c.dtype)
        yb_c[...] = jnp.zeros(yb_c.shape, yb_c.dtype)

    @pl.when(j == 0)
    def _():
        ext_ref[:, 0:TAIL, :] = jnp.zeros((N_LANE_BLOCKS, TAIL, LANES), _F32)

    vec = functools.partial(_vec, vecs_ref)
    xn_prev = xn_c[...]
    g_a = _in_proj(xn_prev, w_in_ref, COL_GA)
    g_b = _in_proj(xn_prev, w_in_ref, COL_GB)
    x = x_ref[0]
    xn = _rmsnorm(x, vec(VEC_PRE_G)).astype(_BF16)
    xn_c[...] = xn
    u_raw = _in_proj(xn, w_in_ref, COL_U)
    v_raw = _in_proj(xn, w_in_ref, COL_V)
    gate_a = _sigmoid(g_a)
    gate_b = _sigmoid(g_b)
    proj_a = _dot(ya_c[...], _unpack(w_pa_ref))
    proj_b = _dot(yb_c[...], _unpack(w_pb_ref))
    u = _gelu_tanh(u_raw)
    vn = _layernorm(_gelu_tanh(v_raw), vec(VEC_LN_G), vec(VEC_LN_B))
    vn_ref[0] = vn[tile_m - CHUNK:, :]
    vn_b = vn.astype(_BF16)
    za_raw = _in_proj(xn, w_in_ref, COL_ZA)
    x_b = _in_proj(xn, w_in_ref, COL_XB)
    m = (gate_a * proj_a + gate_b * proj_b).astype(_BF16)

    w_tril = [_unpack(w_s_ref, rows=slice(hd * CHUNK // 2, (hd + 1) * CHUNK // 2))
              for hd in range(N_HEADS)]
    bias = [bs_t_ref[:, hd:hd + 1] for hd in range(N_HEADS)]
    s_rows = []
    for c in range(tile_m // CHUNK):
        heads = []
        for hd in range(N_HEADS):
            blk = vn_b[c * CHUNK:(c + 1) * CHUNK, hd * CHUNK:(hd + 1) * CHUNK]
            heads.append(_dot(w_tril[hd], blk) + bias[hd])
        s_rows.append(jnp.concatenate(heads, axis=1))
    s = jnp.concatenate(s_rows, axis=0)
    o = _dot(m, _unpack(w_out_ref))
    ya_c[...] = (u * s * _silu(za_raw)).astype(_BF16)

    for c in range(N_LANE_BLOCKS):
        ext_ref[c, TAIL:TAIL + tile_m, :] = x_b[:, c * LANES:(c + 1) * LANES]
    pos = j * tile_m + lax.broadcasted_iota(jnp.int32, (tile_m, 1), 0)
    blocks = []
    for c in range(N_LANE_BLOCKS):
        w = POOL_WINDOWS[c * LANES // POOL_GROUP]
        cur = x_b[:, c * LANES:(c + 1) * LANES]
        win = cur
        for k in range(1, w):
            win = win + ext_ref[c, TAIL - k:TAIL - k + tile_m, :]
        cnt = jnp.minimum(pos + 1, w).astype(_F32)
        blocks.append(win / cnt - cur)
    pooled = jnp.concatenate(blocks, axis=1)
    for c in range(N_LANE_BLOCKS):
        pool_ref[0, :, c * LANES:(c + 1) * LANES] = (
            ext_ref[c, TAIL + tile_m - POOL_BUF:TAIL + tile_m, :])
        ext_ref[c, 0:TAIL, :] = ext_ref[c, tile_m:tile_m + TAIL, :]
    h = x_lag_ref[0] + _rmsnorm(o, vec(VEC_POST_G))
    hn = _rmsnorm(h, vec(VEC_PLE_IN_G)).astype(_BF16)
    e = _dot(p_lag_ref[0].astype(_BF16), _unpack(w_ple_ref))
    gate_raw = _dot(hn, _unpack(w_pg_ref))
    mixed = _pool_mix(pooled, w_pool_ref, vecs_ref)
    zb_raw = _in_proj(xn, w_in_ref, COL_ZB)
    y_ref[0] = h + _rmsnorm(_sigmoid(gate_raw) * e, vec(VEC_PLE_OUT_G))
    yb_c[...] = (mixed * _silu(zb_raw)).astype(_BF16)


def _sample_kernel(x_ref, p_ref, state_ref, taps_ref, w_in_ref, w_s_ref, bs_t_ref, w_pool_ref,
                   w_pa_ref, w_pb_ref, w_out_ref, w_ple_ref, w_pg_ref, vecs_ref,
                   y_ref, vn_ref, pool_ref, *, pos0):
    x = x_ref[...]
    xn = _rmsnorm(x, _vec(vecs_ref, VEC_PRE_G)).astype(_BF16)

    u = _gelu_tanh(_in_proj(xn, w_in_ref, COL_U))
    vn = _layernorm(_gelu_tanh(_in_proj(xn, w_in_ref, COL_V)),
                    _vec(vecs_ref, VEC_LN_G), _vec(vecs_ref, VEC_LN_B))
    vn_ref[...] = vn
    heads = []
    for hd in range(N_HEADS):
        tap = taps_ref[hd:hd + 1, 0:1]
        heads.append(vn[:, hd * CHUNK:(hd + 1) * CHUNK] * tap + bs_t_ref[0:1, hd:hd + 1])
    s = jnp.concatenate(heads, axis=1)
    y_a = u * s * _silu(_in_proj(xn, w_in_ref, COL_ZA))

    x_b = _in_proj(xn, w_in_ref, COL_XB)
    groups = []
    for g, w in enumerate(POOL_WINDOWS):
        lo, hi = g * POOL_GROUP, (g + 1) * POOL_GROUP
        cur = x_b[:, lo:hi]
        win = cur
        for k in range(1, w):
            win = win + state_ref[POOL_BUF - k, :, lo:hi]
        groups.append(win / float(min(pos0 + 1, w)) - cur)
    pooled = jnp.concatenate(groups, axis=1)
    pool_ref[0:POOL_BUF - 1] = state_ref[1:POOL_BUF]
    pool_ref[POOL_BUF - 1] = x_b
    y_b = _pool_mix(pooled, w_pool_ref, vecs_ref) * _silu(_in_proj(xn, w_in_ref, COL_ZB))

    y_ref[...] = _merge_and_embed(x, p_ref[...], xn, y_a, y_b, w_in_ref, w_pa_ref, w_pb_ref,
                                  w_out_ref, w_ple_ref, w_pg_ref, vecs_ref)


def _pack_kernel(*refs, tril_period):
    n = len(refs) // 2
    for src, dst in zip(refs[:n], refs[n:]):
        w = src[...]
        if tril_period is not None:
            row = lax.broadcasted_iota(jnp.int32, w.shape, 0) % tril_period
            w = jnp.where(row >= lax.broadcasted_iota(jnp.int32, w.shape, 1), w, 0.0)
        dst[...] = pltpu.bitcast(w.astype(_BF16), jnp.uint32)


def _pack_bf16(mats, block_rows, tril_period=None):
    k, n = mats[0].shape
    assert k % block_rows == 0 and block_rows % (2 * BF16_ROWS) == 0
    assert tril_period is None or block_rows % tril_period == 0
    return pl.pallas_call(
        functools.partial(_pack_kernel, tril_period=tril_period),
        grid=(k // block_rows,),
        in_specs=[pl.BlockSpec((block_rows, n), lambda i: (i, 0))] * len(mats),
        out_specs=[pl.BlockSpec((block_rows // 2, n), lambda i: (i, 0))] * len(mats),
        out_shape=[jax.ShapeDtypeStruct((k // 2, n), jnp.uint32)] * len(mats),
        compiler_params=pltpu.CompilerParams(dimension_semantics=("arbitrary",)),
        name="pack_bf16",
    )(*mats)


def _resident(shape):
    zeros = (0,) * len(shape)
    return pl.BlockSpec(shape, lambda *_: zeros, pipeline_mode=pl.Buffered(1))


def _prompt_layer(h, p, weights, tile_m):
    batch, seq, _ = h.shape
    assert seq % tile_m == 0 and tile_m % CHUNK == 0 and seq >= POOL_BUF
    tiles_per_prompt = seq // tile_m
    n_tiles = batch * tiles_per_prompt

    def front_tile(s):
        t = jnp.minimum(s, n_tiles - 1)
        return t // tiles_per_prompt, t % tiles_per_prompt, 0

    def back_tile(s):
        t = jnp.maximum(s - 1, 0)
        return t // tiles_per_prompt, t % tiles_per_prompt, 0

    per_prompt = lambda s: (jnp.minimum(s, n_tiles - 1) // tiles_per_prompt, 0, 0)
    return pl.pallas_call(
        functools.partial(_prompt_kernel, tile_m=tile_m, tiles_per_prompt=tiles_per_prompt,
                          n_tiles=n_tiles),
        grid=(n_tiles + 1,),
        in_specs=[pl.BlockSpec((1, tile_m, D_MODEL), front_tile),
                  pl.BlockSpec((1, tile_m, D_MODEL), back_tile),
                  pl.BlockSpec((1, tile_m, PLE_DIM), back_tile)]
                 + [_resident(w.shape) for w in weights],
        out_specs=[pl.BlockSpec((1, tile_m, D_MODEL), back_tile),
                   pl.BlockSpec((1, CHUNK, D_MODEL), per_prompt),
                   pl.BlockSpec((1, POOL_BUF, D_MODEL), per_prompt)],
        out_shape=[jax.ShapeDtypeStruct((batch, seq, D_MODEL), _F32),
                   jax.ShapeDtypeStruct((batch, CHUNK, D_MODEL), _F32),
                   jax.ShapeDtypeStruct((batch, POOL_BUF, D_MODEL), _F32)],
        scratch_shapes=[pltpu.VMEM((N_LANE_BLOCKS, TAIL + tile_m, LANES), _F32),
                        pltpu.VMEM((tile_m, D_MODEL), _BF16),
                        pltpu.VMEM((tile_m, D_MODEL), _BF16),
                        pltpu.VMEM((tile_m, D_MODEL), _BF16)],
        compiler_params=pltpu.CompilerParams(
            dimension_semantics=("arbitrary",),
            vmem_limit_bytes=VMEM_LIMIT_BYTES),
        name="prompt_layer",
    )(h, h, p, *weights)


def _sample_layer(h, p, state, taps, weights, pos0):
    rows = h.shape[0]
    return pl.pallas_call(
        functools.partial(_sample_kernel, pos0=pos0),
        out_shape=[jax.ShapeDtypeStruct((rows, D_MODEL), _F32),
                   jax.ShapeDtypeStruct((rows, D_MODEL), _F32),
                   jax.ShapeDtypeStruct((POOL_BUF, rows, D_MODEL), _F32)],
        compiler_params=pltpu.CompilerParams(vmem_limit_bytes=VMEM_LIMIT_BYTES),
        name="sample_layer",
    )(h, p, state, taps, *weights)


def kernel(x_prompt, x_sample, state_pool, p_prompt, p_sample, pre_g, w_in, ln_g, ln_b, w_s, b_s,
           w_pool, pool_scale, w_pa, w_pb, w_out, post_g, w_ple, w_pg, ple_in_g, ple_out_g):
    depth = w_in.shape[0]
    dec_batch, dec_seq, _ = x_sample.shape
    assert dec_seq == 1 and x_prompt.shape[2] == D_MODEL
    hp = x_prompt
    hs = x_sample.reshape(dec_batch, D_MODEL)
    pv, pp, sv, sp = [], [], [], []
    for i in range(depth):
        vecs = jnp.stack([pre_g[i], ln_g[i], ln_b[i], pool_scale[i], post_g[i], ple_in_g[i],
                          ple_out_g[i]] + [jnp.zeros_like(pre_g[i])] * (N_VEC_ROWS - 7))
        (w_in_p,) = _pack_bf16([w_in[i]], PACK_ROWS_WIDE)
        w_pa_p, w_pb_p, w_out_p, w_pg_p = _pack_bf16([w_pa[i], w_pb[i], w_out[i], w_pg[i]],
                                                     PACK_ROWS)
        (w_ple_p,) = _pack_bf16([w_ple[i]], PLE_DIM)
        (w_pool_p,) = _pack_bf16([w_pool[i].reshape(D_MODEL, POOL_GROUP)], D_MODEL)
        (w_s_p,) = _pack_bf16([w_s[i].reshape(D_MODEL, CHUNK)], D_MODEL, tril_period=CHUNK)
        weights = (w_in_p, w_s_p, b_s[i].T, w_pool_p, w_pa_p, w_pb_p, w_out_p, w_ple_p, w_pg_p,
                   vecs)
        hp, v, pool = _prompt_layer(hp, p_prompt[i], weights, TILE_M)
        pv.append(v)
        pp.append(pool)
        hs, v, pool = _sample_layer(hs, p_sample[i].reshape(dec_batch, PLE_DIM),
                                    jnp.swapaxes(state_pool[i], 0, 1), w_s[i][:, 0, :], weights,
                                    PAST_LEN)
        sv.append(v.reshape(dec_batch, 1, D_MODEL))
        sp.append(jnp.swapaxes(pool, 0, 1))
    return (hp, hs.reshape(dec_batch, 1, D_MODEL), jnp.stack(pv), jnp.stack(pp), jnp.stack(sv),
            jnp.stack(sp))
```

```python
import functools

import jax
import jax.numpy as jnp
from jax import lax
from jax.experimental import pallas as pl
from jax.experimental.pallas import tpu as pltpu

D_MODEL = 1024
CHUNK = 128
N_HEADS = D_MODEL // CHUNK
POOL_WINDOWS = (2, 4, 8, 16)
POOL_GROUP = D_MODEL // len(POOL_WINDOWS)
POOL_BUF = max(POOL_WINDOWS) - 1
PLE_DIM = 256
PAST_LEN = 16384
EPS = 1e-6
COL_U, COL_V, COL_ZA, COL_XB, COL_ZB, COL_GA, COL_GB = range(7)
VEC_PRE_G, VEC_LN_G, VEC_LN_B, VEC_POOL_SCALE, VEC_POST_G, VEC_PLE_IN_G, VEC_PLE_OUT_G = range(7)
N_VEC_ROWS = 8

LANES = 128
N_LANE_BLOCKS = D_MODEL // LANES
BF16_ROWS = 16
TAIL = 16
PACK_ROWS = 256
PACK_ROWS_WIDE = 128
TILE_M = 512
VMEM_LIMIT_BYTES = 56 * 1024 * 1024

_BF16 = jnp.bfloat16
_F32 = jnp.float32
_GELU_C = 0.7978845608028654


def _dot(a, b):
    return jnp.dot(a, b, preferred_element_type=_F32)


def _rmsnorm(x, g):
    return x * lax.rsqrt(jnp.mean(x * x, axis=-1, keepdims=True) + EPS) * g


def _layernorm(x, g, b):
    xc = x - jnp.mean(x, axis=-1, keepdims=True)
    return xc * lax.rsqrt(jnp.mean(xc * xc, axis=-1, keepdims=True) + EPS) * g + b


def _gelu_tanh(x):
    return 0.5 * x * (1.0 + jnp.tanh(_GELU_C * (x + 0.044715 * (x * x * x))))


def _sigmoid(x):
    return 0.5 * jnp.tanh(0.5 * x) + 0.5


def _silu(x):
    return x * _sigmoid(x)


def _unpack(ref, rows=slice(None), cols=slice(None)):
    return pltpu.bitcast(ref[rows, cols], _BF16)


def _vec(vecs_ref, row):
    return vecs_ref[row:row + 1, :]


def _in_proj(xn, w_in_ref, col):
    return _dot(xn, _unpack(w_in_ref, cols=slice(col * D_MODEL, (col + 1) * D_MODEL)))


def _pool_mix(pooled, w_pool_ref, vecs_ref):
    parts = []
    for g in range(len(POOL_WINDOWS)):
        q = pooled[:, g * POOL_GROUP:(g + 1) * POOL_GROUP].astype(_BF16)
        rows = slice(g * POOL_GROUP // 2, (g + 1) * POOL_GROUP // 2)
        parts.append(_dot(q, _unpack(w_pool_ref, rows=rows)))
    return jnp.concatenate(parts, axis=1) * _vec(vecs_ref, VEC_POOL_SCALE)


def _merge_and_embed(h, p, xn, y_a, y_b, w_in_ref, w_pa_ref, w_pb_ref, w_out_ref, w_ple_ref,
                     w_pg_ref, vecs_ref):
    g_a = _in_proj(xn, w_in_ref, COL_GA)
    g_b = _in_proj(xn, w_in_ref, COL_GB)
    m = (_sigmoid(g_a) * _dot(y_a.astype(_BF16), _unpack(w_pa_ref))
         + _sigmoid(g_b) * _dot(y_b.astype(_BF16), _unpack(w_pb_ref)))
    h = h + _rmsnorm(_dot(m.astype(_BF16), _unpack(w_out_ref)), _vec(vecs_ref, VEC_POST_G))
    e = _dot(p.astype(_BF16), _unpack(w_ple_ref))
    hn = _rmsnorm(h, _vec(vecs_ref, VEC_PLE_IN_G)).astype(_BF16)
    gate = _sigmoid(_dot(hn, _unpack(w_pg_ref)))
    return h + _rmsnorm(gate * e, _vec(vecs_ref, VEC_PLE_OUT_G))


def _prompt_kernel(x_ref, x_lag_ref, p_lag_ref, w_in_ref, w_s_ref, bs_t_ref, w_pool_ref, w_pa_ref,
                   w_pb_ref, w_out_ref, w_ple_ref, w_pg_ref, vecs_ref,
                   y_ref, vn_ref, pool_ref, ext_ref, xn_c, ya_c, yb_c, *, tile_m, tiles_per_prompt,
                   n_tiles):
    s_id = pl.program_id(0)
    j = lax.rem(jnp.minimum(s_id, n_tiles - 1), tiles_per_prompt)

    @pl.when(s_id == 0)
    def _():
        xn_c[...] = jnp.zeros(xn_c.shape, xn_c.dtype)
        ya_c[...] = jnp.zeros(ya_c.shape, ya_c.dtype)
        yb_c[...] = jnp.zeros(yb_c.shape, yb_c.dtype)

    @pl.when(j == 0)
    def _():
        ext_ref[:, 0:TAIL, :] = jnp.zeros((N_LANE_BLOCKS, TAIL, LANES), _F32)

    vec = functools.partial(_vec, vecs_ref)
    xn_prev = xn_c[...]
    g_a = _in_proj(xn_prev, w_in_ref, COL_GA)
    g_b = _in_proj(xn_prev, w_in_ref, COL_GB)
    x = x_ref[0]
    xn = _rmsnorm(x, vec(VEC_PRE_G)).astype(_BF16)
    xn_c[...] = xn
    u_raw = _in_proj(xn, w_in_ref, COL_U)
    v_raw = _in_proj(xn, w_in_ref, COL_V)
    gate_a = _sigmoid(g_a)
    gate_b = _sigmoid(g_b)
    proj_a = _dot(ya_c[...], _unpack(w_pa_ref))
    proj_b = _dot(yb_c[...], _unpack(w_pb_ref))
    u = _gelu_tanh(u_raw)
    vn = _layernorm(_gelu_tanh(v_raw), vec(VEC_LN_G), vec(VEC_LN_B))
    vn_ref[0] = vn[tile_m - CHUNK:, :]
    vn_b = vn.astype(_BF16)
    za_raw = _in_proj(xn, w_in_ref, COL_ZA)
    x_b = _in_proj(xn, w_in_ref, COL_XB)
    m = (gate_a * proj_a + gate_b * proj_b).astype(_BF16)

    w_tril = [_unpack(w_s_ref, rows=slice(hd * CHUNK // 2, (hd + 1) * CHUNK // 2))
              for hd in range(N_HEADS)]
    bias = [bs_t_ref[:, hd:hd + 1] for hd in range(N_HEADS)]
    s_rows = []
    for c in range(tile_m // CHUNK):
        heads = []
        for hd in range(N_HEADS):
            blk = vn_b[c * CHUNK:(c + 1) * CHUNK, hd * CHUNK:(hd + 1) * CHUNK]
            heads.append(_dot(w_tril[hd], blk) + bias[hd])
        s_rows.append(jnp.concatenate(heads, axis=1))
    s = jnp.concatenate(s_rows, axis=0)
    o = _dot(m, _unpack(w_out_ref))
    ya_c[...] = (u * s * _silu(za_raw)).astype(_BF16)

    for c in range(N_LANE_BLOCKS):
        ext_ref[c, TAIL:TAIL + tile_m, :] = x_b[:, c * LANES:(c + 1) * LANES]
    pos = j * tile_m + lax.broadcasted_iota(jnp.int32, (tile_m, 1), 0)
    blocks = []
    for c in range(N_LANE_BLOCKS):
        w = POOL_WINDOWS[c * LANES // POOL_GROUP]
        cur = x_b[:, c * LANES:(c + 1) * LANES]
        win = cur
        for k in range(1, w):
            win = win + ext_ref[c, TAIL - k:TAIL - k + tile_m, :]
        cnt = jnp.minimum(pos + 1, w).astype(_F32)
        blocks.append(win / cnt - cur)
    pooled = jnp.concatenate(blocks, axis=1)
    for c in range(N_LANE_BLOCKS):
        pool_ref[0, :, c * LANES:(c + 1) * LANES] = (
            ext_ref[c, TAIL + tile_m - POOL_BUF:TAIL + tile_m, :])
        ext_ref[c, 0:TAIL, :] = ext_ref[c, tile_m:tile_m + TAIL, :]
    h = x_lag_ref[0] + _rmsnorm(o, vec(VEC_POST_G))
    hn = _rmsnorm(h, vec(VEC_PLE_IN_G)).astype(_BF16)
    e = _dot(p_lag_ref[0].astype(_BF16), _unpack(w_ple_ref))
    gate_raw = _dot(hn, _unpack(w_pg_ref))
    mixed = _pool_mix(pooled, w_pool_ref, vecs_ref)
    zb_raw = _in_proj(xn, w_in_ref, COL_ZB)
    y_ref[0] = h + _rmsnorm(_sigmoid(gate_raw) * e, vec(VEC_PLE_OUT_G))
    yb_c[...] = (mixed * _silu(zb_raw)).astype(_BF16)


def _sample_kernel(x_ref, p_ref, state_ref, taps_ref, w_in_ref, w_s_ref, bs_t_ref, w_pool_ref,
                   w_pa_ref, w_pb_ref, w_out_ref, w_ple_ref, w_pg_ref, vecs_ref,
                   y_ref, vn_ref, pool_ref, *, pos0):
    x = x_ref[...]
    xn = _rmsnorm(x, _vec(vecs_ref, VEC_PRE_G)).astype(_BF16)

    u = _gelu_tanh(_in_proj(xn, w_in_ref, COL_U))
    vn = _layernorm(_gelu_tanh(_in_proj(xn, w_in_ref, COL_V)),
                    _vec(vecs_ref, VEC_LN_G), _vec(vecs_ref, VEC_LN_B))
    vn_ref[...] = vn
    heads = []
    for hd in range(N_HEADS):
        tap = taps_ref[hd:hd + 1, 0:1]
        heads.append(vn[:, hd * CHUNK:(hd + 1) * CHUNK] * tap + bs_t_ref[0:1, hd:hd + 1])
    s = jnp.concatenate(heads, axis=1)
    y_a = u * s * _silu(_in_proj(xn, w_in_ref, COL_ZA))

    x_b = _in_proj(xn, w_in_ref, COL_XB)
    groups = []
    for g, w in enumerate(POOL_WINDOWS):
        lo, hi = g * POOL_GROUP, (g + 1) * POOL_GROUP
        cur = x_b[:, lo:hi]
        win = cur
        for k in range(1, w):
            win = win + state_ref[POOL_BUF - k, :, lo:hi]
        groups.append(win / float(min(pos0 + 1, w)) - cur)
    pooled = jnp.concatenate(groups, axis=1)
    pool_ref[0:POOL_BUF - 1] = state_ref[1:POOL_BUF]
    pool_ref[POOL_BUF - 1] = x_b
    y_b = _pool_mix(pooled, w_pool_ref, vecs_ref) * _silu(_in_proj(xn, w_in_ref, COL_ZB))

    y_ref[...] = _merge_and_embed(x, p_ref[...], xn, y_a, y_b, w_in_ref, w_pa_ref, w_pb_ref,
                                  w_out_ref, w_ple_ref, w_pg_ref, vecs_ref)


def _pack_kernel(*refs, tril_period):
    n = len(refs) // 2
    for src, dst in zip(refs[:n], refs[n:]):
        w = src[...]
        if tril_period is not None:
            row = lax.broadcasted_iota(jnp.int32, w.shape, 0) % tril_period
            w = jnp.where(row >= lax.broadcasted_iota(jnp.int32, w.shape, 1), w, 0.0)
        dst[...] = pltpu.bitcast(w.astype(_BF16), jnp.uint32)


def _pack_bf16(mats, block_rows, tril_period=None):
    k, n = mats[0].shape
    assert k % block_rows == 0 and block_rows % (2 * BF16_ROWS) == 0
    assert tril_period is None or block_rows % tril_period == 0
    return pl.pallas_call(
        functools.partial(_pack_kernel, tril_period=tril_period),
        grid=(k // block_rows,),
        in_specs=[pl.BlockSpec((block_rows, n), lambda i: (i, 0))] * len(mats),
        out_specs=[pl.BlockSpec((block_rows // 2, n), lambda i: (i, 0))] * len(mats),
        out_shape=[jax.ShapeDtypeStruct((k // 2, n), jnp.uint32)] * len(mats),
        compiler_params=pltpu.CompilerParams(dimension_semantics=("arbitrary",)),
        name="pack_bf16",
    )(*mats)


def _resident(shape):
    zeros = (0,) * len(shape)
    return pl.BlockSpec(shape, lambda *_: zeros, pipeline_mode=pl.Buffered(1))


def _prompt_layer(h, p, weights, tile_m):
    batch, seq, _ = h.shape
    assert seq % tile_m == 0 and tile_m % CHUNK == 0 and seq >= POOL_BUF
    tiles_per_prompt = seq // tile_m
    n_tiles = batch * tiles_per_prompt

    def front_tile(s):
        t = jnp.minimum(s, n_tiles - 1)
        return t // tiles_per_prompt, t % tiles_per_prompt, 0

    def back_tile(s):
        t = jnp.maximum(s - 1, 0)
        return t // tiles_per_prompt, t % tiles_per_prompt, 0

    per_prompt = lambda s: (jnp.minimum(s, n_tiles - 1) // tiles_per_prompt, 0, 0)
    return pl.pallas_call(
        functools.partial(_prompt_kernel, tile_m=tile_m, tiles_per_prompt=tiles_per_prompt,
                          n_tiles=n_tiles),
        grid=(n_tiles + 1,),
        in_specs=[pl.BlockSpec((1, tile_m, D_MODEL), front_tile),
                  pl.BlockSpec((1, tile_m, D_MODEL), back_tile),
                  pl.BlockSpec((1, tile_m, PLE_DIM), back_tile)]
                 + [_resident(w.shape) for w in weights],
        out_specs=[pl.BlockSpec((1, tile_m, D_MODEL), back_tile),
                   pl.BlockSpec((1, CHUNK, D_MODEL), per_prompt),
                   pl.BlockSpec((1, POOL_BUF, D_MODEL), per_prompt)],
        out_shape=[jax.ShapeDtypeStruct((batch, seq, D_MODEL), _F32),
                   jax.ShapeDtypeStruct((batch, CHUNK, D_MODEL), _F32),
                   jax.ShapeDtypeStruct((batch, POOL_BUF, D_MODEL), _F32)],
        scratch_shapes=[pltpu.VMEM((N_LANE_BLOCKS, TAIL + tile_m, LANES), _F32),
                        pltpu.VMEM((tile_m, D_MODEL), _BF16),
                        pltpu.VMEM((tile_m, D_MODEL), _BF16),
                        pltpu.VMEM((tile_m, D_MODEL), _BF16)],
        compiler_params=pltpu.CompilerParams(
            dimension_semantics=("arbitrary",),
            vmem_limit_bytes=VMEM_LIMIT_BYTES),
        name="prompt_layer",
    )(h, h, p, *weights)


def _sample_layer(h, p, state, taps, weights, pos0):
    rows = h.shape[0]
    return pl.pallas_call(
        functools.partial(_sample_kernel, pos0=pos0),
        out_shape=[jax.ShapeDtypeStruct((rows, D_MODEL), _F32),
                   jax.ShapeDtypeStruct((rows, D_MODEL), _F32),
                   jax.ShapeDtypeStruct((POOL_BUF, rows, D_MODEL), _F32)],
        compiler_params=pltpu.CompilerParams(vmem_limit_bytes=VMEM_LIMIT_BYTES),
        name="sample_layer",
    )(h, p, state, taps, *weights)


def kernel(x_prompt, x_sample, state_pool, p_prompt, p_sample, pre_g, w_in, ln_g, ln_b, w_s, b_s,
           w_pool, pool_scale, w_pa, w_pb, w_out, post_g, w_ple, w_pg, ple_in_g, ple_out_g):
    depth = w_in.shape[0]
    dec_batch, dec_seq, _ = x_sample.shape
    assert dec_seq == 1 and x_prompt.shape[2] == D_MODEL
    hp = x_prompt
    hs = x_sample.reshape(dec_batch, D_MODEL)
    pv, pp, sv, sp = [], [], [], []
    for i in range(depth):
        vecs = jnp.stack([pre_g[i], ln_g[i], ln_b[i], pool_scale[i], post_g[i], ple_in_g[i],
                          ple_out_g[i]] + [jnp.zeros_like(pre_g[i])] * (N_VEC_ROWS - 7))
        (w_in_p,) = _pack_bf16([w_in[i]], PACK_ROWS_WIDE)
        w_pa_p, w_pb_p, w_out_p, w_pg_p = _pack_bf16([w_pa[i], w_pb[i], w_out[i], w_pg[i]],
                                                     PACK_ROWS)
        (w_ple_p,) = _pack_bf16([w_ple[i]], PLE_DIM)
        (w_pool_p,) = _pack_bf16([w_pool[i].reshape(D_MODEL, POOL_GROUP)], D_MODEL)
        (w_s_p,) = _pack_bf16([w_s[i].reshape(D_MODEL, CHUNK)], D_MODEL, tril_period=CHUNK)
        weights = (w_in_p, w_s_p, b_s[i].T, w_pool_p, w_pa_p, w_pb_p, w_out_p, w_ple_p, w_pg_p,
                   vecs)
        hp, v, pool = _prompt_layer(hp, p_prompt[i], weights, TILE_M)
        pv.append(v)
        pp.append(pool)
        hs, v, pool = _sample_layer(hs, p_sample[i].reshape(dec_batch, PLE_DIM),
                                    jnp.swapaxes(state_pool[i], 0, 1), w_s[i][:, 0, :], weights,
                                    PAST_LEN)
        sv.append(v.reshape(dec_batch, 1, D_MODEL))
        sp.append(jnp.swapaxes(pool, 0, 1))
    return (hp, hs.reshape(dec_batch, 1, D_MODEL), jnp.stack(pv), jnp.stack(pp), jnp.stack(sv),
            jnp.stack(sp))
```

```python
import functools

import jax
import jax.numpy as jnp
from jax import lax
from jax.experimental import pallas as pl
from jax.experimental.pallas import tpu as pltpu

D_MODEL = 1024
CHUNK = 128
N_HEADS = D_MODEL // CHUNK
POOL_WINDOWS = (2, 4, 8, 16)
POOL_GROUP = D_MODEL // len(POOL_WINDOWS)
POOL_BUF = max(POOL_WINDOWS) - 1
PLE_DIM = 256
PAST_LEN = 16384
EPS = 1e-6
COL_U, COL_V, COL_ZA, COL_XB, COL_ZB, COL_GA, COL_GB = range(7)
VEC_PRE_G, VEC_LN_G, VEC_LN_B, VEC_POOL_SCALE, VEC_POST_G, VEC_PLE_IN_G, VEC_PLE_OUT_G = range(7)
N_VEC_ROWS = 8

LANES = 128
N_LANE_BLOCKS = D_MODEL // LANES
BF16_ROWS = 16
TAIL = 16
PACK_ROWS = 256
PACK_ROWS_WIDE = 128
TILE_M = 512
VMEM_LIMIT_BYTES = 56 * 1024 * 1024

_BF16 = jnp.bfloat16
_F32 = jnp.float32
_GELU_C = 0.7978845608028654


def _dot(a, b):
    return jnp.dot(a, b, preferred_element_type=_F32)


def _rmsnorm(x, g):
    return x * lax.rsqrt(jnp.mean(x * x, axis=-1, keepdims=True) + EPS) * g


def _layernorm(x, g, b):
    xc = x - jnp.mean(x, axis=-1, keepdims=True)
    return xc * lax.rsqrt(jnp.mean(xc * xc, axis=-1, keepdims=True) + EPS) * g + b


def _gelu_tanh(x):
    return 0.5 * x * (1.0 + jnp.tanh(_GELU_C * (x + 0.044715 * (x * x * x))))


def _sigmoid(x):
    return 0.5 * jnp.tanh(0.5 * x) + 0.5


def _silu(x):
    return x * _sigmoid(x)


def _unpack(ref, rows=slice(None), cols=slice(None)):
    return pltpu.bitcast(ref[rows, cols], _BF16)


def _vec(vecs_ref, row):
    return vecs_ref[row:row + 1, :]


def _in_proj(xn, w_in_ref, col):
    return _dot(xn, _unpack(w_in_ref, cols=slice(col * D_MODEL, (col + 1) * D_MODEL)))


def _pool_mix(pooled, w_pool_ref, vecs_ref):
    parts = []
    for g in range(len(POOL_WINDOWS)):
        q = pooled[:, g * POOL_GROUP:(g + 1) * POOL_GROUP].astype(_BF16)
        rows = slice(g * POOL_GROUP // 2, (g + 1) * POOL_GROUP // 2)
        parts.append(_dot(q, _unpack(w_pool_ref, rows=rows)))
    return jnp.concatenate(parts, axis=1) * _vec(vecs_ref, VEC_POOL_SCALE)


def _merge_and_embed(h, p, xn, y_a, y_b, w_in_ref, w_pa_ref, w_pb_ref, w_out_ref, w_ple_ref,
                     w_pg_ref, vecs_ref):
    g_a = _in_proj(xn, w_in_ref, COL_GA)
    g_b = _in_proj(xn, w_in_ref, COL_GB)
    m = (_sigmoid(g_a) * _dot(y_a.astype(_BF16), _unpack(w_pa_ref))
         + _sigmoid(g_b) * _dot(y_b.astype(_BF16), _unpack(w_pb_ref)))
    h = h + _rmsnorm(_dot(m.astype(_BF16), _unpack(w_out_ref)), _vec(vecs_ref, VEC_POST_G))
    e = _dot(p.astype(_BF16), _unpack(w_ple_ref))
    hn = _rmsnorm(h, _vec(vecs_ref, VEC_PLE_IN_G)).astype(_BF16)
    gate = _sigmoid(_dot(hn, _unpack(w_pg_ref)))
    return h + _rmsnorm(gate * e, _vec(vecs_ref, VEC_PLE_OUT_G))


def _spatial_mix(vn_b, w_s_ref, bs_t_ref, tile_m):
    w_tril = [_unpack(w_s_ref, rows=slice(hd * CHUNK // 2, (hd + 1) * CHUNK // 2))
              for hd in range(N_HEADS)]
    bias = [bs_t_ref[:, hd:hd + 1] for hd in range(N_HEADS)]
    s_rows = []
    for c in range(tile_m // CHUNK):
        heads = []
        for hd in range(N_HEADS):
            blk = vn_b[c * CHUNK:(c + 1) * CHUNK, hd * CHUNK:(hd + 1) * CHUNK]
            heads.append(_dot(w_tril[hd], blk) + bias[hd])
        s_rows.append(jnp.concatenate(heads, axis=1))
    return jnp.concatenate(s_rows, axis=0)


def _window_pool(x_b, ext_ref, pool_ref, j, tile_m):
    for c in range(N_LANE_BLOCKS):
        ext_ref[c, TAIL:TAIL + tile_m, :] = x_b[:, c * LANES:(c + 1) * LANES]
    pos = j * tile_m + lax.broadcasted_iota(jnp.int32, (tile_m, 1), 0)
    blocks = []
    for c in range(N_LANE_BLOCKS):
        w = POOL_WINDOWS[c * LANES // POOL_GROUP]
        cur = x_b[:, c * LANES:(c + 1) * LANES]
        win = cur
        for k in range(1, w):
            win = win + ext_ref[c, TAIL - k:TAIL - k + tile_m, :]
        cnt = jnp.minimum(pos + 1, w).astype(_F32)
        blocks.append(win / cnt - cur)
    for c in range(N_LANE_BLOCKS):
        pool_ref[0, :, c * LANES:(c + 1) * LANES] = (
            ext_ref[c, TAIL + tile_m - POOL_BUF:TAIL + tile_m, :])
        ext_ref[c, 0:TAIL, :] = ext_ref[c, tile_m:tile_m + TAIL, :]
    return jnp.concatenate(blocks, axis=1)


def _prompt_kernel(x_ref, x_lag_ref, p_lag_ref, w_in_ref, w_s_ref, bs_t_ref, w_pool_ref, w_pa_ref,
                   w_pb_ref, w_out_ref, w_ple_ref, w_pg_ref, vecs_ref,
                   y_ref, vn_ref, pool_ref, ext_ref, xn_c, ya_c, yb_c, *, tile_m, tiles_per_prompt,
                   n_tiles):
    s_id = pl.program_id(0)
    j = lax.rem(jnp.minimum(s_id, n_tiles - 1), tiles_per_prompt)
    vec = functools.partial(_vec, vecs_ref)
    proj = functools.partial(_in_proj, w_in_ref=w_in_ref)

    @pl.when(j == 0)
    def _():
        ext_ref[:, 0:TAIL, :] = jnp.zeros((N_LANE_BLOCKS, TAIL, LANES), _F32)

    @pl.when(s_id == 0)
    def _front_only():
        xn = _rmsnorm(x_ref[0], vec(VEC_PRE_G)).astype(_BF16)
        xn_c[...] = xn
        vn = _layernorm(_gelu_tanh(proj(xn, col=COL_V)), vec(VEC_LN_G), vec(VEC_LN_B))
        vn_ref[0] = vn[tile_m - CHUNK:, :]
        s = _spatial_mix(vn.astype(_BF16), w_s_ref, bs_t_ref, tile_m)
        ya_c[...] = (_gelu_tanh(proj(xn, col=COL_U)) * s
                     * _silu(proj(xn, col=COL_ZA))).astype(_BF16)
        pooled = _window_pool(proj(xn, col=COL_XB), ext_ref, pool_ref, j, tile_m)
        yb_c[...] = (_pool_mix(pooled, w_pool_ref, vecs_ref)
                     * _silu(proj(xn, col=COL_ZB))).astype(_BF16)

    @pl.when(s_id == n_tiles)
    def _back_only():
        y_ref[0] = _merge_and_embed(x_lag_ref[0], p_lag_ref[0], xn_c[...], ya_c[...], yb_c[...],
                                    w_in_ref, w_pa_ref, w_pb_ref, w_out_ref, w_ple_ref, w_pg_ref,
                                    vecs_ref)

    @pl.when(jnp.logical_and(s_id > 0, s_id < n_tiles))
    def _both():
        xn_prev = xn_c[...]
        g_a = proj(xn_prev, col=COL_GA)
        xn = _rmsnorm(x_ref[0], vec(VEC_PRE_G)).astype(_BF16)
        xn_c[...] = xn
        v_raw = proj(xn, col=COL_V)
        g_b = proj(xn_prev, col=COL_GB)
        vn = _layernorm(_gelu_tanh(v_raw), vec(VEC_LN_G), vec(VEC_LN_B))
        vn_ref[0] = vn[tile_m - CHUNK:, :]
        u_raw = proj(xn, col=COL_U)
        s = _spatial_mix(vn.astype(_BF16), w_s_ref, bs_t_ref, tile_m)
        proj_a = _dot(ya_c[...], _unpack(w_pa_ref))
        proj_b = _dot(yb_c[...], _unpack(w_pb_ref))
        za_raw = proj(xn, col=COL_ZA)
        m = (_sigmoid(g_a) * proj_a + _sigmoid(g_b) * proj_b).astype(_BF16)
        o = _dot(m, _unpack(w_out_ref))
        x_b = proj(xn, col=COL_XB)
        ya_c[...] = (_gelu_tanh(u_raw) * s * _silu(za_raw)).astype(_BF16)
        zb_raw = proj(xn, col=COL_ZB)
        pooled = _window_pool(x_b, ext_ref, pool_ref, j, tile_m)
        mixed = _pool_mix(pooled, w_pool_ref, vecs_ref)
        h = x_lag_ref[0] + _rmsnorm(o, vec(VEC_POST_G))
        hn = _rmsnorm(h, vec(VEC_PLE_IN_G)).astype(_BF16)
        e = _dot(p_lag_ref[0].astype(_BF16), _unpack(w_ple_ref))
        gate_raw = _dot(hn, _unpack(w_pg_ref))
        yb_c[...] = (mixed * _silu(zb_raw)).astype(_BF16)
        y_ref[0] = h + _rmsnorm(_sigmoid(gate_raw) * e, vec(VEC_PLE_OUT_G))


def _sample_kernel(x_ref, p_ref, state_ref, taps_ref, w_in_ref, w_s_ref, bs_t_ref, w_pool_ref,
                   w_pa_ref, w_pb_ref, w_out_ref, w_ple_ref, w_pg_ref, vecs_ref,
                   y_ref, vn_ref, pool_ref, *, pos0):
    x = x_ref[...]
    xn = _rmsnorm(x, _vec(vecs_ref, VEC_PRE_G)).astype(_BF16)

    u = _gelu_tanh(_in_proj(xn, w_in_ref, COL_U))
    vn = _layernorm(_gelu_tanh(_in_proj(xn, w_in_ref, COL_V)),
                    _vec(vecs_ref, VEC_LN_G), _vec(vecs_ref, VEC_LN_B))
    vn_ref[...] = vn
    heads = []
    for hd in range(N_HEADS):
        tap = taps_ref[hd:hd + 1, 0:1]
        heads.append(vn[:, hd * CHUNK:(hd + 1) * CHUNK] * tap + bs_t_ref[0:1, hd:hd + 1])
    s = jnp.concatenate(heads, axis=1)
    y_a = u * s * _silu(_in_proj(xn, w_in_ref, COL_ZA))

    x_b = _in_proj(xn, w_in_ref, COL_XB)
    groups = []
    for g, w in enumerate(POOL_WINDOWS):
        lo, hi = g * POOL_GROUP, (g + 1) * POOL_GROUP
        cur = x_b[:, lo:hi]
        win = cur
        for k in range(1, w):
            win = win + state_ref[POOL_BUF - k, :, lo:hi]
        groups.append(win / float(min(pos0 + 1, w)) - cur)
    pooled = jnp.concatenate(groups, axis=1)
    pool_ref[0:POOL_BUF - 1] = state_ref[1:POOL_BUF]
    pool_ref[POOL_BUF - 1] = x_b
    y_b = _pool_mix(pooled, w_pool_ref, vecs_ref) * _silu(_in_proj(xn, w_in_ref, COL_ZB))

    y_ref[...] = _merge_and_embed(x, p_ref[...], xn, y_a, y_b, w_in_ref, w_pa_ref, w_pb_ref,
                                  w_out_ref, w_ple_ref, w_pg_ref, vecs_ref)


def _pack_kernel(*refs, tril_period):
    n = len(refs) // 2
    for src, dst in zip(refs[:n], refs[n:]):
        w = src[...]
        if tril_period is not None:
            row = lax.broadcasted_iota(jnp.int32, w.shape, 0) % tril_period
            w = jnp.where(row >= lax.broadcasted_iota(jnp.int32, w.shape, 1), w, 0.0)
        dst[...] = pltpu.bitcast(w.astype(_BF16), jnp.uint32)


def _pack_bf16(mats, block_rows, tril_period=None):
    k, n = mats[0].shape
    assert k % block_rows == 0 and block_rows % (2 * BF16_ROWS) == 0
    assert tril_period is None or block_rows % tril_period == 0
    return pl.pallas_call(
        functools.partial(_pack_kernel, tril_period=tril_period),
        grid=(k // block_rows,),
        in_specs=[pl.BlockSpec((block_rows, n), lambda i: (i, 0))] * len(mats),
        out_specs=[pl.BlockSpec((block_rows // 2, n), lambda i: (i, 0))] * len(mats),
        out_shape=[jax.ShapeDtypeStruct((k // 2, n), jnp.uint32)] * len(mats),
        compiler_params=pltpu.CompilerParams(dimension_semantics=("arbitrary",)),
        name="pack_bf16",
    )(*mats)


def _resident(shape):
    zeros = (0,) * len(shape)
    return pl.BlockSpec(shape, lambda *_: zeros, pipeline_mode=pl.Buffered(1))


def _prompt_layer(h, p, weights, tile_m):
    batch, seq, _ = h.shape
    assert seq % tile_m == 0 and tile_m % CHUNK == 0 and seq >= POOL_BUF
    tiles_per_prompt = seq // tile_m
    n_tiles = batch * tiles_per_prompt

    def front_tile(s):
        t = jnp.minimum(s, n_tiles - 1)
        return t // tiles_per_prompt, t % tiles_per_prompt, 0

    def back_tile(s):
        t = jnp.maximum(s - 1, 0)
        return t // tiles_per_prompt, t % tiles_per_prompt, 0

    per_prompt = lambda s: (jnp.minimum(s, n_tiles - 1) // tiles_per_prompt, 0, 0)
    return pl.pallas_call(
        functools.partial(_prompt_kernel, tile_m=tile_m, tiles_per_prompt=tiles_per_prompt,
                          n_tiles=n_tiles),
        grid=(n_tiles + 1,),
        in_specs=[pl.BlockSpec((1, tile_m, D_MODEL), front_tile),
                  pl.BlockSpec((1, tile_m, D_MODEL), back_tile),
                  pl.BlockSpec((1, tile_m, PLE_DIM), back_tile)]
                 + [_resident(w.shape) for w in weights],
        out_specs=[pl.BlockSpec((1, tile_m, D_MODEL), back_tile),
                   pl.BlockSpec((1, CHUNK, D_MODEL), per_prompt),
                   pl.BlockSpec((1, POOL_BUF, D_MODEL), per_prompt)],
        out_shape=[jax.ShapeDtypeStruct((batch, seq, D_MODEL), _F32),
                   jax.ShapeDtypeStruct((batch, CHUNK, D_MODEL), _F32),
                   jax.ShapeDtypeStruct((batch, POOL_BUF, D_MODEL), _F32)],
        scratch_shapes=[pltpu.VMEM((N_LANE_BLOCKS, TAIL + tile_m, LANES), _F32),
                        pltpu.VMEM((tile_m, D_MODEL), _BF16),
                        pltpu.VMEM((tile_m, D_MODEL), _BF16),
                        pltpu.VMEM((tile_m, D_MODEL), _BF16)],
        compiler_params=pltpu.CompilerParams(
            dimension_semantics=("arbitrary",),
            vmem_limit_bytes=VMEM_LIMIT_BYTES),
        name="prompt_layer",
    )(h, h, p, *weights)


def _sample_layer(h, p, state, taps, weights, pos0):
    rows = h.shape[0]
    return pl.pallas_call(
        functools.partial(_sample_kernel, pos0=pos0),
        out_shape=[jax.ShapeDtypeStruct((rows, D_MODEL), _F32),
                   jax.ShapeDtypeStruct((rows, D_MODEL), _F32),
                   jax.ShapeDtypeStruct((POOL_BUF, rows, D_MODEL), _F32)],
        compiler_params=pltpu.CompilerParams(vmem_limit_bytes=VMEM_LIMIT_BYTES),
        name="sample_layer",
    )(h, p, state, taps, *weights)


def kernel(x_prompt, x_sample, state_pool, p_prompt, p_sample, pre_g, w_in, ln_g, ln_b, w_s, b_s,
           w_pool, pool_scale, w_pa, w_pb, w_out, post_g, w_ple, w_pg, ple_in_g, ple_out_g):
    depth = w_in.shape[0]
    dec_batch, dec_seq, _ = x_sample.shape
    assert dec_seq == 1 and x_prompt.shape[2] == D_MODEL
    hp = x_prompt
    hs = x_sample.reshape(dec_batch, D_MODEL)
    pv, pp, sv, sp = [], [], [], []
    for i in range(depth):
        vecs = jnp.stack([pre_g[i], ln_g[i], ln_b[i], pool_scale[i], post_g[i], ple_in_g[i],
                          ple_out_g[i]] + [jnp.zeros_like(pre_g[i])] * (N_VEC_ROWS - 7))
        (w_in_p,) = _pack_bf16([w_in[i]], PACK_ROWS_WIDE)
        w_pa_p, w_pb_p, w_out_p, w_pg_p = _pack_bf16([w_pa[i], w_pb[i], w_out[i], w_pg[i]],
                                                     PACK_ROWS)
        (w_ple_p,) = _pack_bf16([w_ple[i]], PLE_DIM)
        (w_pool_p,) = _pack_bf16([w_pool[i].reshape(D_MODEL, POOL_GROUP)], D_MODEL)
        (w_s_p,) = _pack_bf16([w_s[i].reshape(D_MODEL, CHUNK)], D_MODEL, tril_period=CHUNK)
        weights = (w_in_p, w_s_p, b_s[i].T, w_pool_p, w_pa_p, w_pb_p, w_out_p, w_ple_p, w_pg_p,
                   vecs)
        hp, v, pool = _prompt_layer(hp, p_prompt[i], weights, TILE_M)
        pv.append(v)
        pp.append(pool)
        hs, v, pool = _sample_layer(hs, p_sample[i].reshape(dec_batch, PLE_DIM),
                                    jnp.swapaxes(state_pool[i], 0, 1), w_s[i][:, 0, :], weights,
                                    PAST_LEN)
        sv.append(v.reshape(dec_batch, 1, D_MODEL))
        sp.append(jnp.swapaxes(pool, 0, 1))
    return (hp, hs.reshape(dec_batch, 1, D_MODEL), jnp.stack(pv), jnp.stack(pp), jnp.stack(sv),
            jnp.stack(sp))
```

```python
import functools

import jax
import jax.numpy as jnp
from jax import lax
from jax.experimental import pallas as pl
from jax.experimental.pallas import tpu as pltpu

D_MODEL = 1024
CHUNK = 128
N_HEADS = D_MODEL // CHUNK
POOL_WINDOWS = (2, 4, 8, 16)
POOL_GROUP = D_MODEL // len(POOL_WINDOWS)
POOL_BUF = max(POOL_WINDOWS) - 1
PLE_DIM = 256
PAST_LEN = 16384
SAMPLE_ROWS = 128
EPS = 1e-6
COL_U, COL_V, COL_ZA, COL_XB, COL_ZB, COL_GA, COL_GB = range(7)
VEC_PRE_G, VEC_LN_G, VEC_LN_B, VEC_POOL_SCALE, VEC_POST_G, VEC_PLE_IN_G, VEC_PLE_OUT_G = range(7)
N_VEC_ROWS = 8

LANES = 128
N_LANE_BLOCKS = D_MODEL // LANES
BF16_ROWS = 16
TAIL = 16
PACK_STEPS = 8
TILE_M = 512
VMEM_LIMIT_BYTES = 56 * 1024 * 1024

_BF16 = jnp.bfloat16
_F32 = jnp.float32
_GELU_C = 0.7978845608028654


def _dot(a, b):
    return jnp.dot(a, b, preferred_element_type=_F32)


def _rmsnorm(x, g):
    return x * lax.rsqrt(jnp.mean(x * x, axis=-1, keepdims=True) + EPS) * g


def _layernorm(x, g, b):
    xc = x - jnp.mean(x, axis=-1, keepdims=True)
    return xc * lax.rsqrt(jnp.mean(xc * xc, axis=-1, keepdims=True) + EPS) * g + b


def _gelu_tanh(x):
    return 0.5 * x * (1.0 + jnp.tanh(_GELU_C * (x + 0.044715 * (x * x * x))))


def _sigmoid(x):
    return 0.5 * jnp.tanh(0.5 * x) + 0.5


def _silu(x):
    return x * _sigmoid(x)


def _unpack(ref, rows=slice(None), cols=slice(None)):
    return pltpu.bitcast(ref[rows, cols], _BF16)


def _vec(vecs_ref, row):
    return vecs_ref[row:row + 1, :]


def _in_proj(xn, w_in_ref, col):
    return _dot(xn, _unpack(w_in_ref, cols=slice(col * D_MODEL, (col + 1) * D_MODEL)))


def _pool_mix(pooled, w_pool_ref, vecs_ref):
    parts = []
    for g in range(len(POOL_WINDOWS)):
        q = pooled[:, g * POOL_GROUP:(g + 1) * POOL_GROUP].astype(_BF16)
        rows = slice(g * POOL_GROUP // 2, (g + 1) * POOL_GROUP // 2)
        parts.append(_dot(q, _unpack(w_pool_ref, rows=rows)))
    return jnp.concatenate(parts, axis=1) * _vec(vecs_ref, VEC_POOL_SCALE)


def _merge_and_embed(h, p, xn, y_a, y_b, w_in_ref, w_pa_ref, w_pb_ref, w_out_ref, w_ple_ref,
                     w_pg_ref, vecs_ref):
    g_a = _in_proj(xn, w_in_ref, COL_GA)
    g_b = _in_proj(xn, w_in_ref, COL_GB)
    m = (_sigmoid(g_a) * _dot(y_a.astype(_BF16), _unpack(w_pa_ref))
         + _sigmoid(g_b) * _dot(y_b.astype(_BF16), _unpack(w_pb_ref)))
    h = h + _rmsnorm(_dot(m.astype(_BF16), _unpack(w_out_ref)), _vec(vecs_ref, VEC_POST_G))
    e = _dot(p.astype(_BF16), _unpack(w_ple_ref))
    hn = _rmsnorm(h, _vec(vecs_ref, VEC_PLE_IN_G)).astype(_BF16)
    gate = _sigmoid(_dot(hn, _unpack(w_pg_ref)))
    return h + _rmsnorm(gate * e, _vec(vecs_ref, VEC_PLE_OUT_G))


def _spatial_mix(vn_b, w_s_ref, bs_t_ref, tile_m):
    w_tril = [_unpack(w_s_ref, rows=slice(hd * CHUNK // 2, (hd + 1) * CHUNK // 2))
              for hd in range(N_HEADS)]
    bias = [bs_t_ref[:, hd:hd + 1] for hd in range(N_HEADS)]
    s_rows = []
    for c in range(tile_m // CHUNK):
        heads = []
        for hd in range(N_HEADS):
            blk = vn_b[c * CHUNK:(c + 1) * CHUNK, hd * CHUNK:(hd + 1) * CHUNK]
            heads.append(_dot(w_tril[hd], blk) + bias[hd])
        s_rows.append(jnp.concatenate(heads, axis=1))
    return jnp.concatenate(s_rows, axis=0)


def _window_pool(x_b, ext_ref, pool_ref, j, tile_m):
    for c in range(N_LANE_BLOCKS):
        ext_ref[c, TAIL:TAIL + tile_m, :] = x_b[:, c * LANES:(c + 1) * LANES]
    pos = j * tile_m + lax.broadcasted_iota(jnp.int32, (tile_m, 1), 0)
    blocks = []
    for c in range(N_LANE_BLOCKS):
        w = POOL_WINDOWS[c * LANES // POOL_GROUP]
        cur = x_b[:, c * LANES:(c + 1) * LANES]
        win = cur
        for k in range(1, w):
            win = win + ext_ref[c, TAIL - k:TAIL - k + tile_m, :]
        cnt = jnp.minimum(pos + 1, w).astype(_F32)
        blocks.append(win / cnt - cur)
    for c in range(N_LANE_BLOCKS):
        pool_ref[0, :, c * LANES:(c + 1) * LANES] = (
            ext_ref[c, TAIL + tile_m - POOL_BUF:TAIL + tile_m, :])
        ext_ref[c, 0:TAIL, :] = ext_ref[c, tile_m:tile_m + TAIL, :]
    return jnp.concatenate(blocks, axis=1)


def _prompt_kernel(x_ref, x_lag_ref, p_lag_ref, w_in_ref, w_s_ref, bs_t_ref, w_pool_ref, w_pa_ref,
                   w_pb_ref, w_out_ref, w_ple_ref, w_pg_ref, vecs_ref,
                   y_ref, vn_ref, pool_ref, ext_ref, xn_c, ya_c, yb_c, *, tile_m, tiles_per_prompt,
                   n_tiles):
    s_id = pl.program_id(0)
    j = lax.rem(jnp.minimum(s_id, n_tiles - 1), tiles_per_prompt)
    vec = functools.partial(_vec, vecs_ref)
    proj = functools.partial(_in_proj, w_in_ref=w_in_ref)

    @pl.when(j == 0)
    def _():
        ext_ref[:, 0:TAIL, :] = jnp.zeros((N_LANE_BLOCKS, TAIL, LANES), _F32)

    @pl.when(s_id == 0)
    def _front_only():
        xn = _rmsnorm(x_ref[0], vec(VEC_PRE_G)).astype(_BF16)
        xn_c[...] = xn
        vn = _layernorm(_gelu_tanh(proj(xn, col=COL_V)), vec(VEC_LN_G), vec(VEC_LN_B))
        vn_ref[0] = vn[tile_m - CHUNK:, :]
        s = _spatial_mix(vn.astype(_BF16), w_s_ref, bs_t_ref, tile_m)
        ya_c[...] = (_gelu_tanh(proj(xn, col=COL_U)) * s
                     * _silu(proj(xn, col=COL_ZA))).astype(_BF16)
        pooled = _window_pool(proj(xn, col=COL_XB), ext_ref, pool_ref, j, tile_m)
        yb_c[...] = (_pool_mix(pooled, w_pool_ref, vecs_ref)
                     * _silu(proj(xn, col=COL_ZB))).astype(_BF16)

    @pl.when(s_id == n_tiles)
    def _back_only():
        y_ref[0] = _merge_and_embed(x_lag_ref[0], p_lag_ref[0], xn_c[...], ya_c[...], yb_c[...],
                                    w_in_ref, w_pa_ref, w_pb_ref, w_out_ref, w_ple_ref, w_pg_ref,
                                    vecs_ref)

    @pl.when(jnp.logical_and(s_id > 0, s_id < n_tiles))
    def _both():
        xn_prev = xn_c[...]
        g_a = proj(xn_prev, col=COL_GA)
        xn = _rmsnorm(x_ref[0], vec(VEC_PRE_G)).astype(_BF16)
        xn_c[...] = xn
        v_raw = proj(xn, col=COL_V)
        g_b = proj(xn_prev, col=COL_GB)
        vn = _layernorm(_gelu_tanh(v_raw), vec(VEC_LN_G), vec(VEC_LN_B))
        vn_ref[0] = vn[tile_m - CHUNK:, :]
        u_raw = proj(xn, col=COL_U)
        s = _spatial_mix(vn.astype(_BF16), w_s_ref, bs_t_ref, tile_m)
        proj_a = _dot(ya_c[...], _unpack(w_pa_ref))
        proj_b = _dot(yb_c[...], _unpack(w_pb_ref))
        za_raw = proj(xn, col=COL_ZA)
        m = (_sigmoid(g_a) * proj_a + _sigmoid(g_b) * proj_b).astype(_BF16)
        o = _dot(m, _unpack(w_out_ref))
        ya_c[...] = (_gelu_tanh(u_raw) * s * _silu(za_raw)).astype(_BF16)
        x_b = proj(xn, col=COL_XB)
        h = x_lag_ref[0] + _rmsnorm(o, vec(VEC_POST_G))
        hn = _rmsnorm(h, vec(VEC_PLE_IN_G)).astype(_BF16)
        gate_raw = _dot(hn, _unpack(w_pg_ref))
        pooled = _window_pool(x_b, ext_ref, pool_ref, j, tile_m)
        e = _dot(p_lag_ref[0].astype(_BF16), _unpack(w_ple_ref))
        mixed = _pool_mix(pooled, w_pool_ref, vecs_ref)
        y_ref[0] = h + _rmsnorm(_sigmoid(gate_raw) * e, vec(VEC_PLE_OUT_G))
        zb_raw = proj(xn, col=COL_ZB)
        yb_c[...] = (mixed * _silu(zb_raw)).astype(_BF16)


def _gather_rows(ref):
    n = ref.shape[0] // SAMPLE_ROWS
    return jnp.concatenate([ref[pl.ds(c, SAMPLE_ROWS, stride=n), :] for c in range(n)], axis=1)


def _scatter_rows(ref, value):
    n = ref.shape[0] // SAMPLE_ROWS
    for c in range(n):
        ref[pl.ds(c, SAMPLE_ROWS, stride=n), :] = value[:, c * LANES:(c + 1) * LANES]


def _sample_kernel(x_ref, p_ref, state_ref, taps_ref, w_in_ref, w_s_ref, bs_t_ref, w_pool_ref,
                   w_pa_ref, w_pb_ref, w_out_ref, w_ple_ref, w_pg_ref, vecs_ref,
                   y_ref, vn_ref, pool_ref, *, pos0):
    x = _gather_rows(x_ref)
    xn = _rmsnorm(x, _vec(vecs_ref, VEC_PRE_G)).astype(_BF16)

    u = _gelu_tanh(_in_proj(xn, w_in_ref, COL_U))
    vn = _layernorm(_gelu_tanh(_in_proj(xn, w_in_ref, COL_V)),
                    _vec(vecs_ref, VEC_LN_G), _vec(vecs_ref, VEC_LN_B))
    _scatter_rows(vn_ref, vn)
    heads = []
    for hd in range(N_HEADS):
        tap = taps_ref[hd * CHUNK:hd * CHUNK + 1, 0:1]
        heads.append(vn[:, hd * CHUNK:(hd + 1) * CHUNK] * tap + bs_t_ref[0:1, hd:hd + 1])
    s = jnp.concatenate(heads, axis=1)
    y_a = u * s * _silu(_in_proj(xn, w_in_ref, COL_ZA))

    x_b = _in_proj(xn, w_in_ref, COL_XB)
    groups = []
    for g, w in enumerate(POOL_WINDOWS):
        lo, hi = g * POOL_GROUP, (g + 1) * POOL_GROUP
        cur = x_b[:, lo:hi]
        win = cur
        for k in range(1, w):
            win = win + state_ref[POOL_BUF - k, :, lo:hi]
        groups.append(win / float(min(pos0 + 1, w)) - cur)
    pooled = jnp.concatenate(groups, axis=1)
    pool_ref[0:POOL_BUF - 1] = state_ref[1:POOL_BUF]
    pool_ref[POOL_BUF - 1] = x_b
    y_b = _pool_mix(pooled, w_pool_ref, vecs_ref) * _silu(_in_proj(xn, w_in_ref, COL_ZB))

    _scatter_rows(y_ref, _merge_and_embed(x, _gather_rows(p_ref), xn, y_a, y_b, w_in_ref, w_pa_ref,
                                          w_pb_ref, w_out_ref, w_ple_ref, w_pg_ref, vecs_ref))


def _pack_kernel(*refs, tril_periods):
    n = len(refs) // 2
    for src, dst, period in zip(refs[:n], refs[n:], tril_periods):
        w = src[...]
        if period is not None:
            row = lax.broadcasted_iota(jnp.int32, w.shape, 0) % period
            w = jnp.where(row >= lax.broadcasted_iota(jnp.int32, w.shape, 1), w, 0.0)
        dst[...] = pltpu.bitcast(w.astype(_BF16), jnp.uint32)


def _pack_bf16(mats, tril_periods, n_steps):
    blocks = [m.shape[0] // n_steps for m in mats]
    for m, rows, period in zip(mats, blocks, tril_periods):
        assert rows * n_steps == m.shape[0] and rows % (2 * BF16_ROWS) == 0
        assert period is None or rows % period == 0
    return pl.pallas_call(
        functools.partial(_pack_kernel, tril_periods=tril_periods),
        grid=(n_steps,),
        in_specs=[pl.BlockSpec((rows, m.shape[1]), lambda i: (i, 0))
                  for m, rows in zip(mats, blocks)],
        out_specs=[pl.BlockSpec((rows // 2, m.shape[1]), lambda i: (i, 0))
                   for m, rows in zip(mats, blocks)],
        out_shape=[jax.ShapeDtypeStruct((m.shape[0] // 2, m.shape[1]), jnp.uint32) for m in mats],
        compiler_params=pltpu.CompilerParams(dimension_semantics=("arbitrary",),
                                             vmem_limit_bytes=VMEM_LIMIT_BYTES),
        name="pack_bf16",
    )(*mats)


def _resident(shape):
    zeros = (0,) * len(shape)
    return pl.BlockSpec(shape, lambda *_: zeros, pipeline_mode=pl.Buffered(1))


def _prompt_layer(h, p, weights, tile_m):
    batch, seq, _ = h.shape
    assert seq % tile_m == 0 and tile_m % CHUNK == 0 and seq >= POOL_BUF
    tiles_per_prompt = seq // tile_m
    n_tiles = batch * tiles_per_prompt

    def front_tile(s):
        t = jnp.minimum(s, n_tiles - 1)
        return t // tiles_per_prompt, t % tiles_per_prompt, 0

    def back_tile(s):
        t = jnp.maximum(s - 1, 0)
        return t // tiles_per_prompt, t % tiles_per_prompt, 0

    per_prompt = lambda s: (jnp.minimum(s, n_tiles - 1) // tiles_per_prompt, 0, 0)
    return pl.pallas_call(
        functools.partial(_prompt_kernel, tile_m=tile_m, tiles_per_prompt=tiles_per_prompt,
                          n_tiles=n_tiles),
        grid=(n_tiles + 1,),
        in_specs=[pl.BlockSpec((1, tile_m, D_MODEL), front_tile),
                  pl.BlockSpec((1, tile_m, D_MODEL), back_tile),
                  pl.BlockSpec((1, tile_m, PLE_DIM), back_tile)]
                 + [_resident(w.shape) for w in weights],
        out_specs=[pl.BlockSpec((1, tile_m, D_MODEL), back_tile),
                   pl.BlockSpec((1, CHUNK, D_MODEL), per_prompt),
                   pl.BlockSpec((1, POOL_BUF, D_MODEL), per_prompt)],
        out_shape=[jax.ShapeDtypeStruct((batch, seq, D_MODEL), _F32),
                   jax.ShapeDtypeStruct((batch, CHUNK, D_MODEL), _F32),
                   jax.ShapeDtypeStruct((batch, POOL_BUF, D_MODEL), _F32)],
        scratch_shapes=[pltpu.VMEM((N_LANE_BLOCKS, TAIL + tile_m, LANES), _F32),
                        pltpu.VMEM((tile_m, D_MODEL), _BF16),
                        pltpu.VMEM((tile_m, D_MODEL), _BF16),
                        pltpu.VMEM((tile_m, D_MODEL), _BF16)],
        compiler_params=pltpu.CompilerParams(
            dimension_semantics=("arbitrary",),
            vmem_limit_bytes=VMEM_LIMIT_BYTES),
        name="prompt_layer",
    )(h, h, p, *weights)


def _sample_layer(h, p, state, taps, weights, pos0):
    rows = state.shape[1]
    assert rows == SAMPLE_ROWS
    return pl.pallas_call(
        functools.partial(_sample_kernel, pos0=pos0),
        out_shape=[jax.ShapeDtypeStruct(h.shape, _F32),
                   jax.ShapeDtypeStruct(h.shape, _F32),
                   jax.ShapeDtypeStruct((POOL_BUF, rows, D_MODEL), _F32)],
        compiler_params=pltpu.CompilerParams(vmem_limit_bytes=VMEM_LIMIT_BYTES),
        name="sample_layer",
    )(h, p, state, taps, *weights)


def kernel(x_prompt, x_sample, state_pool, p_prompt, p_sample, pre_g, w_in, ln_g, ln_b, w_s, b_s,
           w_pool, pool_scale, w_pa, w_pb, w_out, post_g, w_ple, w_pg, ple_in_g, ple_out_g):
    depth = w_in.shape[0]
    dec_batch, dec_seq, _ = x_sample.shape
    assert dec_seq == 1 and x_prompt.shape[2] == D_MODEL
    hp = x_prompt
    hs = x_sample.reshape(dec_batch * N_LANE_BLOCKS, LANES)
    pv, pp, sv, sp = [], [], [], []
    for i in range(depth):
        vecs = jnp.stack([pre_g[i], ln_g[i], ln_b[i], pool_scale[i], post_g[i], ple_in_g[i],
                          ple_out_g[i]] + [jnp.zeros_like(pre_g[i])] * (N_VEC_ROWS - 7))
        w_s_rows = w_s[i].reshape(D_MODEL, CHUNK)
        w_in_p, w_s_p, w_pool_p, w_pa_p, w_pb_p, w_out_p, w_ple_p, w_pg_p = _pack_bf16(
            [w_in[i], w_s_rows, w_pool[i].reshape(D_MODEL, POOL_GROUP),
             w_pa[i], w_pb[i], w_out[i], w_ple[i], w_pg[i]],
            tril_periods=(None, CHUNK) + (None,) * 6, n_steps=PACK_STEPS)
        weights = (w_in_p, w_s_p, b_s[i].T, w_pool_p, w_pa_p, w_pb_p, w_out_p, w_ple_p, w_pg_p,
                   vecs)
        hp, v, pool = _prompt_layer(hp, p_prompt[i], weights, TILE_M)
        pv.append(v)
        pp.append(pool)
        hs, v, pool = _sample_layer(hs, p_sample[i].reshape(dec_batch * PLE_DIM // LANES, LANES),
                                    jnp.swapaxes(state_pool[i], 0, 1), w_s_rows, weights,
                                    PAST_LEN)
        sv.append(v.reshape(dec_batch, 1, D_MODEL))
        sp.append(jnp.swapaxes(pool, 0, 1))
    return (hp, hs.reshape(dec_batch, 1, D_MODEL), jnp.stack(pv), jnp.stack(pp), jnp.stack(sv),
            jnp.stack(sp))
```

```python
import functools

import jax
import jax.numpy as jnp
from jax import lax
from jax.experimental import pallas as pl
from jax.experimental.pallas import tpu as pltpu

D_MODEL = 1024
CHUNK = 128
N_HEADS = D_MODEL // CHUNK
POOL_WINDOWS = (2, 4, 8, 16)
POOL_GROUP = D_MODEL // len(POOL_WINDOWS)
POOL_BUF = max(POOL_WINDOWS) - 1
PLE_DIM = 256
PAST_LEN = 16384
SAMPLE_ROWS = 128
EPS = 1e-6
COL_U, COL_V, COL_ZA, COL_XB, COL_ZB, COL_GA, COL_GB = range(7)
VEC_PRE_G, VEC_LN_G, VEC_LN_B, VEC_POOL_SCALE, VEC_POST_G, VEC_PLE_IN_G, VEC_PLE_OUT_G = range(7)
N_VEC_ROWS = 8

LANES = 128
N_LANE_BLOCKS = D_MODEL // LANES
BF16_ROWS = 16
TAIL = 16
PACK_STEPS = 8
TILE_M = 256
VMEM_LIMIT_BYTES = 56 * 1024 * 1024

_BF16 = jnp.bfloat16
_F32 = jnp.float32
_GELU_C = 0.7978845608028654


def _dot(a, b):
    return jnp.dot(a, b, preferred_element_type=_F32)


def _rmsnorm(x, g):
    return x * lax.rsqrt(jnp.mean(x * x, axis=-1, keepdims=True) + EPS) * g


def _layernorm(x, g, b):
    xc = x - jnp.mean(x, axis=-1, keepdims=True)
    return xc * lax.rsqrt(jnp.mean(xc * xc, axis=-1, keepdims=True) + EPS) * g + b


def _gelu_tanh(x):
    return 0.5 * x * (1.0 + jnp.tanh(_GELU_C * (x + 0.044715 * (x * x * x))))


def _sigmoid(x):
    return 0.5 * jnp.tanh(0.5 * x) + 0.5


def _silu(x):
    return x * _sigmoid(x)


def _unpack(ref, rows=slice(None), cols=slice(None)):
    return pltpu.bitcast(ref[rows, cols], _BF16)


def _vec(vecs_ref, row):
    return vecs_ref[row:row + 1, :]


def _in_proj(xn, w_in_ref, col):
    return _dot(xn, _unpack(w_in_ref, cols=slice(col * D_MODEL, (col + 1) * D_MODEL)))


def _pool_mix(pooled, w_pool_ref, vecs_ref):
    parts = []
    for g in range(len(POOL_WINDOWS)):
        q = pooled[:, g * POOL_GROUP:(g + 1) * POOL_GROUP].astype(_BF16)
        rows = slice(g * POOL_GROUP // 2, (g + 1) * POOL_GROUP // 2)
        parts.append(_dot(q, _unpack(w_pool_ref, rows=rows)))
    return jnp.concatenate(parts, axis=1) * _vec(vecs_ref, VEC_POOL_SCALE)


def _merge_and_embed(h, p, xn, y_a, y_b, w_in_ref, w_pa_ref, w_pb_ref, w_out_ref, w_ple_ref,
                     w_pg_ref, vecs_ref):
    g_a = _in_proj(xn, w_in_ref, COL_GA)
    g_b = _in_proj(xn, w_in_ref, COL_GB)
    m = (_sigmoid(g_a) * _dot(y_a.astype(_BF16), _unpack(w_pa_ref))
         + _sigmoid(g_b) * _dot(y_b.astype(_BF16), _unpack(w_pb_ref)))
    h = h + _rmsnorm(_dot(m.astype(_BF16), _unpack(w_out_ref)), _vec(vecs_ref, VEC_POST_G))
    e = _dot(p.astype(_BF16), _unpack(w_ple_ref))
    hn = _rmsnorm(h, _vec(vecs_ref, VEC_PLE_IN_G)).astype(_BF16)
    gate = _sigmoid(_dot(hn, _unpack(w_pg_ref)))
    return h + _rmsnorm(gate * e, _vec(vecs_ref, VEC_PLE_OUT_G))


def _spatial_mix(vn_b, w_s_ref, bs_t_ref, tile_m):
    w_tril = [_unpack(w_s_ref, rows=slice(hd * CHUNK // 2, (hd + 1) * CHUNK // 2))
              for hd in range(N_HEADS)]
    bias = [bs_t_ref[:, hd:hd + 1] for hd in range(N_HEADS)]
    s_rows = []
    for c in range(tile_m // CHUNK):
        heads = []
        for hd in range(N_HEADS):
            blk = vn_b[c * CHUNK:(c + 1) * CHUNK, hd * CHUNK:(hd + 1) * CHUNK]
            heads.append(_dot(w_tril[hd], blk) + bias[hd])
        s_rows.append(jnp.concatenate(heads, axis=1))
    return jnp.concatenate(s_rows, axis=0)


def _window_pool(x_b, ext_ref, pool_ref, j, tile_m):
    for c in range(N_LANE_BLOCKS):
        ext_ref[c, TAIL:TAIL + tile_m, :] = x_b[:, c * LANES:(c + 1) * LANES]
    pos = j * tile_m + lax.broadcasted_iota(jnp.int32, (tile_m, 1), 0)
    blocks = []
    for c in range(N_LANE_BLOCKS):
        w = POOL_WINDOWS[c * LANES // POOL_GROUP]
        cur = x_b[:, c * LANES:(c + 1) * LANES]
        win = cur
        for k in range(1, w):
            win = win + ext_ref[c, TAIL - k:TAIL - k + tile_m, :]
        cnt = jnp.minimum(pos + 1, w).astype(_F32)
        blocks.append(win / cnt - cur)
    for c in range(N_LANE_BLOCKS):
        pool_ref[0, :, c * LANES:(c + 1) * LANES] = (
            ext_ref[c, TAIL + tile_m - POOL_BUF:TAIL + tile_m, :])
        ext_ref[c, 0:TAIL, :] = ext_ref[c, tile_m:tile_m + TAIL, :]
    return jnp.concatenate(blocks, axis=1)


def _prompt_kernel(x_ref, x_lag_ref, p_lag_ref, w_in_ref, w_s_ref, bs_t_ref, w_pool_ref, w_pa_ref,
                   w_pb_ref, w_out_ref, w_ple_ref, w_pg_ref, vecs_ref,
                   y_ref, vn_ref, pool_ref, ext_ref, xn_c, ya_c, yb_c, *, tile_m, tiles_per_prompt,
                   n_tiles):
    s_id = pl.program_id(0)
    j = lax.rem(jnp.minimum(s_id, n_tiles - 1), tiles_per_prompt)
    vec = functools.partial(_vec, vecs_ref)
    proj = functools.partial(_in_proj, w_in_ref=w_in_ref)

    @pl.when(j == 0)
    def _():
        ext_ref[:, 0:TAIL, :] = jnp.zeros((N_LANE_BLOCKS, TAIL, LANES), _F32)

    @pl.when(s_id == 0)
    def _front_only():
        xn = _rmsnorm(x_ref[0], vec(VEC_PRE_G)).astype(_BF16)
        xn_c[...] = xn
        vn = _layernorm(_gelu_tanh(proj(xn, col=COL_V)), vec(VEC_LN_G), vec(VEC_LN_B))
        vn_ref[0] = vn[tile_m - CHUNK:, :]
        s = _spatial_mix(vn.astype(_BF16), w_s_ref, bs_t_ref, tile_m)
        ya_c[...] = (_gelu_tanh(proj(xn, col=COL_U)) * s
                     * _silu(proj(xn, col=COL_ZA))).astype(_BF16)
        pooled = _window_pool(proj(xn, col=COL_XB), ext_ref, pool_ref, j, tile_m)
        yb_c[...] = (_pool_mix(pooled, w_pool_ref, vecs_ref)
                     * _silu(proj(xn, col=COL_ZB))).astype(_BF16)

    @pl.when(s_id == n_tiles)
    def _back_only():
        y_ref[0] = _merge_and_embed(x_lag_ref[0], p_lag_ref[0], xn_c[...], ya_c[...], yb_c[...],
                                    w_in_ref, w_pa_ref, w_pb_ref, w_out_ref, w_ple_ref, w_pg_ref,
                                    vecs_ref)

    @pl.when(jnp.logical_and(s_id > 0, s_id < n_tiles))
    def _both():
        xn_prev = xn_c[...]
        g_a = proj(xn_prev, col=COL_GA)
        xn = _rmsnorm(x_ref[0], vec(VEC_PRE_G)).astype(_BF16)
        xn_c[...] = xn
        v_raw = proj(xn, col=COL_V)
        g_b = proj(xn_prev, col=COL_GB)
        vn = _layernorm(_gelu_tanh(v_raw), vec(VEC_LN_G), vec(VEC_LN_B))
        vn_ref[0] = vn[tile_m - CHUNK:, :]
        u_raw = proj(xn, col=COL_U)
        s = _spatial_mix(vn.astype(_BF16), w_s_ref, bs_t_ref, tile_m)
        proj_a = _dot(ya_c[...], _unpack(w_pa_ref))
        proj_b = _dot(yb_c[...], _unpack(w_pb_ref))
        za_raw = proj(xn, col=COL_ZA)
        m = (_sigmoid(g_a) * proj_a + _sigmoid(g_b) * proj_b).astype(_BF16)
        o = _dot(m, _unpack(w_out_ref))
        ya_c[...] = (_gelu_tanh(u_raw) * s * _silu(za_raw)).astype(_BF16)
        x_b = proj(xn, col=COL_XB)
        h = x_lag_ref[0] + _rmsnorm(o, vec(VEC_POST_G))
        hn = _rmsnorm(h, vec(VEC_PLE_IN_G)).astype(_BF16)
        gate_raw = _dot(hn, _unpack(w_pg_ref))
        pooled = _window_pool(x_b, ext_ref, pool_ref, j, tile_m)
        e = _dot(p_lag_ref[0].astype(_BF16), _unpack(w_ple_ref))
        mixed = _pool_mix(pooled, w_pool_ref, vecs_ref)
        y_ref[0] = h + _rmsnorm(_sigmoid(gate_raw) * e, vec(VEC_PLE_OUT_G))
        zb_raw = proj(xn, col=COL_ZB)
        yb_c[...] = (mixed * _silu(zb_raw)).astype(_BF16)


def _gather_rows(ref):
    n = ref.shape[0] // SAMPLE_ROWS
    return jnp.concatenate([ref[pl.ds(c, SAMPLE_ROWS, stride=n), :] for c in range(n)], axis=1)


def _scatter_rows(ref, value):
    n = ref.shape[0] // SAMPLE_ROWS
    for c in range(n):
        ref[pl.ds(c, SAMPLE_ROWS, stride=n), :] = value[:, c * LANES:(c + 1) * LANES]


def _sample_kernel(x_ref, p_ref, state_ref, taps_ref, w_in_ref, w_s_ref, bs_t_ref, w_pool_ref,
                   w_pa_ref, w_pb_ref, w_out_ref, w_ple_ref, w_pg_ref, vecs_ref,
                   y_ref, vn_ref, pool_ref, *, pos0):
    x = _gather_rows(x_ref)
    xn = _rmsnorm(x, _vec(vecs_ref, VEC_PRE_G)).astype(_BF16)

    u = _gelu_tanh(_in_proj(xn, w_in_ref, COL_U))
    vn = _layernorm(_gelu_tanh(_in_proj(xn, w_in_ref, COL_V)),
                    _vec(vecs_ref, VEC_LN_G), _vec(vecs_ref, VEC_LN_B))
    _scatter_rows(vn_ref, vn)
    heads = []
    for hd in range(N_HEADS):
        tap = taps_ref[hd * CHUNK:hd * CHUNK + 1, 0:1]
        heads.append(vn[:, hd * CHUNK:(hd + 1) * CHUNK] * tap + bs_t_ref[0:1, hd:hd + 1])
    s = jnp.concatenate(heads, axis=1)
    y_a = u * s * _silu(_in_proj(xn, w_in_ref, COL_ZA))

    x_b = _in_proj(xn, w_in_ref, COL_XB)
    groups = []
    for g, w in enumerate(POOL_WINDOWS):
        lo, hi = g * POOL_GROUP, (g + 1) * POOL_GROUP
        cur = x_b[:, lo:hi]
        win = cur
        for k in range(1, w):
            win = win + state_ref[POOL_BUF - k, :, lo:hi]
        groups.append(win / float(min(pos0 + 1, w)) - cur)
    pooled = jnp.concatenate(groups, axis=1)
    pool_ref[0:POOL_BUF - 1] = state_ref[1:POOL_BUF]
    pool_ref[POOL_BUF - 1] = x_b
    y_b = _pool_mix(pooled, w_pool_ref, vecs_ref) * _silu(_in_proj(xn, w_in_ref, COL_ZB))

    _scatter_rows(y_ref, _merge_and_embed(x, _gather_rows(p_ref), xn, y_a, y_b, w_in_ref, w_pa_ref,
                                          w_pb_ref, w_out_ref, w_ple_ref, w_pg_ref, vecs_ref))


def _pack_kernel(*refs, tril_periods):
    n = len(refs) // 2
    for src, dst, period in zip(refs[:n], refs[n:], tril_periods):
        w = src[...]
        if period is not None:
            row = lax.broadcasted_iota(jnp.int32, w.shape, 0) % period
            w = jnp.where(row >= lax.broadcasted_iota(jnp.int32, w.shape, 1), w, 0.0)
        dst[...] = pltpu.bitcast(w.astype(_BF16), jnp.uint32)


def _pack_bf16(mats, tril_periods, n_steps):
    blocks = [m.shape[0] // n_steps for m in mats]
    for m, rows, period in zip(mats, blocks, tril_periods):
        assert rows * n_steps == m.shape[0] and rows % (2 * BF16_ROWS) == 0
        assert period is None or rows % period == 0
    return pl.pallas_call(
        functools.partial(_pack_kernel, tril_periods=tril_periods),
        grid=(n_steps,),
        in_specs=[pl.BlockSpec((rows, m.shape[1]), lambda i: (i, 0))
                  for m, rows in zip(mats, blocks)],
        out_specs=[pl.BlockSpec((rows // 2, m.shape[1]), lambda i: (i, 0))
                   for m, rows in zip(mats, blocks)],
        out_shape=[jax.ShapeDtypeStruct((m.shape[0] // 2, m.shape[1]), jnp.uint32) for m in mats],
        compiler_params=pltpu.CompilerParams(dimension_semantics=("arbitrary",),
                                             vmem_limit_bytes=VMEM_LIMIT_BYTES),
        name="pack_bf16",
    )(*mats)


def _resident(shape):
    zeros = (0,) * len(shape)
    return pl.BlockSpec(shape, lambda *_: zeros, pipeline_mode=pl.Buffered(1))


def _prompt_layer(h, p, weights, tile_m):
    batch, seq, _ = h.shape
    assert seq % tile_m == 0 and tile_m % CHUNK == 0 and seq >= POOL_BUF
    tiles_per_prompt = seq // tile_m
    n_tiles = batch * tiles_per_prompt

    def front_tile(s):
        t = jnp.minimum(s, n_tiles - 1)
        return t // tiles_per_prompt, t % tiles_per_prompt, 0

    def back_tile(s):
        t = jnp.maximum(s - 1, 0)
        return t // tiles_per_prompt, t % tiles_per_prompt, 0

    per_prompt = lambda s: (jnp.minimum(s, n_tiles - 1) // tiles_per_prompt, 0, 0)
    return pl.pallas_call(
        functools.partial(_prompt_kernel, tile_m=tile_m, tiles_per_prompt=tiles_per_prompt,
                          n_tiles=n_tiles),
        grid=(n_tiles + 1,),
        in_specs=[pl.BlockSpec((1, tile_m, D_MODEL), front_tile),
                  pl.BlockSpec((1, tile_m, D_MODEL), back_tile),
                  pl.BlockSpec((1, tile_m, PLE_DIM), back_tile)]
                 + [_resident(w.shape) for w in weights],
        out_specs=[pl.BlockSpec((1, tile_m, D_MODEL), back_tile),
                   pl.BlockSpec((1, CHUNK, D_MODEL), per_prompt),
                   pl.BlockSpec((1, POOL_BUF, D_MODEL), per_prompt)],
        out_shape=[jax.ShapeDtypeStruct((batch, seq, D_MODEL), _F32),
                   jax.ShapeDtypeStruct((batch, CHUNK, D_MODEL), _F32),
                   jax.ShapeDtypeStruct((batch, POOL_BUF, D_MODEL), _F32)],
        scratch_shapes=[pltpu.VMEM((N_LANE_BLOCKS, TAIL + tile_m, LANES), _F32),
                        pltpu.VMEM((tile_m, D_MODEL), _BF16),
                        pltpu.VMEM((tile_m, D_MODEL), _BF16),
                        pltpu.VMEM((tile_m, D_MODEL), _BF16)],
        compiler_params=pltpu.CompilerParams(
            dimension_semantics=("arbitrary",),
            vmem_limit_bytes=VMEM_LIMIT_BYTES),
        name="prompt_layer",
    )(h, h, p, *weights)


def _sample_layer(h, p, state, taps, weights, pos0):
    rows = state.shape[1]
    assert rows == SAMPLE_ROWS
    return pl.pallas_call(
        functools.partial(_sample_kernel, pos0=pos0),
        out_shape=[jax.ShapeDtypeStruct(h.shape, _F32),
                   jax.ShapeDtypeStruct(h.shape, _F32),
                   jax.ShapeDtypeStruct((POOL_BUF, rows, D_MODEL), _F32)],
        compiler_params=pltpu.CompilerParams(vmem_limit_bytes=VMEM_LIMIT_BYTES),
        name="sample_layer",
    )(h, p, state, taps, *weights)


def kernel(x_prompt, x_sample, state_pool, p_prompt, p_sample, pre_g, w_in, ln_g, ln_b, w_s, b_s,
           w_pool, pool_scale, w_pa, w_pb, w_out, post_g, w_ple, w_pg, ple_in_g, ple_out_g):
    depth = w_in.shape[0]
    dec_batch, dec_seq, _ = x_sample.shape
    assert dec_seq == 1 and x_prompt.shape[2] == D_MODEL
    hp = x_prompt
    hs = x_sample.reshape(dec_batch * N_LANE_BLOCKS, LANES)
    pv, pp, sv, sp = [], [], [], []
    for i in range(depth):
        vecs = jnp.stack([pre_g[i], ln_g[i], ln_b[i], pool_scale[i], post_g[i], ple_in_g[i],
                          ple_out_g[i]] + [jnp.zeros_like(pre_g[i])] * (N_VEC_ROWS - 7))
        w_s_rows = w_s[i].reshape(D_MODEL, CHUNK)
        w_in_p, w_s_p, w_pool_p, w_pa_p, w_pb_p, w_out_p, w_ple_p, w_pg_p = _pack_bf16(
            [w_in[i], w_s_rows, w_pool[i].reshape(D_MODEL, POOL_GROUP),
             w_pa[i], w_pb[i], w_out[i], w_ple[i], w_pg[i]],
            tril_periods=(None, CHUNK) + (None,) * 6, n_steps=PACK_STEPS)
        weights = (w_in_p, w_s_p, b_s[i].T, w_pool_p, w_pa_p, w_pb_p, w_out_p, w_ple_p, w_pg_p,
                   vecs)
        hp, v, pool = _prompt_layer(hp, p_prompt[i], weights, TILE_M)
        pv.append(v)
        pp.append(pool)
        hs, v, pool = _sample_layer(hs, p_sample[i].reshape(dec_batch * PLE_DIM // LANES, LANES),
                                    jnp.swapaxes(state_pool[i], 0, 1), w_s_rows, weights,
                                    PAST_LEN)
        sv.append(v.reshape(dec_batch, 1, D_MODEL))
        sp.append(jnp.swapaxes(pool, 0, 1))
    return (hp, hs.reshape(dec_batch, 1, D_MODEL), jnp.stack(pv), jnp.stack(pp), jnp.stack(sv),
            jnp.stack(sp))
```

```python
import functools

import jax
import jax.numpy as jnp
from jax import lax
from jax.experimental import pallas as pl
from jax.experimental.pallas import tpu as pltpu

D_MODEL = 1024
CHUNK = 128
N_HEADS = D_MODEL // CHUNK
POOL_WINDOWS = (2, 4, 8, 16)
POOL_GROUP = D_MODEL // len(POOL_WINDOWS)
POOL_BUF = max(POOL_WINDOWS) - 1
PLE_DIM = 256
PAST_LEN = 16384
SAMPLE_ROWS = 128
EPS = 1e-6
COL_U, COL_V, COL_ZA, COL_XB, COL_ZB, COL_GA, COL_GB = range(7)
VEC_PRE_G, VEC_LN_G, VEC_LN_B, VEC_POOL_SCALE, VEC_POST_G, VEC_PLE_IN_G, VEC_PLE_OUT_G = range(7)
N_VEC_ROWS = 8

LANES = 128
N_LANE_BLOCKS = D_MODEL // LANES
BF16_ROWS = 16
TAIL = 16
PACK_STEPS = 8
TILE_M = 256
VMEM_LIMIT_BYTES = 56 * 1024 * 1024

_BF16 = jnp.bfloat16
_F32 = jnp.float32
_GELU_C = 0.7978845608028654


def _dot(a, b):
    return jnp.dot(a, b, preferred_element_type=_F32)


def _rmsnorm(x, g):
    return x * lax.rsqrt(jnp.mean(x * x, axis=-1, keepdims=True) + EPS) * g


def _layernorm(x, g, b):
    xc = x - jnp.mean(x, axis=-1, keepdims=True)
    return xc * lax.rsqrt(jnp.mean(xc * xc, axis=-1, keepdims=True) + EPS) * g + b


def _gelu_tanh(x):
    return 0.5 * x * (1.0 + jnp.tanh(_GELU_C * (x + 0.044715 * (x * x * x))))


def _sigmoid(x):
    return 0.5 * jnp.tanh(0.5 * x) + 0.5


def _silu(x):
    return x * _sigmoid(x)


def _unpack(ref, rows=slice(None), cols=slice(None)):
    return pltpu.bitcast(ref[rows, cols], _BF16)


def _vec(vecs_ref, row):
    return vecs_ref[row:row + 1, :]


def _in_proj(xn, w_in_ref, col):
    return _dot(xn, _unpack(w_in_ref, cols=slice(col * D_MODEL, (col + 1) * D_MODEL)))


def _pool_mix(pooled, w_pool_ref, vecs_ref):
    parts = []
    for g in range(len(POOL_WINDOWS)):
        q = pooled[:, g * POOL_GROUP:(g + 1) * POOL_GROUP].astype(_BF16)
        rows = slice(g * POOL_GROUP // 2, (g + 1) * POOL_GROUP // 2)
        parts.append(_dot(q, _unpack(w_pool_ref, rows=rows)))
    return jnp.concatenate(parts, axis=1) * _vec(vecs_ref, VEC_POOL_SCALE)


def _merge_and_embed(h, p, xn, y_a, y_b, w_in_ref, w_pa_ref, w_pb_ref, w_out_ref, w_ple_ref,
                     w_pg_ref, vecs_ref):
    g_a = _in_proj(xn, w_in_ref, COL_GA)
    g_b = _in_proj(xn, w_in_ref, COL_GB)
    m = (_sigmoid(g_a) * _dot(y_a.astype(_BF16), _unpack(w_pa_ref))
         + _sigmoid(g_b) * _dot(y_b.astype(_BF16), _unpack(w_pb_ref)))
    h = h + _rmsnorm(_dot(m.astype(_BF16), _unpack(w_out_ref)), _vec(vecs_ref, VEC_POST_G))
    e = _dot(p.astype(_BF16), _unpack(w_ple_ref))
    hn = _rmsnorm(h, _vec(vecs_ref, VEC_PLE_IN_G)).astype(_BF16)
    gate = _sigmoid(_dot(hn, _unpack(w_pg_ref)))
    return h + _rmsnorm(gate * e, _vec(vecs_ref, VEC_PLE_OUT_G))


def _spatial_mix(vn_b, w_s_ref, bs_t_ref, tile_m):
    w_tril = [_unpack(w_s_ref, rows=slice(hd * CHUNK // 2, (hd + 1) * CHUNK // 2))
              for hd in range(N_HEADS)]
    bias = [bs_t_ref[:, hd:hd + 1] for hd in range(N_HEADS)]
    s_rows = []
    for c in range(tile_m // CHUNK):
        heads = []
        for hd in range(N_HEADS):
            blk = vn_b[c * CHUNK:(c + 1) * CHUNK, hd * CHUNK:(hd + 1) * CHUNK]
            heads.append(_dot(w_tril[hd], blk) + bias[hd])
        s_rows.append(jnp.concatenate(heads, axis=1))
    return jnp.concatenate(s_rows, axis=0)


def _window_pool(x_b, ext_ref, pool_ref, j, tile_m):
    for c in range(N_LANE_BLOCKS):
        ext_ref[c, TAIL:TAIL + tile_m, :] = x_b[:, c * LANES:(c + 1) * LANES]
    pos = j * tile_m + lax.broadcasted_iota(jnp.int32, (tile_m, 1), 0)
    blocks = []
    for c in range(N_LANE_BLOCKS):
        w = POOL_WINDOWS[c * LANES // POOL_GROUP]
        cur = x_b[:, c * LANES:(c + 1) * LANES]
        win = cur
        for k in range(1, w):
            win = win + ext_ref[c, TAIL - k:TAIL - k + tile_m, :]
        cnt = jnp.minimum(pos + 1, w).astype(_F32)
        blocks.append(win / cnt - cur)
    for c in range(N_LANE_BLOCKS):
        pool_ref[0, :, c * LANES:(c + 1) * LANES] = (
            ext_ref[c, TAIL + tile_m - POOL_BUF:TAIL + tile_m, :])
        ext_ref[c, 0:TAIL, :] = ext_ref[c, tile_m:tile_m + TAIL, :]
    return jnp.concatenate(blocks, axis=1)


def _layer_kernel(x_ref, x_lag_ref, p_lag_ref, xs_ref, ps_ref, taps_ref, state_hbm,
                  w_in_ref, w_s_ref, bs_t_ref, w_pool_ref, w_pa_ref, w_pb_ref, w_out_ref, w_ple_ref,
                  w_pg_ref, vecs_ref,
                  y_ref, vn_ref, pool_ref, ys_ref, vns_ref, spool_hbm,
                  ext_ref, xn_c, ya_c, yb_c, hist0, hist1, hist2, hist3, xbs_ref, sems,
                  *, tile_m, tiles_per_prompt, n_tiles, pos0):
    s_id = pl.program_id(0)
    j = lax.rem(jnp.minimum(s_id, n_tiles - 1), tiles_per_prompt)
    vec = functools.partial(_vec, vecs_ref)
    proj = functools.partial(_in_proj, w_in_ref=w_in_ref)

    @pl.when(j == 0)
    def _():
        ext_ref[:, 0:TAIL, :] = jnp.zeros((N_LANE_BLOCKS, TAIL, LANES), _F32)

    @pl.when(s_id == 0)
    def _front_only():
        xn = _rmsnorm(x_ref[0], vec(VEC_PRE_G)).astype(_BF16)
        xn_c[...] = xn
        vn = _layernorm(_gelu_tanh(proj(xn, col=COL_V)), vec(VEC_LN_G), vec(VEC_LN_B))
        vn_ref[0] = vn[tile_m - CHUNK:, :]
        s = _spatial_mix(vn.astype(_BF16), w_s_ref, bs_t_ref, tile_m)
        ya_c[...] = (_gelu_tanh(proj(xn, col=COL_U)) * s
                     * _silu(proj(xn, col=COL_ZA))).astype(_BF16)
        pooled = _window_pool(proj(xn, col=COL_XB), ext_ref, pool_ref, j, tile_m)
        yb_c[...] = (_pool_mix(pooled, w_pool_ref, vecs_ref)
                     * _silu(proj(xn, col=COL_ZB))).astype(_BF16)

    @pl.when(s_id == n_tiles)
    def _back_only_and_sample():
        hist_refs = (hist0, hist1, hist2, hist3)
        history, shift, newest = _sample_copies(state_hbm, spool_hbm, hist_refs, xbs_ref, sems)
        for copy in history:
            copy.start()
        shift.start()
        y_ref[0] = _merge_and_embed(x_lag_ref[0], p_lag_ref[0], xn_c[...], ya_c[...], yb_c[...],
                                    w_in_ref, w_pa_ref, w_pb_ref, w_out_ref, w_ple_ref, w_pg_ref,
                                    vecs_ref)
        _sample_rows(xs_ref, ps_ref, taps_ref, hist_refs, history, newest, xbs_ref, w_in_ref,
                     bs_t_ref, w_pool_ref, w_pa_ref, w_pb_ref, w_out_ref, w_ple_ref, w_pg_ref,
                     vecs_ref, ys_ref, vns_ref, pos0)
        shift.wait()
        newest.wait()

    @pl.when(jnp.logical_and(s_id > 0, s_id < n_tiles))
    def _both():
        xn_prev = xn_c[...]
        g_a = proj(xn_prev, col=COL_GA)
        xn = _rmsnorm(x_ref[0], vec(VEC_PRE_G)).astype(_BF16)
        xn_c[...] = xn
        v_raw = proj(xn, col=COL_V)
        g_b = proj(xn_prev, col=COL_GB)
        vn = _layernorm(_gelu_tanh(v_raw), vec(VEC_LN_G), vec(VEC_LN_B))
        vn_ref[0] = vn[tile_m - CHUNK:, :]
        u_raw = proj(xn, col=COL_U)
        s = _spatial_mix(vn.astype(_BF16), w_s_ref, bs_t_ref, tile_m)
        proj_a = _dot(ya_c[...], _unpack(w_pa_ref))
        proj_b = _dot(yb_c[...], _unpack(w_pb_ref))
        za_raw = proj(xn, col=COL_ZA)
        m = (_sigmoid(g_a) * proj_a + _sigmoid(g_b) * proj_b).astype(_BF16)
        o = _dot(m, _unpack(w_out_ref))
        ya_c[...] = (_gelu_tanh(u_raw) * s * _silu(za_raw)).astype(_BF16)
        x_b = proj(xn, col=COL_XB)
        h = x_lag_ref[0] + _rmsnorm(o, vec(VEC_POST_G))
        hn = _rmsnorm(h, vec(VEC_PLE_IN_G)).astype(_BF16)
        gate_raw = _dot(hn, _unpack(w_pg_ref))
        pooled = _window_pool(x_b, ext_ref, pool_ref, j, tile_m)
        e = _dot(p_lag_ref[0].astype(_BF16), _unpack(w_ple_ref))
        mixed = _pool_mix(pooled, w_pool_ref, vecs_ref)
        y_ref[0] = h + _rmsnorm(_sigmoid(gate_raw) * e, vec(VEC_PLE_OUT_G))
        zb_raw = proj(xn, col=COL_ZB)
        yb_c[...] = (mixed * _silu(zb_raw)).astype(_BF16)


def _gather_rows(ref):
    n = ref.shape[0] // SAMPLE_ROWS
    return jnp.concatenate([ref[pl.ds(c, SAMPLE_ROWS, stride=n), :] for c in range(n)], axis=1)


def _scatter_rows(ref, value):
    n = ref.shape[0] // SAMPLE_ROWS
    for c in range(n):
        ref[pl.ds(c, SAMPLE_ROWS, stride=n), :] = value[:, c * LANES:(c + 1) * LANES]


def _sample_copies(state_hbm, spool_hbm, hist_refs, xbs_ref, sems):
    history = []
    for g, w in enumerate(POOL_WINDOWS):
        src = state_hbm.at[pl.ds(POOL_BUF - (w - 1), w - 1), :, pl.ds(g * POOL_GROUP, POOL_GROUP)]
        history.append(pltpu.make_async_copy(src, hist_refs[g], sems.at[g]))
    n_groups = len(POOL_WINDOWS)
    shift = pltpu.make_async_copy(state_hbm.at[pl.ds(1, POOL_BUF - 1)],
                                  spool_hbm.at[pl.ds(0, POOL_BUF - 1)], sems.at[n_groups])
    newest = pltpu.make_async_copy(xbs_ref, spool_hbm.at[POOL_BUF - 1], sems.at[n_groups + 1])
    return history, shift, newest


def _sample_rows(x_ref, p_ref, taps_ref, hist_refs, history, newest, xbs_ref, w_in_ref, bs_t_ref,
                 w_pool_ref, w_pa_ref, w_pb_ref, w_out_ref, w_ple_ref, w_pg_ref, vecs_ref,
                 y_ref, vn_ref, pos0):
    x = _gather_rows(x_ref)
    xn = _rmsnorm(x, _vec(vecs_ref, VEC_PRE_G)).astype(_BF16)

    u = _gelu_tanh(_in_proj(xn, w_in_ref, COL_U))
    vn = _layernorm(_gelu_tanh(_in_proj(xn, w_in_ref, COL_V)),
                    _vec(vecs_ref, VEC_LN_G), _vec(vecs_ref, VEC_LN_B))
    _scatter_rows(vn_ref, vn)
    heads = []
    for hd in range(N_HEADS):
        tap = taps_ref[hd * CHUNK:hd * CHUNK + 1, 0:1]
        heads.append(vn[:, hd * CHUNK:(hd + 1) * CHUNK] * tap + bs_t_ref[0:1, hd:hd + 1])
    s = jnp.concatenate(heads, axis=1)
    y_a = u * s * _silu(_in_proj(xn, w_in_ref, COL_ZA))

    x_b = _in_proj(xn, w_in_ref, COL_XB)
    xbs_ref[...] = x_b
    newest.start()
    groups = []
    for g, w in enumerate(POOL_WINDOWS):
        history[g].wait()
        cur = x_b[:, g * POOL_GROUP:(g + 1) * POOL_GROUP]
        win = cur
        for r in range(w - 1):
            win = win + hist_refs[g][r]
        groups.append(win / float(min(pos0 + 1, w)) - cur)
    pooled = jnp.concatenate(groups, axis=1)
    y_b = _pool_mix(pooled, w_pool_ref, vecs_ref) * _silu(_in_proj(xn, w_in_ref, COL_ZB))

    _scatter_rows(y_ref, _merge_and_embed(x, _gather_rows(p_ref), xn, y_a, y_b, w_in_ref, w_pa_ref,
                                          w_pb_ref, w_out_ref, w_ple_ref, w_pg_ref, vecs_ref))


def _pack_kernel(*refs, tril_periods):
    n = len(refs) // 2
    for src, dst, period in zip(refs[:n], refs[n:], tril_periods):
        w = src[...]
        if period is not None:
            row = lax.broadcasted_iota(jnp.int32, w.shape, 0) % period
            w = jnp.where(row >= lax.broadcasted_iota(jnp.int32, w.shape, 1), w, 0.0)
        dst[...] = pltpu.bitcast(w.astype(_BF16), jnp.uint32)


def _pack_bf16(mats, tril_periods, n_steps):
    blocks = [m.shape[0] // n_steps for m in mats]
    for m, rows, period in zip(mats, blocks, tril_periods):
        assert rows * n_steps == m.shape[0] and rows % (2 * BF16_ROWS) == 0
        assert period is None or rows % period == 0
    return pl.pallas_call(
        functools.partial(_pack_kernel, tril_periods=tril_periods),
        grid=(n_steps,),
        in_specs=[pl.BlockSpec((rows, m.shape[1]), lambda i: (i, 0))
                  for m, rows in zip(mats, blocks)],
        out_specs=[pl.BlockSpec((rows // 2, m.shape[1]), lambda i: (i, 0))
                   for m, rows in zip(mats, blocks)],
        out_shape=[jax.ShapeDtypeStruct((m.shape[0] // 2, m.shape[1]), jnp.uint32) for m in mats],
        compiler_params=pltpu.CompilerParams(dimension_semantics=("arbitrary",),
                                             vmem_limit_bytes=VMEM_LIMIT_BYTES),
        name="pack_bf16",
    )(*mats)


def _resident(shape):
    zeros = (0,) * len(shape)
    return pl.BlockSpec(shape, lambda *_: zeros, pipeline_mode=pl.Buffered(1))


def _layer(h, p, hs, ps, state, taps, weights, tile_m, pos0):
    batch, seq, _ = h.shape
    rows = state.shape[1]
    assert seq % tile_m == 0 and tile_m % CHUNK == 0 and seq >= POOL_BUF and rows == SAMPLE_ROWS
    tiles_per_prompt = seq // tile_m
    n_tiles = batch * tiles_per_prompt

    def front_tile(s):
        t = jnp.minimum(s, n_tiles - 1)
        return t // tiles_per_prompt, t % tiles_per_prompt, 0

    def back_tile(s):
        t = jnp.maximum(s - 1, 0)
        return t // tiles_per_prompt, t % tiles_per_prompt, 0

    per_prompt = lambda s: (jnp.minimum(s, n_tiles - 1) // tiles_per_prompt, 0, 0)
    hbm = pl.BlockSpec(memory_space=pl.ANY)
    whole = lambda a: pl.BlockSpec(a.shape, lambda s: (0,) * a.ndim)
    return pl.pallas_call(
        functools.partial(_layer_kernel, tile_m=tile_m, tiles_per_prompt=tiles_per_prompt,
                          n_tiles=n_tiles, pos0=pos0),
        grid=(n_tiles + 1,),
        in_specs=[pl.BlockSpec((1, tile_m, D_MODEL), front_tile),
                  pl.BlockSpec((1, tile_m, D_MODEL), back_tile),
                  pl.BlockSpec((1, tile_m, PLE_DIM), back_tile),
                  _resident(hs.shape), _resident(ps.shape), _resident(taps.shape), hbm]
                 + [_resident(w.shape) for w in weights],
        out_specs=[pl.BlockSpec((1, tile_m, D_MODEL), back_tile),
                   pl.BlockSpec((1, CHUNK, D_MODEL), per_prompt),
                   pl.BlockSpec((1, POOL_BUF, D_MODEL), per_prompt),
                   whole(hs), whole(hs), hbm],
        out_shape=[jax.ShapeDtypeStruct((batch, seq, D_MODEL), _F32),
                   jax.ShapeDtypeStruct((batch, CHUNK, D_MODEL), _F32),
                   jax.ShapeDtypeStruct((batch, POOL_BUF, D_MODEL), _F32),
                   jax.ShapeDtypeStruct(hs.shape, _F32),
                   jax.ShapeDtypeStruct(hs.shape, _F32),
                   jax.ShapeDtypeStruct(state.shape, _F32)],
        scratch_shapes=[pltpu.VMEM((N_LANE_BLOCKS, TAIL + tile_m, LANES), _F32),
                        pltpu.VMEM((tile_m, D_MODEL), _BF16),
                        pltpu.VMEM((tile_m, D_MODEL), _BF16),
                        pltpu.VMEM((tile_m, D_MODEL), _BF16)]
                       + [pltpu.VMEM((w - 1, rows, POOL_GROUP), _F32) for w in POOL_WINDOWS]
                       + [pltpu.VMEM((rows, D_MODEL), _F32),
                          pltpu.SemaphoreType.DMA((len(POOL_WINDOWS) + 2,))],
        compiler_params=pltpu.CompilerParams(
            dimension_semantics=("arbitrary",),
            vmem_limit_bytes=VMEM_LIMIT_BYTES),
        name="layer",
    )(h, h, p, hs, ps, taps, state, *weights)


def kernel(x_prompt, x_sample, state_pool, p_prompt, p_sample, pre_g, w_in, ln_g, ln_b, w_s, b_s,
           w_pool, pool_scale, w_pa, w_pb, w_out, post_g, w_ple, w_pg, ple_in_g, ple_out_g):
    depth = w_in.shape[0]
    dec_batch, dec_seq, _ = x_sample.shape
    assert dec_seq == 1 and x_prompt.shape[2] == D_MODEL
    hp = x_prompt
    hs = x_sample.reshape(dec_batch * N_LANE_BLOCKS, LANES)
    pv, pp, sv, sp = [], [], [], []
    for i in range(depth):
        vecs = jnp.stack([pre_g[i], ln_g[i], ln_b[i], pool_scale[i], post_g[i], ple_in_g[i],
                          ple_out_g[i]] + [jnp.zeros_like(pre_g[i])] * (N_VEC_ROWS - 7))
        w_s_rows = w_s[i].reshape(D_MODEL, CHUNK)
        w_in_p, w_s_p, w_pool_p, w_pa_p, w_pb_p, w_out_p, w_ple_p, w_pg_p = _pack_bf16(
            [w_in[i], w_s_rows, w_pool[i].reshape(D_MODEL, POOL_GROUP),
             w_pa[i], w_pb[i], w_out[i], w_ple[i], w_pg[i]],
            tril_periods=(None, CHUNK) + (None,) * 6, n_steps=PACK_STEPS)
        weights = (w_in_p, w_s_p, b_s[i].T, w_pool_p, w_pa_p, w_pb_p, w_out_p, w_ple_p, w_pg_p,
                   vecs)
        hp, v, pool, hs, v_s, pool_s = _layer(
            hp, p_prompt[i], hs, p_sample[i].reshape(dec_batch * PLE_DIM // LANES, LANES),
            jnp.swapaxes(state_pool[i], 0, 1), w_s_rows, weights, TILE_M, PAST_LEN)
        pv.append(v)
        pp.append(pool)
        sv.append(v_s.reshape(dec_batch, 1, D_MODEL))
        sp.append(jnp.swapaxes(pool_s, 0, 1))
    return (hp, hs.reshape(dec_batch, 1, D_MODEL), jnp.stack(pv), jnp.stack(pp), jnp.stack(sv),
            jnp.stack(sp))
```

```python
import functools

import jax
import jax.numpy as jnp
from jax import lax
from jax.experimental import pallas as pl
from jax.experimental.pallas import tpu as pltpu

D_MODEL = 1024
CHUNK = 128
N_HEADS = D_MODEL // CHUNK
POOL_WINDOWS = (2, 4, 8, 16)
POOL_GROUP = D_MODEL // len(POOL_WINDOWS)
POOL_BUF = max(POOL_WINDOWS) - 1
PLE_DIM = 256
PAST_LEN = 16384
SAMPLE_ROWS = 128
EPS = 1e-6
COL_U, COL_V, COL_ZA, COL_XB, COL_ZB, COL_GA, COL_GB = range(7)
VEC_PRE_G, VEC_LN_G, VEC_LN_B, VEC_POOL_SCALE, VEC_POST_G, VEC_PLE_IN_G, VEC_PLE_OUT_G = range(7)
N_VEC_ROWS = 8

LANES = 128
N_LANE_BLOCKS = D_MODEL // LANES
BF16_ROWS = 16
TAIL = 16
PACK_STEPS = 8
TILE_M = 256
VMEM_LIMIT_BYTES = 56 * 1024 * 1024

_BF16 = jnp.bfloat16
_F32 = jnp.float32
_GELU_C = 0.7978845608028654


def _dot(a, b):
    return jnp.dot(a, b, preferred_element_type=_F32)


def _rmsnorm(x, g):
    return x * lax.rsqrt(jnp.mean(x * x, axis=-1, keepdims=True) + EPS) * g


def _layernorm(x, g, b):
    xc = x - jnp.mean(x, axis=-1, keepdims=True)
    return xc * lax.rsqrt(jnp.mean(xc * xc, axis=-1, keepdims=True) + EPS) * g + b


def _gelu_tanh(x):
    return 0.5 * x * (1.0 + jnp.tanh(_GELU_C * (x + 0.044715 * (x * x * x))))


def _sigmoid(x):
    return 0.5 * jnp.tanh(0.5 * x) + 0.5


def _silu(x):
    return x * _sigmoid(x)


def _unpack(ref, rows=slice(None), cols=slice(None)):
    return pltpu.bitcast(ref[rows, cols], _BF16)


def _vec(vecs_ref, row):
    return vecs_ref[row:row + 1, :]


def _in_proj(xn, w_in_ref, col):
    return _dot(xn, _unpack(w_in_ref, cols=slice(col * D_MODEL, (col + 1) * D_MODEL)))


def _pool_mix(pooled, w_pool_ref, vecs_ref):
    parts = []
    for g in range(len(POOL_WINDOWS)):
        q = pooled[:, g * POOL_GROUP:(g + 1) * POOL_GROUP].astype(_BF16)
        rows = slice(g * POOL_GROUP // 2, (g + 1) * POOL_GROUP // 2)
        parts.append(_dot(q, _unpack(w_pool_ref, rows=rows)))
    return jnp.concatenate(parts, axis=1) * _vec(vecs_ref, VEC_POOL_SCALE)


def _merge_and_embed(h, p, xn, y_a, y_b, w_in_ref, w_pa_ref, w_pb_ref, w_out_ref, w_ple_ref,
                     w_pg_ref, vecs_ref):
    g_a = _in_proj(xn, w_in_ref, COL_GA)
    g_b = _in_proj(xn, w_in_ref, COL_GB)
    m = (_sigmoid(g_a) * _dot(y_a.astype(_BF16), _unpack(w_pa_ref))
         + _sigmoid(g_b) * _dot(y_b.astype(_BF16), _unpack(w_pb_ref)))
    h = h + _rmsnorm(_dot(m.astype(_BF16), _unpack(w_out_ref)), _vec(vecs_ref, VEC_POST_G))
    e = _dot(p.astype(_BF16), _unpack(w_ple_ref))
    hn = _rmsnorm(h, _vec(vecs_ref, VEC_PLE_IN_G)).astype(_BF16)
    gate = _sigmoid(_dot(hn, _unpack(w_pg_ref)))
    return h + _rmsnorm(gate * e, _vec(vecs_ref, VEC_PLE_OUT_G))


def _spatial_mix(vn_b, w_s_ref, bs_t_ref, tile_m):
    w_tril = [_unpack(w_s_ref, rows=slice(hd * CHUNK // 2, (hd + 1) * CHUNK // 2))
              for hd in range(N_HEADS)]
    bias = [bs_t_ref[:, hd:hd + 1] for hd in range(N_HEADS)]
    s_rows = []
    for c in range(tile_m // CHUNK):
        heads = []
        for hd in range(N_HEADS):
            blk = vn_b[c * CHUNK:(c + 1) * CHUNK, hd * CHUNK:(hd + 1) * CHUNK]
            heads.append(_dot(w_tril[hd], blk) + bias[hd])
        s_rows.append(jnp.concatenate(heads, axis=1))
    return jnp.concatenate(s_rows, axis=0)


def _window_pool(x_b, ext_ref, pool_ref, j, tile_m):
    for c in range(N_LANE_BLOCKS):
        ext_ref[c, TAIL:TAIL + tile_m, :] = x_b[:, c * LANES:(c + 1) * LANES]
    pos = j * tile_m + lax.broadcasted_iota(jnp.int32, (tile_m, 1), 0)
    blocks = []
    for c in range(N_LANE_BLOCKS):
        w = POOL_WINDOWS[c * LANES // POOL_GROUP]
        cur = x_b[:, c * LANES:(c + 1) * LANES]
        win = cur
        for k in range(1, w):
            win = win + ext_ref[c, TAIL - k:TAIL - k + tile_m, :]
        cnt = jnp.minimum(pos + 1, w).astype(_F32)
        blocks.append(win / cnt - cur)
    for c in range(N_LANE_BLOCKS):
        pool_ref[0, :, c * LANES:(c + 1) * LANES] = (
            ext_ref[c, TAIL + tile_m - POOL_BUF:TAIL + tile_m, :])
        ext_ref[c, 0:TAIL, :] = ext_ref[c, tile_m:tile_m + TAIL, :]
    return jnp.concatenate(blocks, axis=1)


def _layer_kernel(x_ref, x_lag_ref, p_lag_ref, xs_ref, ps_ref, taps_ref, state_hbm,
                  w_in_ref, w_s_ref, bs_t_ref, w_pool_ref, w_pa_ref, w_pb_ref, w_out_ref, w_ple_ref,
                  w_pg_ref, vecs_ref,
                  y_ref, vn_ref, pool_ref, ys_ref, vns_ref, spool_hbm,
                  ext_ref, xn_c, ya_c, yb_c, hist_ref, xbs_ref, sems,
                  *, tile_m, tiles_per_prompt, n_tiles, pos0):
    s_id = pl.program_id(0)
    j = lax.rem(jnp.minimum(s_id, n_tiles - 1), tiles_per_prompt)
    vec = functools.partial(_vec, vecs_ref)
    proj = functools.partial(_in_proj, w_in_ref=w_in_ref)

    @pl.when(j == 0)
    def _():
        ext_ref[:, 0:TAIL, :] = jnp.zeros((N_LANE_BLOCKS, TAIL, LANES), _F32)

    @pl.when(s_id == 0)
    def _front_only():
        xn = _rmsnorm(x_ref[0], vec(VEC_PRE_G)).astype(_BF16)
        xn_c[...] = xn
        vn = _layernorm(_gelu_tanh(proj(xn, col=COL_V)), vec(VEC_LN_G), vec(VEC_LN_B))
        vn_ref[0] = vn[tile_m - CHUNK:, :]
        s = _spatial_mix(vn.astype(_BF16), w_s_ref, bs_t_ref, tile_m)
        ya_c[...] = (_gelu_tanh(proj(xn, col=COL_U)) * s
                     * _silu(proj(xn, col=COL_ZA))).astype(_BF16)
        pooled = _window_pool(proj(xn, col=COL_XB), ext_ref, pool_ref, j, tile_m)
        yb_c[...] = (_pool_mix(pooled, w_pool_ref, vecs_ref)
                     * _silu(proj(xn, col=COL_ZB))).astype(_BF16)

    @pl.when(s_id == n_tiles)
    def _back_only_and_sample():
        history, shift, newest = _sample_copies(state_hbm, spool_hbm, hist_ref, xbs_ref, sems)
        history.start()
        y_ref[0] = _merge_and_embed(x_lag_ref[0], p_lag_ref[0], xn_c[...], ya_c[...], yb_c[...],
                                    w_in_ref, w_pa_ref, w_pb_ref, w_out_ref, w_ple_ref, w_pg_ref,
                                    vecs_ref)
        history.wait()
        shift.start()
        _sample_rows(xs_ref, ps_ref, taps_ref, hist_ref, newest, xbs_ref, w_in_ref,
                     bs_t_ref, w_pool_ref, w_pa_ref, w_pb_ref, w_out_ref, w_ple_ref, w_pg_ref,
                     vecs_ref, ys_ref, vns_ref, pos0)
        shift.wait()
        newest.wait()

    @pl.when(jnp.logical_and(s_id > 0, s_id < n_tiles))
    def _both():
        xn_prev = xn_c[...]
        g_a = proj(xn_prev, col=COL_GA)
        xn = _rmsnorm(x_ref[0], vec(VEC_PRE_G)).astype(_BF16)
        xn_c[...] = xn
        v_raw = proj(xn, col=COL_V)
        g_b = proj(xn_prev, col=COL_GB)
        vn = _layernorm(_gelu_tanh(v_raw), vec(VEC_LN_G), vec(VEC_LN_B))
        vn_ref[0] = vn[tile_m - CHUNK:, :]
        u_raw = proj(xn, col=COL_U)
        s = _spatial_mix(vn.astype(_BF16), w_s_ref, bs_t_ref, tile_m)
        proj_a = _dot(ya_c[...], _unpack(w_pa_ref))
        proj_b = _dot(yb_c[...], _unpack(w_pb_ref))
        za_raw = proj(xn, col=COL_ZA)
        m = (_sigmoid(g_a) * proj_a + _sigmoid(g_b) * proj_b).astype(_BF16)
        o = _dot(m, _unpack(w_out_ref))
        ya_c[...] = (_gelu_tanh(u_raw) * s * _silu(za_raw)).astype(_BF16)
        x_b = proj(xn, col=COL_XB)
        h = x_lag_ref[0] + _rmsnorm(o, vec(VEC_POST_G))
        hn = _rmsnorm(h, vec(VEC_PLE_IN_G)).astype(_BF16)
        gate_raw = _dot(hn, _unpack(w_pg_ref))
        pooled = _window_pool(x_b, ext_ref, pool_ref, j, tile_m)
        e = _dot(p_lag_ref[0].astype(_BF16), _unpack(w_ple_ref))
        mixed = _pool_mix(pooled, w_pool_ref, vecs_ref)
        y_ref[0] = h + _rmsnorm(_sigmoid(gate_raw) * e, vec(VEC_PLE_OUT_G))
        zb_raw = proj(xn, col=COL_ZB)
        yb_c[...] = (mixed * _silu(zb_raw)).astype(_BF16)


def _gather_rows(ref):
    n = ref.shape[0] // SAMPLE_ROWS
    return jnp.concatenate([ref[pl.ds(c, SAMPLE_ROWS, stride=n), :] for c in range(n)], axis=1)


def _scatter_rows(ref, value):
    n = ref.shape[0] // SAMPLE_ROWS
    for c in range(n):
        ref[pl.ds(c, SAMPLE_ROWS, stride=n), :] = value[:, c * LANES:(c + 1) * LANES]


def _sample_copies(state_hbm, spool_hbm, hist_ref, xbs_ref, sems):
    history = pltpu.make_async_copy(state_hbm, hist_ref, sems.at[0])
    shift = pltpu.make_async_copy(hist_ref.at[pl.ds(1, POOL_BUF - 1)],
                                  spool_hbm.at[pl.ds(0, POOL_BUF - 1)], sems.at[1])
    newest = pltpu.make_async_copy(xbs_ref, spool_hbm.at[POOL_BUF - 1], sems.at[2])
    return history, shift, newest


def _sample_rows(x_ref, p_ref, taps_ref, hist_ref, newest, xbs_ref, w_in_ref, bs_t_ref,
                 w_pool_ref, w_pa_ref, w_pb_ref, w_out_ref, w_ple_ref, w_pg_ref, vecs_ref,
                 y_ref, vn_ref, pos0):
    x = _gather_rows(x_ref)
    xn = _rmsnorm(x, _vec(vecs_ref, VEC_PRE_G)).astype(_BF16)

    u = _gelu_tanh(_in_proj(xn, w_in_ref, COL_U))
    vn = _layernorm(_gelu_tanh(_in_proj(xn, w_in_ref, COL_V)),
                    _vec(vecs_ref, VEC_LN_G), _vec(vecs_ref, VEC_LN_B))
    _scatter_rows(vn_ref, vn)
    heads = []
    for hd in range(N_HEADS):
        tap = taps_ref[hd * CHUNK:hd * CHUNK + 1, 0:1]
        heads.append(vn[:, hd * CHUNK:(hd + 1) * CHUNK] * tap + bs_t_ref[0:1, hd:hd + 1])
    s = jnp.concatenate(heads, axis=1)
    y_a = u * s * _silu(_in_proj(xn, w_in_ref, COL_ZA))

    x_b = _in_proj(xn, w_in_ref, COL_XB)
    xbs_ref[...] = x_b
    newest.start()
    groups = []
    for g, w in enumerate(POOL_WINDOWS):
        lo, hi = g * POOL_GROUP, (g + 1) * POOL_GROUP
        cur = x_b[:, lo:hi]
        win = cur
        for k in range(1, w):
            win = win + hist_ref[POOL_BUF - k, :, lo:hi]
        groups.append(win / float(min(pos0 + 1, w)) - cur)
    pooled = jnp.concatenate(groups, axis=1)
    y_b = _pool_mix(pooled, w_pool_ref, vecs_ref) * _silu(_in_proj(xn, w_in_ref, COL_ZB))

    _scatter_rows(y_ref, _merge_and_embed(x, _gather_rows(p_ref), xn, y_a, y_b, w_in_ref, w_pa_ref,
                                          w_pb_ref, w_out_ref, w_ple_ref, w_pg_ref, vecs_ref))


def _pack_kernel(*refs, tril_periods):
    n = len(refs) // 2
    for src, dst, period in zip(refs[:n], refs[n:], tril_periods):
        w = src[...]
        if period is not None:
            row = lax.broadcasted_iota(jnp.int32, w.shape, 0) % period
            w = jnp.where(row >= lax.broadcasted_iota(jnp.int32, w.shape, 1), w, 0.0)
        dst[...] = pltpu.bitcast(w.astype(_BF16), jnp.uint32)


def _pack_bf16(mats, tril_periods, n_steps):
    blocks = [m.shape[0] // n_steps for m in mats]
    for m, rows, period in zip(mats, blocks, tril_periods):
        assert rows * n_steps == m.shape[0] and rows % (2 * BF16_ROWS) == 0
        assert period is None or rows % period == 0
    return pl.pallas_call(
        functools.partial(_pack_kernel, tril_periods=tril_periods),
        grid=(n_steps,),
        in_specs=[pl.BlockSpec((rows, m.shape[1]), lambda i: (i, 0))
                  for m, rows in zip(mats, blocks)],
        out_specs=[pl.BlockSpec((rows // 2, m.shape[1]), lambda i: (i, 0))
                   for m, rows in zip(mats, blocks)],
        out_shape=[jax.ShapeDtypeStruct((m.shape[0] // 2, m.shape[1]), jnp.uint32) for m in mats],
        compiler_params=pltpu.CompilerParams(dimension_semantics=("arbitrary",),
                                             vmem_limit_bytes=VMEM_LIMIT_BYTES),
        name="pack_bf16",
    )(*mats)


def _resident(shape):
    zeros = (0,) * len(shape)
    return pl.BlockSpec(shape, lambda *_: zeros, pipeline_mode=pl.Buffered(1))


def _layer(h, p, hs, ps, state, taps, weights, tile_m, pos0):
    batch, seq, _ = h.shape
    rows = state.shape[1]
    assert seq % tile_m == 0 and tile_m % CHUNK == 0 and seq >= POOL_BUF and rows == SAMPLE_ROWS
    tiles_per_prompt = seq // tile_m
    n_tiles = batch * tiles_per_prompt

    def front_tile(s):
        t = jnp.minimum(s, n_tiles - 1)
        return t // tiles_per_prompt, t % tiles_per_prompt, 0

    def back_tile(s):
        t = jnp.maximum(s - 1, 0)
        return t // tiles_per_prompt, t % tiles_per_prompt, 0

    per_prompt = lambda s: (jnp.minimum(s, n_tiles - 1) // tiles_per_prompt, 0, 0)
    hbm = pl.BlockSpec(memory_space=pl.ANY)
    whole = lambda a: pl.BlockSpec(a.shape, lambda s: (0,) * a.ndim)
    return pl.pallas_call(
        functools.partial(_layer_kernel, tile_m=tile_m, tiles_per_prompt=tiles_per_prompt,
                          n_tiles=n_tiles, pos0=pos0),
        grid=(n_tiles + 1,),
        in_specs=[pl.BlockSpec((1, tile_m, D_MODEL), front_tile),
                  pl.BlockSpec((1, tile_m, D_MODEL), back_tile),
                  pl.BlockSpec((1, tile_m, PLE_DIM), back_tile),
                  _resident(hs.shape), _resident(ps.shape), _resident(taps.shape), hbm]
                 + [_resident(w.shape) for w in weights],
        out_specs=[pl.BlockSpec((1, tile_m, D_MODEL), back_tile),
                   pl.BlockSpec((1, CHUNK, D_MODEL), per_prompt),
                   pl.BlockSpec((1, POOL_BUF, D_MODEL), per_prompt),
                   whole(hs), whole(hs), hbm],
        out_shape=[jax.ShapeDtypeStruct((batch, seq, D_MODEL), _F32),
                   jax.ShapeDtypeStruct((batch, CHUNK, D_MODEL), _F32),
                   jax.ShapeDtypeStruct((batch, POOL_BUF, D_MODEL), _F32),
                   jax.ShapeDtypeStruct(hs.shape, _F32),
                   jax.ShapeDtypeStruct(hs.shape, _F32),
                   jax.ShapeDtypeStruct(state.shape, _F32)],
        scratch_shapes=[pltpu.VMEM((N_LANE_BLOCKS, TAIL + tile_m, LANES), _F32),
                        pltpu.VMEM((tile_m, D_MODEL), _BF16),
                        pltpu.VMEM((tile_m, D_MODEL), _BF16),
                        pltpu.VMEM((tile_m, D_MODEL), _BF16)]
                       + [pltpu.VMEM(state.shape, _F32),
                          pltpu.VMEM((rows, D_MODEL), _F32),
                          pltpu.SemaphoreType.DMA((3,))],
        compiler_params=pltpu.CompilerParams(
            dimension_semantics=("arbitrary",),
            vmem_limit_bytes=VMEM_LIMIT_BYTES),
        name="layer",
    )(h, h, p, hs, ps, taps, state, *weights)


def kernel(x_prompt, x_sample, state_pool, p_prompt, p_sample, pre_g, w_in, ln_g, ln_b, w_s, b_s,
           w_pool, pool_scale, w_pa, w_pb, w_out, post_g, w_ple, w_pg, ple_in_g, ple_out_g):
    depth = w_in.shape[0]
    dec_batch, dec_seq, _ = x_sample.shape
    assert dec_seq == 1 and x_prompt.shape[2] == D_MODEL
    hp = x_prompt
    hs = x_sample.reshape(dec_batch * N_LANE_BLOCKS, LANES)
    pv, pp, sv, sp = [], [], [], []
    for i in range(depth):
        vecs = jnp.stack([pre_g[i], ln_g[i], ln_b[i], pool_scale[i], post_g[i], ple_in_g[i],
                          ple_out_g[i]] + [jnp.zeros_like(pre_g[i])] * (N_VEC_ROWS - 7))
        w_s_rows = w_s[i].reshape(D_MODEL, CHUNK)
        w_in_p, w_s_p, w_pool_p, w_pa_p, w_pb_p, w_out_p, w_ple_p, w_pg_p = _pack_bf16(
            [w_in[i], w_s_rows, w_pool[i].reshape(D_MODEL, POOL_GROUP),
             w_pa[i], w_pb[i], w_out[i], w_ple[i], w_pg[i]],
            tril_periods=(None, CHUNK) + (None,) * 6, n_steps=PACK_STEPS)
        weights = (w_in_p, w_s_p, b_s[i].T, w_pool_p, w_pa_p, w_pb_p, w_out_p, w_ple_p, w_pg_p,
                   vecs)
        hp, v, pool, hs, v_s, pool_s = _layer(
            hp, p_prompt[i], hs, p_sample[i].reshape(dec_batch * PLE_DIM // LANES, LANES),
            jnp.swapaxes(state_pool[i], 0, 1), w_s_rows, weights, TILE_M, PAST_LEN)
        pv.append(v)
        pp.append(pool)
        sv.append(v_s.reshape(dec_batch, 1, D_MODEL))
        sp.append(jnp.swapaxes(pool_s, 0, 1))
    return (hp, hs.reshape(dec_batch, 1, D_MODEL), jnp.stack(pv), jnp.stack(pp), jnp.stack(sv),
            jnp.stack(sp))
```

```python
import functools
from typing import Any, NamedTuple

import jax
import jax.numpy as jnp
from jax import lax
from jax.experimental import pallas as pl
from jax.experimental.pallas import tpu as pltpu

D_MODEL = 1024
CHUNK = 128
N_HEADS = D_MODEL // CHUNK
POOL_WINDOWS = (2, 4, 8, 16)
POOL_GROUP = D_MODEL // len(POOL_WINDOWS)
POOL_BUF = max(POOL_WINDOWS) - 1
PLE_DIM = 256
PAST_LEN = 16384
SAMPLE_ROWS = 128
EPS = 1e-6
COL_U, COL_V, COL_ZA, COL_XB, COL_ZB, COL_GA, COL_GB = range(7)
N_FRONT_COLS = 5
VEC_PRE_G, VEC_LN_G, VEC_LN_B, VEC_POOL_SCALE, VEC_POST_G, VEC_PLE_IN_G, VEC_PLE_OUT_G = range(7)
N_VEC_ROWS = 8

LANES = 128
N_LANE_BLOCKS = D_MODEL // LANES
BF16_ROWS = 16
TAIL = 16
PACK_STEPS = 8
STAGE_ROWS = 256
N_STAGE_SLOTS = 4
TILE_M = 256
VMEM_LIMIT_BYTES = 56 * 1024 * 1024

_BF16 = jnp.bfloat16
_F32 = jnp.float32
_GELU_C = 0.7978845608028654


def _dot(a, b):
    return jnp.dot(a, b, preferred_element_type=_F32)


def _rmsnorm(x, g):
    return x * lax.rsqrt(jnp.mean(x * x, axis=-1, keepdims=True) + EPS) * g


def _layernorm(x, g, b):
    xc = x - jnp.mean(x, axis=-1, keepdims=True)
    return xc * lax.rsqrt(jnp.mean(xc * xc, axis=-1, keepdims=True) + EPS) * g + b


def _gelu_tanh(x):
    return 0.5 * x * (1.0 + jnp.tanh(_GELU_C * (x + 0.044715 * (x * x * x))))


def _sigmoid(x):
    return 0.5 * jnp.tanh(0.5 * x) + 0.5


def _silu(x):
    return x * _sigmoid(x)


def _unpack(ref, rows=slice(None), cols=slice(None)):
    return pltpu.bitcast(ref[rows, cols], _BF16)


class _Weights(NamedTuple):
    in_front: Any
    spatial: Any
    pool: Any
    in_gates: Any
    pa: Any
    pb: Any
    out: Any
    ple: Any
    pg: Any
    bias_t: Any
    vecs: Any

    def vec(self, row):
        return self.vecs[row:row + 1, :]

    def in_proj(self, xn, col):
        if col < N_FRONT_COLS:
            w = _unpack(self.in_front, cols=slice(col * D_MODEL, (col + 1) * D_MODEL))
        else:
            w = self.in_gates[:, (col - N_FRONT_COLS) * D_MODEL:(col - N_FRONT_COLS + 1) * D_MODEL]
        return _dot(xn, w)


def _pool_mix(pooled, w):
    parts = []
    for g in range(len(POOL_WINDOWS)):
        q = pooled[:, g * POOL_GROUP:(g + 1) * POOL_GROUP].astype(_BF16)
        rows = slice(g * POOL_GROUP // 2, (g + 1) * POOL_GROUP // 2)
        parts.append(_dot(q, _unpack(w.pool, rows=rows)))
    return jnp.concatenate(parts, axis=1) * w.vec(VEC_POOL_SCALE)


def _merge_and_embed(h, p, xn, y_a, y_b, w):
    g_a = w.in_proj(xn, COL_GA)
    g_b = w.in_proj(xn, COL_GB)
    m = (_sigmoid(g_a) * _dot(y_a.astype(_BF16), w.pa[...])
         + _sigmoid(g_b) * _dot(y_b.astype(_BF16), w.pb[...]))
    h = h + _rmsnorm(_dot(m.astype(_BF16), w.out[...]), w.vec(VEC_POST_G))
    e = _dot(p.astype(_BF16), w.ple[...])
    hn = _rmsnorm(h, w.vec(VEC_PLE_IN_G)).astype(_BF16)
    gate = _sigmoid(_dot(hn, w.pg[...]))
    return h + _rmsnorm(gate * e, w.vec(VEC_PLE_OUT_G))


def _spatial_mix(vn_b, w, tile_m):
    w_tril = [_unpack(w.spatial, rows=slice(hd * CHUNK // 2, (hd + 1) * CHUNK // 2))
              for hd in range(N_HEADS)]
    bias = [w.bias_t[:, hd:hd + 1] for hd in range(N_HEADS)]
    s_rows = []
    for c in range(tile_m // CHUNK):
        heads = []
        for hd in range(N_HEADS):
            blk = vn_b[c * CHUNK:(c + 1) * CHUNK, hd * CHUNK:(hd + 1) * CHUNK]
            heads.append(_dot(w_tril[hd], blk) + bias[hd])
        s_rows.append(jnp.concatenate(heads, axis=1))
    return jnp.concatenate(s_rows, axis=0)


def _window_pool(x_b, ext_ref, pool_ref, j, tile_m):
    for c in range(N_LANE_BLOCKS):
        ext_ref[c, TAIL:TAIL + tile_m, :] = x_b[:, c * LANES:(c + 1) * LANES]
    pos = j * tile_m + lax.broadcasted_iota(jnp.int32, (tile_m, 1), 0)
    blocks = []
    for c in range(N_LANE_BLOCKS):
        w = POOL_WINDOWS[c * LANES // POOL_GROUP]
        cur = x_b[:, c * LANES:(c + 1) * LANES]
        win = cur
        for k in range(1, w):
            win = win + ext_ref[c, TAIL - k:TAIL - k + tile_m, :]
        cnt = jnp.minimum(pos + 1, w).astype(_F32)
        blocks.append(win / cnt - cur)
    for c in range(N_LANE_BLOCKS):
        pool_ref[0, :, c * LANES:(c + 1) * LANES] = (
            ext_ref[c, TAIL + tile_m - POOL_BUF:TAIL + tile_m, :])
        ext_ref[c, 0:TAIL, :] = ext_ref[c, tile_m:tile_m + TAIL, :]
    return jnp.concatenate(blocks, axis=1)


def _back_weight_copies(w_in_hbm, back_hbm, w, stage_ref, sems):
    slabs = []
    for col in (COL_GA, COL_GB):
        for r in range(0, D_MODEL, STAGE_ROWS):
            src = w_in_hbm.at[pl.ds(r, STAGE_ROWS), pl.ds(col * D_MODEL, D_MODEL)]
            slabs.append((src, w.in_gates, r, (col - N_FRONT_COLS) * D_MODEL))
    for src_ref, dst in zip(back_hbm, (w.pa, w.pb, w.out, w.ple, w.pg)):
        for r in range(0, src_ref.shape[0], STAGE_ROWS):
            slabs.append((src_ref.at[pl.ds(r, STAGE_ROWS), :], dst, r, 0))
    return [(pltpu.make_async_copy(src, stage_ref.at[i % N_STAGE_SLOTS],
                                   sems.at[i % N_STAGE_SLOTS]), dst, r, c)
            for i, (src, dst, r, c) in enumerate(slabs)]


def _round_back_weights(copies, stage_ref):
    for i, (copy, dst, r, c) in enumerate(copies):
        copy.wait()
        dst[r:r + STAGE_ROWS, c:c + D_MODEL] = stage_ref[i % N_STAGE_SLOTS].astype(_BF16)
        if i + N_STAGE_SLOTS < len(copies):
            copies[i + N_STAGE_SLOTS][0].start()


def _layer_kernel(x_ref, x_lag_ref, p_lag_ref, xs_ref, ps_ref, taps_ref, state_hbm,
                  w_in_hbm, w_pa_hbm, w_pb_hbm, w_out_hbm, w_ple_hbm, w_pg_hbm,
                  w_in_front_ref, w_s_ref, w_pool_ref, bs_t_ref, vecs_ref,
                  y_ref, vn_ref, pool_ref, ys_ref, vns_ref, spool_hbm,
                  ext_ref, xn_c, ya_c, yb_c, hist_ref, xbs_ref,
                  gates_s, pa_s, pb_s, out_s, ple_s, pg_s, stage_ref, sems, stage_sems,
                  *, tile_m, tiles_per_prompt, n_tiles, pos0):
    s_id = pl.program_id(0)
    j = lax.rem(jnp.minimum(s_id, n_tiles - 1), tiles_per_prompt)
    w = _Weights(in_front=w_in_front_ref, spatial=w_s_ref, pool=w_pool_ref, in_gates=gates_s,
                 pa=pa_s, pb=pb_s, out=out_s, ple=ple_s, pg=pg_s, bias_t=bs_t_ref, vecs=vecs_ref)

    @pl.when(j == 0)
    def _():
        ext_ref[:, 0:TAIL, :] = jnp.zeros((N_LANE_BLOCKS, TAIL, LANES), _F32)

    @pl.when(s_id == 0)
    def _front_only():
        copies = _back_weight_copies(w_in_hbm, (w_pa_hbm, w_pb_hbm, w_out_hbm, w_ple_hbm, w_pg_hbm),
                                     w, stage_ref, stage_sems)
        for copy, *_ in copies[:N_STAGE_SLOTS]:
            copy.start()
        xn = _rmsnorm(x_ref[0], w.vec(VEC_PRE_G)).astype(_BF16)
        xn_c[...] = xn
        vn = _layernorm(_gelu_tanh(w.in_proj(xn, COL_V)), w.vec(VEC_LN_G), w.vec(VEC_LN_B))
        vn_ref[0] = vn[tile_m - CHUNK:, :]
        s = _spatial_mix(vn.astype(_BF16), w, tile_m)
        ya_c[...] = (_gelu_tanh(w.in_proj(xn, COL_U)) * s
                     * _silu(w.in_proj(xn, COL_ZA))).astype(_BF16)
        pooled = _window_pool(w.in_proj(xn, COL_XB), ext_ref, pool_ref, j, tile_m)
        yb_c[...] = (_pool_mix(pooled, w) * _silu(w.in_proj(xn, COL_ZB))).astype(_BF16)
        _round_back_weights(copies, stage_ref)

    @pl.when(s_id == n_tiles)
    def _back_only_and_sample():
        history, shift, newest = _sample_copies(state_hbm, spool_hbm, hist_ref, xbs_ref, sems)
        history.start()
        y_ref[0] = _merge_and_embed(x_lag_ref[0], p_lag_ref[0], xn_c[...], ya_c[...], yb_c[...], w)
        history.wait()
        shift.start()
        _sample_rows(xs_ref, ps_ref, taps_ref, hist_ref, newest, xbs_ref, w, ys_ref, vns_ref, pos0)
        shift.wait()
        newest.wait()

    @pl.when(jnp.logical_and(s_id > 0, s_id < n_tiles))
    def _both():
        xn_prev = xn_c[...]
        g_a = w.in_proj(xn_prev, COL_GA)
        xn = _rmsnorm(x_ref[0], w.vec(VEC_PRE_G)).astype(_BF16)
        xn_c[...] = xn
        v_raw = w.in_proj(xn, COL_V)
        g_b = w.in_proj(xn_prev, COL_GB)
        vn = _layernorm(_gelu_tanh(v_raw), w.vec(VEC_LN_G), w.vec(VEC_LN_B))
        vn_ref[0] = vn[tile_m - CHUNK:, :]
        u_raw = w.in_proj(xn, COL_U)
        s = _spatial_mix(vn.astype(_BF16), w, tile_m)
        proj_a = _dot(ya_c[...], w.pa[...])
        proj_b = _dot(yb_c[...], w.pb[...])
        za_raw = w.in_proj(xn, COL_ZA)
        m = (_sigmoid(g_a) * proj_a + _sigmoid(g_b) * proj_b).astype(_BF16)
        o = _dot(m, w.out[...])
        ya_c[...] = (_gelu_tanh(u_raw) * s * _silu(za_raw)).astype(_BF16)
        x_b = w.in_proj(xn, COL_XB)
        h = x_lag_ref[0] + _rmsnorm(o, w.vec(VEC_POST_G))
        hn = _rmsnorm(h, w.vec(VEC_PLE_IN_G)).astype(_BF16)
        gate_raw = _dot(hn, w.pg[...])
        pooled = _window_pool(x_b, ext_ref, pool_ref, j, tile_m)
        e = _dot(p_lag_ref[0].astype(_BF16), w.ple[...])
        mixed = _pool_mix(pooled, w)
        y_ref[0] = h + _rmsnorm(_sigmoid(gate_raw) * e, w.vec(VEC_PLE_OUT_G))
        zb_raw = w.in_proj(xn, COL_ZB)
        yb_c[...] = (mixed * _silu(zb_raw)).astype(_BF16)


def _gather_rows(ref):
    n = ref.shape[0] // SAMPLE_ROWS
    return jnp.concatenate([ref[pl.ds(c, SAMPLE_ROWS, stride=n), :] for c in range(n)], axis=1)


def _scatter_rows(ref, value):
    n = ref.shape[0] // SAMPLE_ROWS
    for c in range(n):
        ref[pl.ds(c, SAMPLE_ROWS, stride=n), :] = value[:, c * LANES:(c + 1) * LANES]


def _sample_copies(state_hbm, spool_hbm, hist_ref, xbs_ref, sems):
    history = pltpu.make_async_copy(state_hbm, hist_ref, sems.at[0])
    shift = pltpu.make_async_copy(hist_ref.at[pl.ds(1, POOL_BUF - 1)],
                                  spool_hbm.at[pl.ds(0, POOL_BUF - 1)], sems.at[1])
    newest = pltpu.make_async_copy(xbs_ref, spool_hbm.at[POOL_BUF - 1], sems.at[2])
    return history, shift, newest


def _sample_rows(x_ref, p_ref, taps_ref, hist_ref, newest, xbs_ref, w, y_ref, vn_ref, pos0):
    x = _gather_rows(x_ref)
    xn = _rmsnorm(x, w.vec(VEC_PRE_G)).astype(_BF16)

    u = _gelu_tanh(w.in_proj(xn, COL_U))
    vn = _layernorm(_gelu_tanh(w.in_proj(xn, COL_V)), w.vec(VEC_LN_G), w.vec(VEC_LN_B))
    _scatter_rows(vn_ref, vn)
    heads = []
    for hd in range(N_HEADS):
        tap = taps_ref[hd * CHUNK:hd * CHUNK + 1, 0:1]
        heads.append(vn[:, hd * CHUNK:(hd + 1) * CHUNK] * tap + w.bias_t[0:1, hd:hd + 1])
    s = jnp.concatenate(heads, axis=1)
    y_a = u * s * _silu(w.in_proj(xn, COL_ZA))

    x_b = w.in_proj(xn, COL_XB)
    xbs_ref[...] = x_b
    newest.start()
    groups = []
    for g, win_len in enumerate(POOL_WINDOWS):
        lo, hi = g * POOL_GROUP, (g + 1) * POOL_GROUP
        cur = x_b[:, lo:hi]
        win = cur
        for k in range(1, win_len):
            win = win + hist_ref[POOL_BUF - k, :, lo:hi]
        groups.append(win / float(min(pos0 + 1, win_len)) - cur)
    pooled = jnp.concatenate(groups, axis=1)
    y_b = _pool_mix(pooled, w) * _silu(w.in_proj(xn, COL_ZB))

    _scatter_rows(y_ref, _merge_and_embed(x, _gather_rows(p_ref), xn, y_a, y_b, w))


def _pack_kernel(*refs, tril_periods):
    n = len(refs) // 2
    for src, dst, period in zip(refs[:n], refs[n:], tril_periods):
        m = src[...]
        if period is not None:
            row = lax.broadcasted_iota(jnp.int32, m.shape, 0) % period
            m = jnp.where(row >= lax.broadcasted_iota(jnp.int32, m.shape, 1), m, 0.0)
        dst[...] = pltpu.bitcast(m.astype(_BF16), jnp.uint32)


def _pack_bf16(mats, n_cols, tril_periods, n_steps):
    blocks = [m.shape[0] // n_steps for m in mats]
    for m, rows, cols, period in zip(mats, blocks, n_cols, tril_periods):
        assert rows * n_steps == m.shape[0] and rows % (2 * BF16_ROWS) == 0
        assert cols % LANES == 0 and cols <= m.shape[1]
        assert period is None or rows % period == 0
    return pl.pallas_call(
        functools.partial(_pack_kernel, tril_periods=tril_periods),
        grid=(n_steps,),
        in_specs=[pl.BlockSpec((rows, cols), lambda i: (i, 0)) for rows, cols in zip(blocks, n_cols)],
        out_specs=[pl.BlockSpec((rows // 2, cols), lambda i: (i, 0))
                   for rows, cols in zip(blocks, n_cols)],
        out_shape=[jax.ShapeDtypeStruct((m.shape[0] // 2, cols), jnp.uint32)
                   for m, cols in zip(mats, n_cols)],
        compiler_params=pltpu.CompilerParams(dimension_semantics=("arbitrary",),
                                             vmem_limit_bytes=VMEM_LIMIT_BYTES),
        name="pack_bf16",
    )(*mats)


def _resident(shape):
    zeros = (0,) * len(shape)
    return pl.BlockSpec(shape, lambda *_: zeros, pipeline_mode=pl.Buffered(1))


def _layer(h, p, hs, ps, state, taps, back_weights, front_operands, tile_m, pos0):
    batch, seq, _ = h.shape
    rows = state.shape[1]
    assert seq % tile_m == 0 and tile_m % CHUNK == 0 and seq >= POOL_BUF and rows == SAMPLE_ROWS
    assert D_MODEL % STAGE_ROWS == 0 and PLE_DIM % STAGE_ROWS == 0
    tiles_per_prompt = seq // tile_m
    n_tiles = batch * tiles_per_prompt
    assert n_tiles >= 2

    def front_tile(s):
        t = jnp.minimum(s, n_tiles - 1)
        return t // tiles_per_prompt, t % tiles_per_prompt, 0

    def back_tile(s):
        t = jnp.maximum(s - 1, 0)
        return t // tiles_per_prompt, t % tiles_per_prompt, 0

    per_prompt = lambda s: (jnp.minimum(s, n_tiles - 1) // tiles_per_prompt, 0, 0)
    hbm = pl.BlockSpec(memory_space=pl.ANY)
    whole = lambda a: pl.BlockSpec(a.shape, lambda s: (0,) * a.ndim)
    square = pltpu.VMEM((D_MODEL, D_MODEL), _BF16)
    return pl.pallas_call(
        functools.partial(_layer_kernel, tile_m=tile_m, tiles_per_prompt=tiles_per_prompt,
                          n_tiles=n_tiles, pos0=pos0),
        grid=(n_tiles + 1,),
        in_specs=[pl.BlockSpec((1, tile_m, D_MODEL), front_tile),
                  pl.BlockSpec((1, tile_m, D_MODEL), back_tile),
                  pl.BlockSpec((1, tile_m, PLE_DIM), back_tile),
                  _resident(hs.shape), _resident(ps.shape), _resident(taps.shape), hbm]
                 + [hbm] * len(back_weights)
                 + [_resident(a.shape) for a in front_operands],
        out_specs=[pl.BlockSpec((1, tile_m, D_MODEL), back_tile),
                   pl.BlockSpec((1, CHUNK, D_MODEL), per_prompt),
                   pl.BlockSpec((1, POOL_BUF, D_MODEL), per_prompt),
                   whole(hs), whole(hs), hbm],
        out_shape=[jax.ShapeDtypeStruct((batch, seq, D_MODEL), _F32),
                   jax.ShapeDtypeStruct((batch, CHUNK, D_MODEL), _F32),
                   jax.ShapeDtypeStruct((batch, POOL_BUF, D_MODEL), _F32),
                   jax.ShapeDtypeStruct(hs.shape, _F32),
                   jax.ShapeDtypeStruct(hs.shape, _F32),
                   jax.ShapeDtypeStruct(state.shape, _F32)],
        scratch_shapes=[pltpu.VMEM((N_LANE_BLOCKS, TAIL + tile_m, LANES), _F32),
                        pltpu.VMEM((tile_m, D_MODEL), _BF16),
                        pltpu.VMEM((tile_m, D_MODEL), _BF16),
                        pltpu.VMEM((tile_m, D_MODEL), _BF16),
                        pltpu.VMEM(state.shape, _F32),
                        pltpu.VMEM((rows, D_MODEL), _F32),
                        pltpu.VMEM((D_MODEL, 2 * D_MODEL), _BF16), square, square, square,
                        pltpu.VMEM((PLE_DIM, D_MODEL), _BF16), square,
                        pltpu.VMEM((N_STAGE_SLOTS, STAGE_ROWS, D_MODEL), _F32),
                        pltpu.SemaphoreType.DMA((3,)),
                        pltpu.SemaphoreType.DMA((N_STAGE_SLOTS,))],
        compiler_params=pltpu.CompilerParams(
            dimension_semantics=("arbitrary",),
            vmem_limit_bytes=VMEM_LIMIT_BYTES),
        name="layer",
    )(h, h, p, hs, ps, taps, state, *back_weights, *front_operands)


def kernel(x_prompt, x_sample, state_pool, p_prompt, p_sample, pre_g, w_in, ln_g, ln_b, w_s, b_s,
           w_pool, pool_scale, w_pa, w_pb, w_out, post_g, w_ple, w_pg, ple_in_g, ple_out_g):
    depth = w_in.shape[0]
    dec_batch, dec_seq, _ = x_sample.shape
    assert dec_seq == 1 and x_prompt.shape[2] == D_MODEL
    hp = x_prompt
    hs = x_sample.reshape(dec_batch * N_LANE_BLOCKS, LANES)
    pv, pp, sv, sp = [], [], [], []
    for i in range(depth):
        vecs = jnp.stack([pre_g[i], ln_g[i], ln_b[i], pool_scale[i], post_g[i], ple_in_g[i],
                          ple_out_g[i]] + [jnp.zeros_like(pre_g[i])] * (N_VEC_ROWS - 7))
        w_s_rows = w_s[i].reshape(D_MODEL, CHUNK)
        w_in_front, w_s_p, w_pool_p = _pack_bf16(
            [w_in[i], w_s_rows, w_pool[i].reshape(D_MODEL, POOL_GROUP)],
            n_cols=(N_FRONT_COLS * D_MODEL, CHUNK, POOL_GROUP),
            tril_periods=(None, CHUNK, None), n_steps=PACK_STEPS)
        hp, v, pool, hs, v_s, pool_s = _layer(
            hp, p_prompt[i], hs, p_sample[i].reshape(dec_batch * PLE_DIM // LANES, LANES),
            jnp.swapaxes(state_pool[i], 0, 1), w_s_rows,
            (w_in[i], w_pa[i], w_pb[i], w_out[i], w_ple[i], w_pg[i]),
            (w_in_front, w_s_p, w_pool_p, b_s[i].T, vecs), TILE_M, PAST_LEN)
        pv.append(v)
        pp.append(pool)
        sv.append(v_s.reshape(dec_batch, 1, D_MODEL))
        sp.append(jnp.swapaxes(pool_s, 0, 1))
    return (hp, hs.reshape(dec_batch, 1, D_MODEL), jnp.stack(pv), jnp.stack(pp), jnp.stack(sv),
            jnp.stack(sp))
```

```python
import functools
from typing import Any, NamedTuple

import jax
import jax.numpy as jnp
from jax import lax
from jax.experimental import pallas as pl
from jax.experimental.pallas import tpu as pltpu

D_MODEL = 1024
CHUNK = 128
N_HEADS = D_MODEL // CHUNK
POOL_WINDOWS = (2, 4, 8, 16)
POOL_GROUP = D_MODEL // len(POOL_WINDOWS)
POOL_BUF = max(POOL_WINDOWS) - 1
PLE_DIM = 256
PAST_LEN = 16384
SAMPLE_ROWS = 128
EPS = 1e-6
COL_U, COL_V, COL_ZA, COL_XB, COL_ZB, COL_GA, COL_GB = range(7)
VEC_PRE_G, VEC_LN_G, VEC_LN_B, VEC_POOL_SCALE, VEC_POST_G, VEC_PLE_IN_G, VEC_PLE_OUT_G = range(7)
N_VEC_ROWS = 8

LANES = 128
N_LANE_BLOCKS = D_MODEL // LANES
BF16_ROWS = 16
TAIL = 16
PACK_STEPS = 8
TILE_M = 256
VMEM_LIMIT_BYTES = 56 * 1024 * 1024

_BF16 = jnp.bfloat16
_F32 = jnp.float32
_GELU_C = 0.7978845608028654


def _dot(a, b):
    return jnp.dot(a, b, preferred_element_type=_F32)


def _rmsnorm(x, g):
    return x * lax.rsqrt(jnp.mean(x * x, axis=-1, keepdims=True) + EPS) * g


def _layernorm(x, g, b):
    xc = x - jnp.mean(x, axis=-1, keepdims=True)
    return xc * lax.rsqrt(jnp.mean(xc * xc, axis=-1, keepdims=True) + EPS) * g + b


def _gelu_tanh(x):
    return 0.5 * x * (1.0 + jnp.tanh(_GELU_C * (x + 0.044715 * (x * x * x))))


def _sigmoid(x):
    return 0.5 * jnp.tanh(0.5 * x) + 0.5


def _silu(x):
    return x * _sigmoid(x)


def _unpack(ref, rows=slice(None), cols=slice(None)):
    return pltpu.bitcast(ref[rows, cols], _BF16)


class _Weights(NamedTuple):
    in_proj_w: Any
    spatial: Any
    pool: Any
    pa: Any
    pb: Any
    out: Any
    ple: Any
    pg: Any
    bias_t: Any
    vecs: Any

    def vec(self, row):
        return self.vecs[row:row + 1, :]

    def in_proj(self, xn, col):
        return _dot(xn, _unpack(self.in_proj_w, cols=slice(col * D_MODEL, (col + 1) * D_MODEL)))


def _pool_mix(pooled, w):
    parts = []
    for g in range(len(POOL_WINDOWS)):
        q = pooled[:, g * POOL_GROUP:(g + 1) * POOL_GROUP].astype(_BF16)
        rows = slice(g * POOL_GROUP // 2, (g + 1) * POOL_GROUP // 2)
        parts.append(_dot(q, _unpack(w.pool, rows=rows)))
    return jnp.concatenate(parts, axis=1) * w.vec(VEC_POOL_SCALE)


def _merge_and_embed(h, p, xn, y_a, y_b, w):
    g_a = w.in_proj(xn, COL_GA)
    g_b = w.in_proj(xn, COL_GB)
    m = (_sigmoid(g_a) * _dot(y_a.astype(_BF16), _unpack(w.pa))
         + _sigmoid(g_b) * _dot(y_b.astype(_BF16), _unpack(w.pb)))
    h = h + _rmsnorm(_dot(m.astype(_BF16), _unpack(w.out)), w.vec(VEC_POST_G))
    e = _dot(p.astype(_BF16), _unpack(w.ple))
    hn = _rmsnorm(h, w.vec(VEC_PLE_IN_G)).astype(_BF16)
    gate = _sigmoid(_dot(hn, _unpack(w.pg)))
    return h + _rmsnorm(gate * e, w.vec(VEC_PLE_OUT_G))


def _spatial_mix(vn_b, w, tile_m):
    w_tril = [_unpack(w.spatial, rows=slice(hd * CHUNK // 2, (hd + 1) * CHUNK // 2))
              for hd in range(N_HEADS)]
    bias = [w.bias_t[:, hd:hd + 1] for hd in range(N_HEADS)]
    s_rows = []
    for c in range(tile_m // CHUNK):
        heads = []
        for hd in range(N_HEADS):
            blk = vn_b[c * CHUNK:(c + 1) * CHUNK, hd * CHUNK:(hd + 1) * CHUNK]
            heads.append(_dot(w_tril[hd], blk) + bias[hd])
        s_rows.append(jnp.concatenate(heads, axis=1))
    return jnp.concatenate(s_rows, axis=0)


def _window_pool(x_b, ext_ref, pool_ref, j, tile_m):
    for c in range(N_LANE_BLOCKS):
        ext_ref[c, TAIL:TAIL + tile_m, :] = x_b[:, c * LANES:(c + 1) * LANES]
    pos = j * tile_m + lax.broadcasted_iota(jnp.int32, (tile_m, 1), 0)
    blocks = []
    for c in range(N_LANE_BLOCKS):
        w = POOL_WINDOWS[c * LANES // POOL_GROUP]
        cur = x_b[:, c * LANES:(c + 1) * LANES]
        win = cur
        for k in range(1, w):
            win = win + ext_ref[c, TAIL - k:TAIL - k + tile_m, :]
        cnt = jnp.minimum(pos + 1, w).astype(_F32)
        blocks.append(win / cnt - cur)
    for c in range(N_LANE_BLOCKS):
        pool_ref[0, :, c * LANES:(c + 1) * LANES] = (
            ext_ref[c, TAIL + tile_m - POOL_BUF:TAIL + tile_m, :])
        ext_ref[c, 0:TAIL, :] = ext_ref[c, tile_m:tile_m + TAIL, :]
    return jnp.concatenate(blocks, axis=1)


def _layer_kernel(x_ref, x_lag_ref, p_lag_ref, xs_ref, ps_ref, taps_ref, state_hbm,
                  w_in_ref, w_s_ref, w_pool_ref, w_pa_ref, w_pb_ref, w_out_ref, w_ple_ref,
                  w_pg_ref, bs_t_ref, vecs_ref,
                  y_ref, vn_ref, pool_ref, ys_ref, vns_ref, spool_hbm,
                  ext_ref, xn_c, ya_c, yb_c, hist_ref, xbs_ref, sems,
                  *, tile_m, tiles_per_prompt, n_tiles, pos0):
    s_id = pl.program_id(0)
    j = lax.rem(jnp.minimum(s_id, n_tiles - 1), tiles_per_prompt)
    w = _Weights(in_proj_w=w_in_ref, spatial=w_s_ref, pool=w_pool_ref, pa=w_pa_ref, pb=w_pb_ref,
                 out=w_out_ref, ple=w_ple_ref, pg=w_pg_ref, bias_t=bs_t_ref, vecs=vecs_ref)

    @pl.when(j == 0)
    def _():
        ext_ref[:, 0:TAIL, :] = jnp.zeros((N_LANE_BLOCKS, TAIL, LANES), _F32)

    @pl.when(s_id == 0)
    def _front_only():
        xn = _rmsnorm(x_ref[0], w.vec(VEC_PRE_G)).astype(_BF16)
        xn_c[...] = xn
        vn = _layernorm(_gelu_tanh(w.in_proj(xn, COL_V)), w.vec(VEC_LN_G), w.vec(VEC_LN_B))
        vn_ref[0] = vn[tile_m - CHUNK:, :]
        s = _spatial_mix(vn.astype(_BF16), w, tile_m)
        ya_c[...] = (_gelu_tanh(w.in_proj(xn, COL_U)) * s
                     * _silu(w.in_proj(xn, COL_ZA))).astype(_BF16)
        pooled = _window_pool(w.in_proj(xn, COL_XB), ext_ref, pool_ref, j, tile_m)
        yb_c[...] = (_pool_mix(pooled, w) * _silu(w.in_proj(xn, COL_ZB))).astype(_BF16)

    @pl.when(s_id == n_tiles)
    def _back_only_and_sample():
        history, shift, newest = _sample_copies(state_hbm, spool_hbm, hist_ref, xbs_ref, sems)
        history.start()
        y_ref[0] = _merge_and_embed(x_lag_ref[0], p_lag_ref[0], xn_c[...], ya_c[...], yb_c[...], w)
        history.wait()
        shift.start()
        _sample_rows(xs_ref, ps_ref, taps_ref, hist_ref, newest, xbs_ref, w, ys_ref, vns_ref, pos0)
        shift.wait()
        newest.wait()

    @pl.when(jnp.logical_and(s_id > 0, s_id < n_tiles))
    def _both():
        xn_prev = xn_c[...]
        g_a = w.in_proj(xn_prev, COL_GA)
        xn = _rmsnorm(x_ref[0], w.vec(VEC_PRE_G)).astype(_BF16)
        xn_c[...] = xn
        v_raw = w.in_proj(xn, COL_V)
        g_b = w.in_proj(xn_prev, COL_GB)
        vn = _layernorm(_gelu_tanh(v_raw), w.vec(VEC_LN_G), w.vec(VEC_LN_B))
        vn_ref[0] = vn[tile_m - CHUNK:, :]
        u_raw = w.in_proj(xn, COL_U)
        s = _spatial_mix(vn.astype(_BF16), w, tile_m)
        proj_a = _dot(ya_c[...], _unpack(w.pa))
        proj_b = _dot(yb_c[...], _unpack(w.pb))
        za_raw = w.in_proj(xn, COL_ZA)
        m = (_sigmoid(g_a) * proj_a + _sigmoid(g_b) * proj_b).astype(_BF16)
        o = _dot(m, _unpack(w.out))
        ya_c[...] = (_gelu_tanh(u_raw) * s * _silu(za_raw)).astype(_BF16)
        x_b = w.in_proj(xn, COL_XB)
        h = x_lag_ref[0] + _rmsnorm(o, w.vec(VEC_POST_G))
        hn = _rmsnorm(h, w.vec(VEC_PLE_IN_G)).astype(_BF16)
        gate_raw = _dot(hn, _unpack(w.pg))
        pooled = _window_pool(x_b, ext_ref, pool_ref, j, tile_m)
        e = _dot(p_lag_ref[0].astype(_BF16), _unpack(w.ple))
        mixed = _pool_mix(pooled, w)
        y_ref[0] = h + _rmsnorm(_sigmoid(gate_raw) * e, w.vec(VEC_PLE_OUT_G))
        zb_raw = w.in_proj(xn, COL_ZB)
        yb_c[...] = (mixed * _silu(zb_raw)).astype(_BF16)


def _gather_rows(ref):
    n = ref.shape[0] // SAMPLE_ROWS
    return jnp.concatenate([ref[pl.ds(c, SAMPLE_ROWS, stride=n), :] for c in range(n)], axis=1)


def _scatter_rows(ref, value):
    n = ref.shape[0] // SAMPLE_ROWS
    for c in range(n):
        ref[pl.ds(c, SAMPLE_ROWS, stride=n), :] = value[:, c * LANES:(c + 1) * LANES]


def _sample_copies(state_hbm, spool_hbm, hist_ref, xbs_ref, sems):
    history = pltpu.make_async_copy(state_hbm, hist_ref, sems.at[0])
    shift = pltpu.make_async_copy(hist_ref.at[pl.ds(1, POOL_BUF - 1)],
                                  spool_hbm.at[pl.ds(0, POOL_BUF - 1)], sems.at[1])
    newest = pltpu.make_async_copy(xbs_ref, spool_hbm.at[POOL_BUF - 1], sems.at[2])
    return history, shift, newest


def _sample_rows(x_ref, p_ref, taps_ref, hist_ref, newest, xbs_ref, w, y_ref, vn_ref, pos0):
    x = _gather_rows(x_ref)
    xn = _rmsnorm(x, w.vec(VEC_PRE_G)).astype(_BF16)

    u = _gelu_tanh(w.in_proj(xn, COL_U))
    vn = _layernorm(_gelu_tanh(w.in_proj(xn, COL_V)), w.vec(VEC_LN_G), w.vec(VEC_LN_B))
    _scatter_rows(vn_ref, vn)
    heads = []
    for hd in range(N_HEADS):
        tap = taps_ref[hd * CHUNK:hd * CHUNK + 1, 0:1]
        heads.append(vn[:, hd * CHUNK:(hd + 1) * CHUNK] * tap + w.bias_t[0:1, hd:hd + 1])
    s = jnp.concatenate(heads, axis=1)
    y_a = u * s * _silu(w.in_proj(xn, COL_ZA))

    x_b = w.in_proj(xn, COL_XB)
    xbs_ref[...] = x_b
    newest.start()
    groups = []
    for g, win_len in enumerate(POOL_WINDOWS):
        lo, hi = g * POOL_GROUP, (g + 1) * POOL_GROUP
        cur = x_b[:, lo:hi]
        win = cur
        for k in range(1, win_len):
            win = win + hist_ref[POOL_BUF - k, :, lo:hi]
        groups.append(win / float(min(pos0 + 1, win_len)) - cur)
    pooled = jnp.concatenate(groups, axis=1)
    y_b = _pool_mix(pooled, w) * _silu(w.in_proj(xn, COL_ZB))

    _scatter_rows(y_ref, _merge_and_embed(x, _gather_rows(p_ref), xn, y_a, y_b, w))


def _pack_kernel(*refs, tril_periods):
    n = len(refs) // 2
    for src, dst, period in zip(refs[:n], refs[n:], tril_periods):
        m = src[...]
        if period is not None:
            row = lax.broadcasted_iota(jnp.int32, m.shape, 0) % period
            m = jnp.where(row >= lax.broadcasted_iota(jnp.int32, m.shape, 1), m, 0.0)
        dst[...] = pltpu.bitcast(m.astype(_BF16), jnp.uint32)


def _pack_bf16(mats, tril_periods, n_steps):
    blocks = [m.shape[0] // n_steps for m in mats]
    for m, rows, period in zip(mats, blocks, tril_periods):
        assert rows * n_steps == m.shape[0] and rows % (2 * BF16_ROWS) == 0
        assert period is None or rows % period == 0
    return pl.pallas_call(
        functools.partial(_pack_kernel, tril_periods=tril_periods),
        grid=(n_steps,),
        in_specs=[pl.BlockSpec((rows, m.shape[1]), lambda i: (i, 0))
                  for m, rows in zip(mats, blocks)],
        out_specs=[pl.BlockSpec((rows // 2, m.shape[1]), lambda i: (i, 0))
                   for m, rows in zip(mats, blocks)],
        out_shape=[jax.ShapeDtypeStruct((m.shape[0] // 2, m.shape[1]), jnp.uint32) for m in mats],
        compiler_params=pltpu.CompilerParams(dimension_semantics=("arbitrary",),
                                             vmem_limit_bytes=VMEM_LIMIT_BYTES),
        name="pack_bf16",
    )(*mats)


def _resident(shape):
    zeros = (0,) * len(shape)
    return pl.BlockSpec(shape, lambda *_: zeros, pipeline_mode=pl.Buffered(1))


def _layer(h, p, hs, ps, state, taps, weights, tile_m, pos0):
    batch, seq, _ = h.shape
    rows = state.shape[1]
    assert seq % tile_m == 0 and tile_m % CHUNK == 0 and seq >= POOL_BUF and rows == SAMPLE_ROWS
    tiles_per_prompt = seq // tile_m
    n_tiles = batch * tiles_per_prompt
    assert n_tiles >= 2

    def front_tile(s):
        t = jnp.minimum(s, n_tiles - 1)
        return t // tiles_per_prompt, t % tiles_per_prompt, 0

    def back_tile(s):
        t = jnp.maximum(s - 1, 0)
        return t // tiles_per_prompt, t % tiles_per_prompt, 0

    per_prompt = lambda s: (jnp.minimum(s, n_tiles - 1) // tiles_per_prompt, 0, 0)
    hbm = pl.BlockSpec(memory_space=pl.ANY)
    whole = lambda a: pl.BlockSpec(a.shape, lambda s: (0,) * a.ndim)
    return pl.pallas_call(
        functools.partial(_layer_kernel, tile_m=tile_m, tiles_per_prompt=tiles_per_prompt,
                          n_tiles=n_tiles, pos0=pos0),
        grid=(n_tiles + 1,),
        in_specs=[pl.BlockSpec((1, tile_m, D_MODEL), front_tile),
                  pl.BlockSpec((1, tile_m, D_MODEL), back_tile),
                  pl.BlockSpec((1, tile_m, PLE_DIM), back_tile),
                  _resident(hs.shape), _resident(ps.shape), _resident(taps.shape), hbm]
                 + [_resident(a.shape) for a in weights],
        out_specs=[pl.BlockSpec((1, tile_m, D_MODEL), back_tile),
                   pl.BlockSpec((1, CHUNK, D_MODEL), per_prompt),
                   pl.BlockSpec((1, POOL_BUF, D_MODEL), per_prompt),
                   whole(hs), whole(hs), hbm],
        out_shape=[jax.ShapeDtypeStruct((batch, seq, D_MODEL), _F32),
                   jax.ShapeDtypeStruct((batch, CHUNK, D_MODEL), _F32),
                   jax.ShapeDtypeStruct((batch, POOL_BUF, D_MODEL), _F32),
                   jax.ShapeDtypeStruct(hs.shape, _F32),
                   jax.ShapeDtypeStruct(hs.shape, _F32),
                   jax.ShapeDtypeStruct(state.shape, _F32)],
        scratch_shapes=[pltpu.VMEM((N_LANE_BLOCKS, TAIL + tile_m, LANES), _F32),
                        pltpu.VMEM((tile_m, D_MODEL), _BF16),
                        pltpu.VMEM((tile_m, D_MODEL), _BF16),
                        pltpu.VMEM((tile_m, D_MODEL), _BF16),
                        pltpu.VMEM(state.shape, _F32),
                        pltpu.VMEM((rows, D_MODEL), _F32),
                        pltpu.SemaphoreType.DMA((3,))],
        compiler_params=pltpu.CompilerParams(
            dimension_semantics=("arbitrary",),
            vmem_limit_bytes=VMEM_LIMIT_BYTES),
        name="layer",
    )(h, h, p, hs, ps, taps, state, *weights)


def kernel(x_prompt, x_sample, state_pool, p_prompt, p_sample, pre_g, w_in, ln_g, ln_b, w_s, b_s,
           w_pool, pool_scale, w_pa, w_pb, w_out, post_g, w_ple, w_pg, ple_in_g, ple_out_g):
    depth = w_in.shape[0]
    dec_batch, dec_seq, _ = x_sample.shape
    assert dec_seq == 1 and x_prompt.shape[2] == D_MODEL
    hp = x_prompt
    hs = x_sample.reshape(dec_batch * N_LANE_BLOCKS, LANES)
    pv, pp, sv, sp = [], [], [], []
    for i in range(depth):
        vecs = jnp.stack([pre_g[i], ln_g[i], ln_b[i], pool_scale[i], post_g[i], ple_in_g[i],
                          ple_out_g[i]] + [jnp.zeros_like(pre_g[i])] * (N_VEC_ROWS - 7))
        w_s_rows = w_s[i].reshape(D_MODEL, CHUNK)
        packed = _pack_bf16(
            [w_in[i], w_s_rows, w_pool[i].reshape(D_MODEL, POOL_GROUP), w_pa[i], w_pb[i],
             w_out[i], w_ple[i], w_pg[i]],
            tril_periods=(None, CHUNK) + (None,) * 6, n_steps=PACK_STEPS)
        hp, v, pool, hs, v_s, pool_s = _layer(
            hp, p_prompt[i], hs, p_sample[i].reshape(dec_batch * PLE_DIM // LANES, LANES),
            jnp.swapaxes(state_pool[i], 0, 1), w_s_rows, (*packed, b_s[i].T, vecs), TILE_M,
            PAST_LEN)
        pv.append(v)
        pp.append(pool)
        sv.append(v_s.reshape(dec_batch, 1, D_MODEL))
        sp.append(jnp.swapaxes(pool_s, 0, 1))
    return (hp, hs.reshape(dec_batch, 1, D_MODEL), jnp.stack(pv), jnp.stack(pp), jnp.stack(sv),
            jnp.stack(sp))
```

```python
import functools
from typing import Any, NamedTuple

import jax
import jax.numpy as jnp
from jax import lax
from jax.experimental import pallas as pl
from jax.experimental.pallas import tpu as pltpu

D_MODEL = 1024
CHUNK = 128
N_HEADS = D_MODEL // CHUNK
POOL_WINDOWS = (2, 4, 8, 16)
POOL_GROUP = D_MODEL // len(POOL_WINDOWS)
POOL_BUF = max(POOL_WINDOWS) - 1
PLE_DIM = 256
PAST_LEN = 16384
SAMPLE_ROWS = 128
EPS = 1e-6
COL_U, COL_V, COL_ZA, COL_XB, COL_ZB, COL_GA, COL_GB = range(7)
VEC_ROWS = tuple(range(7))
VEC_PRE_G, VEC_LN_G, VEC_LN_B, VEC_POOL_SCALE, VEC_POST_G, VEC_PLE_IN_G, VEC_PLE_OUT_G = VEC_ROWS
N_VEC_ROWS = 8

LANES = 128
N_LANE_BLOCKS = D_MODEL // LANES
BF16_ROWS = 16
TAIL = 16
PACK_STEPS = 8
TILE_M = 256
VMEM_LIMIT_BYTES = 56 * 1024 * 1024

_BF16 = jnp.bfloat16
_F32 = jnp.float32
_GELU_C = 0.7978845608028654


def _dot(a, b):
    return jnp.dot(a, b, preferred_element_type=_F32)


def _rmsnorm(x, g):
    return x * lax.rsqrt(jnp.mean(x * x, axis=-1, keepdims=True) + EPS) * g


def _layernorm(x, g, b):
    xc = x - jnp.mean(x, axis=-1, keepdims=True)
    return xc * lax.rsqrt(jnp.mean(xc * xc, axis=-1, keepdims=True) + EPS) * g + b


def _gelu_tanh(x):
    return 0.5 * x * (1.0 + jnp.tanh(_GELU_C * (x + 0.044715 * (x * x * x))))


def _sigmoid(x):
    return 0.5 * jnp.tanh(0.5 * x) + 0.5


def _silu(x):
    return x * _sigmoid(x)


def _unpack(ref, rows=slice(None), cols=slice(None)):
    return pltpu.bitcast(ref[rows, cols], _BF16)


class _Weights(NamedTuple):
    in_proj_w: Any
    spatial: Any
    pool: Any
    pa: Any
    pb: Any
    out: Any
    ple: Any
    pg: Any
    bias_t: Any
    vecs: Any

    def vec(self, row):
        return self.vecs[row:row + 1, :]

    def in_proj(self, xn, col):
        return _dot(xn, _unpack(self.in_proj_w, cols=slice(col * D_MODEL, (col + 1) * D_MODEL)))


def _pool_mix(pooled, w):
    parts = []
    for g in range(len(POOL_WINDOWS)):
        q = pooled[:, g * POOL_GROUP:(g + 1) * POOL_GROUP].astype(_BF16)
        rows = slice(g * POOL_GROUP // 2, (g + 1) * POOL_GROUP // 2)
        parts.append(_dot(q, _unpack(w.pool, rows=rows)))
    return jnp.concatenate(parts, axis=1) * w.vec(VEC_POOL_SCALE)


def _merge_and_embed(h, p, xn, y_a, y_b, w):
    g_a = w.in_proj(xn, COL_GA)
    g_b = w.in_proj(xn, COL_GB)
    m = (_sigmoid(g_a) * _dot(y_a.astype(_BF16), _unpack(w.pa))
         + _sigmoid(g_b) * _dot(y_b.astype(_BF16), _unpack(w.pb)))
    h = h + _rmsnorm(_dot(m.astype(_BF16), _unpack(w.out)), w.vec(VEC_POST_G))
    e = _dot(p.astype(_BF16), _unpack(w.ple))
    hn = _rmsnorm(h, w.vec(VEC_PLE_IN_G)).astype(_BF16)
    gate = _sigmoid(_dot(hn, _unpack(w.pg)))
    return h + _rmsnorm(gate * e, w.vec(VEC_PLE_OUT_G))


def _spatial_mix(vn_b, w, tile_m):
    w_tril = [_unpack(w.spatial, rows=slice(hd * CHUNK // 2, (hd + 1) * CHUNK // 2))
              for hd in range(N_HEADS)]
    bias = [w.bias_t[:, hd:hd + 1] for hd in range(N_HEADS)]
    s_rows = []
    for c in range(tile_m // CHUNK):
        heads = []
        for hd in range(N_HEADS):
            blk = vn_b[c * CHUNK:(c + 1) * CHUNK, hd * CHUNK:(hd + 1) * CHUNK]
            heads.append(_dot(w_tril[hd], blk) + bias[hd])
        s_rows.append(jnp.concatenate(heads, axis=1))
    return jnp.concatenate(s_rows, axis=0)


def _window_pool(x_b, ext_ref, pool_ref, b, j, tile_m):
    for c in range(N_LANE_BLOCKS):
        ext_ref[c, TAIL:TAIL + tile_m, :] = x_b[:, c * LANES:(c + 1) * LANES]
    pos = j * tile_m + lax.broadcasted_iota(jnp.int32, (tile_m, 1), 0)
    blocks = []
    for c in range(N_LANE_BLOCKS):
        w = POOL_WINDOWS[c * LANES // POOL_GROUP]
        cur = x_b[:, c * LANES:(c + 1) * LANES]
        win = cur
        for k in range(1, w):
            win = win + ext_ref[c, TAIL - k:TAIL - k + tile_m, :]
        cnt = jnp.minimum(pos + 1, w).astype(_F32)
        blocks.append(win / cnt - cur)
    latest = jnp.concatenate([ext_ref[c, TAIL + tile_m - POOL_BUF:TAIL + tile_m, :]
                              for c in range(N_LANE_BLOCKS)], axis=1)
    pool_ref[:, pl.ds(b, 1), :] = latest[:, None, :]
    for c in range(N_LANE_BLOCKS):
        ext_ref[c, 0:TAIL, :] = ext_ref[c, tile_m:tile_m + TAIL, :]
    return jnp.concatenate(blocks, axis=1)


def _layer_kernel(x_ref, x_lag_ref, p_lag_ref, xs_ref, ps_ref, taps_ref, state_hbm,
                  w_in_ref, w_s_ref, w_pool_ref, w_pa_ref, w_pb_ref, w_out_ref, w_ple_ref,
                  w_pg_ref, bs_t_ref, vecs_ref,
                  y_ref, vn_ref, pool_ref, ys_ref, vns_ref, spool_hbm,
                  ext_ref, xn_c, ya_c, yb_c, hist_ref, xbs_ref, sems,
                  *, tile_m, tiles_per_prompt, n_tiles, pos0):
    s_id = pl.program_id(0)
    front = jnp.minimum(s_id, n_tiles - 1)
    b = front // tiles_per_prompt
    j = lax.rem(front, tiles_per_prompt)
    w = _Weights(in_proj_w=w_in_ref, spatial=w_s_ref, pool=w_pool_ref, pa=w_pa_ref, pb=w_pb_ref,
                 out=w_out_ref, ple=w_ple_ref, pg=w_pg_ref, bias_t=bs_t_ref, vecs=vecs_ref)

    @pl.when(j == 0)
    def _():
        ext_ref[:, 0:TAIL, :] = jnp.zeros((N_LANE_BLOCKS, TAIL, LANES), _F32)

    @pl.when(s_id == 0)
    def _front_only():
        xn = _rmsnorm(x_ref[0], w.vec(VEC_PRE_G)).astype(_BF16)
        xn_c[...] = xn
        vn = _layernorm(_gelu_tanh(w.in_proj(xn, COL_V)), w.vec(VEC_LN_G), w.vec(VEC_LN_B))
        vn_ref[0] = vn[tile_m - CHUNK:, :]
        s = _spatial_mix(vn.astype(_BF16), w, tile_m)
        ya_c[...] = (_gelu_tanh(w.in_proj(xn, COL_U)) * s
                     * _silu(w.in_proj(xn, COL_ZA))).astype(_BF16)
        pooled = _window_pool(w.in_proj(xn, COL_XB), ext_ref, pool_ref, b, j, tile_m)
        yb_c[...] = (_pool_mix(pooled, w) * _silu(w.in_proj(xn, COL_ZB))).astype(_BF16)

    @pl.when(s_id == n_tiles)
    def _back_only_and_sample():
        history, shift, newest = _sample_copies(state_hbm, spool_hbm, hist_ref, xbs_ref, sems)
        history.start()
        y_ref[0] = _merge_and_embed(x_lag_ref[0], p_lag_ref[0], xn_c[...], ya_c[...], yb_c[...], w)
        history.wait()
        shift.start()
        _sample_rows(xs_ref, ps_ref, taps_ref, hist_ref, newest, xbs_ref, w, ys_ref, vns_ref, pos0)
        shift.wait()
        newest.wait()

    @pl.when(jnp.logical_and(s_id > 0, s_id < n_tiles))
    def _both():
        xn_prev = xn_c[...]
        g_a = w.in_proj(xn_prev, COL_GA)
        xn = _rmsnorm(x_ref[0], w.vec(VEC_PRE_G)).astype(_BF16)
        xn_c[...] = xn
        v_raw = w.in_proj(xn, COL_V)
        g_b = w.in_proj(xn_prev, COL_GB)
        vn = _layernorm(_gelu_tanh(v_raw), w.vec(VEC_LN_G), w.vec(VEC_LN_B))
        vn_ref[0] = vn[tile_m - CHUNK:, :]
        u_raw = w.in_proj(xn, COL_U)
        s = _spatial_mix(vn.astype(_BF16), w, tile_m)
        proj_a = _dot(ya_c[...], _unpack(w.pa))
        proj_b = _dot(yb_c[...], _unpack(w.pb))
        za_raw = w.in_proj(xn, COL_ZA)
        m = (_sigmoid(g_a) * proj_a + _sigmoid(g_b) * proj_b).astype(_BF16)
        o = _dot(m, _unpack(w.out))
        ya_c[...] = (_gelu_tanh(u_raw) * s * _silu(za_raw)).astype(_BF16)
        x_b = w.in_proj(xn, COL_XB)
        h = x_lag_ref[0] + _rmsnorm(o, w.vec(VEC_POST_G))
        hn = _rmsnorm(h, w.vec(VEC_PLE_IN_G)).astype(_BF16)
        gate_raw = _dot(hn, _unpack(w.pg))
        pooled = _window_pool(x_b, ext_ref, pool_ref, b, j, tile_m)
        e = _dot(p_lag_ref[0].astype(_BF16), _unpack(w.ple))
        mixed = _pool_mix(pooled, w)
        y_ref[0] = h + _rmsnorm(_sigmoid(gate_raw) * e, w.vec(VEC_PLE_OUT_G))
        zb_raw = w.in_proj(xn, COL_ZB)
        yb_c[...] = (mixed * _silu(zb_raw)).astype(_BF16)


def _gather_rows(ref):
    n = ref.shape[0] // SAMPLE_ROWS
    return jnp.concatenate([ref[pl.ds(c, SAMPLE_ROWS, stride=n), :] for c in range(n)], axis=1)


def _scatter_rows(ref, value):
    n = ref.shape[0] // SAMPLE_ROWS
    for c in range(n):
        ref[pl.ds(c, SAMPLE_ROWS, stride=n), :] = value[:, c * LANES:(c + 1) * LANES]


def _sample_copies(state_hbm, spool_hbm, hist_ref, xbs_ref, sems):
    history = pltpu.make_async_copy(state_hbm, hist_ref, sems.at[0])
    shift = pltpu.make_async_copy(hist_ref.at[pl.ds(1, POOL_BUF - 1)],
                                  spool_hbm.at[pl.ds(0, POOL_BUF - 1)], sems.at[1])
    newest = pltpu.make_async_copy(xbs_ref, spool_hbm.at[POOL_BUF - 1], sems.at[2])
    return history, shift, newest


def _sample_rows(x_ref, p_ref, taps_ref, hist_ref, newest, xbs_ref, w, y_ref, vn_ref, pos0):
    x = _gather_rows(x_ref)
    xn = _rmsnorm(x, w.vec(VEC_PRE_G)).astype(_BF16)

    u = _gelu_tanh(w.in_proj(xn, COL_U))
    vn = _layernorm(_gelu_tanh(w.in_proj(xn, COL_V)), w.vec(VEC_LN_G), w.vec(VEC_LN_B))
    _scatter_rows(vn_ref, vn)
    heads = []
    for hd in range(N_HEADS):
        tap = taps_ref[hd * CHUNK:hd * CHUNK + 1, 0:1]
        heads.append(vn[:, hd * CHUNK:(hd + 1) * CHUNK] * tap + w.bias_t[0:1, hd:hd + 1])
    s = jnp.concatenate(heads, axis=1)
    y_a = u * s * _silu(w.in_proj(xn, COL_ZA))

    x_b = w.in_proj(xn, COL_XB)
    xbs_ref[...] = x_b
    newest.start()
    groups = []
    for g, win_len in enumerate(POOL_WINDOWS):
        lo, hi = g * POOL_GROUP, (g + 1) * POOL_GROUP
        cur = x_b[:, lo:hi]
        win = cur
        for k in range(1, win_len):
            win = win + hist_ref[POOL_BUF - k, :, lo:hi]
        groups.append(win / float(min(pos0 + 1, win_len)) - cur)
    pooled = jnp.concatenate(groups, axis=1)
    y_b = _pool_mix(pooled, w) * _silu(w.in_proj(xn, COL_ZB))

    _scatter_rows(y_ref, _merge_and_embed(x, _gather_rows(p_ref), xn, y_a, y_b, w))


def _pack_kernel(*refs, tril_periods):
    n = len(tril_periods)
    n_vecs = len(VEC_ROWS)
    bias_ref, vec_refs = refs[n], refs[n + 1:n + 1 + n_vecs]
    outs = refs[n + 1 + n_vecs:]
    bias_t_ref, vecs_ref = outs[n], outs[n + 1]
    bias_t_ref[...] = bias_ref[...].T
    for row, vec_ref in zip(VEC_ROWS, vec_refs):
        vecs_ref[row:row + 1, :] = vec_ref[...]
    vecs_ref[n_vecs:, :] = jnp.zeros((N_VEC_ROWS - n_vecs, D_MODEL), _F32)
    for src, dst, period in zip(refs[:n], outs[:n], tril_periods):
        m = src[...]
        if period is not None:
            row = lax.broadcasted_iota(jnp.int32, m.shape, 0) % period
            m = jnp.where(row >= lax.broadcasted_iota(jnp.int32, m.shape, 1), m, 0.0)
        dst[...] = pltpu.bitcast(m.astype(_BF16), jnp.uint32)


def _pack_params(mats, tril_periods, bias, vecs, n_steps):
    blocks = [m.shape[0] // n_steps for m in mats]
    small = lambda shape: pl.BlockSpec(shape, lambda i: (0, 0))
    for m, rows, period in zip(mats, blocks, tril_periods):
        assert rows * n_steps == m.shape[0] and rows % (2 * BF16_ROWS) == 0
        assert period is None or rows % period == 0
    return pl.pallas_call(
        functools.partial(_pack_kernel, tril_periods=tril_periods),
        grid=(n_steps,),
        in_specs=[pl.BlockSpec((rows, m.shape[1]), lambda i: (i, 0))
                  for m, rows in zip(mats, blocks)]
                 + [small(bias.shape)] + [small(v.shape) for v in vecs],
        out_specs=[pl.BlockSpec((rows // 2, m.shape[1]), lambda i: (i, 0))
                   for m, rows in zip(mats, blocks)]
                  + [small(bias.shape[::-1]), small((N_VEC_ROWS, D_MODEL))],
        out_shape=[jax.ShapeDtypeStruct((m.shape[0] // 2, m.shape[1]), jnp.uint32) for m in mats]
                  + [jax.ShapeDtypeStruct(bias.shape[::-1], _F32),
                     jax.ShapeDtypeStruct((N_VEC_ROWS, D_MODEL), _F32)],
        compiler_params=pltpu.CompilerParams(dimension_semantics=("arbitrary",),
                                             vmem_limit_bytes=VMEM_LIMIT_BYTES),
        name="pack_params",
    )(*mats, bias, *vecs)


def _resident(shape):
    zeros = (0,) * len(shape)
    return pl.BlockSpec(shape, lambda *_: zeros, pipeline_mode=pl.Buffered(1))


def _layer(h, p, hs, ps, state, taps, weights, tile_m, pos0):
    batch, seq, _ = h.shape
    rows = state.shape[1]
    assert seq % tile_m == 0 and tile_m % CHUNK == 0 and seq >= POOL_BUF and rows == SAMPLE_ROWS
    tiles_per_prompt = seq // tile_m
    n_tiles = batch * tiles_per_prompt
    assert n_tiles >= 2

    def front_tile(s):
        t = jnp.minimum(s, n_tiles - 1)
        return t // tiles_per_prompt, t % tiles_per_prompt, 0

    def back_tile(s):
        t = jnp.maximum(s - 1, 0)
        return t // tiles_per_prompt, t % tiles_per_prompt, 0

    per_prompt = lambda s: (jnp.minimum(s, n_tiles - 1) // tiles_per_prompt, 0, 0)
    hbm = pl.BlockSpec(memory_space=pl.ANY)
    whole = lambda a: pl.BlockSpec(a.shape, lambda s: (0,) * a.ndim)
    return pl.pallas_call(
        functools.partial(_layer_kernel, tile_m=tile_m, tiles_per_prompt=tiles_per_prompt,
                          n_tiles=n_tiles, pos0=pos0),
        grid=(n_tiles + 1,),
        in_specs=[pl.BlockSpec((1, tile_m, D_MODEL), front_tile),
                  pl.BlockSpec((1, tile_m, D_MODEL), back_tile),
                  pl.BlockSpec((1, tile_m, PLE_DIM), back_tile),
                  _resident(hs.shape), _resident(ps.shape), _resident(taps.shape), hbm]
                 + [_resident(a.shape) for a in weights],
        out_specs=[pl.BlockSpec((1, tile_m, D_MODEL), back_tile),
                   pl.BlockSpec((1, CHUNK, D_MODEL), per_prompt),
                   pl.BlockSpec((POOL_BUF, batch, D_MODEL), lambda s: (0, 0, 0)),
                   whole(hs), whole(hs), hbm],
        out_shape=[jax.ShapeDtypeStruct((batch, seq, D_MODEL), _F32),
                   jax.ShapeDtypeStruct((batch, CHUNK, D_MODEL), _F32),
                   jax.ShapeDtypeStruct((POOL_BUF, batch, D_MODEL), _F32),
                   jax.ShapeDtypeStruct(hs.shape, _F32),
                   jax.ShapeDtypeStruct(hs.shape, _F32),
                   jax.ShapeDtypeStruct(state.shape, _F32)],
        scratch_shapes=[pltpu.VMEM((N_LANE_BLOCKS, TAIL + tile_m, LANES), _F32),
                        pltpu.VMEM((tile_m, D_MODEL), _BF16),
                        pltpu.VMEM((tile_m, D_MODEL), _BF16),
                        pltpu.VMEM((tile_m, D_MODEL), _BF16),
                        pltpu.VMEM(state.shape, _F32),
                        pltpu.VMEM((rows, D_MODEL), _F32),
                        pltpu.SemaphoreType.DMA((3,))],
        compiler_params=pltpu.CompilerParams(
            dimension_semantics=("arbitrary",),
            vmem_limit_bytes=VMEM_LIMIT_BYTES),
        name="layer",
    )(h, h, p, hs, ps, taps, state, *weights)


def kernel(x_prompt, x_sample, state_pool, p_prompt, p_sample, pre_g, w_in, ln_g, ln_b, w_s, b_s,
           w_pool, pool_scale, w_pa, w_pb, w_out, post_g, w_ple, w_pg, ple_in_g, ple_out_g):
    depth = w_in.shape[0]
    dec_batch, dec_seq, _ = x_sample.shape
    assert dec_seq == 1 and x_prompt.shape[2] == D_MODEL
    hp = x_prompt
    hs = x_sample.reshape(dec_batch * N_LANE_BLOCKS, LANES)
    pv, pp, sv, sp = [], [], [], []
    for i in range(depth):
        w_s_rows = w_s[i].reshape(D_MODEL, CHUNK)
        params = _pack_params(
            [w_in[i], w_s_rows, w_pool[i].reshape(D_MODEL, POOL_GROUP), w_pa[i], w_pb[i],
             w_out[i], w_ple[i], w_pg[i]],
            (None, CHUNK) + (None,) * 6, b_s[i],
            [v[i:i + 1] for v in (pre_g, ln_g, ln_b, pool_scale, post_g, ple_in_g, ple_out_g)],
            n_steps=PACK_STEPS)
        hp, v, pool, hs, v_s, pool_s = _layer(
            hp, p_prompt[i], hs, p_sample[i].reshape(dec_batch * PLE_DIM // LANES, LANES),
            jnp.swapaxes(state_pool[i], 0, 1), w_s_rows, params, TILE_M, PAST_LEN)
        pv.append(v)
        pp.append(jnp.swapaxes(pool, 0, 1))
        sv.append(v_s.reshape(dec_batch, 1, D_MODEL))
        sp.append(jnp.swapaxes(pool_s, 0, 1))
    return (hp, hs.reshape(dec_batch, 1, D_MODEL), jnp.stack(pv), jnp.stack(pp), jnp.stack(sv),
            jnp.stack(sp))
```

```python
import functools
from typing import Any, NamedTuple

import jax
import jax.numpy as jnp
from jax import lax
from jax.experimental import pallas as pl
from jax.experimental.pallas import tpu as pltpu

D_MODEL = 1024
CHUNK = 128
N_HEADS = D_MODEL // CHUNK
POOL_WINDOWS = (2, 4, 8, 16)
POOL_GROUP = D_MODEL // len(POOL_WINDOWS)
POOL_BUF = max(POOL_WINDOWS) - 1
PLE_DIM = 256
PAST_LEN = 16384
SAMPLE_ROWS = 128
EPS = 1e-6
COL_U, COL_V, COL_ZA, COL_XB, COL_ZB, COL_GA, COL_GB = range(7)
VEC_ROWS = tuple(range(7))
VEC_PRE_G, VEC_LN_G, VEC_LN_B, VEC_POOL_SCALE, VEC_POST_G, VEC_PLE_IN_G, VEC_PLE_OUT_G = VEC_ROWS
N_VEC_ROWS = 8

LANES = 128
N_LANE_BLOCKS = D_MODEL // LANES
BF16_ROWS = 16
TAIL = 16
PACK_STEPS = 4
TILE_M = 256
VMEM_LIMIT_BYTES = 56 * 1024 * 1024

_BF16 = jnp.bfloat16
_F32 = jnp.float32
_GELU_C = 0.7978845608028654


def _dot(a, b):
    return jnp.dot(a, b, preferred_element_type=_F32)


def _rmsnorm(x, g):
    return x * lax.rsqrt(jnp.mean(x * x, axis=-1, keepdims=True) + EPS) * g


def _layernorm(x, g, b):
    xc = x - jnp.mean(x, axis=-1, keepdims=True)
    return xc * lax.rsqrt(jnp.mean(xc * xc, axis=-1, keepdims=True) + EPS) * g + b


def _gelu_tanh(x):
    return 0.5 * x * (1.0 + jnp.tanh(_GELU_C * (x + 0.044715 * (x * x * x))))


def _sigmoid(x):
    return 0.5 * jnp.tanh(0.5 * x) + 0.5


def _silu(x):
    return x * _sigmoid(x)


def _unpack(ref, rows=slice(None), cols=slice(None)):
    return pltpu.bitcast(ref[rows, cols], _BF16)


class _Weights(NamedTuple):
    in_proj_w: Any
    spatial: Any
    pool: Any
    pa: Any
    pb: Any
    out: Any
    ple: Any
    pg: Any
    bias_t: Any
    vecs: Any

    def vec(self, row):
        return self.vecs[row:row + 1, :]

    def in_proj(self, xn, col):
        return _dot(xn, _unpack(self.in_proj_w, cols=slice(col * D_MODEL, (col + 1) * D_MODEL)))


def _pool_mix(pooled, w):
    parts = []
    for g in range(len(POOL_WINDOWS)):
        q = pooled[:, g * POOL_GROUP:(g + 1) * POOL_GROUP].astype(_BF16)
        rows = slice(g * POOL_GROUP // 2, (g + 1) * POOL_GROUP // 2)
        parts.append(_dot(q, _unpack(w.pool, rows=rows)))
    return jnp.concatenate(parts, axis=1) * w.vec(VEC_POOL_SCALE)


def _merge_and_embed(h, p, xn, y_a, y_b, w):
    g_a = w.in_proj(xn, COL_GA)
    g_b = w.in_proj(xn, COL_GB)
    m = (_sigmoid(g_a) * _dot(y_a.astype(_BF16), _unpack(w.pa))
         + _sigmoid(g_b) * _dot(y_b.astype(_BF16), _unpack(w.pb)))
    h = h + _rmsnorm(_dot(m.astype(_BF16), _unpack(w.out)), w.vec(VEC_POST_G))
    e = _dot(p.astype(_BF16), _unpack(w.ple))
    hn = _rmsnorm(h, w.vec(VEC_PLE_IN_G)).astype(_BF16)
    gate = _sigmoid(_dot(hn, _unpack(w.pg)))
    return h + _rmsnorm(gate * e, w.vec(VEC_PLE_OUT_G))


def _spatial_mix(vn_b, w, tile_m):
    w_tril = [_unpack(w.spatial, rows=slice(hd * CHUNK // 2, (hd + 1) * CHUNK // 2))
              for hd in range(N_HEADS)]
    bias = [w.bias_t[:, hd:hd + 1] for hd in range(N_HEADS)]
    s_rows = []
    for c in range(tile_m // CHUNK):
        heads = []
        for hd in range(N_HEADS):
            blk = vn_b[c * CHUNK:(c + 1) * CHUNK, hd * CHUNK:(hd + 1) * CHUNK]
            heads.append(_dot(w_tril[hd], blk) + bias[hd])
        s_rows.append(jnp.concatenate(heads, axis=1))
    return jnp.concatenate(s_rows, axis=0)


def _window_pool(x_b, ext_ref, pool_ref, b, j, tile_m):
    for c in range(N_LANE_BLOCKS):
        ext_ref[c, TAIL:TAIL + tile_m, :] = x_b[:, c * LANES:(c + 1) * LANES]
    pos = j * tile_m + lax.broadcasted_iota(jnp.int32, (tile_m, 1), 0)
    blocks = []
    for c in range(N_LANE_BLOCKS):
        w = POOL_WINDOWS[c * LANES // POOL_GROUP]
        cur = x_b[:, c * LANES:(c + 1) * LANES]
        win = cur
        for k in range(1, w):
            win = win + ext_ref[c, TAIL - k:TAIL - k + tile_m, :]
        cnt = jnp.minimum(pos + 1, w).astype(_F32)
        blocks.append(win / cnt - cur)
    latest = jnp.concatenate([ext_ref[c, TAIL + tile_m - POOL_BUF:TAIL + tile_m, :]
                              for c in range(N_LANE_BLOCKS)], axis=1)
    pool_ref[:, pl.ds(b, 1), :] = latest[:, None, :]
    for c in range(N_LANE_BLOCKS):
        ext_ref[c, 0:TAIL, :] = ext_ref[c, tile_m:tile_m + TAIL, :]
    return jnp.concatenate(blocks, axis=1)


def _layer_kernel(x_ref, x_lag_ref, p_lag_ref, xs_ref, ps_ref, taps_ref, state_hbm,
                  w_in_ref, w_s_ref, w_pool_ref, w_pa_ref, w_pb_ref, w_out_ref, w_ple_ref,
                  w_pg_ref, bs_t_ref, vecs_ref,
                  y_ref, vn_ref, pool_ref, ys_ref, vns_ref, spool_hbm,
                  ext_ref, xn_c, ya_c, mix_c, zb_c, hist_ref, xbs_ref, sems,
                  *, tile_m, tiles_per_prompt, n_tiles, pos0):
    s_id = pl.program_id(0)
    front = jnp.minimum(s_id, n_tiles - 1)
    b = front // tiles_per_prompt
    j = lax.rem(front, tiles_per_prompt)
    w = _Weights(in_proj_w=w_in_ref, spatial=w_s_ref, pool=w_pool_ref, pa=w_pa_ref, pb=w_pb_ref,
                 out=w_out_ref, ple=w_ple_ref, pg=w_pg_ref, bias_t=bs_t_ref, vecs=vecs_ref)

    @pl.when(j == 0)
    def _():
        ext_ref[:, 0:TAIL, :] = jnp.zeros((N_LANE_BLOCKS, TAIL, LANES), _F32)

    @pl.when(s_id == 0)
    def _front_only():
        xn = _rmsnorm(x_ref[0], w.vec(VEC_PRE_G)).astype(_BF16)
        xn_c[...] = xn
        vn = _layernorm(_gelu_tanh(w.in_proj(xn, COL_V)), w.vec(VEC_LN_G), w.vec(VEC_LN_B))
        vn_ref[0] = vn[tile_m - CHUNK:, :]
        s = _spatial_mix(vn.astype(_BF16), w, tile_m)
        ya_c[...] = (_gelu_tanh(w.in_proj(xn, COL_U)) * s
                     * _silu(w.in_proj(xn, COL_ZA))).astype(_BF16)
        pooled = _window_pool(w.in_proj(xn, COL_XB), ext_ref, pool_ref, b, j, tile_m)
        mix_c[...] = _pool_mix(pooled, w)
        zb_c[...] = w.in_proj(xn, COL_ZB)

    @pl.when(s_id == n_tiles)
    def _back_only_and_sample():
        history, shift, newest = _sample_copies(state_hbm, spool_hbm, hist_ref, xbs_ref, sems)
        history.start()
        y_ref[0] = _merge_and_embed(x_lag_ref[0], p_lag_ref[0], xn_c[...], ya_c[...],
                                    mix_c[...] * _silu(zb_c[...]), w)
        history.wait()
        shift.start()
        _sample_rows(xs_ref, ps_ref, taps_ref, hist_ref, newest, xbs_ref, w, ys_ref, vns_ref, pos0)
        shift.wait()
        newest.wait()

    @pl.when(jnp.logical_and(s_id > 0, s_id < n_tiles))
    def _both():
        xn_prev = xn_c[...]
        g_a = w.in_proj(xn_prev, COL_GA)
        xn = _rmsnorm(x_ref[0], w.vec(VEC_PRE_G)).astype(_BF16)
        xn_c[...] = xn
        v_raw = w.in_proj(xn, COL_V)
        g_b = w.in_proj(xn_prev, COL_GB)
        vn = _layernorm(_gelu_tanh(v_raw), w.vec(VEC_LN_G), w.vec(VEC_LN_B))
        vn_ref[0] = vn[tile_m - CHUNK:, :]
        u_raw = w.in_proj(xn, COL_U)
        s = _spatial_mix(vn.astype(_BF16), w, tile_m)
        proj_a = _dot(ya_c[...], _unpack(w.pa))
        y_b = (mix_c[...] * _silu(zb_c[...])).astype(_BF16)
        proj_b = _dot(y_b, _unpack(w.pb))
        za_raw = w.in_proj(xn, COL_ZA)
        m = (_sigmoid(g_a) * proj_a + _sigmoid(g_b) * proj_b).astype(_BF16)
        o = _dot(m, _unpack(w.out))
        ya_c[...] = (_gelu_tanh(u_raw) * s * _silu(za_raw)).astype(_BF16)
        x_b = w.in_proj(xn, COL_XB)
        h = x_lag_ref[0] + _rmsnorm(o, w.vec(VEC_POST_G))
        hn = _rmsnorm(h, w.vec(VEC_PLE_IN_G)).astype(_BF16)
        gate_raw = _dot(hn, _unpack(w.pg))
        pooled = _window_pool(x_b, ext_ref, pool_ref, b, j, tile_m)
        e = _dot(p_lag_ref[0].astype(_BF16), _unpack(w.ple))
        mixed = _pool_mix(pooled, w)
        y_ref[0] = h + _rmsnorm(_sigmoid(gate_raw) * e, w.vec(VEC_PLE_OUT_G))
        mix_c[...] = mixed
        zb_c[...] = w.in_proj(xn, COL_ZB)


def _gather_rows(ref):
    n = ref.shape[0] // SAMPLE_ROWS
    return jnp.concatenate([ref[pl.ds(c, SAMPLE_ROWS, stride=n), :] for c in range(n)], axis=1)


def _scatter_rows(ref, value):
    n = ref.shape[0] // SAMPLE_ROWS
    for c in range(n):
        ref[pl.ds(c, SAMPLE_ROWS, stride=n), :] = value[:, c * LANES:(c + 1) * LANES]


def _sample_copies(state_hbm, spool_hbm, hist_ref, xbs_ref, sems):
    history = pltpu.make_async_copy(state_hbm, hist_ref, sems.at[0])
    shift = pltpu.make_async_copy(hist_ref.at[pl.ds(1, POOL_BUF - 1)],
                                  spool_hbm.at[pl.ds(0, POOL_BUF - 1)], sems.at[1])
    newest = pltpu.make_async_copy(xbs_ref, spool_hbm.at[POOL_BUF - 1], sems.at[2])
    return history, shift, newest


def _sample_rows(x_ref, p_ref, taps_ref, hist_ref, newest, xbs_ref, w, y_ref, vn_ref, pos0):
    x = _gather_rows(x_ref)
    xn = _rmsnorm(x, w.vec(VEC_PRE_G)).astype(_BF16)

    u = _gelu_tanh(w.in_proj(xn, COL_U))
    vn = _layernorm(_gelu_tanh(w.in_proj(xn, COL_V)), w.vec(VEC_LN_G), w.vec(VEC_LN_B))
    _scatter_rows(vn_ref, vn)
    heads = []
    for hd in range(N_HEADS):
        tap = taps_ref[hd * CHUNK:hd * CHUNK + 1, 0:1]
        heads.append(vn[:, hd * CHUNK:(hd + 1) * CHUNK] * tap + w.bias_t[0:1, hd:hd + 1])
    s = jnp.concatenate(heads, axis=1)
    y_a = u * s * _silu(w.in_proj(xn, COL_ZA))

    x_b = w.in_proj(xn, COL_XB)
    xbs_ref[...] = x_b
    newest.start()
    groups = []
    for g, win_len in enumerate(POOL_WINDOWS):
        lo, hi = g * POOL_GROUP, (g + 1) * POOL_GROUP
        cur = x_b[:, lo:hi]
        win = cur
        for k in range(1, win_len):
            win = win + hist_ref[POOL_BUF - k, :, lo:hi]
        groups.append(win / float(min(pos0 + 1, win_len)) - cur)
    pooled = jnp.concatenate(groups, axis=1)
    y_b = _pool_mix(pooled, w) * _silu(w.in_proj(xn, COL_ZB))

    _scatter_rows(y_ref, _merge_and_embed(x, _gather_rows(p_ref), xn, y_a, y_b, w))


def _pack_kernel(*refs, tril_periods):
    n = len(tril_periods)
    n_vecs = len(VEC_ROWS)
    bias_ref, vec_refs = refs[n], refs[n + 1:n + 1 + n_vecs]
    outs = refs[n + 1 + n_vecs:]
    bias_t_ref, vecs_ref = outs[n], outs[n + 1]
    bias_t_ref[...] = bias_ref[...].T
    for row, vec_ref in zip(VEC_ROWS, vec_refs):
        vecs_ref[row:row + 1, :] = vec_ref[...]
    vecs_ref[n_vecs:, :] = jnp.zeros((N_VEC_ROWS - n_vecs, D_MODEL), _F32)
    for src, dst, period in zip(refs[:n], outs[:n], tril_periods):
        m = src[...]
        if period is not None:
            row = lax.broadcasted_iota(jnp.int32, m.shape, 0) % period
            m = jnp.where(row >= lax.broadcasted_iota(jnp.int32, m.shape, 1), m, 0.0)
        dst[...] = pltpu.bitcast(m.astype(_BF16), jnp.uint32)


def _pack_params(mats, tril_periods, bias, vecs, n_steps):
    blocks = [m.shape[0] // n_steps for m in mats]
    small = lambda shape: pl.BlockSpec(shape, lambda i: (0, 0))
    for m, rows, period in zip(mats, blocks, tril_periods):
        assert rows * n_steps == m.shape[0] and rows % (2 * BF16_ROWS) == 0
        assert period is None or rows % period == 0
    return pl.pallas_call(
        functools.partial(_pack_kernel, tril_periods=tril_periods),
        grid=(n_steps,),
        in_specs=[pl.BlockSpec((rows, m.shape[1]), lambda i: (i, 0))
                  for m, rows in zip(mats, blocks)]
                 + [small(bias.shape)] + [small(v.shape) for v in vecs],
        out_specs=[pl.BlockSpec((rows // 2, m.shape[1]), lambda i: (i, 0))
                   for m, rows in zip(mats, blocks)]
                  + [small(bias.shape[::-1]), small((N_VEC_ROWS, D_MODEL))],
        out_shape=[jax.ShapeDtypeStruct((m.shape[0] // 2, m.shape[1]), jnp.uint32) for m in mats]
                  + [jax.ShapeDtypeStruct(bias.shape[::-1], _F32),
                     jax.ShapeDtypeStruct((N_VEC_ROWS, D_MODEL), _F32)],
        compiler_params=pltpu.CompilerParams(dimension_semantics=("arbitrary",),
                                             vmem_limit_bytes=VMEM_LIMIT_BYTES),
        name="pack_params",
    )(*mats, bias, *vecs)


def _resident(shape):
    zeros = (0,) * len(shape)
    return pl.BlockSpec(shape, lambda *_: zeros, pipeline_mode=pl.Buffered(1))


def _layer(h, p, hs, ps, state, taps, weights, tile_m, pos0):
    batch, seq, _ = h.shape
    rows = state.shape[1]
    assert seq % tile_m == 0 and tile_m % CHUNK == 0 and seq >= POOL_BUF and rows == SAMPLE_ROWS
    tiles_per_prompt = seq // tile_m
    n_tiles = batch * tiles_per_prompt
    assert n_tiles >= 2

    def front_tile(s):
        t = jnp.minimum(s, n_tiles - 1)
        return t // tiles_per_prompt, t % tiles_per_prompt, 0

    def back_tile(s):
        t = jnp.maximum(s - 1, 0)
        return t // tiles_per_prompt, t % tiles_per_prompt, 0

    per_prompt = lambda s: (jnp.minimum(s, n_tiles - 1) // tiles_per_prompt, 0, 0)
    hbm = pl.BlockSpec(memory_space=pl.ANY)
    whole = lambda a: pl.BlockSpec(a.shape, lambda s: (0,) * a.ndim)
    return pl.pallas_call(
        functools.partial(_layer_kernel, tile_m=tile_m, tiles_per_prompt=tiles_per_prompt,
                          n_tiles=n_tiles, pos0=pos0),
        grid=(n_tiles + 1,),
        in_specs=[pl.BlockSpec((1, tile_m, D_MODEL), front_tile),
                  pl.BlockSpec((1, tile_m, D_MODEL), back_tile),
                  pl.BlockSpec((1, tile_m, PLE_DIM), back_tile),
                  _resident(hs.shape), _resident(ps.shape), _resident(taps.shape), hbm]
                 + [_resident(a.shape) for a in weights],
        out_specs=[pl.BlockSpec((1, tile_m, D_MODEL), back_tile),
                   pl.BlockSpec((1, CHUNK, D_MODEL), per_prompt),
                   pl.BlockSpec((POOL_BUF, batch, D_MODEL), lambda s: (0, 0, 0)),
                   whole(hs), whole(hs), hbm],
        out_shape=[jax.ShapeDtypeStruct((batch, seq, D_MODEL), _F32),
                   jax.ShapeDtypeStruct((batch, CHUNK, D_MODEL), _F32),
                   jax.ShapeDtypeStruct((POOL_BUF, batch, D_MODEL), _F32),
                   jax.ShapeDtypeStruct(hs.shape, _F32),
                   jax.ShapeDtypeStruct(hs.shape, _F32),
                   jax.ShapeDtypeStruct(state.shape, _F32)],
        scratch_shapes=[pltpu.VMEM((N_LANE_BLOCKS, TAIL + tile_m, LANES), _F32),
                        pltpu.VMEM((tile_m, D_MODEL), _BF16),
                        pltpu.VMEM((tile_m, D_MODEL), _BF16),
                        pltpu.VMEM((tile_m, D_MODEL), _F32),
                        pltpu.VMEM((tile_m, D_MODEL), _F32),
                        pltpu.VMEM(state.shape, _F32),
                        pltpu.VMEM((rows, D_MODEL), _F32),
                        pltpu.SemaphoreType.DMA((3,))],
        compiler_params=pltpu.CompilerParams(
            dimension_semantics=("arbitrary",),
            vmem_limit_bytes=VMEM_LIMIT_BYTES),
        name="layer",
    )(h, h, p, hs, ps, taps, state, *weights)


def kernel(x_prompt, x_sample, state_pool, p_prompt, p_sample, pre_g, w_in, ln_g, ln_b, w_s, b_s,
           w_pool, pool_scale, w_pa, w_pb, w_out, post_g, w_ple, w_pg, ple_in_g, ple_out_g):
    depth = w_in.shape[0]
    dec_batch, dec_seq, _ = x_sample.shape
    assert dec_seq == 1 and x_prompt.shape[2] == D_MODEL
    hp = x_prompt
    hs = x_sample.reshape(dec_batch * N_LANE_BLOCKS, LANES)
    pv, pp, sv, sp = [], [], [], []
    for i in range(depth):
        w_s_rows = w_s[i].reshape(D_MODEL, CHUNK)
        params = _pack_params(
            [w_in[i], w_s_rows, w_pool[i].reshape(D_MODEL, POOL_GROUP), w_pa[i], w_pb[i],
             w_out[i], w_ple[i], w_pg[i]],
            (None, CHUNK) + (None,) * 6, b_s[i],
            [v[i:i + 1] for v in (pre_g, ln_g, ln_b, pool_scale, post_g, ple_in_g, ple_out_g)],
            n_steps=PACK_STEPS)
        hp, v, pool, hs, v_s, pool_s = _layer(
            hp, p_prompt[i], hs, p_sample[i].reshape(dec_batch * PLE_DIM // LANES, LANES),
            jnp.swapaxes(state_pool[i], 0, 1), w_s_rows, params, TILE_M, PAST_LEN)
        pv.append(v)
        pp.append(jnp.swapaxes(pool, 0, 1))
        sv.append(v_s.reshape(dec_batch, 1, D_MODEL))
        sp.append(jnp.swapaxes(pool_s, 0, 1))
    return (hp, hs.reshape(dec_batch, 1, D_MODEL), jnp.stack(pv), jnp.stack(pp), jnp.stack(sv),
            jnp.stack(sp))
```

```python
import functools
from typing import Any, NamedTuple

import jax
import jax.numpy as jnp
from jax import lax
from jax.experimental import pallas as pl
from jax.experimental.pallas import tpu as pltpu

D_MODEL = 1024
CHUNK = 128
N_HEADS = D_MODEL // CHUNK
POOL_WINDOWS = (2, 4, 8, 16)
POOL_GROUP = D_MODEL // len(POOL_WINDOWS)
POOL_BUF = max(POOL_WINDOWS) - 1
PLE_DIM = 256
PAST_LEN = 16384
SAMPLE_ROWS = 128
EPS = 1e-6
COL_U, COL_V, COL_ZA, COL_XB, COL_ZB, COL_GA, COL_GB = range(7)
VEC_ROWS = tuple(range(7))
VEC_PRE_G, VEC_LN_G, VEC_LN_B, VEC_POOL_SCALE, VEC_POST_G, VEC_PLE_IN_G, VEC_PLE_OUT_G = VEC_ROWS
N_VEC_ROWS = 8

LANES = 128
N_LANE_BLOCKS = D_MODEL // LANES
BF16_ROWS = 16
TAIL = 16
PACK_STEPS = 8
TILE_M = 256
VMEM_LIMIT_BYTES = 56 * 1024 * 1024

_BF16 = jnp.bfloat16
_F32 = jnp.float32
_GELU_C = 0.7978845608028654


def _dot(a, b):
    return jnp.dot(a, b, preferred_element_type=_F32)


def _rmsnorm(x, g):
    return x * lax.rsqrt(jnp.mean(x * x, axis=-1, keepdims=True) + EPS) * g


def _layernorm(x, g, b):
    xc = x - jnp.mean(x, axis=-1, keepdims=True)
    return xc * lax.rsqrt(jnp.mean(xc * xc, axis=-1, keepdims=True) + EPS) * g + b


def _gelu_tanh(x):
    return 0.5 * x * (1.0 + jnp.tanh(_GELU_C * (x + 0.044715 * (x * x * x))))


def _sigmoid(x):
    return 0.5 * jnp.tanh(0.5 * x) + 0.5


def _silu(x):
    return x * _sigmoid(x)


def _unpack(ref, rows=slice(None), cols=slice(None)):
    return pltpu.bitcast(ref[rows, cols], _BF16)


class _Weights(NamedTuple):
    in_proj_w: Any
    spatial: Any
    pool: Any
    pa: Any
    pb: Any
    out: Any
    ple: Any
    pg: Any
    bias_t: Any
    vecs: Any

    def vec(self, row):
        return self.vecs[row:row + 1, :]

    def in_proj(self, xn, col):
        return _dot(xn, _unpack(self.in_proj_w, cols=slice(col * D_MODEL, (col + 1) * D_MODEL)))


def _pool_mix(pooled, w):
    parts = []
    for g in range(len(POOL_WINDOWS)):
        q = pooled[:, g * POOL_GROUP:(g + 1) * POOL_GROUP].astype(_BF16)
        rows = slice(g * POOL_GROUP // 2, (g + 1) * POOL_GROUP // 2)
        parts.append(_dot(q, _unpack(w.pool, rows=rows)))
    return jnp.concatenate(parts, axis=1) * w.vec(VEC_POOL_SCALE)


def _merge_and_embed(h, p, xn, y_a, y_b, w):
    g_a = w.in_proj(xn, COL_GA)
    g_b = w.in_proj(xn, COL_GB)
    m = (_sigmoid(g_a) * _dot(y_a.astype(_BF16), _unpack(w.pa))
         + _sigmoid(g_b) * _dot(y_b.astype(_BF16), _unpack(w.pb)))
    h = h + _rmsnorm(_dot(m.astype(_BF16), _unpack(w.out)), w.vec(VEC_POST_G))
    e = _dot(p.astype(_BF16), _unpack(w.ple))
    hn = _rmsnorm(h, w.vec(VEC_PLE_IN_G)).astype(_BF16)
    gate = _sigmoid(_dot(hn, _unpack(w.pg)))
    return h + _rmsnorm(gate * e, w.vec(VEC_PLE_OUT_G))


def _spatial_mix(vn_b, w, tile_m):
    w_tril = [_unpack(w.spatial, rows=slice(hd * CHUNK // 2, (hd + 1) * CHUNK // 2))
              for hd in range(N_HEADS)]
    bias = [w.bias_t[:, hd:hd + 1] for hd in range(N_HEADS)]
    s_rows = []
    for c in range(tile_m // CHUNK):
        heads = []
        for hd in range(N_HEADS):
            blk = vn_b[c * CHUNK:(c + 1) * CHUNK, hd * CHUNK:(hd + 1) * CHUNK]
            heads.append(_dot(w_tril[hd], blk) + bias[hd])
        s_rows.append(jnp.concatenate(heads, axis=1))
    return jnp.concatenate(s_rows, axis=0)


def _window_pool(x_b, ext_ref, pool_ref, b, j, tile_m):
    for c in range(N_LANE_BLOCKS):
        ext_ref[c, TAIL:TAIL + tile_m, :] = x_b[:, c * LANES:(c + 1) * LANES]
    pos = j * tile_m + lax.broadcasted_iota(jnp.int32, (tile_m, 1), 0)
    blocks = []
    for c in range(N_LANE_BLOCKS):
        w = POOL_WINDOWS[c * LANES // POOL_GROUP]
        cur = x_b[:, c * LANES:(c + 1) * LANES]
        win = cur
        for k in range(1, w):
            win = win + ext_ref[c, TAIL - k:TAIL - k + tile_m, :]
        cnt = jnp.minimum(pos + 1, w).astype(_F32)
        blocks.append(win / cnt - cur)
    latest = jnp.concatenate([ext_ref[c, TAIL + tile_m - POOL_BUF:TAIL + tile_m, :]
                              for c in range(N_LANE_BLOCKS)], axis=1)
    pool_ref[:, pl.ds(b, 1), :] = latest[:, None, :]
    for c in range(N_LANE_BLOCKS):
        ext_ref[c, 0:TAIL, :] = ext_ref[c, tile_m:tile_m + TAIL, :]
    return jnp.concatenate(blocks, axis=1)


def _layer_kernel(x_ref, x_lag_ref, p_lag_ref, xs_ref, ps_ref, taps_ref, state_hbm,
                  w_in_ref, w_s_ref, w_pool_ref, w_pa_ref, w_pb_ref, w_out_ref, w_ple_ref,
                  w_pg_ref, bs_t_ref, vecs_ref,
                  y_ref, vn_ref, pool_ref, ys_ref, vns_ref, spool_hbm,
                  ext_ref, gu_c, s_c, sza_c, mix_c, szb_c, ga_c, gb_c, hist_ref, xbs_ref, sems,
                  *, tile_m, tiles_per_prompt, n_tiles, pos0):
    s_id = pl.program_id(0)
    front = jnp.minimum(s_id, n_tiles - 1)
    b = front // tiles_per_prompt
    j = lax.rem(front, tiles_per_prompt)
    w = _Weights(in_proj_w=w_in_ref, spatial=w_s_ref, pool=w_pool_ref, pa=w_pa_ref, pb=w_pb_ref,
                 out=w_out_ref, ple=w_ple_ref, pg=w_pg_ref, bias_t=bs_t_ref, vecs=vecs_ref)

    @pl.when(j == 0)
    def _():
        ext_ref[:, 0:TAIL, :] = jnp.zeros((N_LANE_BLOCKS, TAIL, LANES), _F32)

    def back_stage():
        y_a = (gu_c[...] * s_c[...] * sza_c[...]).astype(_BF16)
        y_b = (mix_c[...] * szb_c[...]).astype(_BF16)
        m = (ga_c[...] * _dot(y_a, _unpack(w.pa)) + gb_c[...] * _dot(y_b, _unpack(w.pb)))
        h = x_lag_ref[0] + _rmsnorm(_dot(m.astype(_BF16), _unpack(w.out)), w.vec(VEC_POST_G))
        e = _dot(p_lag_ref[0].astype(_BF16), _unpack(w.ple))
        hn = _rmsnorm(h, w.vec(VEC_PLE_IN_G)).astype(_BF16)
        gate = _sigmoid(_dot(hn, _unpack(w.pg)))
        y_ref[0] = h + _rmsnorm(gate * e, w.vec(VEC_PLE_OUT_G))

    @pl.when(s_id == 0)
    def _front_only():
        xn = _rmsnorm(x_ref[0], w.vec(VEC_PRE_G)).astype(_BF16)
        vn = _layernorm(_gelu_tanh(w.in_proj(xn, COL_V)), w.vec(VEC_LN_G), w.vec(VEC_LN_B))
        vn_ref[0] = vn[tile_m - CHUNK:, :]
        gu_c[...] = _gelu_tanh(w.in_proj(xn, COL_U))
        s_c[...] = _spatial_mix(vn.astype(_BF16), w, tile_m)
        ga_c[...] = _sigmoid(w.in_proj(xn, COL_GA))
        gb_c[...] = _sigmoid(w.in_proj(xn, COL_GB))
        sza_c[...] = _silu(w.in_proj(xn, COL_ZA))
        pooled = _window_pool(w.in_proj(xn, COL_XB), ext_ref, pool_ref, b, j, tile_m)
        mix_c[...] = _pool_mix(pooled, w)
        szb_c[...] = _silu(w.in_proj(xn, COL_ZB))

    @pl.when(s_id == n_tiles)
    def _back_only_and_sample():
        history, shift, newest = _sample_copies(state_hbm, spool_hbm, hist_ref, xbs_ref, sems)
        history.start()
        back_stage()
        history.wait()
        shift.start()
        _sample_rows(xs_ref, ps_ref, taps_ref, hist_ref, newest, xbs_ref, w, ys_ref, vns_ref, pos0)
        shift.wait()
        newest.wait()

    @pl.when(jnp.logical_and(s_id > 0, s_id < n_tiles))
    def _both():
        y_a = (gu_c[...] * s_c[...] * sza_c[...]).astype(_BF16)
        xn = _rmsnorm(x_ref[0], w.vec(VEC_PRE_G)).astype(_BF16)
        proj_a = _dot(y_a, _unpack(w.pa))
        v_raw = w.in_proj(xn, COL_V)
        y_b = (mix_c[...] * szb_c[...]).astype(_BF16)
        proj_b = _dot(y_b, _unpack(w.pb))
        vn = _layernorm(_gelu_tanh(v_raw), w.vec(VEC_LN_G), w.vec(VEC_LN_B))
        vn_ref[0] = vn[tile_m - CHUNK:, :]
        gu_c[...] = _gelu_tanh(w.in_proj(xn, COL_U))
        m = (ga_c[...] * proj_a + gb_c[...] * proj_b).astype(_BF16)
        s_c[...] = _spatial_mix(vn.astype(_BF16), w, tile_m)
        o = _dot(m, _unpack(w.out))
        ga_c[...] = _sigmoid(w.in_proj(xn, COL_GA))
        gb_c[...] = _sigmoid(w.in_proj(xn, COL_GB))
        h = x_lag_ref[0] + _rmsnorm(o, w.vec(VEC_POST_G))
        hn = _rmsnorm(h, w.vec(VEC_PLE_IN_G)).astype(_BF16)
        sza_c[...] = _silu(w.in_proj(xn, COL_ZA))
        gate_raw = _dot(hn, _unpack(w.pg))
        x_b = w.in_proj(xn, COL_XB)
        e = _dot(p_lag_ref[0].astype(_BF16), _unpack(w.ple))
        pooled = _window_pool(x_b, ext_ref, pool_ref, b, j, tile_m)
        y_ref[0] = h + _rmsnorm(_sigmoid(gate_raw) * e, w.vec(VEC_PLE_OUT_G))
        mix_c[...] = _pool_mix(pooled, w)
        szb_c[...] = _silu(w.in_proj(xn, COL_ZB))


def _gather_rows(ref):
    n = ref.shape[0] // SAMPLE_ROWS
    return jnp.concatenate([ref[pl.ds(c, SAMPLE_ROWS, stride=n), :] for c in range(n)], axis=1)


def _scatter_rows(ref, value):
    n = ref.shape[0] // SAMPLE_ROWS
    for c in range(n):
        ref[pl.ds(c, SAMPLE_ROWS, stride=n), :] = value[:, c * LANES:(c + 1) * LANES]


def _sample_copies(state_hbm, spool_hbm, hist_ref, xbs_ref, sems):
    history = pltpu.make_async_copy(state_hbm, hist_ref, sems.at[0])
    shift = pltpu.make_async_copy(hist_ref.at[pl.ds(1, POOL_BUF - 1)],
                                  spool_hbm.at[pl.ds(0, POOL_BUF - 1)], sems.at[1])
    newest = pltpu.make_async_copy(xbs_ref, spool_hbm.at[POOL_BUF - 1], sems.at[2])
    return history, shift, newest


def _sample_rows(x_ref, p_ref, taps_ref, hist_ref, newest, xbs_ref, w, y_ref, vn_ref, pos0):
    x = _gather_rows(x_ref)
    xn = _rmsnorm(x, w.vec(VEC_PRE_G)).astype(_BF16)

    u = _gelu_tanh(w.in_proj(xn, COL_U))
    vn = _layernorm(_gelu_tanh(w.in_proj(xn, COL_V)), w.vec(VEC_LN_G), w.vec(VEC_LN_B))
    _scatter_rows(vn_ref, vn)
    heads = []
    for hd in range(N_HEADS):
        tap = taps_ref[hd * CHUNK:hd * CHUNK + 1, 0:1]
        heads.append(vn[:, hd * CHUNK:(hd + 1) * CHUNK] * tap + w.bias_t[0:1, hd:hd + 1])
    s = jnp.concatenate(heads, axis=1)
    y_a = u * s * _silu(w.in_proj(xn, COL_ZA))

    x_b = w.in_proj(xn, COL_XB)
    xbs_ref[...] = x_b
    newest.start()
    groups = []
    for g, win_len in enumerate(POOL_WINDOWS):
        lo, hi = g * POOL_GROUP, (g + 1) * POOL_GROUP
        cur = x_b[:, lo:hi]
        win = cur
        for k in range(1, win_len):
            win = win + hist_ref[POOL_BUF - k, :, lo:hi]
        groups.append(win / float(min(pos0 + 1, win_len)) - cur)
    pooled = jnp.concatenate(groups, axis=1)
    y_b = _pool_mix(pooled, w) * _silu(w.in_proj(xn, COL_ZB))

    _scatter_rows(y_ref, _merge_and_embed(x, _gather_rows(p_ref), xn, y_a, y_b, w))


def _pack_kernel(*refs, tril_periods):
    n = len(tril_periods)
    n_vecs = len(VEC_ROWS)
    bias_ref, vec_refs = refs[n], refs[n + 1:n + 1 + n_vecs]
    outs = refs[n + 1 + n_vecs:]
    bias_t_ref, vecs_ref = outs[n], outs[n + 1]
    bias_t_ref[...] = bias_ref[...].T
    for row, vec_ref in zip(VEC_ROWS, vec_refs):
        vecs_ref[row:row + 1, :] = vec_ref[...]
    vecs_ref[n_vecs:, :] = jnp.zeros((N_VEC_ROWS - n_vecs, D_MODEL), _F32)
    for src, dst, period in zip(refs[:n], outs[:n], tril_periods):
        m = src[...]
        if period is not None:
            row = lax.broadcasted_iota(jnp.int32, m.shape, 0) % period
            m = jnp.where(row >= lax.broadcasted_iota(jnp.int32, m.shape, 1), m, 0.0)
        dst[...] = pltpu.bitcast(m.astype(_BF16), jnp.uint32)


def _pack_params(mats, tril_periods, bias, vecs, n_steps):
    blocks = [m.shape[0] // n_steps for m in mats]
    small = lambda shape: pl.BlockSpec(shape, lambda i: (0, 0))
    for m, rows, period in zip(mats, blocks, tril_periods):
        assert rows * n_steps == m.shape[0] and rows % (2 * BF16_ROWS) == 0
        assert period is None or rows % period == 0
    return pl.pallas_call(
        functools.partial(_pack_kernel, tril_periods=tril_periods),
        grid=(n_steps,),
        in_specs=[pl.BlockSpec((rows, m.shape[1]), lambda i: (i, 0))
                  for m, rows in zip(mats, blocks)]
                 + [small(bias.shape)] + [small(v.shape) for v in vecs],
        out_specs=[pl.BlockSpec((rows // 2, m.shape[1]), lambda i: (i, 0))
                   for m, rows in zip(mats, blocks)]
                  + [small(bias.shape[::-1]), small((N_VEC_ROWS, D_MODEL))],
        out_shape=[jax.ShapeDtypeStruct((m.shape[0] // 2, m.shape[1]), jnp.uint32) for m in mats]
                  + [jax.ShapeDtypeStruct(bias.shape[::-1], _F32),
                     jax.ShapeDtypeStruct((N_VEC_ROWS, D_MODEL), _F32)],
        compiler_params=pltpu.CompilerParams(dimension_semantics=("arbitrary",),
                                             vmem_limit_bytes=VMEM_LIMIT_BYTES),
        name="pack_params",
    )(*mats, bias, *vecs)


def _resident(shape):
    zeros = (0,) * len(shape)
    return pl.BlockSpec(shape, lambda *_: zeros, pipeline_mode=pl.Buffered(1))


def _layer(h, p, hs, ps, state, taps, weights, tile_m, pos0):
    batch, seq, _ = h.shape
    rows = state.shape[1]
    assert seq % tile_m == 0 and tile_m % CHUNK == 0 and seq >= POOL_BUF and rows == SAMPLE_ROWS
    tiles_per_prompt = seq // tile_m
    n_tiles = batch * tiles_per_prompt
    assert n_tiles >= 2

    def front_tile(s):
        t = jnp.minimum(s, n_tiles - 1)
        return t // tiles_per_prompt, t % tiles_per_prompt, 0

    def back_tile(s):
        t = jnp.maximum(s - 1, 0)
        return t // tiles_per_prompt, t % tiles_per_prompt, 0

    per_prompt = lambda s: (jnp.minimum(s, n_tiles - 1) // tiles_per_prompt, 0, 0)
    hbm = pl.BlockSpec(memory_space=pl.ANY)
    whole = lambda a: pl.BlockSpec(a.shape, lambda s: (0,) * a.ndim)
    return pl.pallas_call(
        functools.partial(_layer_kernel, tile_m=tile_m, tiles_per_prompt=tiles_per_prompt,
                          n_tiles=n_tiles, pos0=pos0),
        grid=(n_tiles + 1,),
        in_specs=[pl.BlockSpec((1, tile_m, D_MODEL), front_tile),
                  pl.BlockSpec((1, tile_m, D_MODEL), back_tile),
                  pl.BlockSpec((1, tile_m, PLE_DIM), back_tile),
                  _resident(hs.shape), _resident(ps.shape), _resident(taps.shape), hbm]
                 + [_resident(a.shape) for a in weights],
        out_specs=[pl.BlockSpec((1, tile_m, D_MODEL), back_tile),
                   pl.BlockSpec((1, CHUNK, D_MODEL), per_prompt),
                   pl.BlockSpec((POOL_BUF, batch, D_MODEL), lambda s: (0, 0, 0)),
                   whole(hs), whole(hs), hbm],
        out_shape=[jax.ShapeDtypeStruct((batch, seq, D_MODEL), _F32),
                   jax.ShapeDtypeStruct((batch, CHUNK, D_MODEL), _F32),
                   jax.ShapeDtypeStruct((POOL_BUF, batch, D_MODEL), _F32),
                   jax.ShapeDtypeStruct(hs.shape, _F32),
                   jax.ShapeDtypeStruct(hs.shape, _F32),
                   jax.ShapeDtypeStruct(state.shape, _F32)],
        scratch_shapes=[pltpu.VMEM((N_LANE_BLOCKS, TAIL + tile_m, LANES), _F32)]
                       + [pltpu.VMEM((tile_m, D_MODEL), _F32)] * 7
                       + [pltpu.VMEM(state.shape, _F32),
                        pltpu.VMEM((rows, D_MODEL), _F32),
                        pltpu.SemaphoreType.DMA((3,))],
        compiler_params=pltpu.CompilerParams(
            dimension_semantics=("arbitrary",),
            vmem_limit_bytes=VMEM_LIMIT_BYTES),
        name="layer",
    )(h, h, p, hs, ps, taps, state, *weights)


def kernel(x_prompt, x_sample, state_pool, p_prompt, p_sample, pre_g, w_in, ln_g, ln_b, w_s, b_s,
           w_pool, pool_scale, w_pa, w_pb, w_out, post_g, w_ple, w_pg, ple_in_g, ple_out_g):
    depth = w_in.shape[0]
    dec_batch, dec_seq, _ = x_sample.shape
    assert dec_seq == 1 and x_prompt.shape[2] == D_MODEL
    hp = x_prompt
    hs = x_sample.reshape(dec_batch * N_LANE_BLOCKS, LANES)
    pv, pp, sv, sp = [], [], [], []
    for i in range(depth):
        w_s_rows = w_s[i].reshape(D_MODEL, CHUNK)
        params = _pack_params(
            [w_in[i], w_s_rows, w_pool[i].reshape(D_MODEL, POOL_GROUP), w_pa[i], w_pb[i],
             w_out[i], w_ple[i], w_pg[i]],
            (None, CHUNK) + (None,) * 6, b_s[i],
            [v[i:i + 1] for v in (pre_g, ln_g, ln_b, pool_scale, post_g, ple_in_g, ple_out_g)],
            n_steps=PACK_STEPS)
        hp, v, pool, hs, v_s, pool_s = _layer(
            hp, p_prompt[i], hs, p_sample[i].reshape(dec_batch * PLE_DIM // LANES, LANES),
            jnp.swapaxes(state_pool[i], 0, 1), w_s_rows, params, TILE_M, PAST_LEN)
        pv.append(v)
        pp.append(jnp.swapaxes(pool, 0, 1))
        sv.append(v_s.reshape(dec_batch, 1, D_MODEL))
        sp.append(jnp.swapaxes(pool_s, 0, 1))
    return (hp, hs.reshape(dec_batch, 1, D_MODEL), jnp.stack(pv), jnp.stack(pp), jnp.stack(sv),
            jnp.stack(sp))
```

```python
import functools
from typing import Any, NamedTuple

import jax
import jax.numpy as jnp
from jax import lax
from jax.experimental import pallas as pl
from jax.experimental.pallas import tpu as pltpu

D_MODEL = 1024
CHUNK = 128
N_HEADS = D_MODEL // CHUNK
POOL_WINDOWS = (2, 4, 8, 16)
POOL_GROUP = D_MODEL // len(POOL_WINDOWS)
POOL_BUF = max(POOL_WINDOWS) - 1
PLE_DIM = 256
PAST_LEN = 16384
SAMPLE_ROWS = 128
EPS = 1e-6
COL_U, COL_V, COL_ZA, COL_XB, COL_ZB, COL_GA, COL_GB = range(7)
VEC_ROWS = tuple(range(7))
VEC_PRE_G, VEC_LN_G, VEC_LN_B, VEC_POOL_SCALE, VEC_POST_G, VEC_PLE_IN_G, VEC_PLE_OUT_G = VEC_ROWS
N_VEC_ROWS = 8

LANES = 128
N_LANE_BLOCKS = D_MODEL // LANES
BF16_ROWS = 16
TAIL = 16
PACK_STEPS = 8
TILE_M = 256
VMEM_LIMIT_BYTES = 56 * 1024 * 1024

_BF16 = jnp.bfloat16
_F32 = jnp.float32
_GELU_C = 0.7978845608028654


def _dot(a, b):
    return jnp.dot(a, b, preferred_element_type=_F32)


def _rmsnorm(x, g):
    return x * lax.rsqrt(jnp.mean(x * x, axis=-1, keepdims=True) + EPS) * g


def _layernorm(x, g, b):
    xc = x - jnp.mean(x, axis=-1, keepdims=True)
    return xc * lax.rsqrt(jnp.mean(xc * xc, axis=-1, keepdims=True) + EPS) * g + b


def _gelu_tanh(x):
    return 0.5 * x * (1.0 + jnp.tanh(_GELU_C * (x + 0.044715 * (x * x * x))))


def _sigmoid(x):
    return 0.5 * jnp.tanh(0.5 * x) + 0.5


def _silu(x):
    return x * _sigmoid(x)


def _unpack(ref, rows=slice(None), cols=slice(None)):
    return pltpu.bitcast(ref[rows, cols], _BF16)


class _Weights(NamedTuple):
    in_proj_w: Any
    spatial: Any
    pool: Any
    pa: Any
    pb: Any
    out: Any
    ple: Any
    pg: Any
    bias_t: Any
    vecs: Any

    def vec(self, row):
        return self.vecs[row:row + 1, :]

    def in_proj(self, xn, col):
        return _dot(xn, _unpack(self.in_proj_w, cols=slice(col * D_MODEL, (col + 1) * D_MODEL)))


def _pool_mix(pooled, w):
    parts = []
    for g in range(len(POOL_WINDOWS)):
        q = pooled[:, g * POOL_GROUP:(g + 1) * POOL_GROUP].astype(_BF16)
        rows = slice(g * POOL_GROUP // 2, (g + 1) * POOL_GROUP // 2)
        parts.append(_dot(q, _unpack(w.pool, rows=rows)))
    return jnp.concatenate(parts, axis=1) * w.vec(VEC_POOL_SCALE)


def _merge_and_embed(h, p, xn, y_a, y_b, w):
    g_a = w.in_proj(xn, COL_GA)
    g_b = w.in_proj(xn, COL_GB)
    m = (_sigmoid(g_a) * _dot(y_a.astype(_BF16), _unpack(w.pa))
         + _sigmoid(g_b) * _dot(y_b.astype(_BF16), _unpack(w.pb)))
    h = h + _rmsnorm(_dot(m.astype(_BF16), _unpack(w.out)), w.vec(VEC_POST_G))
    e = _dot(p.astype(_BF16), _unpack(w.ple))
    hn = _rmsnorm(h, w.vec(VEC_PLE_IN_G)).astype(_BF16)
    gate = _sigmoid(_dot(hn, _unpack(w.pg)))
    return h + _rmsnorm(gate * e, w.vec(VEC_PLE_OUT_G))


def _spatial_mix(vn_b, w, tile_m):
    w_tril = [_unpack(w.spatial, rows=slice(hd * CHUNK // 2, (hd + 1) * CHUNK // 2))
              for hd in range(N_HEADS)]
    bias = [w.bias_t[:, hd:hd + 1] for hd in range(N_HEADS)]
    s_rows = []
    for c in range(tile_m // CHUNK):
        heads = []
        for hd in range(N_HEADS):
            blk = vn_b[c * CHUNK:(c + 1) * CHUNK, hd * CHUNK:(hd + 1) * CHUNK]
            heads.append(_dot(w_tril[hd], blk) + bias[hd])
        s_rows.append(jnp.concatenate(heads, axis=1))
    return jnp.concatenate(s_rows, axis=0)


def _window_pool(x_b, ext_ref, pool_ref, b, j, tile_m):
    for c in range(N_LANE_BLOCKS):
        ext_ref[c, TAIL:TAIL + tile_m, :] = x_b[:, c * LANES:(c + 1) * LANES]
    pos = j * tile_m + lax.broadcasted_iota(jnp.int32, (tile_m, 1), 0)
    blocks = []
    for c in range(N_LANE_BLOCKS):
        w = POOL_WINDOWS[c * LANES // POOL_GROUP]
        cur = x_b[:, c * LANES:(c + 1) * LANES]
        win = cur
        for k in range(1, w):
            win = win + ext_ref[c, TAIL - k:TAIL - k + tile_m, :]
        cnt = jnp.minimum(pos + 1, w).astype(_F32)
        blocks.append(win / cnt - cur)
    latest = jnp.concatenate([ext_ref[c, TAIL + tile_m - POOL_BUF:TAIL + tile_m, :]
                              for c in range(N_LANE_BLOCKS)], axis=1)
    pool_ref[:, pl.ds(b, 1), :] = latest[:, None, :]
    for c in range(N_LANE_BLOCKS):
        ext_ref[c, 0:TAIL, :] = ext_ref[c, tile_m:tile_m + TAIL, :]
    return jnp.concatenate(blocks, axis=1)


def _layer_kernel(x_ref, x_lag_ref, p_lag_ref, xs_ref, ps_ref, taps_ref, state_hbm,
                  w_in_ref, w_s_ref, w_pool_ref, w_pa_ref, w_pb_ref, w_out_ref, w_ple_ref,
                  w_pg_ref, bs_t_ref, vecs_ref,
                  y_ref, vn_ref, pool_ref, ys_ref, vns_ref, spool_hbm,
                  ext_ref, xn_c, ya_c, yb_c, hist_ref, xbs_ref, sems,
                  *, tile_m, tiles_per_prompt, n_tiles, pos0):
    s_id = pl.program_id(0)
    front = jnp.minimum(s_id, n_tiles - 1)
    b = front // tiles_per_prompt
    j = lax.rem(front, tiles_per_prompt)
    w = _Weights(in_proj_w=w_in_ref, spatial=w_s_ref, pool=w_pool_ref, pa=w_pa_ref, pb=w_pb_ref,
                 out=w_out_ref, ple=w_ple_ref, pg=w_pg_ref, bias_t=bs_t_ref, vecs=vecs_ref)

    @pl.when(j == 0)
    def _():
        ext_ref[:, 0:TAIL, :] = jnp.zeros((N_LANE_BLOCKS, TAIL, LANES), _F32)

    @pl.when(s_id == 0)
    def _front_only():
        xn = _rmsnorm(x_ref[0], w.vec(VEC_PRE_G)).astype(_BF16)
        xn_c[...] = xn
        vn = _layernorm(_gelu_tanh(w.in_proj(xn, COL_V)), w.vec(VEC_LN_G), w.vec(VEC_LN_B))
        vn_ref[0] = vn[tile_m - CHUNK:, :]
        s = _spatial_mix(vn.astype(_BF16), w, tile_m)
        ya_c[...] = (_gelu_tanh(w.in_proj(xn, COL_U)) * s
                     * _silu(w.in_proj(xn, COL_ZA))).astype(_BF16)
        pooled = _window_pool(w.in_proj(xn, COL_XB), ext_ref, pool_ref, b, j, tile_m)
        yb_c[...] = (_pool_mix(pooled, w) * _silu(w.in_proj(xn, COL_ZB))).astype(_BF16)

    @pl.when(s_id == n_tiles)
    def _back_only_and_sample():
        history, shift, newest = _sample_copies(state_hbm, spool_hbm, hist_ref, xbs_ref, sems)
        history.start()
        y_ref[0] = _merge_and_embed(x_lag_ref[0], p_lag_ref[0], xn_c[...], ya_c[...], yb_c[...], w)
        history.wait()
        shift.start()
        _sample_rows(xs_ref, ps_ref, taps_ref, hist_ref, newest, xbs_ref, w, ys_ref, vns_ref, pos0)
        shift.wait()
        newest.wait()

    @pl.when(jnp.logical_and(s_id > 0, s_id < n_tiles))
    def _both():
        xn_prev = xn_c[...]
        g_a = w.in_proj(xn_prev, COL_GA)
        xn = _rmsnorm(x_ref[0], w.vec(VEC_PRE_G)).astype(_BF16)
        xn_c[...] = xn
        v_raw = w.in_proj(xn, COL_V)
        g_b = w.in_proj(xn_prev, COL_GB)
        vn = _layernorm(_gelu_tanh(v_raw), w.vec(VEC_LN_G), w.vec(VEC_LN_B))
        vn_ref[0] = vn[tile_m - CHUNK:, :]
        u_raw = w.in_proj(xn, COL_U)
        s = _spatial_mix(vn.astype(_BF16), w, tile_m)
        proj_a = _dot(ya_c[...], _unpack(w.pa))
        proj_b = _dot(yb_c[...], _unpack(w.pb))
        za_raw = w.in_proj(xn, COL_ZA)
        m = (_sigmoid(g_a) * proj_a + _sigmoid(g_b) * proj_b).astype(_BF16)
        o = _dot(m, _unpack(w.out))
        x_b = w.in_proj(xn, COL_XB)
        ya_c[...] = (_gelu_tanh(u_raw) * s * _silu(za_raw)).astype(_BF16)
        zb_raw = w.in_proj(xn, COL_ZB)
        pooled = _window_pool(x_b, ext_ref, pool_ref, b, j, tile_m)
        mixed = _pool_mix(pooled, w)
        h = x_lag_ref[0] + _rmsnorm(o, w.vec(VEC_POST_G))
        hn = _rmsnorm(h, w.vec(VEC_PLE_IN_G)).astype(_BF16)
        e = _dot(p_lag_ref[0].astype(_BF16), _unpack(w.ple))
        gate_raw = _dot(hn, _unpack(w.pg))
        yb_c[...] = (mixed * _silu(zb_raw)).astype(_BF16)
        y_ref[0] = h + _rmsnorm(_sigmoid(gate_raw) * e, w.vec(VEC_PLE_OUT_G))


def _gather_rows(ref):
    n = ref.shape[0] // SAMPLE_ROWS
    return jnp.concatenate([ref[pl.ds(c, SAMPLE_ROWS, stride=n), :] for c in range(n)], axis=1)


def _scatter_rows(ref, value):
    n = ref.shape[0] // SAMPLE_ROWS
    for c in range(n):
        ref[pl.ds(c, SAMPLE_ROWS, stride=n), :] = value[:, c * LANES:(c + 1) * LANES]


def _sample_copies(state_hbm, spool_hbm, hist_ref, xbs_ref, sems):
    history = pltpu.make_async_copy(state_hbm, hist_ref, sems.at[0])
    shift = pltpu.make_async_copy(hist_ref.at[pl.ds(1, POOL_BUF - 1)],
                                  spool_hbm.at[pl.ds(0, POOL_BUF - 1)], sems.at[1])
    newest = pltpu.make_async_copy(xbs_ref, spool_hbm.at[POOL_BUF - 1], sems.at[2])
    return history, shift, newest


def _sample_rows(x_ref, p_ref, taps_ref, hist_ref, newest, xbs_ref, w, y_ref, vn_ref, pos0):
    x = _gather_rows(x_ref)
    xn = _rmsnorm(x, w.vec(VEC_PRE_G)).astype(_BF16)

    u = _gelu_tanh(w.in_proj(xn, COL_U))
    vn = _layernorm(_gelu_tanh(w.in_proj(xn, COL_V)), w.vec(VEC_LN_G), w.vec(VEC_LN_B))
    _scatter_rows(vn_ref, vn)
    heads = []
    for hd in range(N_HEADS):
        tap = taps_ref[hd * CHUNK:hd * CHUNK + 1, 0:1]
        heads.append(vn[:, hd * CHUNK:(hd + 1) * CHUNK] * tap + w.bias_t[0:1, hd:hd + 1])
    s = jnp.concatenate(heads, axis=1)
    y_a = u * s * _silu(w.in_proj(xn, COL_ZA))

    x_b = w.in_proj(xn, COL_XB)
    xbs_ref[...] = x_b
    newest.start()
    groups = []
    for g, win_len in enumerate(POOL_WINDOWS):
        lo, hi = g * POOL_GROUP, (g + 1) * POOL_GROUP
        cur = x_b[:, lo:hi]
        win = cur
        for k in range(1, win_len):
            win = win + hist_ref[POOL_BUF - k, :, lo:hi]
        groups.append(win / float(min(pos0 + 1, win_len)) - cur)
    pooled = jnp.concatenate(groups, axis=1)
    y_b = _pool_mix(pooled, w) * _silu(w.in_proj(xn, COL_ZB))

    _scatter_rows(y_ref, _merge_and_embed(x, _gather_rows(p_ref), xn, y_a, y_b, w))


def _pack_kernel(*refs, tril_periods):
    n = len(tril_periods)
    n_vecs = len(VEC_ROWS)
    bias_ref, vec_refs = refs[n], refs[n + 1:n + 1 + n_vecs]
    outs = refs[n + 1 + n_vecs:]
    bias_t_ref, vecs_ref = outs[n], outs[n + 1]
    bias_t_ref[...] = bias_ref[...].T
    for row, vec_ref in zip(VEC_ROWS, vec_refs):
        vecs_ref[row:row + 1, :] = vec_ref[...]
    vecs_ref[n_vecs:, :] = jnp.zeros((N_VEC_ROWS - n_vecs, D_MODEL), _F32)
    for src, dst, period in zip(refs[:n], outs[:n], tril_periods):
        m = src[...]
        if period is not None:
            row = lax.broadcasted_iota(jnp.int32, m.shape, 0) % period
            m = jnp.where(row >= lax.broadcasted_iota(jnp.int32, m.shape, 1), m, 0.0)
        dst[...] = pltpu.bitcast(m.astype(_BF16), jnp.uint32)


def _pack_params(mats, tril_periods, bias, vecs, n_steps):
    blocks = [m.shape[0] // n_steps for m in mats]
    small = lambda shape: pl.BlockSpec(shape, lambda i: (0, 0))
    for m, rows, period in zip(mats, blocks, tril_periods):
        assert rows * n_steps == m.shape[0] and rows % (2 * BF16_ROWS) == 0
        assert period is None or rows % period == 0
    return pl.pallas_call(
        functools.partial(_pack_kernel, tril_periods=tril_periods),
        grid=(n_steps,),
        in_specs=[pl.BlockSpec((rows, m.shape[1]), lambda i: (i, 0))
                  for m, rows in zip(mats, blocks)]
                 + [small(bias.shape)] + [small(v.shape) for v in vecs],
        out_specs=[pl.BlockSpec((rows // 2, m.shape[1]), lambda i: (i, 0))
                   for m, rows in zip(mats, blocks)]
                  + [small(bias.shape[::-1]), small((N_VEC_ROWS, D_MODEL))],
        out_shape=[jax.ShapeDtypeStruct((m.shape[0] // 2, m.shape[1]), jnp.uint32) for m in mats]
                  + [jax.ShapeDtypeStruct(bias.shape[::-1], _F32),
                     jax.ShapeDtypeStruct((N_VEC_ROWS, D_MODEL), _F32)],
        compiler_params=pltpu.CompilerParams(dimension_semantics=("arbitrary",),
                                             vmem_limit_bytes=VMEM_LIMIT_BYTES),
        name="pack_params",
    )(*mats, bias, *vecs)


def _resident(shape):
    zeros = (0,) * len(shape)
    return pl.BlockSpec(shape, lambda *_: zeros, pipeline_mode=pl.Buffered(1))


def _layer(h, p, hs, ps, state, taps, weights, tile_m, pos0):
    batch, seq, _ = h.shape
    rows = state.shape[1]
    assert seq % tile_m == 0 and tile_m % CHUNK == 0 and seq >= POOL_BUF and rows == SAMPLE_ROWS
    tiles_per_prompt = seq // tile_m
    n_tiles = batch * tiles_per_prompt
    assert n_tiles >= 2

    def front_tile(s):
        t = jnp.minimum(s, n_tiles - 1)
        return t // tiles_per_prompt, t % tiles_per_prompt, 0

    def back_tile(s):
        t = jnp.maximum(s - 1, 0)
        return t // tiles_per_prompt, t % tiles_per_prompt, 0

    per_prompt = lambda s: (jnp.minimum(s, n_tiles - 1) // tiles_per_prompt, 0, 0)
    hbm = pl.BlockSpec(memory_space=pl.ANY)
    whole = lambda a: pl.BlockSpec(a.shape, lambda s: (0,) * a.ndim)
    return pl.pallas_call(
        functools.partial(_layer_kernel, tile_m=tile_m, tiles_per_prompt=tiles_per_prompt,
                          n_tiles=n_tiles, pos0=pos0),
        grid=(n_tiles + 1,),
        in_specs=[pl.BlockSpec((1, tile_m, D_MODEL), front_tile),
                  pl.BlockSpec((1, tile_m, D_MODEL), back_tile),
                  pl.BlockSpec((1, tile_m, PLE_DIM), back_tile),
                  _resident(hs.shape), _resident(ps.shape), _resident(taps.shape), hbm]
                 + [_resident(a.shape) for a in weights],
        out_specs=[pl.BlockSpec((1, tile_m, D_MODEL), back_tile),
                   pl.BlockSpec((1, CHUNK, D_MODEL), per_prompt),
                   pl.BlockSpec((POOL_BUF, batch, D_MODEL), lambda s: (0, 0, 0)),
                   whole(hs), whole(hs), hbm],
        out_shape=[jax.ShapeDtypeStruct((batch, seq, D_MODEL), _F32),
                   jax.ShapeDtypeStruct((batch, CHUNK, D_MODEL), _F32),
                   jax.ShapeDtypeStruct((POOL_BUF, batch, D_MODEL), _F32),
                   jax.ShapeDtypeStruct(hs.shape, _F32),
                   jax.ShapeDtypeStruct(hs.shape, _F32),
                   jax.ShapeDtypeStruct(state.shape, _F32)],
        scratch_shapes=[pltpu.VMEM((N_LANE_BLOCKS, TAIL + tile_m, LANES), _F32),
                        pltpu.VMEM((tile_m, D_MODEL), _BF16),
                        pltpu.VMEM((tile_m, D_MODEL), _BF16),
                        pltpu.VMEM((tile_m, D_MODEL), _BF16),
                        pltpu.VMEM(state.shape, _F32),
                        pltpu.VMEM((rows, D_MODEL), _F32),
                        pltpu.SemaphoreType.DMA((3,))],
        compiler_params=pltpu.CompilerParams(
            dimension_semantics=("arbitrary",),
            vmem_limit_bytes=VMEM_LIMIT_BYTES),
        name="layer",
    )(h, h, p, hs, ps, taps, state, *weights)


def kernel(x_prompt, x_sample, state_pool, p_prompt, p_sample, pre_g, w_in, ln_g, ln_b, w_s, b_s,
           w_pool, pool_scale, w_pa, w_pb, w_out, post_g, w_ple, w_pg, ple_in_g, ple_out_g):
    depth = w_in.shape[0]
    dec_batch, dec_seq, _ = x_sample.shape
    assert dec_seq == 1 and x_prompt.shape[2] == D_MODEL
    hp = x_prompt
    hs = x_sample.reshape(dec_batch * N_LANE_BLOCKS, LANES)
    pv, pp, sv, sp = [], [], [], []
    for i in range(depth):
        w_s_rows = w_s[i].reshape(D_MODEL, CHUNK)
        params = _pack_params(
            [w_in[i], w_s_rows, w_pool[i].reshape(D_MODEL, POOL_GROUP), w_pa[i], w_pb[i],
             w_out[i], w_ple[i], w_pg[i]],
            (None, CHUNK) + (None,) * 6, b_s[i],
            [v[i:i + 1] for v in (pre_g, ln_g, ln_b, pool_scale, post_g, ple_in_g, ple_out_g)],
            n_steps=PACK_STEPS)
        hp, v, pool, hs, v_s, pool_s = _layer(
            hp, p_prompt[i], hs, p_sample[i].reshape(dec_batch * PLE_DIM // LANES, LANES),
            jnp.swapaxes(state_pool[i], 0, 1), w_s_rows, params, TILE_M, PAST_LEN)
        pv.append(v)
        pp.append(jnp.swapaxes(pool, 0, 1))
        sv.append(v_s.reshape(dec_batch, 1, D_MODEL))
        sp.append(jnp.swapaxes(pool_s, 0, 1))
    return (hp, hs.reshape(dec_batch, 1, D_MODEL), jnp.stack(pv), jnp.stack(pp), jnp.stack(sv),
            jnp.stack(sp))
```

```python
import functools
from typing import Any, NamedTuple

import jax
import jax.numpy as jnp
from jax import lax
from jax.experimental import pallas as pl
from jax.experimental.pallas import tpu as pltpu

D_MODEL = 1024
CHUNK = 128
N_HEADS = D_MODEL // CHUNK
POOL_WINDOWS = (2, 4, 8, 16)
POOL_GROUP = D_MODEL // len(POOL_WINDOWS)
POOL_BUF = max(POOL_WINDOWS) - 1
PLE_DIM = 256
PAST_LEN = 16384
SAMPLE_ROWS = 128
EPS = 1e-6
COL_U, COL_V, COL_ZA, COL_XB, COL_ZB, COL_GA, COL_GB = range(7)
VEC_ROWS = tuple(range(7))
VEC_PRE_G, VEC_LN_G, VEC_LN_B, VEC_POOL_SCALE, VEC_POST_G, VEC_PLE_IN_G, VEC_PLE_OUT_G = VEC_ROWS
N_VEC_ROWS = 8

LANES = 128
N_LANE_BLOCKS = D_MODEL // LANES
BF16_ROWS = 16
TAIL = 16
PACK_STEPS = 8
TILE_M = 256
VMEM_LIMIT_BYTES = 56 * 1024 * 1024

_BF16 = jnp.bfloat16
_F32 = jnp.float32
_GELU_C = 0.7978845608028654


def _dot(a, b):
    return jnp.dot(a, b, preferred_element_type=_F32)


def _rmsnorm(x, g):
    return x * lax.rsqrt(jnp.mean(x * x, axis=-1, keepdims=True) + EPS) * g


def _layernorm(x, g, b):
    xc = x - jnp.mean(x, axis=-1, keepdims=True)
    return xc * lax.rsqrt(jnp.mean(xc * xc, axis=-1, keepdims=True) + EPS) * g + b


def _gelu_tanh(x):
    return 0.5 * x * (1.0 + jnp.tanh(_GELU_C * (x + 0.044715 * (x * x * x))))


def _sigmoid(x):
    return 0.5 * jnp.tanh(0.5 * x) + 0.5


def _silu(x):
    return x * _sigmoid(x)


def _unpack(ref, rows=slice(None), cols=slice(None)):
    return pltpu.bitcast(ref[rows, cols], _BF16)


class _Weights(NamedTuple):
    in_proj_w: Any
    spatial: Any
    pool: Any
    pa: Any
    pb: Any
    out: Any
    ple: Any
    pg: Any
    bias_t: Any
    vecs: Any

    def vec(self, row):
        return self.vecs[row:row + 1, :]

    def in_proj(self, xn, col):
        return _dot(xn, _unpack(self.in_proj_w, cols=slice(col * D_MODEL, (col + 1) * D_MODEL)))


def _pool_mix(pooled, w):
    parts = []
    for g in range(len(POOL_WINDOWS)):
        q = pooled[:, g * POOL_GROUP:(g + 1) * POOL_GROUP].astype(_BF16)
        rows = slice(g * POOL_GROUP // 2, (g + 1) * POOL_GROUP // 2)
        parts.append(_dot(q, _unpack(w.pool, rows=rows)))
    return jnp.concatenate(parts, axis=1) * w.vec(VEC_POOL_SCALE)


def _merge_and_embed(h, p, xn, y_a, y_b, w):
    g_a = w.in_proj(xn, COL_GA)
    g_b = w.in_proj(xn, COL_GB)
    m = (_sigmoid(g_a) * _dot(y_a.astype(_BF16), _unpack(w.pa))
         + _sigmoid(g_b) * _dot(y_b.astype(_BF16), _unpack(w.pb)))
    h = h + _rmsnorm(_dot(m.astype(_BF16), _unpack(w.out)), w.vec(VEC_POST_G))
    e = _dot(p.astype(_BF16), _unpack(w.ple))
    hn = _rmsnorm(h, w.vec(VEC_PLE_IN_G)).astype(_BF16)
    gate = _sigmoid(_dot(hn, _unpack(w.pg)))
    return h + _rmsnorm(gate * e, w.vec(VEC_PLE_OUT_G))


def _spatial_mix(vn_b, w, tile_m):
    w_tril = [_unpack(w.spatial, rows=slice(hd * CHUNK // 2, (hd + 1) * CHUNK // 2))
              for hd in range(N_HEADS)]
    bias = [w.bias_t[:, hd:hd + 1] for hd in range(N_HEADS)]
    s_rows = []
    for c in range(tile_m // CHUNK):
        heads = []
        for hd in range(N_HEADS):
            blk = vn_b[c * CHUNK:(c + 1) * CHUNK, hd * CHUNK:(hd + 1) * CHUNK]
            heads.append(_dot(w_tril[hd], blk) + bias[hd])
        s_rows.append(jnp.concatenate(heads, axis=1))
    return jnp.concatenate(s_rows, axis=0)


def _window_pool(x_b, ext_ref, pool_ref, b, j, tile_m):
    for c in range(N_LANE_BLOCKS):
        ext_ref[c, TAIL:TAIL + tile_m, :] = x_b[:, c * LANES:(c + 1) * LANES]
    pos = j * tile_m + lax.broadcasted_iota(jnp.int32, (tile_m, 1), 0)
    blocks = []
    for c in range(N_LANE_BLOCKS):
        w = POOL_WINDOWS[c * LANES // POOL_GROUP]
        cur = x_b[:, c * LANES:(c + 1) * LANES]
        win = cur
        for k in range(1, w):
            win = win + ext_ref[c, TAIL - k:TAIL - k + tile_m, :]
        cnt = jnp.minimum(pos + 1, w).astype(_F32)
        blocks.append(win / cnt - cur)
    latest = jnp.concatenate([ext_ref[c, TAIL + tile_m - POOL_BUF:TAIL + tile_m, :]
                              for c in range(N_LANE_BLOCKS)], axis=1)
    pool_ref[:, pl.ds(b, 1), :] = latest[:, None, :]
    for c in range(N_LANE_BLOCKS):
        ext_ref[c, 0:TAIL, :] = ext_ref[c, tile_m:tile_m + TAIL, :]
    return jnp.concatenate(blocks, axis=1)


def _layer_kernel(x_ref, x_lag_ref, p_lag_ref, xs_ref, ps_ref, taps_ref, state_hbm,
                  w_in_ref, w_s_ref, w_pool_ref, w_pa_ref, w_pb_ref, w_out_ref, w_ple_ref,
                  w_pg_ref, bs_t_ref, vecs_ref,
                  y_ref, vn_ref, pool_ref, ys_ref, vns_ref, spool_hbm,
                  ext_ref, xn_c, ya_c, yb_c, hist_ref, xbs_ref, sems,
                  *, tile_m, tiles_per_prompt, n_tiles, pos0):
    s_id = pl.program_id(0)
    front = jnp.minimum(s_id, n_tiles - 1)
    b = front // tiles_per_prompt
    j = lax.rem(front, tiles_per_prompt)
    w = _Weights(in_proj_w=w_in_ref, spatial=w_s_ref, pool=w_pool_ref, pa=w_pa_ref, pb=w_pb_ref,
                 out=w_out_ref, ple=w_ple_ref, pg=w_pg_ref, bias_t=bs_t_ref, vecs=vecs_ref)

    @pl.when(j == 0)
    def _():
        ext_ref[:, 0:TAIL, :] = jnp.zeros((N_LANE_BLOCKS, TAIL, LANES), _F32)

    @pl.when(s_id == 0)
    def _front_only():
        xn = _rmsnorm(x_ref[0], w.vec(VEC_PRE_G)).astype(_BF16)
        xn_c[...] = xn
        vn = _layernorm(_gelu_tanh(w.in_proj(xn, COL_V)), w.vec(VEC_LN_G), w.vec(VEC_LN_B))
        vn_ref[0] = vn[tile_m - CHUNK:, :]
        s = _spatial_mix(vn.astype(_BF16), w, tile_m)
        ya_c[...] = (_gelu_tanh(w.in_proj(xn, COL_U)) * s
                     * _silu(w.in_proj(xn, COL_ZA))).astype(_BF16)
        pooled = _window_pool(w.in_proj(xn, COL_XB), ext_ref, pool_ref, b, j, tile_m)
        yb_c[...] = (_pool_mix(pooled, w) * _silu(w.in_proj(xn, COL_ZB))).astype(_BF16)

    @pl.when(s_id == n_tiles)
    def _back_only_and_sample():
        history, shift, newest = _sample_copies(state_hbm, spool_hbm, hist_ref, xbs_ref, sems)
        history.start()
        y_ref[0] = _merge_and_embed(x_lag_ref[0], p_lag_ref[0], xn_c[...], ya_c[...], yb_c[...], w)
        history.wait()
        shift.start()
        _sample_rows(xs_ref, ps_ref, taps_ref, hist_ref, newest, xbs_ref, w, ys_ref, vns_ref, pos0)
        shift.wait()
        newest.wait()

    @pl.when(jnp.logical_and(s_id > 0, s_id < n_tiles))
    def _both():
        xn_prev = xn_c[...]
        g_a = w.in_proj(xn_prev, COL_GA)
        xn = _rmsnorm(x_ref[0], w.vec(VEC_PRE_G)).astype(_BF16)
        xn_c[...] = xn
        v_raw = w.in_proj(xn, COL_V)
        g_b = w.in_proj(xn_prev, COL_GB)
        vn = _layernorm(_gelu_tanh(v_raw), w.vec(VEC_LN_G), w.vec(VEC_LN_B))
        vn_ref[0] = vn[tile_m - CHUNK:, :]
        u_raw = w.in_proj(xn, COL_U)
        s = _spatial_mix(vn.astype(_BF16), w, tile_m)
        proj_a = _dot(ya_c[...], _unpack(w.pa))
        proj_b = _dot(yb_c[...], _unpack(w.pb))
        m = (_sigmoid(g_a) * proj_a + _sigmoid(g_b) * proj_b).astype(_BF16)
        o = _dot(m, _unpack(w.out))
        za_raw = w.in_proj(xn, COL_ZA)
        x_b = w.in_proj(xn, COL_XB)
        ya_c[...] = (_gelu_tanh(u_raw) * s * _silu(za_raw)).astype(_BF16)
        pooled = _window_pool(x_b, ext_ref, pool_ref, b, j, tile_m)
        mixed = _pool_mix(pooled, w)
        h = x_lag_ref[0] + _rmsnorm(o, w.vec(VEC_POST_G))
        hn = _rmsnorm(h, w.vec(VEC_PLE_IN_G)).astype(_BF16)
        e = _dot(p_lag_ref[0].astype(_BF16), _unpack(w.ple))
        gate_raw = _dot(hn, _unpack(w.pg))
        zb_raw = w.in_proj(xn, COL_ZB)
        y_ref[0] = h + _rmsnorm(_sigmoid(gate_raw) * e, w.vec(VEC_PLE_OUT_G))
        yb_c[...] = (mixed * _silu(zb_raw)).astype(_BF16)


def _gather_rows(ref):
    n = ref.shape[0] // SAMPLE_ROWS
    return jnp.concatenate([ref[pl.ds(c, SAMPLE_ROWS, stride=n), :] for c in range(n)], axis=1)


def _scatter_rows(ref, value):
    n = ref.shape[0] // SAMPLE_ROWS
    for c in range(n):
        ref[pl.ds(c, SAMPLE_ROWS, stride=n), :] = value[:, c * LANES:(c + 1) * LANES]


def _sample_copies(state_hbm, spool_hbm, hist_ref, xbs_ref, sems):
    history = pltpu.make_async_copy(state_hbm, hist_ref, sems.at[0])
    shift = pltpu.make_async_copy(hist_ref.at[pl.ds(1, POOL_BUF - 1)],
                                  spool_hbm.at[pl.ds(0, POOL_BUF - 1)], sems.at[1])
    newest = pltpu.make_async_copy(xbs_ref, spool_hbm.at[POOL_BUF - 1], sems.at[2])
    return history, shift, newest


def _sample_rows(x_ref, p_ref, taps_ref, hist_ref, newest, xbs_ref, w, y_ref, vn_ref, pos0):
    x = _gather_rows(x_ref)
    xn = _rmsnorm(x, w.vec(VEC_PRE_G)).astype(_BF16)

    u = _gelu_tanh(w.in_proj(xn, COL_U))
    vn = _layernorm(_gelu_tanh(w.in_proj(xn, COL_V)), w.vec(VEC_LN_G), w.vec(VEC_LN_B))
    _scatter_rows(vn_ref, vn)
    heads = []
    for hd in range(N_HEADS):
        tap = taps_ref[hd * CHUNK:hd * CHUNK + 1, 0:1]
        heads.append(vn[:, hd * CHUNK:(hd + 1) * CHUNK] * tap + w.bias_t[0:1, hd:hd + 1])
    s = jnp.concatenate(heads, axis=1)
    y_a = u * s * _silu(w.in_proj(xn, COL_ZA))

    x_b = w.in_proj(xn, COL_XB)
    xbs_ref[...] = x_b
    newest.start()
    groups = []
    for g, win_len in enumerate(POOL_WINDOWS):
        lo, hi = g * POOL_GROUP, (g + 1) * POOL_GROUP
        cur = x_b[:, lo:hi]
        win = cur
        for k in range(1, win_len):
            win = win + hist_ref[POOL_BUF - k, :, lo:hi]
        groups.append(win / float(min(pos0 + 1, win_len)) - cur)
    pooled = jnp.concatenate(groups, axis=1)
    y_b = _pool_mix(pooled, w) * _silu(w.in_proj(xn, COL_ZB))

    _scatter_rows(y_ref, _merge_and_embed(x, _gather_rows(p_ref), xn, y_a, y_b, w))


def _pack_kernel(*refs, tril_periods):
    n = len(tril_periods)
    n_vecs = len(VEC_ROWS)
    bias_ref, vec_refs = refs[n], refs[n + 1:n + 1 + n_vecs]
    outs = refs[n + 1 + n_vecs:]
    bias_t_ref, vecs_ref = outs[n], outs[n + 1]
    bias_t_ref[...] = bias_ref[...].T
    for row, vec_ref in zip(VEC_ROWS, vec_refs):
        vecs_ref[row:row + 1, :] = vec_ref[...]
    vecs_ref[n_vecs:, :] = jnp.zeros((N_VEC_ROWS - n_vecs, D_MODEL), _F32)
    for src, dst, period in zip(refs[:n], outs[:n], tril_periods):
        m = src[...]
        if period is not None:
            row = lax.broadcasted_iota(jnp.int32, m.shape, 0) % period
            m = jnp.where(row >= lax.broadcasted_iota(jnp.int32, m.shape, 1), m, 0.0)
        dst[...] = pltpu.bitcast(m.astype(_BF16), jnp.uint32)


def _pack_params(mats, tril_periods, bias, vecs, n_steps):
    blocks = [m.shape[0] // n_steps for m in mats]
    small = lambda shape: pl.BlockSpec(shape, lambda i: (0, 0))
    for m, rows, period in zip(mats, blocks, tril_periods):
        assert rows * n_steps == m.shape[0] and rows % (2 * BF16_ROWS) == 0
        assert period is None or rows % period == 0
    return pl.pallas_call(
        functools.partial(_pack_kernel, tril_periods=tril_periods),
        grid=(n_steps,),
        in_specs=[pl.BlockSpec((rows, m.shape[1]), lambda i: (i, 0))
                  for m, rows in zip(mats, blocks)]
                 + [small(bias.shape)] + [small(v.shape) for v in vecs],
        out_specs=[pl.BlockSpec((rows // 2, m.shape[1]), lambda i: (i, 0))
                   for m, rows in zip(mats, blocks)]
                  + [small(bias.shape[::-1]), small((N_VEC_ROWS, D_MODEL))],
        out_shape=[jax.ShapeDtypeStruct((m.shape[0] // 2, m.shape[1]), jnp.uint32) for m in mats]
                  + [jax.ShapeDtypeStruct(bias.shape[::-1], _F32),
                     jax.ShapeDtypeStruct((N_VEC_ROWS, D_MODEL), _F32)],
        compiler_params=pltpu.CompilerParams(dimension_semantics=("arbitrary",),
                                             vmem_limit_bytes=VMEM_LIMIT_BYTES),
        name="pack_params",
    )(*mats, bias, *vecs)


def _resident(shape):
    zeros = (0,) * len(shape)
    return pl.BlockSpec(shape, lambda *_: zeros, pipeline_mode=pl.Buffered(1))


def _layer(h, p, hs, ps, state, taps, weights, tile_m, pos0):
    batch, seq, _ = h.shape
    rows = state.shape[1]
    assert seq % tile_m == 0 and tile_m % CHUNK == 0 and seq >= POOL_BUF and rows == SAMPLE_ROWS
    tiles_per_prompt = seq // tile_m
    n_tiles = batch * tiles_per_prompt
    assert n_tiles >= 2

    def front_tile(s):
        t = jnp.minimum(s, n_tiles - 1)
        return t // tiles_per_prompt, t % tiles_per_prompt, 0

    def back_tile(s):
        t = jnp.maximum(s - 1, 0)
        return t // tiles_per_prompt, t % tiles_per_prompt, 0

    per_prompt = lambda s: (jnp.minimum(s, n_tiles - 1) // tiles_per_prompt, 0, 0)
    hbm = pl.BlockSpec(memory_space=pl.ANY)
    whole = lambda a: pl.BlockSpec(a.shape, lambda s: (0,) * a.ndim)
    return pl.pallas_call(
        functools.partial(_layer_kernel, tile_m=tile_m, tiles_per_prompt=tiles_per_prompt,
                          n_tiles=n_tiles, pos0=pos0),
        grid=(n_tiles + 1,),
        in_specs=[pl.BlockSpec((1, tile_m, D_MODEL), front_tile),
                  pl.BlockSpec((1, tile_m, D_MODEL), back_tile),
                  pl.BlockSpec((1, tile_m, PLE_DIM), back_tile),
                  _resident(hs.shape), _resident(ps.shape), _resident(taps.shape), hbm]
                 + [_resident(a.shape) for a in weights],
        out_specs=[pl.BlockSpec((1, tile_m, D_MODEL), back_tile),
                   pl.BlockSpec((1, CHUNK, D_MODEL), per_prompt),
                   pl.BlockSpec((POOL_BUF, batch, D_MODEL), lambda s: (0, 0, 0)),
                   whole(hs), whole(hs), hbm],
        out_shape=[jax.ShapeDtypeStruct((batch, seq, D_MODEL), _F32),
                   jax.ShapeDtypeStruct((batch, CHUNK, D_MODEL), _F32),
                   jax.ShapeDtypeStruct((POOL_BUF, batch, D_MODEL), _F32),
                   jax.ShapeDtypeStruct(hs.shape, _F32),
                   jax.ShapeDtypeStruct(hs.shape, _F32),
                   jax.ShapeDtypeStruct(state.shape, _F32)],
        scratch_shapes=[pltpu.VMEM((N_LANE_BLOCKS, TAIL + tile_m, LANES), _F32),
                        pltpu.VMEM((tile_m, D_MODEL), _BF16),
                        pltpu.VMEM((tile_m, D_MODEL), _BF16),
                        pltpu.VMEM((tile_m, D_MODEL), _BF16),
                        pltpu.VMEM(state.shape, _F32),
                        pltpu.VMEM((rows, D_MODEL), _F32),
                        pltpu.SemaphoreType.DMA((3,))],
        compiler_params=pltpu.CompilerParams(
            dimension_semantics=("arbitrary",),
            vmem_limit_bytes=VMEM_LIMIT_BYTES),
        name="layer",
    )(h, h, p, hs, ps, taps, state, *weights)


def kernel(x_prompt, x_sample, state_pool, p_prompt, p_sample, pre_g, w_in, ln_g, ln_b, w_s, b_s,
           w_pool, pool_scale, w_pa, w_pb, w_out, post_g, w_ple, w_pg, ple_in_g, ple_out_g):
    depth = w_in.shape[0]
    dec_batch, dec_seq, _ = x_sample.shape
    assert dec_seq == 1 and x_prompt.shape[2] == D_MODEL
    hp = x_prompt
    hs = x_sample.reshape(dec_batch * N_LANE_BLOCKS, LANES)
    pv, pp, sv, sp = [], [], [], []
    for i in range(depth):
        w_s_rows = w_s[i].reshape(D_MODEL, CHUNK)
        params = _pack_params(
            [w_in[i], w_s_rows, w_pool[i].reshape(D_MODEL, POOL_GROUP), w_pa[i], w_pb[i],
             w_out[i], w_ple[i], w_pg[i]],
            (None, CHUNK) + (None,) * 6, b_s[i],
            [v[i:i + 1] for v in (pre_g, ln_g, ln_b, pool_scale, post_g, ple_in_g, ple_out_g)],
            n_steps=PACK_STEPS)
        hp, v, pool, hs, v_s, pool_s = _layer(
            hp, p_prompt[i], hs, p_sample[i].reshape(dec_batch * PLE_DIM // LANES, LANES),
            jnp.swapaxes(state_pool[i], 0, 1), w_s_rows, params, TILE_M, PAST_LEN)
        pv.append(v)
        pp.append(jnp.swapaxes(pool, 0, 1))
        sv.append(v_s.reshape(dec_batch, 1, D_MODEL))
        sp.append(jnp.swapaxes(pool_s, 0, 1))
    return (hp, hs.reshape(dec_batch, 1, D_MODEL), jnp.stack(pv), jnp.stack(pp), jnp.stack(sv),
            jnp.stack(sp))
```

```python
import functools
from typing import Any, NamedTuple

import jax
import jax.numpy as jnp
from jax import lax
from jax.experimental import pallas as pl
from jax.experimental.pallas import tpu as pltpu

D_MODEL = 1024
CHUNK = 128
N_HEADS = D_MODEL // CHUNK
POOL_WINDOWS = (2, 4, 8, 16)
POOL_GROUP = D_MODEL // len(POOL_WINDOWS)
POOL_BUF = max(POOL_WINDOWS) - 1
PLE_DIM = 256
PAST_LEN = 16384
SAMPLE_ROWS = 128
EPS = 1e-6
COL_U, COL_V, COL_ZA, COL_XB, COL_ZB, COL_GA, COL_GB = range(7)
VEC_ROWS = tuple(range(7))
VEC_PRE_G, VEC_LN_G, VEC_LN_B, VEC_POOL_SCALE, VEC_POST_G, VEC_PLE_IN_G, VEC_PLE_OUT_G = VEC_ROWS
N_VEC_ROWS = 8

LANES = 128
N_LANE_BLOCKS = D_MODEL // LANES
BF16_ROWS = 16
SUBLANES = 8
TAIL = 32
DOUBLING_MIN_WINDOW = 8
PACK_STEPS = 8
TILE_M = 256
VMEM_LIMIT_BYTES = 56 * 1024 * 1024

_BF16 = jnp.bfloat16
_F32 = jnp.float32
_GELU_C = 0.7978845608028654


def _dot(a, b):
    return jnp.dot(a, b, preferred_element_type=_F32)


def _rmsnorm(x, g):
    return x * lax.rsqrt(jnp.mean(x * x, axis=-1, keepdims=True) + EPS) * g


def _layernorm(x, g, b):
    xc = x - jnp.mean(x, axis=-1, keepdims=True)
    return xc * lax.rsqrt(jnp.mean(xc * xc, axis=-1, keepdims=True) + EPS) * g + b


def _gelu_tanh(x):
    return 0.5 * x * (1.0 + jnp.tanh(_GELU_C * (x + 0.044715 * (x * x * x))))


def _sigmoid(x):
    return 0.5 * jnp.tanh(0.5 * x) + 0.5


def _silu(x):
    return x * _sigmoid(x)


def _unpack(ref, rows=slice(None), cols=slice(None)):
    return pltpu.bitcast(ref[rows, cols], _BF16)


class _Weights(NamedTuple):
    in_proj_w: Any
    spatial: Any
    pool: Any
    pa: Any
    pb: Any
    out: Any
    ple: Any
    pg: Any
    bias_t: Any
    vecs: Any

    def vec(self, row):
        return self.vecs[row:row + 1, :]

    def in_proj(self, xn, col):
        return _dot(xn, _unpack(self.in_proj_w, cols=slice(col * D_MODEL, (col + 1) * D_MODEL)))


def _pool_mix(pooled, w):
    parts = []
    for g in range(len(POOL_WINDOWS)):
        q = pooled[:, g * POOL_GROUP:(g + 1) * POOL_GROUP].astype(_BF16)
        rows = slice(g * POOL_GROUP // 2, (g + 1) * POOL_GROUP // 2)
        parts.append(_dot(q, _unpack(w.pool, rows=rows)))
    return jnp.concatenate(parts, axis=1) * w.vec(VEC_POOL_SCALE)


def _merge_and_embed(h, p, xn, y_a, y_b, w):
    g_a = w.in_proj(xn, COL_GA)
    g_b = w.in_proj(xn, COL_GB)
    m = (_sigmoid(g_a) * _dot(y_a.astype(_BF16), _unpack(w.pa))
         + _sigmoid(g_b) * _dot(y_b.astype(_BF16), _unpack(w.pb)))
    h = h + _rmsnorm(_dot(m.astype(_BF16), _unpack(w.out)), w.vec(VEC_POST_G))
    e = _dot(p.astype(_BF16), _unpack(w.ple))
    hn = _rmsnorm(h, w.vec(VEC_PLE_IN_G)).astype(_BF16)
    gate = _sigmoid(_dot(hn, _unpack(w.pg)))
    return h + _rmsnorm(gate * e, w.vec(VEC_PLE_OUT_G))


def _spatial_mix(vn_b, w, tile_m):
    w_tril = [_unpack(w.spatial, rows=slice(hd * CHUNK // 2, (hd + 1) * CHUNK // 2))
              for hd in range(N_HEADS)]
    bias = [w.bias_t[:, hd:hd + 1] for hd in range(N_HEADS)]
    s_rows = []
    for c in range(tile_m // CHUNK):
        heads = []
        for hd in range(N_HEADS):
            blk = vn_b[c * CHUNK:(c + 1) * CHUNK, hd * CHUNK:(hd + 1) * CHUNK]
            heads.append(_dot(w_tril[hd], blk) + bias[hd])
        s_rows.append(jnp.concatenate(heads, axis=1))
    return jnp.concatenate(s_rows, axis=0)


def _window_sum(src, lvl, w, rows):
    n_levels = w.bit_length() - 1
    assert w == 1 << n_levels and SUBLANES * (n_levels - 1) + 1 <= TAIL
    for l in range(1, n_levels):
        start = TAIL - SUBLANES * (n_levels - l)
        half = 1 << (l - 1)
        dst = lvl.at[(l - 1) % 2]
        dst[start:rows, :] = src[start:rows, :] + src[start - half:rows - half, :]
        src = dst
    half = w // 2
    return src[TAIL:rows, :] + src[TAIL - half:rows - half, :]


def _window_pool(x_b, ext_ref, lvl_ref, pool_ref, b, j, tile_m):
    rows = TAIL + tile_m
    for c in range(N_LANE_BLOCKS):
        ext_ref[c, TAIL:rows, :] = x_b[:, c * LANES:(c + 1) * LANES]
    pos = j * tile_m + lax.broadcasted_iota(jnp.int32, (tile_m, 1), 0)
    blocks = []
    for c in range(N_LANE_BLOCKS):
        w = POOL_WINDOWS[c * LANES // POOL_GROUP]
        cur = x_b[:, c * LANES:(c + 1) * LANES]
        if w >= DOUBLING_MIN_WINDOW:
            win = _window_sum(ext_ref.at[c], lvl_ref.at[c], w, rows)
        else:
            win = cur
            for k in range(1, w):
                win = win + ext_ref[c, TAIL - k:TAIL - k + tile_m, :]
        cnt = jnp.minimum(pos + 1, w).astype(_F32)
        blocks.append(win / cnt - cur)
    latest = jnp.concatenate([ext_ref[c, TAIL + tile_m - POOL_BUF:TAIL + tile_m, :]
                              for c in range(N_LANE_BLOCKS)], axis=1)
    pool_ref[:, pl.ds(b, 1), :] = latest[:, None, :]
    for c in range(N_LANE_BLOCKS):
        ext_ref[c, 0:TAIL, :] = ext_ref[c, tile_m:tile_m + TAIL, :]
    return jnp.concatenate(blocks, axis=1)


def _layer_kernel(x_ref, x_lag_ref, p_lag_ref, xs_ref, ps_ref, taps_ref, state_hbm,
                  w_in_ref, w_s_ref, w_pool_ref, w_pa_ref, w_pb_ref, w_out_ref, w_ple_ref,
                  w_pg_ref, bs_t_ref, vecs_ref,
                  y_ref, vn_ref, pool_ref, ys_ref, vns_ref, spool_hbm,
                  ext_ref, lvl_ref, xn_c, ya_c, yb_c, hist_ref, xbs_ref, sems,
                  *, tile_m, tiles_per_prompt, n_tiles, pos0):
    s_id = pl.program_id(0)
    front = jnp.minimum(s_id, n_tiles - 1)
    b = front // tiles_per_prompt
    j = lax.rem(front, tiles_per_prompt)
    w = _Weights(in_proj_w=w_in_ref, spatial=w_s_ref, pool=w_pool_ref, pa=w_pa_ref, pb=w_pb_ref,
                 out=w_out_ref, ple=w_ple_ref, pg=w_pg_ref, bias_t=bs_t_ref, vecs=vecs_ref)

    @pl.when(j == 0)
    def _():
        ext_ref[:, 0:TAIL, :] = jnp.zeros((N_LANE_BLOCKS, TAIL, LANES), _F32)

    @pl.when(s_id == 0)
    def _front_only():
        xn = _rmsnorm(x_ref[0], w.vec(VEC_PRE_G)).astype(_BF16)
        xn_c[...] = xn
        vn = _layernorm(_gelu_tanh(w.in_proj(xn, COL_V)), w.vec(VEC_LN_G), w.vec(VEC_LN_B))
        vn_ref[0] = vn[tile_m - CHUNK:, :]
        s = _spatial_mix(vn.astype(_BF16), w, tile_m)
        ya_c[...] = (_gelu_tanh(w.in_proj(xn, COL_U)) * s
                     * _silu(w.in_proj(xn, COL_ZA))).astype(_BF16)
        pooled = _window_pool(w.in_proj(xn, COL_XB), ext_ref, lvl_ref, pool_ref, b, j, tile_m)
        yb_c[...] = (_pool_mix(pooled, w) * _silu(w.in_proj(xn, COL_ZB))).astype(_BF16)

    @pl.when(s_id == n_tiles)
    def _back_only_and_sample():
        history, shift, newest = _sample_copies(state_hbm, spool_hbm, hist_ref, xbs_ref, sems)
        history.start()
        y_ref[0] = _merge_and_embed(x_lag_ref[0], p_lag_ref[0], xn_c[...], ya_c[...], yb_c[...], w)
        history.wait()
        shift.start()
        _sample_rows(xs_ref, ps_ref, taps_ref, hist_ref, newest, xbs_ref, w, ys_ref, vns_ref, pos0)
        shift.wait()
        newest.wait()

    @pl.when(jnp.logical_and(s_id > 0, s_id < n_tiles))
    def _both():
        xn_prev = xn_c[...]
        g_a = w.in_proj(xn_prev, COL_GA)
        xn = _rmsnorm(x_ref[0], w.vec(VEC_PRE_G)).astype(_BF16)
        xn_c[...] = xn
        v_raw = w.in_proj(xn, COL_V)
        g_b = w.in_proj(xn_prev, COL_GB)
        vn = _layernorm(_gelu_tanh(v_raw), w.vec(VEC_LN_G), w.vec(VEC_LN_B))
        vn_ref[0] = vn[tile_m - CHUNK:, :]
        u_raw = w.in_proj(xn, COL_U)
        s = _spatial_mix(vn.astype(_BF16), w, tile_m)
        proj_a = _dot(ya_c[...], _unpack(w.pa))
        proj_b = _dot(yb_c[...], _unpack(w.pb))
        za_raw = w.in_proj(xn, COL_ZA)
        m = (_sigmoid(g_a) * proj_a + _sigmoid(g_b) * proj_b).astype(_BF16)
        o = _dot(m, _unpack(w.out))
        x_b = w.in_proj(xn, COL_XB)
        ya_c[...] = (_gelu_tanh(u_raw) * s * _silu(za_raw)).astype(_BF16)
        zb_raw = w.in_proj(xn, COL_ZB)
        pooled = _window_pool(x_b, ext_ref, lvl_ref, pool_ref, b, j, tile_m)
        mixed = _pool_mix(pooled, w)
        h = x_lag_ref[0] + _rmsnorm(o, w.vec(VEC_POST_G))
        hn = _rmsnorm(h, w.vec(VEC_PLE_IN_G)).astype(_BF16)
        e = _dot(p_lag_ref[0].astype(_BF16), _unpack(w.ple))
        gate_raw = _dot(hn, _unpack(w.pg))
        yb_c[...] = (mixed * _silu(zb_raw)).astype(_BF16)
        y_ref[0] = h + _rmsnorm(_sigmoid(gate_raw) * e, w.vec(VEC_PLE_OUT_G))


def _gather_rows(ref):
    n = ref.shape[0] // SAMPLE_ROWS
    return jnp.concatenate([ref[pl.ds(c, SAMPLE_ROWS, stride=n), :] for c in range(n)], axis=1)


def _scatter_rows(ref, value):
    n = ref.shape[0] // SAMPLE_ROWS
    for c in range(n):
        ref[pl.ds(c, SAMPLE_ROWS, stride=n), :] = value[:, c * LANES:(c + 1) * LANES]


def _sample_copies(state_hbm, spool_hbm, hist_ref, xbs_ref, sems):
    history = pltpu.make_async_copy(state_hbm, hist_ref, sems.at[0])
    shift = pltpu.make_async_copy(hist_ref.at[pl.ds(1, POOL_BUF - 1)],
                                  spool_hbm.at[pl.ds(0, POOL_BUF - 1)], sems.at[1])
    newest = pltpu.make_async_copy(xbs_ref, spool_hbm.at[POOL_BUF - 1], sems.at[2])
    return history, shift, newest


def _sample_rows(x_ref, p_ref, taps_ref, hist_ref, newest, xbs_ref, w, y_ref, vn_ref, pos0):
    x = _gather_rows(x_ref)
    xn = _rmsnorm(x, w.vec(VEC_PRE_G)).astype(_BF16)

    u = _gelu_tanh(w.in_proj(xn, COL_U))
    vn = _layernorm(_gelu_tanh(w.in_proj(xn, COL_V)), w.vec(VEC_LN_G), w.vec(VEC_LN_B))
    _scatter_rows(vn_ref, vn)
    heads = []
    for hd in range(N_HEADS):
        tap = taps_ref[hd * CHUNK:hd * CHUNK + 1, 0:1]
        heads.append(vn[:, hd * CHUNK:(hd + 1) * CHUNK] * tap + w.bias_t[0:1, hd:hd + 1])
    s = jnp.concatenate(heads, axis=1)
    y_a = u * s * _silu(w.in_proj(xn, COL_ZA))

    x_b = w.in_proj(xn, COL_XB)
    xbs_ref[...] = x_b
    newest.start()
    groups = []
    for g, win_len in enumerate(POOL_WINDOWS):
        lo, hi = g * POOL_GROUP, (g + 1) * POOL_GROUP
        cur = x_b[:, lo:hi]
        win = cur
        for k in range(1, win_len):
            win = win + hist_ref[POOL_BUF - k, :, lo:hi]
        groups.append(win / float(min(pos0 + 1, win_len)) - cur)
    pooled = jnp.concatenate(groups, axis=1)
    y_b = _pool_mix(pooled, w) * _silu(w.in_proj(xn, COL_ZB))

    _scatter_rows(y_ref, _merge_and_embed(x, _gather_rows(p_ref), xn, y_a, y_b, w))


def _pack_kernel(*refs, tril_periods):
    n = len(tril_periods)
    n_vecs = len(VEC_ROWS)
    bias_ref, vec_refs = refs[n], refs[n + 1:n + 1 + n_vecs]
    outs = refs[n + 1 + n_vecs:]
    bias_t_ref, vecs_ref = outs[n], outs[n + 1]
    bias_t_ref[...] = bias_ref[...].T
    for row, vec_ref in zip(VEC_ROWS, vec_refs):
        vecs_ref[row:row + 1, :] = vec_ref[...]
    vecs_ref[n_vecs:, :] = jnp.zeros((N_VEC_ROWS - n_vecs, D_MODEL), _F32)
    for src, dst, period in zip(refs[:n], outs[:n], tril_periods):
        m = src[...]
        if period is not None:
            row = lax.broadcasted_iota(jnp.int32, m.shape, 0) % period
            m = jnp.where(row >= lax.broadcasted_iota(jnp.int32, m.shape, 1), m, 0.0)
        dst[...] = pltpu.bitcast(m.astype(_BF16), jnp.uint32)


def _pack_params(mats, tril_periods, bias, vecs, n_steps):
    blocks = [m.shape[0] // n_steps for m in mats]
    small = lambda shape: pl.BlockSpec(shape, lambda i: (0, 0))
    for m, rows, period in zip(mats, blocks, tril_periods):
        assert rows * n_steps == m.shape[0] and rows % (2 * BF16_ROWS) == 0
        assert period is None or rows % period == 0
    return pl.pallas_call(
        functools.partial(_pack_kernel, tril_periods=tril_periods),
        grid=(n_steps,),
        in_specs=[pl.BlockSpec((rows, m.shape[1]), lambda i: (i, 0))
                  for m, rows in zip(mats, blocks)]
                 + [small(bias.shape)] + [small(v.shape) for v in vecs],
        out_specs=[pl.BlockSpec((rows // 2, m.shape[1]), lambda i: (i, 0))
                   for m, rows in zip(mats, blocks)]
                  + [small(bias.shape[::-1]), small((N_VEC_ROWS, D_MODEL))],
        out_shape=[jax.ShapeDtypeStruct((m.shape[0] // 2, m.shape[1]), jnp.uint32) for m in mats]
                  + [jax.ShapeDtypeStruct(bias.shape[::-1], _F32),
                     jax.ShapeDtypeStruct((N_VEC_ROWS, D_MODEL), _F32)],
        compiler_params=pltpu.CompilerParams(dimension_semantics=("arbitrary",),
                                             vmem_limit_bytes=VMEM_LIMIT_BYTES),
        name="pack_params",
    )(*mats, bias, *vecs)


def _resident(shape):
    zeros = (0,) * len(shape)
    return pl.BlockSpec(shape, lambda *_: zeros, pipeline_mode=pl.Buffered(1))


def _layer(h, p, hs, ps, state, taps, weights, tile_m, pos0):
    batch, seq, _ = h.shape
    rows = state.shape[1]
    assert seq % tile_m == 0 and tile_m % CHUNK == 0 and seq >= POOL_BUF and rows == SAMPLE_ROWS
    tiles_per_prompt = seq // tile_m
    n_tiles = batch * tiles_per_prompt
    assert n_tiles >= 2

    def front_tile(s):
        t = jnp.minimum(s, n_tiles - 1)
        return t // tiles_per_prompt, t % tiles_per_prompt, 0

    def back_tile(s):
        t = jnp.maximum(s - 1, 0)
        return t // tiles_per_prompt, t % tiles_per_prompt, 0

    per_prompt = lambda s: (jnp.minimum(s, n_tiles - 1) // tiles_per_prompt, 0, 0)
    hbm = pl.BlockSpec(memory_space=pl.ANY)
    whole = lambda a: pl.BlockSpec(a.shape, lambda s: (0,) * a.ndim)
    return pl.pallas_call(
        functools.partial(_layer_kernel, tile_m=tile_m, tiles_per_prompt=tiles_per_prompt,
                          n_tiles=n_tiles, pos0=pos0),
        grid=(n_tiles + 1,),
        in_specs=[pl.BlockSpec((1, tile_m, D_MODEL), front_tile),
                  pl.BlockSpec((1, tile_m, D_MODEL), back_tile),
                  pl.BlockSpec((1, tile_m, PLE_DIM), back_tile),
                  _resident(hs.shape), _resident(ps.shape), _resident(taps.shape), hbm]
                 + [_resident(a.shape) for a in weights],
        out_specs=[pl.BlockSpec((1, tile_m, D_MODEL), back_tile),
                   pl.BlockSpec((1, CHUNK, D_MODEL), per_prompt),
                   pl.BlockSpec((POOL_BUF, batch, D_MODEL), lambda s: (0, 0, 0)),
                   whole(hs), whole(hs), hbm],
        out_shape=[jax.ShapeDtypeStruct((batch, seq, D_MODEL), _F32),
                   jax.ShapeDtypeStruct((batch, CHUNK, D_MODEL), _F32),
                   jax.ShapeDtypeStruct((POOL_BUF, batch, D_MODEL), _F32),
                   jax.ShapeDtypeStruct(hs.shape, _F32),
                   jax.ShapeDtypeStruct(hs.shape, _F32),
                   jax.ShapeDtypeStruct(state.shape, _F32)],
        scratch_shapes=[pltpu.VMEM((N_LANE_BLOCKS, TAIL + tile_m, LANES), _F32),
                        pltpu.VMEM((N_LANE_BLOCKS, 2, TAIL + tile_m, LANES), _F32),
                        pltpu.VMEM((tile_m, D_MODEL), _BF16),
                        pltpu.VMEM((tile_m, D_MODEL), _BF16),
                        pltpu.VMEM((tile_m, D_MODEL), _BF16),
                        pltpu.VMEM(state.shape, _F32),
                        pltpu.VMEM((rows, D_MODEL), _F32),
                        pltpu.SemaphoreType.DMA((3,))],
        compiler_params=pltpu.CompilerParams(
            dimension_semantics=("arbitrary",),
            vmem_limit_bytes=VMEM_LIMIT_BYTES),
        name="layer",
    )(h, h, p, hs, ps, taps, state, *weights)


def kernel(x_prompt, x_sample, state_pool, p_prompt, p_sample, pre_g, w_in, ln_g, ln_b, w_s, b_s,
           w_pool, pool_scale, w_pa, w_pb, w_out, post_g, w_ple, w_pg, ple_in_g, ple_out_g):
    depth = w_in.shape[0]
    dec_batch, dec_seq, _ = x_sample.shape
    assert dec_seq == 1 and x_prompt.shape[2] == D_MODEL
    hp = x_prompt
    hs = x_sample.reshape(dec_batch * N_LANE_BLOCKS, LANES)
    pv, pp, sv, sp = [], [], [], []
    for i in range(depth):
        w_s_rows = w_s[i].reshape(D_MODEL, CHUNK)
        params = _pack_params(
            [w_in[i], w_s_rows, w_pool[i].reshape(D_MODEL, POOL_GROUP), w_pa[i], w_pb[i],
             w_out[i], w_ple[i], w_pg[i]],
            (None, CHUNK) + (None,) * 6, b_s[i],
            [v[i:i + 1] for v in (pre_g, ln_g, ln_b, pool_scale, post_g, ple_in_g, ple_out_g)],
            n_steps=PACK_STEPS)
        hp, v, pool, hs, v_s, pool_s = _layer(
            hp, p_prompt[i], hs, p_sample[i].reshape(dec_batch * PLE_DIM // LANES, LANES),
            jnp.swapaxes(state_pool[i], 0, 1), w_s_rows, params, TILE_M, PAST_LEN)
        pv.append(v)
        pp.append(jnp.swapaxes(pool, 0, 1))
        sv.append(v_s.reshape(dec_batch, 1, D_MODEL))
        sp.append(jnp.swapaxes(pool_s, 0, 1))
    return (hp, hs.reshape(dec_batch, 1, D_MODEL), jnp.stack(pv), jnp.stack(pp), jnp.stack(sv),
            jnp.stack(sp))
```

```python
import functools
from typing import Any, NamedTuple

import jax
import jax.numpy as jnp
from jax import lax
from jax.experimental import pallas as pl
from jax.experimental.pallas import tpu as pltpu

D_MODEL = 1024
CHUNK = 128
N_HEADS = D_MODEL // CHUNK
POOL_WINDOWS = (2, 4, 8, 16)
POOL_GROUP = D_MODEL // len(POOL_WINDOWS)
POOL_BUF = max(POOL_WINDOWS) - 1
PLE_DIM = 256
PAST_LEN = 16384
SAMPLE_ROWS = 128
EPS = 1e-6
COL_U, COL_V, COL_ZA, COL_XB, COL_ZB, COL_GA, COL_GB = range(7)
VEC_ROWS = tuple(range(7))
VEC_PRE_G, VEC_LN_G, VEC_LN_B, VEC_POOL_SCALE, VEC_POST_G, VEC_PLE_IN_G, VEC_PLE_OUT_G = VEC_ROWS
N_VEC_ROWS = 8

LANES = 128
N_LANE_BLOCKS = D_MODEL // LANES
BF16_ROWS = 16
SUBLANES = 8
TAIL = 32
DOUBLING_MIN_WINDOW = 8
PACK_STEPS = 8
TILE_M = 256
VMEM_LIMIT_BYTES = 56 * 1024 * 1024

_BF16 = jnp.bfloat16
_F32 = jnp.float32
_GELU_C = 0.7978845608028654


def _dot(a, b):
    return jnp.dot(a, b, preferred_element_type=_F32)


def _rmsnorm(x, g):
    return x * lax.rsqrt(jnp.mean(x * x, axis=-1, keepdims=True) + EPS) * g


def _layernorm(x, g, b):
    xc = x - jnp.mean(x, axis=-1, keepdims=True)
    return xc * lax.rsqrt(jnp.mean(xc * xc, axis=-1, keepdims=True) + EPS) * g + b


def _gelu_tanh(x):
    return 0.5 * x * (1.0 + jnp.tanh(_GELU_C * (x + 0.044715 * (x * x * x))))


def _sigmoid(x):
    return 0.5 * jnp.tanh(0.5 * x) + 0.5


def _silu(x):
    return x * _sigmoid(x)


def _unpack(ref, rows=slice(None), cols=slice(None)):
    return pltpu.bitcast(ref[rows, cols], _BF16)


class _Weights(NamedTuple):
    in_proj_w: Any
    spatial: Any
    pool: Any
    pa: Any
    pb: Any
    out: Any
    ple: Any
    pg: Any
    bias_t: Any
    vecs: Any

    def vec(self, row):
        return self.vecs[row:row + 1, :]

    def in_proj(self, xn, col):
        return _dot(xn, _unpack(self.in_proj_w, cols=slice(col * D_MODEL, (col + 1) * D_MODEL)))


def _pool_mix(pooled, w):
    parts = []
    for g in range(len(POOL_WINDOWS)):
        q = pooled[:, g * POOL_GROUP:(g + 1) * POOL_GROUP].astype(_BF16)
        rows = slice(g * POOL_GROUP // 2, (g + 1) * POOL_GROUP // 2)
        parts.append(_dot(q, _unpack(w.pool, rows=rows)))
    return jnp.concatenate(parts, axis=1) * w.vec(VEC_POOL_SCALE)


def _merge_and_embed(h, p, xn, y_a, y_b, w):
    g_a = w.in_proj(xn, COL_GA)
    g_b = w.in_proj(xn, COL_GB)
    m = (_sigmoid(g_a) * _dot(y_a.astype(_BF16), _unpack(w.pa))
         + _sigmoid(g_b) * _dot(y_b.astype(_BF16), _unpack(w.pb)))
    h = h + _rmsnorm(_dot(m.astype(_BF16), _unpack(w.out)), w.vec(VEC_POST_G))
    e = _dot(p.astype(_BF16), _unpack(w.ple))
    hn = _rmsnorm(h, w.vec(VEC_PLE_IN_G)).astype(_BF16)
    gate = _sigmoid(_dot(hn, _unpack(w.pg)))
    return h + _rmsnorm(gate * e, w.vec(VEC_PLE_OUT_G))


def _spatial_mix(vn_b, w, tile_m):
    w_tril = [_unpack(w.spatial, rows=slice(hd * CHUNK // 2, (hd + 1) * CHUNK // 2))
              for hd in range(N_HEADS)]
    bias = [w.bias_t[:, hd:hd + 1] for hd in range(N_HEADS)]
    s_rows = []
    for c in range(tile_m // CHUNK):
        heads = []
        for hd in range(N_HEADS):
            blk = vn_b[c * CHUNK:(c + 1) * CHUNK, hd * CHUNK:(hd + 1) * CHUNK]
            heads.append(_dot(w_tril[hd], blk) + bias[hd])
        s_rows.append(jnp.concatenate(heads, axis=1))
    return jnp.concatenate(s_rows, axis=0)


def _window_sum(src, lvl, w, rows):
    n_levels = w.bit_length() - 1
    assert w == 1 << n_levels and SUBLANES * (n_levels - 1) + 1 <= TAIL
    for l in range(1, n_levels):
        start = TAIL - SUBLANES * (n_levels - l)
        half = 1 << (l - 1)
        dst = lvl.at[(l - 1) % 2]
        dst[start:rows, :] = src[start:rows, :] + src[start - half:rows - half, :]
        src = dst
    half = w // 2
    return src[TAIL:rows, :] + src[TAIL - half:rows - half, :]


def _window_pool(x_b, ext_ref, lvl_ref, pool_ref, b, j, tile_m):
    rows = TAIL + tile_m
    for c in range(N_LANE_BLOCKS):
        ext_ref[c, TAIL:rows, :] = x_b[:, c * LANES:(c + 1) * LANES]
    pos = j * tile_m + lax.broadcasted_iota(jnp.int32, (tile_m, 1), 0)
    blocks = []
    for c in range(N_LANE_BLOCKS):
        w = POOL_WINDOWS[c * LANES // POOL_GROUP]
        cur = x_b[:, c * LANES:(c + 1) * LANES]
        if w >= DOUBLING_MIN_WINDOW:
            win = _window_sum(ext_ref.at[c], lvl_ref.at[c], w, rows)
        else:
            win = cur
            for k in range(1, w):
                win = win + ext_ref[c, TAIL - k:TAIL - k + tile_m, :]
        cnt = jnp.minimum(pos + 1, w).astype(_F32)
        blocks.append(win / cnt - cur)
    latest = jnp.concatenate([ext_ref[c, TAIL + tile_m - POOL_BUF:TAIL + tile_m, :]
                              for c in range(N_LANE_BLOCKS)], axis=1)
    pool_ref[:, pl.ds(b, 1), :] = latest[:, None, :]
    for c in range(N_LANE_BLOCKS):
        ext_ref[c, 0:TAIL, :] = ext_ref[c, tile_m:tile_m + TAIL, :]
    return jnp.concatenate(blocks, axis=1)


def _layer_kernel(x_ref, x_lag_ref, p_lag_ref, xs_ref, ps_ref, taps_ref, state_hbm,
                  w_in_ref, w_s_ref, w_pool_ref, w_pa_ref, w_pb_ref, w_out_ref, w_ple_ref,
                  w_pg_ref, bs_t_ref, vecs_ref,
                  y_ref, vn_ref, pool_ref, ys_ref, vns_ref, spool_hbm,
                  ext_ref, lvl_ref, xn_c, ya_c, yb_c, hist_ref, xbs_ref, sems,
                  *, tile_m, tiles_per_prompt, n_tiles, pos0):
    s_id = pl.program_id(0)
    front = jnp.minimum(s_id, n_tiles - 1)
    b = front // tiles_per_prompt
    j = lax.rem(front, tiles_per_prompt)
    w = _Weights(in_proj_w=w_in_ref, spatial=w_s_ref, pool=w_pool_ref, pa=w_pa_ref, pb=w_pb_ref,
                 out=w_out_ref, ple=w_ple_ref, pg=w_pg_ref, bias_t=bs_t_ref, vecs=vecs_ref)

    @pl.when(j == 0)
    def _():
        ext_ref[:, 0:TAIL, :] = jnp.zeros((N_LANE_BLOCKS, TAIL, LANES), _F32)

    @pl.when(s_id == 0)
    def _front_only():
        xn = _rmsnorm(x_ref[0], w.vec(VEC_PRE_G)).astype(_BF16)
        xn_c[...] = xn
        vn = _layernorm(_gelu_tanh(w.in_proj(xn, COL_V)), w.vec(VEC_LN_G), w.vec(VEC_LN_B))
        vn_ref[0] = vn[tile_m - CHUNK:, :]
        s = _spatial_mix(vn.astype(_BF16), w, tile_m)
        ya_c[...] = (_gelu_tanh(w.in_proj(xn, COL_U)) * s
                     * _silu(w.in_proj(xn, COL_ZA))).astype(_BF16)
        pooled = _window_pool(w.in_proj(xn, COL_XB), ext_ref, lvl_ref, pool_ref, b, j, tile_m)
        yb_c[...] = (_pool_mix(pooled, w) * _silu(w.in_proj(xn, COL_ZB))).astype(_BF16)

    @pl.when(s_id == n_tiles)
    def _back_only_and_sample():
        history, shift, newest = _sample_copies(state_hbm, spool_hbm, hist_ref, xbs_ref, sems)
        history.start()
        y_ref[0] = _merge_and_embed(x_lag_ref[0], p_lag_ref[0], xn_c[...], ya_c[...], yb_c[...], w)
        history.wait()
        shift.start()
        _sample_rows(xs_ref, ps_ref, taps_ref, hist_ref, newest, xbs_ref, w, ys_ref, vns_ref, pos0)
        shift.wait()
        newest.wait()

    @pl.when(jnp.logical_and(s_id > 0, s_id < n_tiles))
    def _both():
        xn_prev = xn_c[...]
        g_a = w.in_proj(xn_prev, COL_GA)
        xn = _rmsnorm(x_ref[0], w.vec(VEC_PRE_G)).astype(_BF16)
        xn_c[...] = xn
        v_raw = w.in_proj(xn, COL_V)
        g_b = w.in_proj(xn_prev, COL_GB)
        e = _dot(p_lag_ref[0].astype(_BF16), _unpack(w.ple))
        vn = _layernorm(_gelu_tanh(v_raw), w.vec(VEC_LN_G), w.vec(VEC_LN_B))
        vn_ref[0] = vn[tile_m - CHUNK:, :]
        u_raw = w.in_proj(xn, COL_U)
        s = _spatial_mix(vn.astype(_BF16), w, tile_m)
        proj_a = _dot(ya_c[...], _unpack(w.pa))
        proj_b = _dot(yb_c[...], _unpack(w.pb))
        za_raw = w.in_proj(xn, COL_ZA)
        m = (_sigmoid(g_a) * proj_a + _sigmoid(g_b) * proj_b).astype(_BF16)
        o = _dot(m, _unpack(w.out))
        x_b = w.in_proj(xn, COL_XB)
        ya_c[...] = (_gelu_tanh(u_raw) * s * _silu(za_raw)).astype(_BF16)
        zb_raw = w.in_proj(xn, COL_ZB)
        pooled = _window_pool(x_b, ext_ref, lvl_ref, pool_ref, b, j, tile_m)
        mixed = _pool_mix(pooled, w)
        h = x_lag_ref[0] + _rmsnorm(o, w.vec(VEC_POST_G))
        hn = _rmsnorm(h, w.vec(VEC_PLE_IN_G)).astype(_BF16)
        gate_raw = _dot(hn, _unpack(w.pg))
        yb_c[...] = (mixed * _silu(zb_raw)).astype(_BF16)
        y_ref[0] = h + _rmsnorm(_sigmoid(gate_raw) * e, w.vec(VEC_PLE_OUT_G))


def _gather_rows(ref):
    n = ref.shape[0] // SAMPLE_ROWS
    return jnp.concatenate([ref[pl.ds(c, SAMPLE_ROWS, stride=n), :] for c in range(n)], axis=1)


def _scatter_rows(ref, value):
    n = ref.shape[0] // SAMPLE_ROWS
    for c in range(n):
        ref[pl.ds(c, SAMPLE_ROWS, stride=n), :] = value[:, c * LANES:(c + 1) * LANES]


def _sample_copies(state_hbm, spool_hbm, hist_ref, xbs_ref, sems):
    history = pltpu.make_async_copy(state_hbm, hist_ref, sems.at[0])
    shift = pltpu.make_async_copy(hist_ref.at[pl.ds(1, POOL_BUF - 1)],
                                  spool_hbm.at[pl.ds(0, POOL_BUF - 1)], sems.at[1])
    newest = pltpu.make_async_copy(xbs_ref, spool_hbm.at[POOL_BUF - 1], sems.at[2])
    return history, shift, newest


def _sample_rows(x_ref, p_ref, taps_ref, hist_ref, newest, xbs_ref, w, y_ref, vn_ref, pos0):
    x = _gather_rows(x_ref)
    xn = _rmsnorm(x, w.vec(VEC_PRE_G)).astype(_BF16)

    u = _gelu_tanh(w.in_proj(xn, COL_U))
    vn = _layernorm(_gelu_tanh(w.in_proj(xn, COL_V)), w.vec(VEC_LN_G), w.vec(VEC_LN_B))
    _scatter_rows(vn_ref, vn)
    heads = []
    for hd in range(N_HEADS):
        tap = taps_ref[hd * CHUNK:hd * CHUNK + 1, 0:1]
        heads.append(vn[:, hd * CHUNK:(hd + 1) * CHUNK] * tap + w.bias_t[0:1, hd:hd + 1])
    s = jnp.concatenate(heads, axis=1)
    y_a = u * s * _silu(w.in_proj(xn, COL_ZA))

    x_b = w.in_proj(xn, COL_XB)
    xbs_ref[...] = x_b
    newest.start()
    groups = []
    for g, win_len in enumerate(POOL_WINDOWS):
        lo, hi = g * POOL_GROUP, (g + 1) * POOL_GROUP
        cur = x_b[:, lo:hi]
        win = cur
        for k in range(1, win_len):
            win = win + hist_ref[POOL_BUF - k, :, lo:hi]
        groups.append(win / float(min(pos0 + 1, win_len)) - cur)
    pooled = jnp.concatenate(groups, axis=1)
    y_b = _pool_mix(pooled, w) * _silu(w.in_proj(xn, COL_ZB))

    _scatter_rows(y_ref, _merge_and_embed(x, _gather_rows(p_ref), xn, y_a, y_b, w))


def _pack_kernel(*refs, tril_periods):
    n = len(tril_periods)
    n_vecs = len(VEC_ROWS)
    bias_ref, vec_refs = refs[n], refs[n + 1:n + 1 + n_vecs]
    outs = refs[n + 1 + n_vecs:]
    bias_t_ref, vecs_ref = outs[n], outs[n + 1]
    bias_t_ref[...] = bias_ref[...].T
    for row, vec_ref in zip(VEC_ROWS, vec_refs):
        vecs_ref[row:row + 1, :] = vec_ref[...]
    vecs_ref[n_vecs:, :] = jnp.zeros((N_VEC_ROWS - n_vecs, D_MODEL), _F32)
    for src, dst, period in zip(refs[:n], outs[:n], tril_periods):
        m = src[...]
        if period is not None:
            row = lax.broadcasted_iota(jnp.int32, m.shape, 0) % period
            m = jnp.where(row >= lax.broadcasted_iota(jnp.int32, m.shape, 1), m, 0.0)
        dst[...] = pltpu.bitcast(m.astype(_BF16), jnp.uint32)


def _pack_params(mats, tril_periods, bias, vecs, n_steps):
    blocks = [m.shape[0] // n_steps for m in mats]
    small = lambda shape: pl.BlockSpec(shape, lambda i: (0, 0))
    for m, rows, period in zip(mats, blocks, tril_periods):
        assert rows * n_steps == m.shape[0] and rows % (2 * BF16_ROWS) == 0
        assert period is None or rows % period == 0
    return pl.pallas_call(
        functools.partial(_pack_kernel, tril_periods=tril_periods),
        grid=(n_steps,),
        in_specs=[pl.BlockSpec((rows, m.shape[1]), lambda i: (i, 0))
                  for m, rows in zip(mats, blocks)]
                 + [small(bias.shape)] + [small(v.shape) for v in vecs],
        out_specs=[pl.BlockSpec((rows // 2, m.shape[1]), lambda i: (i, 0))
                   for m, rows in zip(mats, blocks)]
                  + [small(bias.shape[::-1]), small((N_VEC_ROWS, D_MODEL))],
        out_shape=[jax.ShapeDtypeStruct((m.shape[0] // 2, m.shape[1]), jnp.uint32) for m in mats]
                  + [jax.ShapeDtypeStruct(bias.shape[::-1], _F32),
                     jax.ShapeDtypeStruct((N_VEC_ROWS, D_MODEL), _F32)],
        compiler_params=pltpu.CompilerParams(dimension_semantics=("arbitrary",),
                                             vmem_limit_bytes=VMEM_LIMIT_BYTES),
        name="pack_params",
    )(*mats, bias, *vecs)


def _resident(shape):
    zeros = (0,) * len(shape)
    return pl.BlockSpec(shape, lambda *_: zeros, pipeline_mode=pl.Buffered(1))


def _layer(h, p, hs, ps, state, taps, weights, tile_m, pos0):
    batch, seq, _ = h.shape
    rows = state.shape[1]
    assert seq % tile_m == 0 and tile_m % CHUNK == 0 and seq >= POOL_BUF and rows == SAMPLE_ROWS
    tiles_per_prompt = seq // tile_m
    n_tiles = batch * tiles_per_prompt
    assert n_tiles >= 2

    def front_tile(s):
        t = jnp.minimum(s, n_tiles - 1)
        return t // tiles_per_prompt, t % tiles_per_prompt, 0

    def back_tile(s):
        t = jnp.maximum(s - 1, 0)
        return t // tiles_per_prompt, t % tiles_per_prompt, 0

    per_prompt = lambda s: (jnp.minimum(s, n_tiles - 1) // tiles_per_prompt, 0, 0)
    hbm = pl.BlockSpec(memory_space=pl.ANY)
    whole = lambda a: pl.BlockSpec(a.shape, lambda s: (0,) * a.ndim)
    return pl.pallas_call(
        functools.partial(_layer_kernel, tile_m=tile_m, tiles_per_prompt=tiles_per_prompt,
                          n_tiles=n_tiles, pos0=pos0),
        grid=(n_tiles + 1,),
        in_specs=[pl.BlockSpec((1, tile_m, D_MODEL), front_tile),
                  pl.BlockSpec((1, tile_m, D_MODEL), back_tile),
                  pl.BlockSpec((1, tile_m, PLE_DIM), back_tile),
                  _resident(hs.shape), _resident(ps.shape), _resident(taps.shape), hbm]
                 + [_resident(a.shape) for a in weights],
        out_specs=[pl.BlockSpec((1, tile_m, D_MODEL), back_tile),
                   pl.BlockSpec((1, CHUNK, D_MODEL), per_prompt),
                   pl.BlockSpec((POOL_BUF, batch, D_MODEL), lambda s: (0, 0, 0)),
                   whole(hs), whole(hs), hbm],
        out_shape=[jax.ShapeDtypeStruct((batch, seq, D_MODEL), _F32),
                   jax.ShapeDtypeStruct((batch, CHUNK, D_MODEL), _F32),
                   jax.ShapeDtypeStruct((POOL_BUF, batch, D_MODEL), _F32),
                   jax.ShapeDtypeStruct(hs.shape, _F32),
                   jax.ShapeDtypeStruct(hs.shape, _F32),
                   jax.ShapeDtypeStruct(state.shape, _F32)],
        scratch_shapes=[pltpu.VMEM((N_LANE_BLOCKS, TAIL + tile_m, LANES), _F32),
                        pltpu.VMEM((N_LANE_BLOCKS, 2, TAIL + tile_m, LANES), _F32),
                        pltpu.VMEM((tile_m, D_MODEL), _BF16),
                        pltpu.VMEM((tile_m, D_MODEL), _BF16),
                        pltpu.VMEM((tile_m, D_MODEL), _BF16),
                        pltpu.VMEM(state.shape, _F32),
                        pltpu.VMEM((rows, D_MODEL), _F32),
                        pltpu.SemaphoreType.DMA((3,))],
        compiler_params=pltpu.CompilerParams(
            dimension_semantics=("arbitrary",),
            vmem_limit_bytes=VMEM_LIMIT_BYTES),
        name="layer",
    )(h, h, p, hs, ps, taps, state, *weights)


def kernel(x_prompt, x_sample, state_pool, p_prompt, p_sample, pre_g, w_in, ln_g, ln_b, w_s, b_s,
           w_pool, pool_scale, w_pa, w_pb, w_out, post_g, w_ple, w_pg, ple_in_g, ple_out_g):
    depth = w_in.shape[0]
    dec_batch, dec_seq, _ = x_sample.shape
    assert dec_seq == 1 and x_prompt.shape[2] == D_MODEL
    hp = x_prompt
    hs = x_sample.reshape(dec_batch * N_LANE_BLOCKS, LANES)
    pv, pp, sv, sp = [], [], [], []
    for i in range(depth):
        w_s_rows = w_s[i].reshape(D_MODEL, CHUNK)
        params = _pack_params(
            [w_in[i], w_s_rows, w_pool[i].reshape(D_MODEL, POOL_GROUP), w_pa[i], w_pb[i],
             w_out[i], w_ple[i], w_pg[i]],
            (None, CHUNK) + (None,) * 6, b_s[i],
            [v[i:i + 1] for v in (pre_g, ln_g, ln_b, pool_scale, post_g, ple_in_g, ple_out_g)],
            n_steps=PACK_STEPS)
        hp, v, pool, hs, v_s, pool_s = _layer(
            hp, p_prompt[i], hs, p_sample[i].reshape(dec_batch * PLE_DIM // LANES, LANES),
            jnp.swapaxes(state_pool[i], 0, 1), w_s_rows, params, TILE_M, PAST_LEN)
        pv.append(v)
        pp.append(jnp.swapaxes(pool, 0, 1))
        sv.append(v_s.reshape(dec_batch, 1, D_MODEL))
        sp.append(jnp.swapaxes(pool_s, 0, 1))
    return (hp, hs.reshape(dec_batch, 1, D_MODEL), jnp.stack(pv), jnp.stack(pp), jnp.stack(sv),
            jnp.stack(sp))
```

```python
import functools
from typing import Any, NamedTuple

import jax
import jax.numpy as jnp
from jax import lax
from jax.experimental import pallas as pl
from jax.experimental.pallas import tpu as pltpu

D_MODEL = 1024
CHUNK = 128
N_HEADS = D_MODEL // CHUNK
POOL_WINDOWS = (2, 4, 8, 16)
POOL_GROUP = D_MODEL // len(POOL_WINDOWS)
POOL_BUF = max(POOL_WINDOWS) - 1
PLE_DIM = 256
PAST_LEN = 16384
SAMPLE_ROWS = 128
EPS = 1e-6
COL_U, COL_V, COL_ZA, COL_XB, COL_ZB, COL_GA, COL_GB = range(7)
VEC_ROWS = tuple(range(7))
VEC_PRE_G, VEC_LN_G, VEC_LN_B, VEC_POOL_SCALE, VEC_POST_G, VEC_PLE_IN_G, VEC_PLE_OUT_G = VEC_ROWS
N_VEC_ROWS = 8

LANES = 128
N_LANE_BLOCKS = D_MODEL // LANES
BF16_ROWS = 16
SUBLANES = 8
TAIL = 32
DOUBLING_MIN_WINDOW = 8
PACK_STEPS = 8
TILE_M = 256
VMEM_LIMIT_BYTES = 56 * 1024 * 1024

_BF16 = jnp.bfloat16
_F32 = jnp.float32
_GELU_C = 0.7978845608028654


def _dot(a, b):
    return jnp.dot(a, b, preferred_element_type=_F32)


def _rmsnorm(x, g):
    return x * lax.rsqrt(jnp.mean(x * x, axis=-1, keepdims=True) + EPS) * g


def _layernorm(x, g, b):
    xc = x - jnp.mean(x, axis=-1, keepdims=True)
    return xc * lax.rsqrt(jnp.mean(xc * xc, axis=-1, keepdims=True) + EPS) * g + b


def _gelu_tanh(x):
    return 0.5 * x * (1.0 + jnp.tanh(_GELU_C * (x + 0.044715 * (x * x * x))))


def _sigmoid(x):
    return 0.5 * jnp.tanh(0.5 * x) + 0.5


def _silu(x):
    return x * _sigmoid(x)


def _unpack(ref, rows=slice(None), cols=slice(None)):
    return pltpu.bitcast(ref[rows, cols], _BF16)


class _Weights(NamedTuple):
    in_proj_w: Any
    spatial: Any
    pool: Any
    pa: Any
    pb: Any
    out: Any
    ple: Any
    pg: Any
    bias_t: Any
    vecs: Any

    def vec(self, row):
        return self.vecs[row:row + 1, :]

    def in_proj(self, xn, col):
        return _dot(xn, _unpack(self.in_proj_w, cols=slice(col * D_MODEL, (col + 1) * D_MODEL)))


def _pool_mix(pooled, w):
    parts = []
    for g in range(len(POOL_WINDOWS)):
        q = pooled[:, g * POOL_GROUP:(g + 1) * POOL_GROUP].astype(_BF16)
        rows = slice(g * POOL_GROUP // 2, (g + 1) * POOL_GROUP // 2)
        parts.append(_dot(q, _unpack(w.pool, rows=rows)))
    return jnp.concatenate(parts, axis=1) * w.vec(VEC_POOL_SCALE)


def _merge_and_embed(h, p, xn, y_a, y_b, w):
    g_a = w.in_proj(xn, COL_GA)
    g_b = w.in_proj(xn, COL_GB)
    m = (_sigmoid(g_a) * _dot(y_a.astype(_BF16), _unpack(w.pa))
         + _sigmoid(g_b) * _dot(y_b.astype(_BF16), _unpack(w.pb)))
    h = h + _rmsnorm(_dot(m.astype(_BF16), _unpack(w.out)), w.vec(VEC_POST_G))
    e = _dot(p.astype(_BF16), _unpack(w.ple))
    hn = _rmsnorm(h, w.vec(VEC_PLE_IN_G)).astype(_BF16)
    gate = _sigmoid(_dot(hn, _unpack(w.pg)))
    return h + _rmsnorm(gate * e, w.vec(VEC_PLE_OUT_G))


def _spatial_mix(vn_b, w, tile_m):
    w_tril = [_unpack(w.spatial, rows=slice(hd * CHUNK // 2, (hd + 1) * CHUNK // 2))
              for hd in range(N_HEADS)]
    bias = [w.bias_t[:, hd:hd + 1] for hd in range(N_HEADS)]
    s_rows = []
    for c in range(tile_m // CHUNK):
        heads = []
        for hd in range(N_HEADS):
            blk = vn_b[c * CHUNK:(c + 1) * CHUNK, hd * CHUNK:(hd + 1) * CHUNK]
            heads.append(_dot(w_tril[hd], blk) + bias[hd])
        s_rows.append(jnp.concatenate(heads, axis=1))
    return jnp.concatenate(s_rows, axis=0)


def _window_sum(src, lvl, w, rows):
    n_levels = w.bit_length() - 1
    assert w == 1 << n_levels and SUBLANES * (n_levels - 1) + 1 <= TAIL
    for l in range(1, n_levels):
        start = TAIL - SUBLANES * (n_levels - l)
        half = 1 << (l - 1)
        dst = lvl.at[(l - 1) % 2]
        dst[start:rows, :] = src[start:rows, :] + src[start - half:rows - half, :]
        src = dst
    half = w // 2
    return src[TAIL:rows, :] + src[TAIL - half:rows - half, :]


def _window_pool(x_b, ext_ref, lvl_ref, pool_ref, b, j, tile_m):
    rows = TAIL + tile_m
    for c in range(N_LANE_BLOCKS):
        ext_ref[c, TAIL:rows, :] = x_b[:, c * LANES:(c + 1) * LANES]
    pos = j * tile_m + lax.broadcasted_iota(jnp.int32, (tile_m, 1), 0)
    blocks = []
    for c in range(N_LANE_BLOCKS):
        w = POOL_WINDOWS[c * LANES // POOL_GROUP]
        cur = x_b[:, c * LANES:(c + 1) * LANES]
        if w >= DOUBLING_MIN_WINDOW:
            win = _window_sum(ext_ref.at[c], lvl_ref.at[c], w, rows)
        else:
            win = cur
            for k in range(1, w):
                win = win + ext_ref[c, TAIL - k:TAIL - k + tile_m, :]
        cnt = jnp.minimum(pos + 1, w).astype(_F32)
        blocks.append(win / cnt - cur)
    latest = jnp.concatenate([ext_ref[c, TAIL + tile_m - POOL_BUF:TAIL + tile_m, :]
                              for c in range(N_LANE_BLOCKS)], axis=1)
    pool_ref[:, pl.ds(b, 1), :] = latest[:, None, :]
    for c in range(N_LANE_BLOCKS):
        ext_ref[c, 0:TAIL, :] = ext_ref[c, tile_m:tile_m + TAIL, :]
    return jnp.concatenate(blocks, axis=1)


def _layer_kernel(x_ref, x_lag_ref, p_lag_ref, xs_ref, ps_ref, taps_ref, state_hbm,
                  w_in_ref, w_s_ref, w_pool_ref, w_pa_ref, w_pb_ref, w_out_ref, w_ple_ref,
                  w_pg_ref, bs_t_ref, vecs_ref,
                  y_ref, vn_ref, pool_ref, ys_ref, vns_ref, spool_hbm,
                  ext_ref, lvl_ref, xn_c, ya_c, yb_c, hist_ref, xbs_ref, sems,
                  *, tile_m, tiles_per_prompt, n_tiles, pos0):
    s_id = pl.program_id(0)
    front = jnp.minimum(s_id, n_tiles - 1)
    b = front // tiles_per_prompt
    j = lax.rem(front, tiles_per_prompt)
    w = _Weights(in_proj_w=w_in_ref, spatial=w_s_ref, pool=w_pool_ref, pa=w_pa_ref, pb=w_pb_ref,
                 out=w_out_ref, ple=w_ple_ref, pg=w_pg_ref, bias_t=bs_t_ref, vecs=vecs_ref)

    @pl.when(j == 0)
    def _():
        ext_ref[:, 0:TAIL, :] = jnp.zeros((N_LANE_BLOCKS, TAIL, LANES), _F32)

    @pl.when(s_id == 0)
    def _front_only():
        xn = _rmsnorm(x_ref[0], w.vec(VEC_PRE_G)).astype(_BF16)
        xn_c[...] = xn
        vn = _layernorm(_gelu_tanh(w.in_proj(xn, COL_V)), w.vec(VEC_LN_G), w.vec(VEC_LN_B))
        vn_ref[0] = vn[tile_m - CHUNK:, :]
        s = _spatial_mix(vn.astype(_BF16), w, tile_m)
        ya_c[...] = (_gelu_tanh(w.in_proj(xn, COL_U)) * s
                     * _silu(w.in_proj(xn, COL_ZA))).astype(_BF16)
        pooled = _window_pool(w.in_proj(xn, COL_XB), ext_ref, lvl_ref, pool_ref, b, j, tile_m)
        yb_c[...] = (_pool_mix(pooled, w) * _silu(w.in_proj(xn, COL_ZB))).astype(_BF16)

    @pl.when(s_id == n_tiles)
    def _back_only_and_sample():
        history, shift, newest = _sample_copies(state_hbm, spool_hbm, hist_ref, xbs_ref, sems)
        history.start()
        y_ref[0] = _merge_and_embed(x_lag_ref[0], p_lag_ref[0], xn_c[...], ya_c[...], yb_c[...], w)
        history.wait()
        shift.start()
        _sample_rows(xs_ref, ps_ref, taps_ref, hist_ref, newest, xbs_ref, w, ys_ref, vns_ref, pos0)
        shift.wait()
        newest.wait()

    @pl.when(jnp.logical_and(s_id > 0, s_id < n_tiles))
    def _both():
        xn_prev = xn_c[...]
        g_a = w.in_proj(xn_prev, COL_GA)
        xn = _rmsnorm(x_ref[0], w.vec(VEC_PRE_G)).astype(_BF16)
        xn_c[...] = xn
        v_raw = w.in_proj(xn, COL_V)
        g_b = w.in_proj(xn_prev, COL_GB)
        e = _dot(p_lag_ref[0].astype(_BF16), _unpack(w.ple))
        vn = _layernorm(_gelu_tanh(v_raw), w.vec(VEC_LN_G), w.vec(VEC_LN_B))
        vn_ref[0] = vn[tile_m - CHUNK:, :]
        u_raw = w.in_proj(xn, COL_U)
        s = _spatial_mix(vn.astype(_BF16), w, tile_m)
        proj_a = _dot(ya_c[...], _unpack(w.pa))
        proj_b = _dot(yb_c[...], _unpack(w.pb))
        x_b = w.in_proj(xn, COL_XB)
        m = (_sigmoid(g_a) * proj_a + _sigmoid(g_b) * proj_b).astype(_BF16)
        o = _dot(m, _unpack(w.out))
        za_raw = w.in_proj(xn, COL_ZA)
        pooled = _window_pool(x_b, ext_ref, lvl_ref, pool_ref, b, j, tile_m)
        ya_c[...] = (_gelu_tanh(u_raw) * s * _silu(za_raw)).astype(_BF16)
        zb_raw = w.in_proj(xn, COL_ZB)
        mixed = _pool_mix(pooled, w)
        h = x_lag_ref[0] + _rmsnorm(o, w.vec(VEC_POST_G))
        hn = _rmsnorm(h, w.vec(VEC_PLE_IN_G)).astype(_BF16)
        gate_raw = _dot(hn, _unpack(w.pg))
        yb_c[...] = (mixed * _silu(zb_raw)).astype(_BF16)
        y_ref[0] = h + _rmsnorm(_sigmoid(gate_raw) * e, w.vec(VEC_PLE_OUT_G))


def _gather_rows(ref):
    n = ref.shape[0] // SAMPLE_ROWS
    return jnp.concatenate([ref[pl.ds(c, SAMPLE_ROWS, stride=n), :] for c in range(n)], axis=1)


def _scatter_rows(ref, value):
    n = ref.shape[0] // SAMPLE_ROWS
    for c in range(n):
        ref[pl.ds(c, SAMPLE_ROWS, stride=n), :] = value[:, c * LANES:(c + 1) * LANES]


def _sample_copies(state_hbm, spool_hbm, hist_ref, xbs_ref, sems):
    history = pltpu.make_async_copy(state_hbm, hist_ref, sems.at[0])
    shift = pltpu.make_async_copy(hist_ref.at[pl.ds(1, POOL_BUF - 1)],
                                  spool_hbm.at[pl.ds(0, POOL_BUF - 1)], sems.at[1])
    newest = pltpu.make_async_copy(xbs_ref, spool_hbm.at[POOL_BUF - 1], sems.at[2])
    return history, shift, newest


def _sample_rows(x_ref, p_ref, taps_ref, hist_ref, newest, xbs_ref, w, y_ref, vn_ref, pos0):
    x = _gather_rows(x_ref)
    xn = _rmsnorm(x, w.vec(VEC_PRE_G)).astype(_BF16)

    u = _gelu_tanh(w.in_proj(xn, COL_U))
    vn = _layernorm(_gelu_tanh(w.in_proj(xn, COL_V)), w.vec(VEC_LN_G), w.vec(VEC_LN_B))
    _scatter_rows(vn_ref, vn)
    heads = []
    for hd in range(N_HEADS):
        tap = taps_ref[hd * CHUNK:hd * CHUNK + 1, 0:1]
        heads.append(vn[:, hd * CHUNK:(hd + 1) * CHUNK] * tap + w.bias_t[0:1, hd:hd + 1])
    s = jnp.concatenate(heads, axis=1)
    y_a = u * s * _silu(w.in_proj(xn, COL_ZA))

    x_b = w.in_proj(xn, COL_XB)
    xbs_ref[...] = x_b
    newest.start()
    groups = []
    for g, win_len in enumerate(POOL_WINDOWS):
        lo, hi = g * POOL_GROUP, (g + 1) * POOL_GROUP
        cur = x_b[:, lo:hi]
        win = cur
        for k in range(1, win_len):
            win = win + hist_ref[POOL_BUF - k, :, lo:hi]
        groups.append(win / float(min(pos0 + 1, win_len)) - cur)
    pooled = jnp.concatenate(groups, axis=1)
    y_b = _pool_mix(pooled, w) * _silu(w.in_proj(xn, COL_ZB))

    _scatter_rows(y_ref, _merge_and_embed(x, _gather_rows(p_ref), xn, y_a, y_b, w))


def _pack_kernel(*refs, tril_periods):
    n = len(tril_periods)
    n_vecs = len(VEC_ROWS)
    bias_ref, vec_refs = refs[n], refs[n + 1:n + 1 + n_vecs]
    outs = refs[n + 1 + n_vecs:]
    bias_t_ref, vecs_ref = outs[n], outs[n + 1]
    bias_t_ref[...] = bias_ref[...].T
    for row, vec_ref in zip(VEC_ROWS, vec_refs):
        vecs_ref[row:row + 1, :] = vec_ref[...]
    vecs_ref[n_vecs:, :] = jnp.zeros((N_VEC_ROWS - n_vecs, D_MODEL), _F32)
    for src, dst, period in zip(refs[:n], outs[:n], tril_periods):
        m = src[...]
        if period is not None:
            row = lax.broadcasted_iota(jnp.int32, m.shape, 0) % period
            m = jnp.where(row >= lax.broadcasted_iota(jnp.int32, m.shape, 1), m, 0.0)
        dst[...] = pltpu.bitcast(m.astype(_BF16), jnp.uint32)


def _pack_params(mats, tril_periods, bias, vecs, n_steps):
    blocks = [m.shape[0] // n_steps for m in mats]
    small = lambda shape: pl.BlockSpec(shape, lambda i: (0, 0))
    for m, rows, period in zip(mats, blocks, tril_periods):
        assert rows * n_steps == m.shape[0] and rows % (2 * BF16_ROWS) == 0
        assert period is None or rows % period == 0
    return pl.pallas_call(
        functools.partial(_pack_kernel, tril_periods=tril_periods),
        grid=(n_steps,),
        in_specs=[pl.BlockSpec((rows, m.shape[1]), lambda i: (i, 0))
                  for m, rows in zip(mats, blocks)]
                 + [small(bias.shape)] + [small(v.shape) for v in vecs],
        out_specs=[pl.BlockSpec((rows // 2, m.shape[1]), lambda i: (i, 0))
                   for m, rows in zip(mats, blocks)]
                  + [small(bias.shape[::-1]), small((N_VEC_ROWS, D_MODEL))],
        out_shape=[jax.ShapeDtypeStruct((m.shape[0] // 2, m.shape[1]), jnp.uint32) for m in mats]
                  + [jax.ShapeDtypeStruct(bias.shape[::-1], _F32),
                     jax.ShapeDtypeStruct((N_VEC_ROWS, D_MODEL), _F32)],
        compiler_params=pltpu.CompilerParams(dimension_semantics=("arbitrary",),
                                             vmem_limit_bytes=VMEM_LIMIT_BYTES),
        name="pack_params",
    )(*mats, bias, *vecs)


def _resident(shape):
    zeros = (0,) * len(shape)
    return pl.BlockSpec(shape, lambda *_: zeros, pipeline_mode=pl.Buffered(1))


def _layer(h, p, hs, ps, state, taps, weights, tile_m, pos0):
    batch, seq, _ = h.shape
    rows = state.shape[1]
    assert seq % tile_m == 0 and tile_m % CHUNK == 0 and seq >= POOL_BUF and rows == SAMPLE_ROWS
    tiles_per_prompt = seq // tile_m
    n_tiles = batch * tiles_per_prompt
    assert n_tiles >= 2

    def front_tile(s):
        t = jnp.minimum(s, n_tiles - 1)
        return t // tiles_per_prompt, t % tiles_per_prompt, 0

    def back_tile(s):
        t = jnp.maximum(s - 1, 0)
        return t // tiles_per_prompt, t % tiles_per_prompt, 0

    per_prompt = lambda s: (jnp.minimum(s, n_tiles - 1) // tiles_per_prompt, 0, 0)
    hbm = pl.BlockSpec(memory_space=pl.ANY)
    whole = lambda a: pl.BlockSpec(a.shape, lambda s: (0,) * a.ndim)
    return pl.pallas_call(
        functools.partial(_layer_kernel, tile_m=tile_m, tiles_per_prompt=tiles_per_prompt,
                          n_tiles=n_tiles, pos0=pos0),
        grid=(n_tiles + 1,),
        in_specs=[pl.BlockSpec((1, tile_m, D_MODEL), front_tile),
                  pl.BlockSpec((1, tile_m, D_MODEL), back_tile),
                  pl.BlockSpec((1, tile_m, PLE_DIM), back_tile),
                  _resident(hs.shape), _resident(ps.shape), _resident(taps.shape), hbm]
                 + [_resident(a.shape) for a in weights],
        out_specs=[pl.BlockSpec((1, tile_m, D_MODEL), back_tile),
                   pl.BlockSpec((1, CHUNK, D_MODEL), per_prompt),
                   pl.BlockSpec((POOL_BUF, batch, D_MODEL), lambda s: (0, 0, 0)),
                   whole(hs), whole(hs), hbm],
        out_shape=[jax.ShapeDtypeStruct((batch, seq, D_MODEL), _F32),
                   jax.ShapeDtypeStruct((batch, CHUNK, D_MODEL), _F32),
                   jax.ShapeDtypeStruct((POOL_BUF, batch, D_MODEL), _F32),
                   jax.ShapeDtypeStruct(hs.shape, _F32),
                   jax.ShapeDtypeStruct(hs.shape, _F32),
                   jax.ShapeDtypeStruct(state.shape, _F32)],
        scratch_shapes=[pltpu.VMEM((N_LANE_BLOCKS, TAIL + tile_m, LANES), _F32),
                        pltpu.VMEM((N_LANE_BLOCKS, 2, TAIL + tile_m, LANES), _F32),
                        pltpu.VMEM((tile_m, D_MODEL), _BF16),
                        pltpu.VMEM((tile_m, D_MODEL), _BF16),
                        pltpu.VMEM((tile_m, D_MODEL), _BF16),
                        pltpu.VMEM(state.shape, _F32),
                        pltpu.VMEM((rows, D_MODEL), _F32),
                        pltpu.SemaphoreType.DMA((3,))],
        compiler_params=pltpu.CompilerParams(
            dimension_semantics=("arbitrary",),
            vmem_limit_bytes=VMEM_LIMIT_BYTES),
        name="layer",
    )(h, h, p, hs, ps, taps, state, *weights)


def kernel(x_prompt, x_sample, state_pool, p_prompt, p_sample, pre_g, w_in, ln_g, ln_b, w_s, b_s,
           w_pool, pool_scale, w_pa, w_pb, w_out, post_g, w_ple, w_pg, ple_in_g, ple_out_g):
    depth = w_in.shape[0]
    dec_batch, dec_seq, _ = x_sample.shape
    assert dec_seq == 1 and x_prompt.shape[2] == D_MODEL
    hp = x_prompt
    hs = x_sample.reshape(dec_batch * N_LANE_BLOCKS, LANES)
    pv, pp, sv, sp = [], [], [], []
    for i in range(depth):
        w_s_rows = w_s[i].reshape(D_MODEL, CHUNK)
        params = _pack_params(
            [w_in[i], w_s_rows, w_pool[i].reshape(D_MODEL, POOL_GROUP), w_pa[i], w_pb[i],
             w_out[i], w_ple[i], w_pg[i]],
            (None, CHUNK) + (None,) * 6, b_s[i],
            [v[i:i + 1] for v in (pre_g, ln_g, ln_b, pool_scale, post_g, ple_in_g, ple_out_g)],
            n_steps=PACK_STEPS)
        hp, v, pool, hs, v_s, pool_s = _layer(
            hp, p_prompt[i], hs, p_sample[i].reshape(dec_batch * PLE_DIM // LANES, LANES),
            jnp.swapaxes(state_pool[i], 0, 1), w_s_rows, params, TILE_M, PAST_LEN)
        pv.append(v)
        pp.append(jnp.swapaxes(pool, 0, 1))
        sv.append(v_s.reshape(dec_batch, 1, D_MODEL))
        sp.append(jnp.swapaxes(pool_s, 0, 1))
    return (hp, hs.reshape(dec_batch, 1, D_MODEL), jnp.stack(pv), jnp.stack(pp), jnp.stack(sv),
            jnp.stack(sp))
```

```python
import functools
from typing import Any, NamedTuple

import jax
import jax.numpy as jnp
from jax import lax
from jax.experimental import pallas as pl
from jax.experimental.pallas import tpu as pltpu

D_MODEL = 1024
CHUNK = 128
N_HEADS = D_MODEL // CHUNK
POOL_WINDOWS = (2, 4, 8, 16)
POOL_GROUP = D_MODEL // len(POOL_WINDOWS)
POOL_BUF = max(POOL_WINDOWS) - 1
PLE_DIM = 256
PAST_LEN = 16384
SAMPLE_ROWS = 128
EPS = 1e-6
COL_U, COL_V, COL_ZA, COL_XB, COL_ZB, COL_GA, COL_GB = range(7)
VEC_ROWS = tuple(range(7))
VEC_PRE_G, VEC_LN_G, VEC_LN_B, VEC_POOL_SCALE, VEC_POST_G, VEC_PLE_IN_G, VEC_PLE_OUT_G = VEC_ROWS
N_VEC_ROWS = 8

LANES = 128
N_LANE_BLOCKS = D_MODEL // LANES
BF16_ROWS = 16
SUBLANES = 8
TAIL = 32
DOUBLING_MIN_WINDOW = 8
PACK_STEPS = 8
TILE_M = 256
VMEM_LIMIT_BYTES = 56 * 1024 * 1024

_BF16 = jnp.bfloat16
_F32 = jnp.float32
_GELU_C = 0.7978845608028654


def _dot(a, b):
    return jnp.dot(a, b, preferred_element_type=_F32)


def _rmsnorm(x, g):
    return x * lax.rsqrt(jnp.mean(x * x, axis=-1, keepdims=True) + EPS) * g


def _layernorm(x, g, b):
    xc = x - jnp.mean(x, axis=-1, keepdims=True)
    return xc * lax.rsqrt(jnp.mean(xc * xc, axis=-1, keepdims=True) + EPS) * g + b


def _gelu_tanh(x):
    return 0.5 * x * (1.0 + jnp.tanh(_GELU_C * (x + 0.044715 * (x * x * x))))


def _sigmoid(x):
    return 0.5 * jnp.tanh(0.5 * x) + 0.5


def _silu(x):
    return x * _sigmoid(x)


def _unpack(ref, rows=slice(None), cols=slice(None)):
    return pltpu.bitcast(ref[rows, cols], _BF16)


class _Weights(NamedTuple):
    in_proj_w: Any
    spatial: Any
    pool: Any
    pa: Any
    pb: Any
    out: Any
    ple: Any
    pg: Any
    bias_t: Any
    vecs: Any

    def vec(self, row):
        return self.vecs[row:row + 1, :]

    def in_proj(self, xn, col):
        return _dot(xn, _unpack(self.in_proj_w, cols=slice(col * D_MODEL, (col + 1) * D_MODEL)))


def _pool_mix(pooled, w):
    parts = []
    for g in range(len(POOL_WINDOWS)):
        q = pooled[:, g * POOL_GROUP:(g + 1) * POOL_GROUP].astype(_BF16)
        rows = slice(g * POOL_GROUP // 2, (g + 1) * POOL_GROUP // 2)
        parts.append(_dot(q, _unpack(w.pool, rows=rows)))
    return jnp.concatenate(parts, axis=1) * w.vec(VEC_POOL_SCALE)


def _merge_and_embed(h, p, xn, y_a, y_b, w):
    g_a = w.in_proj(xn, COL_GA)
    g_b = w.in_proj(xn, COL_GB)
    m = (_sigmoid(g_a) * _dot(y_a.astype(_BF16), _unpack(w.pa))
         + _sigmoid(g_b) * _dot(y_b.astype(_BF16), _unpack(w.pb)))
    h = h + _rmsnorm(_dot(m.astype(_BF16), _unpack(w.out)), w.vec(VEC_POST_G))
    e = _dot(p.astype(_BF16), _unpack(w.ple))
    hn = _rmsnorm(h, w.vec(VEC_PLE_IN_G)).astype(_BF16)
    gate = _sigmoid(_dot(hn, _unpack(w.pg)))
    return h + _rmsnorm(gate * e, w.vec(VEC_PLE_OUT_G))


def _spatial_mix(vn_b, w, tile_m):
    w_tril = [_unpack(w.spatial, rows=slice(hd * CHUNK // 2, (hd + 1) * CHUNK // 2))
              for hd in range(N_HEADS)]
    bias = [w.bias_t[:, hd:hd + 1] for hd in range(N_HEADS)]
    s_rows = []
    for c in range(tile_m // CHUNK):
        heads = []
        for hd in range(N_HEADS):
            blk = vn_b[c * CHUNK:(c + 1) * CHUNK, hd * CHUNK:(hd + 1) * CHUNK]
            heads.append(_dot(w_tril[hd], blk) + bias[hd])
        s_rows.append(jnp.concatenate(heads, axis=1))
    return jnp.concatenate(s_rows, axis=0)


def _window_sum(src, lvl, w, rows):
    n_levels = w.bit_length() - 1
    assert w == 1 << n_levels and SUBLANES * (n_levels - 1) + 1 <= TAIL
    for l in range(1, n_levels):
        start = TAIL - SUBLANES * (n_levels - l)
        half = 1 << (l - 1)
        dst = lvl.at[(l - 1) % 2]
        dst[start:rows, :] = src[start:rows, :] + src[start - half:rows - half, :]
        src = dst
    half = w // 2
    return src[TAIL:rows, :] + src[TAIL - half:rows - half, :]


def _window_pool(x_b, ext_ref, lvl_ref, pool_ref, b, j, tile_m):
    rows = TAIL + tile_m
    for c in range(N_LANE_BLOCKS):
        ext_ref[c, TAIL:rows, :] = x_b[:, c * LANES:(c + 1) * LANES]
    pos = j * tile_m + lax.broadcasted_iota(jnp.int32, (tile_m, 1), 0)
    blocks = []
    for c in range(N_LANE_BLOCKS):
        w = POOL_WINDOWS[c * LANES // POOL_GROUP]
        cur = x_b[:, c * LANES:(c + 1) * LANES]
        if w >= DOUBLING_MIN_WINDOW:
            win = _window_sum(ext_ref.at[c], lvl_ref.at[c], w, rows)
        else:
            win = cur
            for k in range(1, w):
                win = win + ext_ref[c, TAIL - k:TAIL - k + tile_m, :]
        cnt = jnp.minimum(pos + 1, w).astype(_F32)
        blocks.append(win / cnt - cur)
    latest = jnp.concatenate([ext_ref[c, TAIL + tile_m - POOL_BUF:TAIL + tile_m, :]
                              for c in range(N_LANE_BLOCKS)], axis=1)
    pool_ref[:, pl.ds(b, 1), :] = latest[:, None, :]
    for c in range(N_LANE_BLOCKS):
        ext_ref[c, 0:TAIL, :] = ext_ref[c, tile_m:tile_m + TAIL, :]
    return jnp.concatenate(blocks, axis=1)


def _layer_kernel(x_ref, x_lag_ref, p_lag_ref, xs_ref, ps_ref, taps_ref, state_hbm,
                  w_in_ref, w_s_ref, w_pool_ref, w_pa_ref, w_pb_ref, w_out_ref, w_ple_ref,
                  w_pg_ref, bs_t_ref, vecs_ref,
                  y_ref, vn_ref, pool_ref, ys_ref, vns_ref, spool_hbm,
                  ext_ref, lvl_ref, xn_c, ya_c, yb_c, hist_ref, xbs_ref, sems,
                  *, tile_m, tiles_per_prompt, n_tiles, pos0):
    s_id = pl.program_id(0)
    front = jnp.minimum(s_id, n_tiles - 1)
    b = front // tiles_per_prompt
    j = lax.rem(front, tiles_per_prompt)
    w = _Weights(in_proj_w=w_in_ref, spatial=w_s_ref, pool=w_pool_ref, pa=w_pa_ref, pb=w_pb_ref,
                 out=w_out_ref, ple=w_ple_ref, pg=w_pg_ref, bias_t=bs_t_ref, vecs=vecs_ref)

    @pl.when(j == 0)
    def _():
        ext_ref[:, 0:TAIL, :] = jnp.zeros((N_LANE_BLOCKS, TAIL, LANES), _F32)

    @pl.when(s_id == 0)
    def _front_only():
        xn = _rmsnorm(x_ref[0], w.vec(VEC_PRE_G)).astype(_BF16)
        xn_c[...] = xn
        vn = _layernorm(_gelu_tanh(w.in_proj(xn, COL_V)), w.vec(VEC_LN_G), w.vec(VEC_LN_B))
        vn_ref[0] = vn[tile_m - CHUNK:, :]
        s = _spatial_mix(vn.astype(_BF16), w, tile_m)
        ya_c[...] = (_gelu_tanh(w.in_proj(xn, COL_U)) * s
                     * _silu(w.in_proj(xn, COL_ZA))).astype(_BF16)
        pooled = _window_pool(w.in_proj(xn, COL_XB), ext_ref, lvl_ref, pool_ref, b, j, tile_m)
        yb_c[...] = (_pool_mix(pooled, w) * _silu(w.in_proj(xn, COL_ZB))).astype(_BF16)

    @pl.when(s_id == n_tiles)
    def _back_only_and_sample():
        history, shift, newest = _sample_copies(state_hbm, spool_hbm, hist_ref, xbs_ref, sems)
        history.start()
        y_ref[0] = _merge_and_embed(x_lag_ref[0], p_lag_ref[0], xn_c[...], ya_c[...], yb_c[...], w)
        history.wait()
        shift.start()
        _sample_rows(xs_ref, ps_ref, taps_ref, hist_ref, newest, xbs_ref, w, ys_ref, vns_ref, pos0)
        shift.wait()
        newest.wait()

    @pl.when(jnp.logical_and(s_id > 0, s_id < n_tiles))
    def _both():
        xn_prev = xn_c[...]
        g_a = w.in_proj(xn_prev, COL_GA)
        xn = _rmsnorm(x_ref[0], w.vec(VEC_PRE_G)).astype(_BF16)
        xn_c[...] = xn
        v_raw = w.in_proj(xn, COL_V)
        g_b = w.in_proj(xn_prev, COL_GB)
        e = _dot(p_lag_ref[0].astype(_BF16), _unpack(w.ple))
        vn = _layernorm(_gelu_tanh(v_raw), w.vec(VEC_LN_G), w.vec(VEC_LN_B))
        vn_ref[0] = vn[tile_m - CHUNK:, :]
        u_raw = w.in_proj(xn, COL_U)
        s = _spatial_mix(vn.astype(_BF16), w, tile_m)
        proj_a = _dot(ya_c[...], _unpack(w.pa))
        proj_b = _dot(yb_c[...], _unpack(w.pb))
        x_b = w.in_proj(xn, COL_XB)
        m = (_sigmoid(g_a) * proj_a + _sigmoid(g_b) * proj_b).astype(_BF16)
        o = _dot(m, _unpack(w.out))
        za_raw = w.in_proj(xn, COL_ZA)
        pooled = _window_pool(x_b, ext_ref, lvl_ref, pool_ref, b, j, tile_m)
        ya_c[...] = (_gelu_tanh(u_raw) * s * _silu(za_raw)).astype(_BF16)
        mixed = _pool_mix(pooled, w)
        h = x_lag_ref[0] + _rmsnorm(o, w.vec(VEC_POST_G))
        hn = _rmsnorm(h, w.vec(VEC_PLE_IN_G)).astype(_BF16)
        gate_raw = _dot(hn, _unpack(w.pg))
        zb_raw = w.in_proj(xn, COL_ZB)
        yb_c[...] = (mixed * _silu(zb_raw)).astype(_BF16)
        y_ref[0] = h + _rmsnorm(_sigmoid(gate_raw) * e, w.vec(VEC_PLE_OUT_G))


def _gather_rows(ref):
    n = ref.shape[0] // SAMPLE_ROWS
    return jnp.concatenate([ref[pl.ds(c, SAMPLE_ROWS, stride=n), :] for c in range(n)], axis=1)


def _scatter_rows(ref, value):
    n = ref.shape[0] // SAMPLE_ROWS
    for c in range(n):
        ref[pl.ds(c, SAMPLE_ROWS, stride=n), :] = value[:, c * LANES:(c + 1) * LANES]


def _sample_copies(state_hbm, spool_hbm, hist_ref, xbs_ref, sems):
    history = pltpu.make_async_copy(state_hbm, hist_ref, sems.at[0])
    shift = pltpu.make_async_copy(hist_ref.at[pl.ds(1, POOL_BUF - 1)],
                                  spool_hbm.at[pl.ds(0, POOL_BUF - 1)], sems.at[1])
    newest = pltpu.make_async_copy(xbs_ref, spool_hbm.at[POOL_BUF - 1], sems.at[2])
    return history, shift, newest


def _sample_rows(x_ref, p_ref, taps_ref, hist_ref, newest, xbs_ref, w, y_ref, vn_ref, pos0):
    x = _gather_rows(x_ref)
    xn = _rmsnorm(x, w.vec(VEC_PRE_G)).astype(_BF16)

    u = _gelu_tanh(w.in_proj(xn, COL_U))
    vn = _layernorm(_gelu_tanh(w.in_proj(xn, COL_V)), w.vec(VEC_LN_G), w.vec(VEC_LN_B))
    _scatter_rows(vn_ref, vn)
    heads = []
    for hd in range(N_HEADS):
        tap = taps_ref[hd * CHUNK:hd * CHUNK + 1, 0:1]
        heads.append(vn[:, hd * CHUNK:(hd + 1) * CHUNK] * tap + w.bias_t[0:1, hd:hd + 1])
    s = jnp.concatenate(heads, axis=1)
    y_a = u * s * _silu(w.in_proj(xn, COL_ZA))

    x_b = w.in_proj(xn, COL_XB)
    xbs_ref[...] = x_b
    newest.start()
    groups = []
    for g, win_len in enumerate(POOL_WINDOWS):
        lo, hi = g * POOL_GROUP, (g + 1) * POOL_GROUP
        cur = x_b[:, lo:hi]
        win = cur
        for k in range(1, win_len):
            win = win + hist_ref[POOL_BUF - k, :, lo:hi]
        groups.append(win / float(min(pos0 + 1, win_len)) - cur)
    pooled = jnp.concatenate(groups, axis=1)
    y_b = _pool_mix(pooled, w) * _silu(w.in_proj(xn, COL_ZB))

    _scatter_rows(y_ref, _merge_and_embed(x, _gather_rows(p_ref), xn, y_a, y_b, w))


def _pack_kernel(*refs, tril_periods):
    n = len(tril_periods)
    n_vecs = len(VEC_ROWS)
    bias_ref, vec_refs = refs[n], refs[n + 1:n + 1 + n_vecs]
    outs = refs[n + 1 + n_vecs:]
    bias_t_ref, vecs_ref = outs[n], outs[n + 1]
    bias_t_ref[...] = bias_ref[...].T
    for row, vec_ref in zip(VEC_ROWS, vec_refs):
        vecs_ref[row:row + 1, :] = vec_ref[...]
    vecs_ref[n_vecs:, :] = jnp.zeros((N_VEC_ROWS - n_vecs, D_MODEL), _F32)
    for src, dst, period in zip(refs[:n], outs[:n], tril_periods):
        m = src[...]
        if period is not None:
            row = lax.broadcasted_iota(jnp.int32, m.shape, 0) % period
            m = jnp.where(row >= lax.broadcasted_iota(jnp.int32, m.shape, 1), m, 0.0)
        dst[...] = pltpu.bitcast(m.astype(_BF16), jnp.uint32)


def _pack_params(mats, tril_periods, bias, vecs, n_steps):
    blocks = [m.shape[0] // n_steps for m in mats]
    small = lambda shape: pl.BlockSpec(shape, lambda i: (0, 0))
    for m, rows, period in zip(mats, blocks, tril_periods):
        assert rows * n_steps == m.shape[0] and rows % (2 * BF16_ROWS) == 0
        assert period is None or rows % period == 0
    return pl.pallas_call(
        functools.partial(_pack_kernel, tril_periods=tril_periods),
        grid=(n_steps,),
        in_specs=[pl.BlockSpec((rows, m.shape[1]), lambda i: (i, 0))
                  for m, rows in zip(mats, blocks)]
                 + [small(bias.shape)] + [small(v.shape) for v in vecs],
        out_specs=[pl.BlockSpec((rows // 2, m.shape[1]), lambda i: (i, 0))
                   for m, rows in zip(mats, blocks)]
                  + [small(bias.shape[::-1]), small((N_VEC_ROWS, D_MODEL))],
        out_shape=[jax.ShapeDtypeStruct((m.shape[0] // 2, m.shape[1]), jnp.uint32) for m in mats]
                  + [jax.ShapeDtypeStruct(bias.shape[::-1], _F32),
                     jax.ShapeDtypeStruct((N_VEC_ROWS, D_MODEL), _F32)],
        compiler_params=pltpu.CompilerParams(dimension_semantics=("arbitrary",),
                                             vmem_limit_bytes=VMEM_LIMIT_BYTES),
        name="pack_params",
    )(*mats, bias, *vecs)


def _resident(shape):
    zeros = (0,) * len(shape)
    return pl.BlockSpec(shape, lambda *_: zeros, pipeline_mode=pl.Buffered(1))


def _layer(h, p, hs, ps, state, taps, weights, tile_m, pos0):
    batch, seq, _ = h.shape
    rows = state.shape[1]
    assert seq % tile_m == 0 and tile_m % CHUNK == 0 and seq >= POOL_BUF and rows == SAMPLE_ROWS
    tiles_per_prompt = seq // tile_m
    n_tiles = batch * tiles_per_prompt
    assert n_tiles >= 2

    def front_tile(s):
        t = jnp.minimum(s, n_tiles - 1)
        return t // tiles_per_prompt, t % tiles_per_prompt, 0

    def back_tile(s):
        t = jnp.maximum(s - 1, 0)
        return t // tiles_per_prompt, t % tiles_per_prompt, 0

    per_prompt = lambda s: (jnp.minimum(s, n_tiles - 1) // tiles_per_prompt, 0, 0)
    hbm = pl.BlockSpec(memory_space=pl.ANY)
    whole = lambda a: pl.BlockSpec(a.shape, lambda s: (0,) * a.ndim)
    return pl.pallas_call(
        functools.partial(_layer_kernel, tile_m=tile_m, tiles_per_prompt=tiles_per_prompt,
                          n_tiles=n_tiles, pos0=pos0),
        grid=(n_tiles + 1,),
        in_specs=[pl.BlockSpec((1, tile_m, D_MODEL), front_tile),
                  pl.BlockSpec((1, tile_m, D_MODEL), back_tile),
                  pl.BlockSpec((1, tile_m, PLE_DIM), back_tile),
                  _resident(hs.shape), _resident(ps.shape), _resident(taps.shape), hbm]
                 + [_resident(a.shape) for a in weights],
        out_specs=[pl.BlockSpec((1, tile_m, D_MODEL), back_tile),
                   pl.BlockSpec((1, CHUNK, D_MODEL), per_prompt),
                   pl.BlockSpec((POOL_BUF, batch, D_MODEL), lambda s: (0, 0, 0)),
                   whole(hs), whole(hs), hbm],
        out_shape=[jax.ShapeDtypeStruct((batch, seq, D_MODEL), _F32),
                   jax.ShapeDtypeStruct((batch, CHUNK, D_MODEL), _F32),
                   jax.ShapeDtypeStruct((POOL_BUF, batch, D_MODEL), _F32),
                   jax.ShapeDtypeStruct(hs.shape, _F32),
                   jax.ShapeDtypeStruct(hs.shape, _F32),
                   jax.ShapeDtypeStruct(state.shape, _F32)],
        scratch_shapes=[pltpu.VMEM((N_LANE_BLOCKS, TAIL + tile_m, LANES), _F32),
                        pltpu.VMEM((N_LANE_BLOCKS, 2, TAIL + tile_m, LANES), _F32),
                        pltpu.VMEM((tile_m, D_MODEL), _BF16),
                        pltpu.VMEM((tile_m, D_MODEL), _BF16),
                        pltpu.VMEM((tile_m, D_MODEL), _BF16),
                        pltpu.VMEM(state.shape, _F32),
                        pltpu.VMEM((rows, D_MODEL), _F32),
                        pltpu.SemaphoreType.DMA((3,))],
        compiler_params=pltpu.CompilerParams(
            dimension_semantics=("arbitrary",),
            vmem_limit_bytes=VMEM_LIMIT_BYTES),
        name="layer",
    )(h, h, p, hs, ps, taps, state, *weights)


def kernel(x_prompt, x_sample, state_pool, p_prompt, p_sample, pre_g, w_in, ln_g, ln_b, w_s, b_s,
           w_pool, pool_scale, w_pa, w_pb, w_out, post_g, w_ple, w_pg, ple_in_g, ple_out_g):
    depth = w_in.shape[0]
    dec_batch, dec_seq, _ = x_sample.shape
    assert dec_seq == 1 and x_prompt.shape[2] == D_MODEL
    hp = x_prompt
    hs = x_sample.reshape(dec_batch * N_LANE_BLOCKS, LANES)
    pv, pp, sv, sp = [], [], [], []
    for i in range(depth):
        w_s_rows = w_s[i].reshape(D_MODEL, CHUNK)
        params = _pack_params(
            [w_in[i], w_s_rows, w_pool[i].reshape(D_MODEL, POOL_GROUP), w_pa[i], w_pb[i],
             w_out[i], w_ple[i], w_pg[i]],
            (None, CHUNK) + (None,) * 6, b_s[i],
            [v[i:i + 1] for v in (pre_g, ln_g, ln_b, pool_scale, post_g, ple_in_g, ple_out_g)],
            n_steps=PACK_STEPS)
        hp, v, pool, hs, v_s, pool_s = _layer(
            hp, p_prompt[i], hs, p_sample[i].reshape(dec_batch * PLE_DIM // LANES, LANES),
            jnp.swapaxes(state_pool[i], 0, 1), w_s_rows, params, TILE_M, PAST_LEN)
        pv.append(v)
        pp.append(jnp.swapaxes(pool, 0, 1))
        sv.append(v_s.reshape(dec_batch, 1, D_MODEL))
        sp.append(jnp.swapaxes(pool_s, 0, 1))
    return (hp, hs.reshape(dec_batch, 1, D_MODEL), jnp.stack(pv), jnp.stack(pp), jnp.stack(sv),
            jnp.stack(sp))
```

```python
import functools
from typing import Any, NamedTuple

import jax
import jax.numpy as jnp
from jax import lax
from jax.experimental import pallas as pl
from jax.experimental.pallas import tpu as pltpu

D_MODEL = 1024
CHUNK = 128
N_HEADS = D_MODEL // CHUNK
POOL_WINDOWS = (2, 4, 8, 16)
POOL_GROUP = D_MODEL // len(POOL_WINDOWS)
POOL_BUF = max(POOL_WINDOWS) - 1
PLE_DIM = 256
PAST_LEN = 16384
SAMPLE_ROWS = 128
EPS = 1e-6
COL_U, COL_V, COL_ZA, COL_XB, COL_ZB, COL_GA, COL_GB = range(7)
VEC_ROWS = tuple(range(7))
VEC_PRE_G, VEC_LN_G, VEC_LN_B, VEC_POOL_SCALE, VEC_POST_G, VEC_PLE_IN_G, VEC_PLE_OUT_G = VEC_ROWS
N_VEC_ROWS = 8

LANES = 128
N_LANE_BLOCKS = D_MODEL // LANES
BF16_ROWS = 16
SUBLANES = 8
TAIL = 32
DOUBLING_MIN_WINDOW = 8
PACK_STEPS = 8
TILE_M = 256
VMEM_LIMIT_BYTES = 56 * 1024 * 1024

_BF16 = jnp.bfloat16
_F32 = jnp.float32
_GELU_C = 0.7978845608028654


def _dot(a, b):
    return jnp.dot(a, b, preferred_element_type=_F32)


def _rmsnorm(x, g):
    return x * lax.rsqrt(jnp.mean(x * x, axis=-1, keepdims=True) + EPS) * g


def _layernorm(x, g, b):
    xc = x - jnp.mean(x, axis=-1, keepdims=True)
    return xc * lax.rsqrt(jnp.mean(xc * xc, axis=-1, keepdims=True) + EPS) * g + b


def _gelu_tanh(x):
    return 0.5 * x * (1.0 + jnp.tanh(_GELU_C * (x + 0.044715 * (x * x * x))))


def _sigmoid(x):
    return 0.5 * jnp.tanh(0.5 * x) + 0.5


def _silu(x):
    return x * _sigmoid(x)


def _unpack(ref, rows=slice(None), cols=slice(None)):
    return pltpu.bitcast(ref[rows, cols], _BF16)


class _Weights(NamedTuple):
    in_proj_w: Any
    spatial: Any
    pool: Any
    pa: Any
    pb: Any
    out: Any
    ple: Any
    pg: Any
    bias_t: Any
    vecs: Any

    def vec(self, row):
        return self.vecs[row:row + 1, :]

    def in_proj(self, xn, col):
        return _dot(xn, _unpack(self.in_proj_w, cols=slice(col * D_MODEL, (col + 1) * D_MODEL)))


def _pool_mix(pooled, w):
    parts = []
    for g in range(len(POOL_WINDOWS)):
        q = pooled[:, g * POOL_GROUP:(g + 1) * POOL_GROUP].astype(_BF16)
        rows = slice(g * POOL_GROUP // 2, (g + 1) * POOL_GROUP // 2)
        parts.append(_dot(q, _unpack(w.pool, rows=rows)))
    return jnp.concatenate(parts, axis=1) * w.vec(VEC_POOL_SCALE)


def _merge_and_embed(h, p, xn, y_a, y_b, w):
    g_a = w.in_proj(xn, COL_GA)
    g_b = w.in_proj(xn, COL_GB)
    m = (_sigmoid(g_a) * _dot(y_a.astype(_BF16), _unpack(w.pa))
         + _sigmoid(g_b) * _dot(y_b.astype(_BF16), _unpack(w.pb)))
    h = h + _rmsnorm(_dot(m.astype(_BF16), _unpack(w.out)), w.vec(VEC_POST_G))
    e = _dot(p.astype(_BF16), _unpack(w.ple))
    hn = _rmsnorm(h, w.vec(VEC_PLE_IN_G)).astype(_BF16)
    gate = _sigmoid(_dot(hn, _unpack(w.pg)))
    return h + _rmsnorm(gate * e, w.vec(VEC_PLE_OUT_G))


def _spatial_mix(vn_b, w, tile_m):
    w_tril = [_unpack(w.spatial, rows=slice(hd * CHUNK // 2, (hd + 1) * CHUNK // 2))
              for hd in range(N_HEADS)]
    bias = [w.bias_t[:, hd:hd + 1] for hd in range(N_HEADS)]
    s_rows = []
    for c in range(tile_m // CHUNK):
        heads = []
        for hd in range(N_HEADS):
            blk = vn_b[c * CHUNK:(c + 1) * CHUNK, hd * CHUNK:(hd + 1) * CHUNK]
            heads.append(_dot(w_tril[hd], blk) + bias[hd])
        s_rows.append(jnp.concatenate(heads, axis=1))
    return jnp.concatenate(s_rows, axis=0)


def _window_sum(src, lvl, w, rows):
    n_levels = w.bit_length() - 1
    assert w == 1 << n_levels and SUBLANES * (n_levels - 1) + 1 <= TAIL
    for l in range(1, n_levels):
        start = TAIL - SUBLANES * (n_levels - l)
        half = 1 << (l - 1)
        dst = lvl.at[(l - 1) % 2]
        dst[start:rows, :] = src[start:rows, :] + src[start - half:rows - half, :]
        src = dst
    half = w // 2
    return src[TAIL:rows, :] + src[TAIL - half:rows - half, :]


def _window_pool(x_b, ext_ref, lvl_ref, pool_ref, b, j, tile_m):
    rows = TAIL + tile_m
    for c in range(N_LANE_BLOCKS):
        ext_ref[c, TAIL:rows, :] = x_b[:, c * LANES:(c + 1) * LANES]
    pos = j * tile_m + lax.broadcasted_iota(jnp.int32, (tile_m, 1), 0)
    blocks = []
    for c in range(N_LANE_BLOCKS):
        w = POOL_WINDOWS[c * LANES // POOL_GROUP]
        cur = x_b[:, c * LANES:(c + 1) * LANES]
        if w >= DOUBLING_MIN_WINDOW:
            win = _window_sum(ext_ref.at[c], lvl_ref.at[c], w, rows)
        else:
            win = cur
            for k in range(1, w):
                win = win + ext_ref[c, TAIL - k:TAIL - k + tile_m, :]
        cnt = jnp.minimum(pos + 1, w).astype(_F32)
        blocks.append(win / cnt - cur)
    latest = jnp.concatenate([ext_ref[c, TAIL + tile_m - POOL_BUF:TAIL + tile_m, :]
                              for c in range(N_LANE_BLOCKS)], axis=1)
    pool_ref[:, pl.ds(b, 1), :] = latest[:, None, :]
    for c in range(N_LANE_BLOCKS):
        ext_ref[c, 0:TAIL, :] = ext_ref[c, tile_m:tile_m + TAIL, :]
    return jnp.concatenate(blocks, axis=1)


def _layer_kernel(x_ref, x_lag_ref, p_lag_ref, xs_ref, ps_ref, taps_ref, state_hbm,
                  w_in_ref, w_s_ref, w_pool_ref, w_pa_ref, w_pb_ref, w_out_ref, w_ple_ref,
                  w_pg_ref, bs_t_ref, vecs_ref,
                  y_ref, vn_ref, pool_ref, ys_ref, vns_ref, spool_hbm,
                  ext_ref, lvl_ref, xn_c, ya_c, yb_c, hist_ref, xbs_ref, sems,
                  *, tile_m, tiles_per_prompt, n_tiles, pos0):
    s_id = pl.program_id(0)
    front = jnp.minimum(s_id, n_tiles - 1)
    b = front // tiles_per_prompt
    j = lax.rem(front, tiles_per_prompt)
    w = _Weights(in_proj_w=w_in_ref, spatial=w_s_ref, pool=w_pool_ref, pa=w_pa_ref, pb=w_pb_ref,
                 out=w_out_ref, ple=w_ple_ref, pg=w_pg_ref, bias_t=bs_t_ref, vecs=vecs_ref)

    @pl.when(j == 0)
    def _():
        ext_ref[:, 0:TAIL, :] = jnp.zeros((N_LANE_BLOCKS, TAIL, LANES), _F32)

    @pl.when(s_id == 0)
    def _front_only():
        xn = _rmsnorm(x_ref[0], w.vec(VEC_PRE_G)).astype(_BF16)
        xn_c[...] = xn
        vn = _layernorm(_gelu_tanh(w.in_proj(xn, COL_V)), w.vec(VEC_LN_G), w.vec(VEC_LN_B))
        vn_ref[0] = vn[tile_m - CHUNK:, :]
        s = _spatial_mix(vn.astype(_BF16), w, tile_m)
        ya_c[...] = (_gelu_tanh(w.in_proj(xn, COL_U)) * s
                     * _silu(w.in_proj(xn, COL_ZA))).astype(_BF16)
        pooled = _window_pool(w.in_proj(xn, COL_XB), ext_ref, lvl_ref, pool_ref, b, j, tile_m)
        yb_c[...] = (_pool_mix(pooled, w) * _silu(w.in_proj(xn, COL_ZB))).astype(_BF16)

    @pl.when(s_id == n_tiles)
    def _back_only_and_sample():
        history, shift, newest = _sample_copies(state_hbm, spool_hbm, hist_ref, xbs_ref, sems)
        history.start()
        y_ref[0] = _merge_and_embed(x_lag_ref[0], p_lag_ref[0], xn_c[...], ya_c[...], yb_c[...], w)
        history.wait()
        shift.start()
        _sample_rows(xs_ref, ps_ref, taps_ref, hist_ref, newest, xbs_ref, w, ys_ref, vns_ref, pos0)
        shift.wait()
        newest.wait()

    @pl.when(jnp.logical_and(s_id > 0, s_id < n_tiles))
    def _both():
        xn_prev = xn_c[...]
        proj_a = _dot(ya_c[...], _unpack(w.pa))
        g_a = w.in_proj(xn_prev, COL_GA)
        xn = _rmsnorm(x_ref[0], w.vec(VEC_PRE_G)).astype(_BF16)
        xn_c[...] = xn
        v_raw = w.in_proj(xn, COL_V)
        proj_b = _dot(yb_c[...], _unpack(w.pb))
        g_b = w.in_proj(xn_prev, COL_GB)
        e = _dot(p_lag_ref[0].astype(_BF16), _unpack(w.ple))
        vn = _layernorm(_gelu_tanh(v_raw), w.vec(VEC_LN_G), w.vec(VEC_LN_B))
        vn_ref[0] = vn[tile_m - CHUNK:, :]
        u_raw = w.in_proj(xn, COL_U)
        s = _spatial_mix(vn.astype(_BF16), w, tile_m)
        x_b = w.in_proj(xn, COL_XB)
        m = (_sigmoid(g_a) * proj_a + _sigmoid(g_b) * proj_b).astype(_BF16)
        o = _dot(m, _unpack(w.out))
        za_raw = w.in_proj(xn, COL_ZA)
        pooled = _window_pool(x_b, ext_ref, lvl_ref, pool_ref, b, j, tile_m)
        ya_c[...] = (_gelu_tanh(u_raw) * s * _silu(za_raw)).astype(_BF16)
        zb_raw = w.in_proj(xn, COL_ZB)
        mixed = _pool_mix(pooled, w)
        h = x_lag_ref[0] + _rmsnorm(o, w.vec(VEC_POST_G))
        hn = _rmsnorm(h, w.vec(VEC_PLE_IN_G)).astype(_BF16)
        gate_raw = _dot(hn, _unpack(w.pg))
        yb_c[...] = (mixed * _silu(zb_raw)).astype(_BF16)
        y_ref[0] = h + _rmsnorm(_sigmoid(gate_raw) * e, w.vec(VEC_PLE_OUT_G))


def _gather_rows(ref):
    n = ref.shape[0] // SAMPLE_ROWS
    return jnp.concatenate([ref[pl.ds(c, SAMPLE_ROWS, stride=n), :] for c in range(n)], axis=1)


def _scatter_rows(ref, value):
    n = ref.shape[0] // SAMPLE_ROWS
    for c in range(n):
        ref[pl.ds(c, SAMPLE_ROWS, stride=n), :] = value[:, c * LANES:(c + 1) * LANES]


def _sample_copies(state_hbm, spool_hbm, hist_ref, xbs_ref, sems):
    history = pltpu.make_async_copy(state_hbm, hist_ref, sems.at[0])
    shift = pltpu.make_async_copy(hist_ref.at[pl.ds(1, POOL_BUF - 1)],
                                  spool_hbm.at[pl.ds(0, POOL_BUF - 1)], sems.at[1])
    newest = pltpu.make_async_copy(xbs_ref, spool_hbm.at[POOL_BUF - 1], sems.at[2])
    return history, shift, newest


def _sample_rows(x_ref, p_ref, taps_ref, hist_ref, newest, xbs_ref, w, y_ref, vn_ref, pos0):
    x = _gather_rows(x_ref)
    xn = _rmsnorm(x, w.vec(VEC_PRE_G)).astype(_BF16)

    u = _gelu_tanh(w.in_proj(xn, COL_U))
    vn = _layernorm(_gelu_tanh(w.in_proj(xn, COL_V)), w.vec(VEC_LN_G), w.vec(VEC_LN_B))
    _scatter_rows(vn_ref, vn)
    heads = []
    for hd in range(N_HEADS):
        tap = taps_ref[hd * CHUNK:hd * CHUNK + 1, 0:1]
        heads.append(vn[:, hd * CHUNK:(hd + 1) * CHUNK] * tap + w.bias_t[0:1, hd:hd + 1])
    s = jnp.concatenate(heads, axis=1)
    y_a = u * s * _silu(w.in_proj(xn, COL_ZA))

    x_b = w.in_proj(xn, COL_XB)
    xbs_ref[...] = x_b
    newest.start()
    groups = []
    for g, win_len in enumerate(POOL_WINDOWS):
        lo, hi = g * POOL_GROUP, (g + 1) * POOL_GROUP
        cur = x_b[:, lo:hi]
        win = cur
        for k in range(1, win_len):
            win = win + hist_ref[POOL_BUF - k, :, lo:hi]
        groups.append(win / float(min(pos0 + 1, win_len)) - cur)
    pooled = jnp.concatenate(groups, axis=1)
    y_b = _pool_mix(pooled, w) * _silu(w.in_proj(xn, COL_ZB))

    _scatter_rows(y_ref, _merge_and_embed(x, _gather_rows(p_ref), xn, y_a, y_b, w))


def _pack_kernel(*refs, tril_periods):
    n = len(tril_periods)
    n_vecs = len(VEC_ROWS)
    bias_ref, vec_refs = refs[n], refs[n + 1:n + 1 + n_vecs]
    outs = refs[n + 1 + n_vecs:]
    bias_t_ref, vecs_ref = outs[n], outs[n + 1]
    bias_t_ref[...] = bias_ref[...].T
    for row, vec_ref in zip(VEC_ROWS, vec_refs):
        vecs_ref[row:row + 1, :] = vec_ref[...]
    vecs_ref[n_vecs:, :] = jnp.zeros((N_VEC_ROWS - n_vecs, D_MODEL), _F32)
    for src, dst, period in zip(refs[:n], outs[:n], tril_periods):
        m = src[...]
        if period is not None:
            row = lax.broadcasted_iota(jnp.int32, m.shape, 0) % period
            m = jnp.where(row >= lax.broadcasted_iota(jnp.int32, m.shape, 1), m, 0.0)
        dst[...] = pltpu.bitcast(m.astype(_BF16), jnp.uint32)


def _pack_params(mats, tril_periods, bias, vecs, n_steps):
    blocks = [m.shape[0] // n_steps for m in mats]
    small = lambda shape: pl.BlockSpec(shape, lambda i: (0, 0))
    for m, rows, period in zip(mats, blocks, tril_periods):
        assert rows * n_steps == m.shape[0] and rows % (2 * BF16_ROWS) == 0
        assert period is None or rows % period == 0
    return pl.pallas_call(
        functools.partial(_pack_kernel, tril_periods=tril_periods),
        grid=(n_steps,),
        in_specs=[pl.BlockSpec((rows, m.shape[1]), lambda i: (i, 0))
                  for m, rows in zip(mats, blocks)]
                 + [small(bias.shape)] + [small(v.shape) for v in vecs],
        out_specs=[pl.BlockSpec((rows // 2, m.shape[1]), lambda i: (i, 0))
                   for m, rows in zip(mats, blocks)]
                  + [small(bias.shape[::-1]), small((N_VEC_ROWS, D_MODEL))],
        out_shape=[jax.ShapeDtypeStruct((m.shape[0] // 2, m.shape[1]), jnp.uint32) for m in mats]
                  + [jax.ShapeDtypeStruct(bias.shape[::-1], _F32),
                     jax.ShapeDtypeStruct((N_VEC_ROWS, D_MODEL), _F32)],
        compiler_params=pltpu.CompilerParams(dimension_semantics=("arbitrary",),
                                             vmem_limit_bytes=VMEM_LIMIT_BYTES),
        name="pack_params",
    )(*mats, bias, *vecs)


def _resident(shape):
    zeros = (0,) * len(shape)
    return pl.BlockSpec(shape, lambda *_: zeros, pipeline_mode=pl.Buffered(1))


def _layer(h, p, hs, ps, state, taps, weights, tile_m, pos0):
    batch, seq, _ = h.shape
    rows = state.shape[1]
    assert seq % tile_m == 0 and tile_m % CHUNK == 0 and seq >= POOL_BUF and rows == SAMPLE_ROWS
    tiles_per_prompt = seq // tile_m
    n_tiles = batch * tiles_per_prompt
    assert n_tiles >= 2

    def front_tile(s):
        t = jnp.minimum(s, n_tiles - 1)
        return t // tiles_per_prompt, t % tiles_per_prompt, 0

    def back_tile(s):
        t = jnp.maximum(s - 1, 0)
        return t // tiles_per_prompt, t % tiles_per_prompt, 0

    per_prompt = lambda s: (jnp.minimum(s, n_tiles - 1) // tiles_per_prompt, 0, 0)
    hbm = pl.BlockSpec(memory_space=pl.ANY)
    whole = lambda a: pl.BlockSpec(a.shape, lambda s: (0,) * a.ndim)
    return pl.pallas_call(
        functools.partial(_layer_kernel, tile_m=tile_m, tiles_per_prompt=tiles_per_prompt,
                          n_tiles=n_tiles, pos0=pos0),
        grid=(n_tiles + 1,),
        in_specs=[pl.BlockSpec((1, tile_m, D_MODEL), front_tile),
                  pl.BlockSpec((1, tile_m, D_MODEL), back_tile),
                  pl.BlockSpec((1, tile_m, PLE_DIM), back_tile),
                  _resident(hs.shape), _resident(ps.shape), _resident(taps.shape), hbm]
                 + [_resident(a.shape) for a in weights],
        out_specs=[pl.BlockSpec((1, tile_m, D_MODEL), back_tile),
                   pl.BlockSpec((1, CHUNK, D_MODEL), per_prompt),
                   pl.BlockSpec((POOL_BUF, batch, D_MODEL), lambda s: (0, 0, 0)),
                   whole(hs), whole(hs), hbm],
        out_shape=[jax.ShapeDtypeStruct((batch, seq, D_MODEL), _F32),
                   jax.ShapeDtypeStruct((batch, CHUNK, D_MODEL), _F32),
                   jax.ShapeDtypeStruct((POOL_BUF, batch, D_MODEL), _F32),
                   jax.ShapeDtypeStruct(hs.shape, _F32),
                   jax.ShapeDtypeStruct(hs.shape, _F32),
                   jax.ShapeDtypeStruct(state.shape, _F32)],
        scratch_shapes=[pltpu.VMEM((N_LANE_BLOCKS, TAIL + tile_m, LANES), _F32),
                        pltpu.VMEM((N_LANE_BLOCKS, 2, TAIL + tile_m, LANES), _F32),
                        pltpu.VMEM((tile_m, D_MODEL), _BF16),
                        pltpu.VMEM((tile_m, D_MODEL), _BF16),
                        pltpu.VMEM((tile_m, D_MODEL), _BF16),
                        pltpu.VMEM(state.shape, _F32),
                        pltpu.VMEM((rows, D_MODEL), _F32),
                        pltpu.SemaphoreType.DMA((3,))],
        compiler_params=pltpu.CompilerParams(
            dimension_semantics=("arbitrary",),
            vmem_limit_bytes=VMEM_LIMIT_BYTES),
        name="layer",
    )(h, h, p, hs, ps, taps, state, *weights)


def kernel(x_prompt, x_sample, state_pool, p_prompt, p_sample, pre_g, w_in, ln_g, ln_b, w_s, b_s,
           w_pool, pool_scale, w_pa, w_pb, w_out, post_g, w_ple, w_pg, ple_in_g, ple_out_g):
    depth = w_in.shape[0]
    dec_batch, dec_seq, _ = x_sample.shape
    assert dec_seq == 1 and x_prompt.shape[2] == D_MODEL
    hp = x_prompt
    hs = x_sample.reshape(dec_batch * N_LANE_BLOCKS, LANES)
    pv, pp, sv, sp = [], [], [], []
    for i in range(depth):
        w_s_rows = w_s[i].reshape(D_MODEL, CHUNK)
        params = _pack_params(
            [w_in[i], w_s_rows, w_pool[i].reshape(D_MODEL, POOL_GROUP), w_pa[i], w_pb[i],
             w_out[i], w_ple[i], w_pg[i]],
            (None, CHUNK) + (None,) * 6, b_s[i],
            [v[i:i + 1] for v in (pre_g, ln_g, ln_b, pool_scale, post_g, ple_in_g, ple_out_g)],
            n_steps=PACK_STEPS)
        hp, v, pool, hs, v_s, pool_s = _layer(
            hp, p_prompt[i], hs, p_sample[i].reshape(dec_batch * PLE_DIM // LANES, LANES),
            jnp.swapaxes(state_pool[i], 0, 1), w_s_rows, params, TILE_M, PAST_LEN)
        pv.append(v)
        pp.append(jnp.swapaxes(pool, 0, 1))
        sv.append(v_s.reshape(dec_batch, 1, D_MODEL))
        sp.append(jnp.swapaxes(pool_s, 0, 1))
    return (hp, hs.reshape(dec_batch, 1, D_MODEL), jnp.stack(pv), jnp.stack(pp), jnp.stack(sv),
            jnp.stack(sp))
```

```python
import functools
from typing import Any, NamedTuple

import jax
import jax.numpy as jnp
from jax import lax
from jax.experimental import pallas as pl
from jax.experimental.pallas import tpu as pltpu

D_MODEL = 1024
CHUNK = 128
N_HEADS = D_MODEL // CHUNK
POOL_WINDOWS = (2, 4, 8, 16)
POOL_GROUP = D_MODEL // len(POOL_WINDOWS)
POOL_BUF = max(POOL_WINDOWS) - 1
PLE_DIM = 256
PAST_LEN = 16384
SAMPLE_ROWS = 128
EPS = 1e-6
COL_U, COL_V, COL_ZA, COL_XB, COL_ZB, COL_GA, COL_GB = range(7)
VEC_ROWS = tuple(range(7))
VEC_PRE_G, VEC_LN_G, VEC_LN_B, VEC_POOL_SCALE, VEC_POST_G, VEC_PLE_IN_G, VEC_PLE_OUT_G = VEC_ROWS
N_VEC_ROWS = 8

LANES = 128
N_LANE_BLOCKS = D_MODEL // LANES
BF16_ROWS = 16
SUBLANES = 8
TAIL = 32
DOUBLING_MIN_WINDOW = 8
PACK_STEPS = 8
TILE_M = 256
VMEM_LIMIT_BYTES = 56 * 1024 * 1024

_BF16 = jnp.bfloat16
_F32 = jnp.float32
_GELU_C = 0.7978845608028654


def _dot(a, b):
    return jnp.dot(a, b, preferred_element_type=_F32)


def _rmsnorm(x, g):
    return x * lax.rsqrt(jnp.mean(x * x, axis=-1, keepdims=True) + EPS) * g


def _layernorm(x, g, b):
    xc = x - jnp.mean(x, axis=-1, keepdims=True)
    return xc * lax.rsqrt(jnp.mean(xc * xc, axis=-1, keepdims=True) + EPS) * g + b


def _gelu_tanh(x):
    return 0.5 * x * (1.0 + jnp.tanh(_GELU_C * (x + 0.044715 * (x * x * x))))


def _sigmoid(x):
    return 0.5 * jnp.tanh(0.5 * x) + 0.5


def _silu(x):
    return x * _sigmoid(x)


def _unpack(ref, rows=slice(None), cols=slice(None)):
    return pltpu.bitcast(ref[rows, cols], _BF16)


class _Weights(NamedTuple):
    in_proj_w: Any
    spatial: Any
    pool: Any
    pa: Any
    pb: Any
    out: Any
    ple: Any
    pg: Any
    bias_t: Any
    vecs: Any

    def vec(self, row):
        return self.vecs[row:row + 1, :]

    def in_proj(self, xn, col):
        return _dot(xn, _unpack(self.in_proj_w, cols=slice(col * D_MODEL, (col + 1) * D_MODEL)))


def _pool_mix(pooled, w):
    parts = []
    for g in range(len(POOL_WINDOWS)):
        q = pooled[:, g * POOL_GROUP:(g + 1) * POOL_GROUP].astype(_BF16)
        rows = slice(g * POOL_GROUP // 2, (g + 1) * POOL_GROUP // 2)
        parts.append(_dot(q, _unpack(w.pool, rows=rows)))
    return jnp.concatenate(parts, axis=1) * w.vec(VEC_POOL_SCALE)


def _merge_and_embed(h, p, xn, y_a, y_b, w):
    g_a = w.in_proj(xn, COL_GA)
    g_b = w.in_proj(xn, COL_GB)
    m = (_sigmoid(g_a) * _dot(y_a.astype(_BF16), _unpack(w.pa))
         + _sigmoid(g_b) * _dot(y_b.astype(_BF16), _unpack(w.pb)))
    h = h + _rmsnorm(_dot(m.astype(_BF16), _unpack(w.out)), w.vec(VEC_POST_G))
    e = _dot(p.astype(_BF16), _unpack(w.ple))
    hn = _rmsnorm(h, w.vec(VEC_PLE_IN_G)).astype(_BF16)
    gate = _sigmoid(_dot(hn, _unpack(w.pg)))
    return h + _rmsnorm(gate * e, w.vec(VEC_PLE_OUT_G))


def _spatial_mix(vn_b, w, tile_m):
    w_tril = [_unpack(w.spatial, rows=slice(hd * CHUNK // 2, (hd + 1) * CHUNK // 2))
              for hd in range(N_HEADS)]
    bias = [w.bias_t[:, hd:hd + 1] for hd in range(N_HEADS)]
    s_rows = []
    for c in range(tile_m // CHUNK):
        heads = []
        for hd in range(N_HEADS):
            blk = vn_b[c * CHUNK:(c + 1) * CHUNK, hd * CHUNK:(hd + 1) * CHUNK]
            heads.append(_dot(w_tril[hd], blk) + bias[hd])
        s_rows.append(jnp.concatenate(heads, axis=1))
    return jnp.concatenate(s_rows, axis=0)


def _window_sum(src, lvl, w, rows):
    n_levels = w.bit_length() - 1
    assert w == 1 << n_levels and SUBLANES * (n_levels - 1) + 1 <= TAIL
    for l in range(1, n_levels):
        start = TAIL - SUBLANES * (n_levels - l)
        half = 1 << (l - 1)
        dst = lvl.at[(l - 1) % 2]
        dst[start:rows, :] = src[start:rows, :] + src[start - half:rows - half, :]
        src = dst
    half = w // 2
    return src[TAIL:rows, :] + src[TAIL - half:rows - half, :]


def _window_pool(x_b, ext_ref, lvl_ref, pool_ref, b, j, tile_m):
    rows = TAIL + tile_m
    for c in range(N_LANE_BLOCKS):
        ext_ref[c, TAIL:rows, :] = x_b[:, c * LANES:(c + 1) * LANES]
    pos = j * tile_m + lax.broadcasted_iota(jnp.int32, (tile_m, 1), 0)
    blocks = []
    for c in range(N_LANE_BLOCKS):
        w = POOL_WINDOWS[c * LANES // POOL_GROUP]
        cur = x_b[:, c * LANES:(c + 1) * LANES]
        if w >= DOUBLING_MIN_WINDOW:
            win = _window_sum(ext_ref.at[c], lvl_ref.at[c], w, rows)
        else:
            win = cur
            for k in range(1, w):
                win = win + ext_ref[c, TAIL - k:TAIL - k + tile_m, :]
        cnt = jnp.minimum(pos + 1, w).astype(_F32)
        blocks.append(win / cnt - cur)
    latest = jnp.concatenate([ext_ref[c, TAIL + tile_m - POOL_BUF:TAIL + tile_m, :]
                              for c in range(N_LANE_BLOCKS)], axis=1)
    pool_ref[:, pl.ds(b, 1), :] = latest[:, None, :]
    for c in range(N_LANE_BLOCKS):
        ext_ref[c, 0:TAIL, :] = ext_ref[c, tile_m:tile_m + TAIL, :]
    return jnp.concatenate(blocks, axis=1)


def _layer_kernel(x_ref, x_lag_ref, p_lag_ref, xs_ref, ps_ref, taps_ref, state_hbm,
                  w_in_ref, w_s_ref, w_pool_ref, w_pa_ref, w_pb_ref, w_out_ref, w_ple_ref,
                  w_pg_ref, bs_t_ref, vecs_ref,
                  y_ref, vn_ref, pool_ref, ys_ref, vns_ref, spool_hbm,
                  ext_ref, lvl_ref, xn_c, ya_c, yb_c, hist_ref, xbs_ref, sems,
                  *, tile_m, tiles_per_prompt, n_tiles, pos0):
    s_id = pl.program_id(0)
    front = jnp.minimum(s_id, n_tiles - 1)
    b = front // tiles_per_prompt
    j = lax.rem(front, tiles_per_prompt)
    w = _Weights(in_proj_w=w_in_ref, spatial=w_s_ref, pool=w_pool_ref, pa=w_pa_ref, pb=w_pb_ref,
                 out=w_out_ref, ple=w_ple_ref, pg=w_pg_ref, bias_t=bs_t_ref, vecs=vecs_ref)

    @pl.when(j == 0)
    def _():
        ext_ref[:, 0:TAIL, :] = jnp.zeros((N_LANE_BLOCKS, TAIL, LANES), _F32)

    @pl.when(s_id == 0)
    def _front_only():
        xn = _rmsnorm(x_ref[0], w.vec(VEC_PRE_G)).astype(_BF16)
        xn_c[...] = xn
        vn = _layernorm(_gelu_tanh(w.in_proj(xn, COL_V)), w.vec(VEC_LN_G), w.vec(VEC_LN_B))
        vn_ref[0] = vn[tile_m - CHUNK:, :]
        s = _spatial_mix(vn.astype(_BF16), w, tile_m)
        ya_c[...] = (_gelu_tanh(w.in_proj(xn, COL_U)) * s
                     * _silu(w.in_proj(xn, COL_ZA))).astype(_BF16)
        pooled = _window_pool(w.in_proj(xn, COL_XB), ext_ref, lvl_ref, pool_ref, b, j, tile_m)
        yb_c[...] = (_pool_mix(pooled, w) * _silu(w.in_proj(xn, COL_ZB))).astype(_BF16)

    @pl.when(s_id == n_tiles)
    def _back_only_and_sample():
        history, shift, newest = _sample_copies(state_hbm, spool_hbm, hist_ref, xbs_ref, sems)
        history.start()
        y_ref[0] = _merge_and_embed(x_lag_ref[0], p_lag_ref[0], xn_c[...], ya_c[...], yb_c[...], w)
        history.wait()
        shift.start()
        _sample_rows(xs_ref, ps_ref, taps_ref, hist_ref, newest, xbs_ref, w, ys_ref, vns_ref, pos0)
        shift.wait()
        newest.wait()

    @pl.when(jnp.logical_and(s_id > 0, s_id < n_tiles))
    def _both():
        xn_prev = xn_c[...]
        g_a = w.in_proj(xn_prev, COL_GA)
        xn = _rmsnorm(x_ref[0], w.vec(VEC_PRE_G)).astype(_BF16)
        xn_c[...] = xn
        v_raw = w.in_proj(xn, COL_V)
        g_b = w.in_proj(xn_prev, COL_GB)
        e = _dot(p_lag_ref[0].astype(_BF16), _unpack(w.ple))
        vn = _layernorm(_gelu_tanh(v_raw), w.vec(VEC_LN_G), w.vec(VEC_LN_B))
        vn_ref[0] = vn[tile_m - CHUNK:, :]
        u_raw = w.in_proj(xn, COL_U)
        s = _spatial_mix(vn.astype(_BF16), w, tile_m)
        proj_a = _dot(ya_c[...], _unpack(w.pa))
        proj_b = _dot(yb_c[...], _unpack(w.pb))
        x_b = w.in_proj(xn, COL_XB)
        m = (_sigmoid(g_a) * proj_a + _sigmoid(g_b) * proj_b).astype(_BF16)
        o = _dot(m, _unpack(w.out))
        zb_raw = w.in_proj(xn, COL_ZB)
        za_raw = w.in_proj(xn, COL_ZA)
        pooled = _window_pool(x_b, ext_ref, lvl_ref, pool_ref, b, j, tile_m)
        ya_c[...] = (_gelu_tanh(u_raw) * s * _silu(za_raw)).astype(_BF16)
        mixed = _pool_mix(pooled, w)
        h = x_lag_ref[0] + _rmsnorm(o, w.vec(VEC_POST_G))
        hn = _rmsnorm(h, w.vec(VEC_PLE_IN_G)).astype(_BF16)
        gate_raw = _dot(hn, _unpack(w.pg))
        yb_c[...] = (mixed * _silu(zb_raw)).astype(_BF16)
        y_ref[0] = h + _rmsnorm(_sigmoid(gate_raw) * e, w.vec(VEC_PLE_OUT_G))


def _gather_rows(ref):
    n = ref.shape[0] // SAMPLE_ROWS
    return jnp.concatenate([ref[pl.ds(c, SAMPLE_ROWS, stride=n), :] for c in range(n)], axis=1)


def _scatter_rows(ref, value):
    n = ref.shape[0] // SAMPLE_ROWS
    for c in range(n):
        ref[pl.ds(c, SAMPLE_ROWS, stride=n), :] = value[:, c * LANES:(c + 1) * LANES]


def _sample_copies(state_hbm, spool_hbm, hist_ref, xbs_ref, sems):
    history = pltpu.make_async_copy(state_hbm, hist_ref, sems.at[0])
    shift = pltpu.make_async_copy(hist_ref.at[pl.ds(1, POOL_BUF - 1)],
                                  spool_hbm.at[pl.ds(0, POOL_BUF - 1)], sems.at[1])
    newest = pltpu.make_async_copy(xbs_ref, spool_hbm.at[POOL_BUF - 1], sems.at[2])
    return history, shift, newest


def _sample_rows(x_ref, p_ref, taps_ref, hist_ref, newest, xbs_ref, w, y_ref, vn_ref, pos0):
    x = _gather_rows(x_ref)
    xn = _rmsnorm(x, w.vec(VEC_PRE_G)).astype(_BF16)

    u = _gelu_tanh(w.in_proj(xn, COL_U))
    vn = _layernorm(_gelu_tanh(w.in_proj(xn, COL_V)), w.vec(VEC_LN_G), w.vec(VEC_LN_B))
    _scatter_rows(vn_ref, vn)
    heads = []
    for hd in range(N_HEADS):
        tap = taps_ref[hd * CHUNK:hd * CHUNK + 1, 0:1]
        heads.append(vn[:, hd * CHUNK:(hd + 1) * CHUNK] * tap + w.bias_t[0:1, hd:hd + 1])
    s = jnp.concatenate(heads, axis=1)
    y_a = u * s * _silu(w.in_proj(xn, COL_ZA))

    x_b = w.in_proj(xn, COL_XB)
    xbs_ref[...] = x_b
    newest.start()
    groups = []
    for g, win_len in enumerate(POOL_WINDOWS):
        lo, hi = g * POOL_GROUP, (g + 1) * POOL_GROUP
        cur = x_b[:, lo:hi]
        win = cur
        for k in range(1, win_len):
            win = win + hist_ref[POOL_BUF - k, :, lo:hi]
        groups.append(win / float(min(pos0 + 1, win_len)) - cur)
    pooled = jnp.concatenate(groups, axis=1)
    y_b = _pool_mix(pooled, w) * _silu(w.in_proj(xn, COL_ZB))

    _scatter_rows(y_ref, _merge_and_embed(x, _gather_rows(p_ref), xn, y_a, y_b, w))


def _pack_kernel(*refs, tril_periods):
    n = len(tril_periods)
    n_vecs = len(VEC_ROWS)
    bias_ref, vec_refs = refs[n], refs[n + 1:n + 1 + n_vecs]
    outs = refs[n + 1 + n_vecs:]
    bias_t_ref, vecs_ref = outs[n], outs[n + 1]
    bias_t_ref[...] = bias_ref[...].T
    for row, vec_ref in zip(VEC_ROWS, vec_refs):
        vecs_ref[row:row + 1, :] = vec_ref[...]
    vecs_ref[n_vecs:, :] = jnp.zeros((N_VEC_ROWS - n_vecs, D_MODEL), _F32)
    for src, dst, period in zip(refs[:n], outs[:n], tril_periods):
        m = src[...]
        if period is not None:
            row = lax.broadcasted_iota(jnp.int32, m.shape, 0) % period
            m = jnp.where(row >= lax.broadcasted_iota(jnp.int32, m.shape, 1), m, 0.0)
        dst[...] = pltpu.bitcast(m.astype(_BF16), jnp.uint32)


def _pack_params(mats, tril_periods, bias, vecs, n_steps):
    blocks = [m.shape[0] // n_steps for m in mats]
    small = lambda shape: pl.BlockSpec(shape, lambda i: (0, 0))
    for m, rows, period in zip(mats, blocks, tril_periods):
        assert rows * n_steps == m.shape[0] and rows % (2 * BF16_ROWS) == 0
        assert period is None or rows % period == 0
    return pl.pallas_call(
        functools.partial(_pack_kernel, tril_periods=tril_periods),
        grid=(n_steps,),
        in_specs=[pl.BlockSpec((rows, m.shape[1]), lambda i: (i, 0))
                  for m, rows in zip(mats, blocks)]
                 + [small(bias.shape)] + [small(v.shape) for v in vecs],
        out_specs=[pl.BlockSpec((rows // 2, m.shape[1]), lambda i: (i, 0))
                   for m, rows in zip(mats, blocks)]
                  + [small(bias.shape[::-1]), small((N_VEC_ROWS, D_MODEL))],
        out_shape=[jax.ShapeDtypeStruct((m.shape[0] // 2, m.shape[1]), jnp.uint32) for m in mats]
                  + [jax.ShapeDtypeStruct(bias.shape[::-1], _F32),
                     jax.ShapeDtypeStruct((N_VEC_ROWS, D_MODEL), _F32)],
        compiler_params=pltpu.CompilerParams(dimension_semantics=("arbitrary",),
                                             vmem_limit_bytes=VMEM_LIMIT_BYTES),
        name="pack_params",
    )(*mats, bias, *vecs)


def _resident(shape):
    zeros = (0,) * len(shape)
    return pl.BlockSpec(shape, lambda *_: zeros, pipeline_mode=pl.Buffered(1))


def _layer(h, p, hs, ps, state, taps, weights, tile_m, pos0):
    batch, seq, _ = h.shape
    rows = state.shape[1]
    assert seq % tile_m == 0 and tile_m % CHUNK == 0 and seq >= POOL_BUF and rows == SAMPLE_ROWS
    tiles_per_prompt = seq // tile_m
    n_tiles = batch * tiles_per_prompt
    assert n_tiles >= 2

    def front_tile(s):
        t = jnp.minimum(s, n_tiles - 1)
        return t // tiles_per_prompt, t % tiles_per_prompt, 0

    def back_tile(s):
        t = jnp.maximum(s - 1, 0)
        return t // tiles_per_prompt, t % tiles_per_prompt, 0

    per_prompt = lambda s: (jnp.minimum(s, n_tiles - 1) // tiles_per_prompt, 0, 0)
    hbm = pl.BlockSpec(memory_space=pl.ANY)
    whole = lambda a: pl.BlockSpec(a.shape, lambda s: (0,) * a.ndim)
    return pl.pallas_call(
        functools.partial(_layer_kernel, tile_m=tile_m, tiles_per_prompt=tiles_per_prompt,
                          n_tiles=n_tiles, pos0=pos0),
        grid=(n_tiles + 1,),
        in_specs=[pl.BlockSpec((1, tile_m, D_MODEL), front_tile),
                  pl.BlockSpec((1, tile_m, D_MODEL), back_tile),
                  pl.BlockSpec((1, tile_m, PLE_DIM), back_tile),
                  _resident(hs.shape), _resident(ps.shape), _resident(taps.shape), hbm]
                 + [_resident(a.shape) for a in weights],
        out_specs=[pl.BlockSpec((1, tile_m, D_MODEL), back_tile),
                   pl.BlockSpec((1, CHUNK, D_MODEL), per_prompt),
                   pl.BlockSpec((POOL_BUF, batch, D_MODEL), lambda s: (0, 0, 0)),
                   whole(hs), whole(hs), hbm],
        out_shape=[jax.ShapeDtypeStruct((batch, seq, D_MODEL), _F32),
                   jax.ShapeDtypeStruct((batch, CHUNK, D_MODEL), _F32),
                   jax.ShapeDtypeStruct((POOL_BUF, batch, D_MODEL), _F32),
                   jax.ShapeDtypeStruct(hs.shape, _F32),
                   jax.ShapeDtypeStruct(hs.shape, _F32),
                   jax.ShapeDtypeStruct(state.shape, _F32)],
        scratch_shapes=[pltpu.VMEM((N_LANE_BLOCKS, TAIL + tile_m, LANES), _F32),
                        pltpu.VMEM((N_LANE_BLOCKS, 2, TAIL + tile_m, LANES), _F32),
                        pltpu.VMEM((tile_m, D_MODEL), _BF16),
                        pltpu.VMEM((tile_m, D_MODEL), _BF16),
                        pltpu.VMEM((tile_m, D_MODEL), _BF16),
                        pltpu.VMEM(state.shape, _F32),
                        pltpu.VMEM((rows, D_MODEL), _F32),
                        pltpu.SemaphoreType.DMA((3,))],
        compiler_params=pltpu.CompilerParams(
            dimension_semantics=("arbitrary",),
            vmem_limit_bytes=VMEM_LIMIT_BYTES),
        name="layer",
    )(h, h, p, hs, ps, taps, state, *weights)


def kernel(x_prompt, x_sample, state_pool, p_prompt, p_sample, pre_g, w_in, ln_g, ln_b, w_s, b_s,
           w_pool, pool_scale, w_pa, w_pb, w_out, post_g, w_ple, w_pg, ple_in_g, ple_out_g):
    depth = w_in.shape[0]
    dec_batch, dec_seq, _ = x_sample.shape
    assert dec_seq == 1 and x_prompt.shape[2] == D_MODEL
    hp = x_prompt
    hs = x_sample.reshape(dec_batch * N_LANE_BLOCKS, LANES)
    pv, pp, sv, sp = [], [], [], []
    for i in range(depth):
        w_s_rows = w_s[i].reshape(D_MODEL, CHUNK)
        params = _pack_params(
            [w_in[i], w_s_rows, w_pool[i].reshape(D_MODEL, POOL_GROUP), w_pa[i], w_pb[i],
             w_out[i], w_ple[i], w_pg[i]],
            (None, CHUNK) + (None,) * 6, b_s[i],
            [v[i:i + 1] for v in (pre_g, ln_g, ln_b, pool_scale, post_g, ple_in_g, ple_out_g)],
            n_steps=PACK_STEPS)
        hp, v, pool, hs, v_s, pool_s = _layer(
            hp, p_prompt[i], hs, p_sample[i].reshape(dec_batch * PLE_DIM // LANES, LANES),
            jnp.swapaxes(state_pool[i], 0, 1), w_s_rows, params, TILE_M, PAST_LEN)
        pv.append(v)
        pp.append(jnp.swapaxes(pool, 0, 1))
        sv.append(v_s.reshape(dec_batch, 1, D_MODEL))
        sp.append(jnp.swapaxes(pool_s, 0, 1))
    return (hp, hs.reshape(dec_batch, 1, D_MODEL), jnp.stack(pv), jnp.stack(pp), jnp.stack(sv),
            jnp.stack(sp))
```

```python
import functools
from typing import Any, NamedTuple

import jax
import jax.numpy as jnp
from jax import lax
from jax.experimental import pallas as pl
from jax.experimental.pallas import tpu as pltpu

D_MODEL = 1024
CHUNK = 128
N_HEADS = D_MODEL // CHUNK
POOL_WINDOWS = (2, 4, 8, 16)
POOL_GROUP = D_MODEL // len(POOL_WINDOWS)
POOL_BUF = max(POOL_WINDOWS) - 1
PLE_DIM = 256
PAST_LEN = 16384
SAMPLE_ROWS = 128
EPS = 1e-6
COL_U, COL_V, COL_ZA, COL_XB, COL_ZB, COL_GA, COL_GB = range(7)
VEC_ROWS = tuple(range(7))
VEC_PRE_G, VEC_LN_G, VEC_LN_B, VEC_POOL_SCALE, VEC_POST_G, VEC_PLE_IN_G, VEC_PLE_OUT_G = VEC_ROWS
N_VEC_ROWS = 8

LANES = 128
N_LANE_BLOCKS = D_MODEL // LANES
BF16_ROWS = 16
SUBLANES = 8
TAIL = 32
DOUBLING_MIN_WINDOW = 8
PACK_STEPS = 8
TILE_M = 256
VMEM_LIMIT_BYTES = 56 * 1024 * 1024

_BF16 = jnp.bfloat16
_F32 = jnp.float32
_GELU_C = 0.7978845608028654


def _dot(a, b):
    return jnp.dot(a, b, preferred_element_type=_F32)


def _rmsnorm(x, g):
    return x * lax.rsqrt(jnp.mean(x * x, axis=-1, keepdims=True) + EPS) * g


def _layernorm(x, g, b):
    xc = x - jnp.mean(x, axis=-1, keepdims=True)
    return xc * lax.rsqrt(jnp.mean(xc * xc, axis=-1, keepdims=True) + EPS) * g + b


def _gelu_tanh(x):
    return 0.5 * x * (1.0 + jnp.tanh(_GELU_C * (x + 0.044715 * (x * x * x))))


def _sigmoid(x):
    return 0.5 * jnp.tanh(0.5 * x) + 0.5


def _silu(x):
    return x * _sigmoid(x)


def _unpack(ref, rows=slice(None), cols=slice(None)):
    return pltpu.bitcast(ref[rows, cols], _BF16)


class _Weights(NamedTuple):
    in_proj_w: Any
    spatial: Any
    pool: Any
    pa: Any
    pb: Any
    out: Any
    ple: Any
    pg: Any
    bias_t: Any
    vecs: Any

    def vec(self, row):
        return self.vecs[row:row + 1, :]

    def in_proj(self, xn, col):
        return _dot(xn, _unpack(self.in_proj_w, cols=slice(col * D_MODEL, (col + 1) * D_MODEL)))


def _pool_mix(pooled, w):
    parts = []
    for g in range(len(POOL_WINDOWS)):
        q = pooled[:, g * POOL_GROUP:(g + 1) * POOL_GROUP].astype(_BF16)
        rows = slice(g * POOL_GROUP // 2, (g + 1) * POOL_GROUP // 2)
        parts.append(_dot(q, _unpack(w.pool, rows=rows)))
    return jnp.concatenate(parts, axis=1) * w.vec(VEC_POOL_SCALE)


def _merge_and_embed(h, p, xn, y_a, y_b, w):
    g_a = w.in_proj(xn, COL_GA)
    g_b = w.in_proj(xn, COL_GB)
    m = (_sigmoid(g_a) * _dot(y_a.astype(_BF16), _unpack(w.pa))
         + _sigmoid(g_b) * _dot(y_b.astype(_BF16), _unpack(w.pb)))
    h = h + _rmsnorm(_dot(m.astype(_BF16), _unpack(w.out)), w.vec(VEC_POST_G))
    e = _dot(p.astype(_BF16), _unpack(w.ple))
    hn = _rmsnorm(h, w.vec(VEC_PLE_IN_G)).astype(_BF16)
    gate = _sigmoid(_dot(hn, _unpack(w.pg)))
    return h + _rmsnorm(gate * e, w.vec(VEC_PLE_OUT_G))


def _spatial_mix(vn_b, w, tile_m):
    w_tril = [_unpack(w.spatial, rows=slice(hd * CHUNK // 2, (hd + 1) * CHUNK // 2))
              for hd in range(N_HEADS)]
    bias = [w.bias_t[:, hd:hd + 1] for hd in range(N_HEADS)]
    s_rows = []
    for c in range(tile_m // CHUNK):
        heads = []
        for hd in range(N_HEADS):
            blk = vn_b[c * CHUNK:(c + 1) * CHUNK, hd * CHUNK:(hd + 1) * CHUNK]
            heads.append(_dot(w_tril[hd], blk) + bias[hd])
        s_rows.append(jnp.concatenate(heads, axis=1))
    return jnp.concatenate(s_rows, axis=0)


def _window_sum(src, lvl, w, rows):
    n_levels = w.bit_length() - 1
    assert w == 1 << n_levels and SUBLANES * (n_levels - 1) + 1 <= TAIL
    for l in range(1, n_levels):
        start = TAIL - SUBLANES * (n_levels - l)
        half = 1 << (l - 1)
        dst = lvl.at[(l - 1) % 2]
        dst[start:rows, :] = src[start:rows, :] + src[start - half:rows - half, :]
        src = dst
    half = w // 2
    return src[TAIL:rows, :] + src[TAIL - half:rows - half, :]


def _window_pool(x_b, ext_ref, lvl_ref, pool_ref, b, j, tile_m):
    rows = TAIL + tile_m
    for c in range(N_LANE_BLOCKS):
        ext_ref[c, TAIL:rows, :] = x_b[:, c * LANES:(c + 1) * LANES]
    pos = j * tile_m + lax.broadcasted_iota(jnp.int32, (tile_m, 1), 0)
    blocks = []
    for c in range(N_LANE_BLOCKS):
        w = POOL_WINDOWS[c * LANES // POOL_GROUP]
        cur = x_b[:, c * LANES:(c + 1) * LANES]
        if w >= DOUBLING_MIN_WINDOW:
            win = _window_sum(ext_ref.at[c], lvl_ref.at[c], w, rows)
        else:
            win = cur
            for k in range(1, w):
                win = win + ext_ref[c, TAIL - k:TAIL - k + tile_m, :]
        cnt = jnp.minimum(pos + 1, w).astype(_F32)
        blocks.append(win / cnt - cur)
    latest = jnp.concatenate([ext_ref[c, TAIL + tile_m - POOL_BUF:TAIL + tile_m, :]
                              for c in range(N_LANE_BLOCKS)], axis=1)
    pool_ref[:, pl.ds(b, 1), :] = latest[:, None, :]
    for c in range(N_LANE_BLOCKS):
        ext_ref[c, 0:TAIL, :] = ext_ref[c, tile_m:tile_m + TAIL, :]
    return jnp.concatenate(blocks, axis=1)


def _layer_kernel(x_ref, x_lag_ref, p_lag_ref, xs_ref, ps_ref, taps_ref, state_hbm,
                  w_in_ref, w_s_ref, w_pool_ref, w_pa_ref, w_pb_ref, w_out_ref, w_ple_ref,
                  w_pg_ref, bs_t_ref, vecs_ref,
                  y_ref, vn_ref, pool_ref, ys_ref, vns_ref, spool_hbm,
                  ext_ref, lvl_ref, xn_c, ya_c, yb_c, hist_ref, xbs_ref, sems,
                  *, tile_m, tiles_per_prompt, n_tiles, pos0):
    s_id = pl.program_id(0)
    front = jnp.minimum(s_id, n_tiles - 1)
    b = front // tiles_per_prompt
    j = lax.rem(front, tiles_per_prompt)
    w = _Weights(in_proj_w=w_in_ref, spatial=w_s_ref, pool=w_pool_ref, pa=w_pa_ref, pb=w_pb_ref,
                 out=w_out_ref, ple=w_ple_ref, pg=w_pg_ref, bias_t=bs_t_ref, vecs=vecs_ref)

    @pl.when(j == 0)
    def _():
        ext_ref[:, 0:TAIL, :] = jnp.zeros((N_LANE_BLOCKS, TAIL, LANES), _F32)

    @pl.when(s_id == 0)
    def _front_only():
        xn = _rmsnorm(x_ref[0], w.vec(VEC_PRE_G)).astype(_BF16)
        xn_c[...] = xn
        vn = _layernorm(_gelu_tanh(w.in_proj(xn, COL_V)), w.vec(VEC_LN_G), w.vec(VEC_LN_B))
        vn_ref[0] = vn[tile_m - CHUNK:, :]
        s = _spatial_mix(vn.astype(_BF16), w, tile_m)
        ya_c[...] = (_gelu_tanh(w.in_proj(xn, COL_U)) * s
                     * _silu(w.in_proj(xn, COL_ZA))).astype(_BF16)
        pooled = _window_pool(w.in_proj(xn, COL_XB), ext_ref, lvl_ref, pool_ref, b, j, tile_m)
        yb_c[...] = (_pool_mix(pooled, w) * _silu(w.in_proj(xn, COL_ZB))).astype(_BF16)

    @pl.when(s_id == n_tiles)
    def _back_only_and_sample():
        history, shift, newest = _sample_copies(state_hbm, spool_hbm, hist_ref, xbs_ref, sems)
        history.start()
        y_ref[0] = _merge_and_embed(x_lag_ref[0], p_lag_ref[0], xn_c[...], ya_c[...], yb_c[...], w)
        history.wait()
        shift.start()
        _sample_rows(xs_ref, ps_ref, taps_ref, hist_ref, newest, xbs_ref, w, ys_ref, vns_ref, pos0)
        shift.wait()
        newest.wait()

    @pl.when(jnp.logical_and(s_id > 0, s_id < n_tiles))
    def _both():
        xn_prev = xn_c[...]
        g_a = w.in_proj(xn_prev, COL_GA)
        xn = _rmsnorm(x_ref[0], w.vec(VEC_PRE_G)).astype(_BF16)
        xn_c[...] = xn
        v_raw = w.in_proj(xn, COL_V)
        g_b = w.in_proj(xn_prev, COL_GB)
        e = _dot(p_lag_ref[0].astype(_BF16), _unpack(w.ple))
        vn = _layernorm(_gelu_tanh(v_raw), w.vec(VEC_LN_G), w.vec(VEC_LN_B))
        vn_ref[0] = vn[tile_m - CHUNK:, :]
        u_raw = w.in_proj(xn, COL_U)
        s = _spatial_mix(vn.astype(_BF16), w, tile_m)
        proj_a = _dot(ya_c[...], _unpack(w.pa))
        za_raw = w.in_proj(xn, COL_ZA)
        proj_b = _dot(yb_c[...], _unpack(w.pb))
        ya_c[...] = (_gelu_tanh(u_raw) * s * _silu(za_raw)).astype(_BF16)
        x_b = w.in_proj(xn, COL_XB)
        m = (_sigmoid(g_a) * proj_a + _sigmoid(g_b) * proj_b).astype(_BF16)
        o = _dot(m, _unpack(w.out))
        zb_raw = w.in_proj(xn, COL_ZB)
        pooled = _window_pool(x_b, ext_ref, lvl_ref, pool_ref, b, j, tile_m)
        mixed = _pool_mix(pooled, w)
        h = x_lag_ref[0] + _rmsnorm(o, w.vec(VEC_POST_G))
        hn = _rmsnorm(h, w.vec(VEC_PLE_IN_G)).astype(_BF16)
        gate_raw = _dot(hn, _unpack(w.pg))
        yb_c[...] = (mixed * _silu(zb_raw)).astype(_BF16)
        y_ref[0] = h + _rmsnorm(_sigmoid(gate_raw) * e, w.vec(VEC_PLE_OUT_G))


def _gather_rows(ref):
    n = ref.shape[0] // SAMPLE_ROWS
    return jnp.concatenate([ref[pl.ds(c, SAMPLE_ROWS, stride=n), :] for c in range(n)], axis=1)


def _scatter_rows(ref, value):
    n = ref.shape[0] // SAMPLE_ROWS
    for c in range(n):
        ref[pl.ds(c, SAMPLE_ROWS, stride=n), :] = value[:, c * LANES:(c + 1) * LANES]


def _sample_copies(state_hbm, spool_hbm, hist_ref, xbs_ref, sems):
    history = pltpu.make_async_copy(state_hbm, hist_ref, sems.at[0])
    shift = pltpu.make_async_copy(hist_ref.at[pl.ds(1, POOL_BUF - 1)],
                                  spool_hbm.at[pl.ds(0, POOL_BUF - 1)], sems.at[1])
    newest = pltpu.make_async_copy(xbs_ref, spool_hbm.at[POOL_BUF - 1], sems.at[2])
    return history, shift, newest


def _sample_rows(x_ref, p_ref, taps_ref, hist_ref, newest, xbs_ref, w, y_ref, vn_ref, pos0):
    x = _gather_rows(x_ref)
    xn = _rmsnorm(x, w.vec(VEC_PRE_G)).astype(_BF16)

    u = _gelu_tanh(w.in_proj(xn, COL_U))
    vn = _layernorm(_gelu_tanh(w.in_proj(xn, COL_V)), w.vec(VEC_LN_G), w.vec(VEC_LN_B))
    _scatter_rows(vn_ref, vn)
    heads = []
    for hd in range(N_HEADS):
        tap = taps_ref[hd * CHUNK:hd * CHUNK + 1, 0:1]
        heads.append(vn[:, hd * CHUNK:(hd + 1) * CHUNK] * tap + w.bias_t[0:1, hd:hd + 1])
    s = jnp.concatenate(heads, axis=1)
    y_a = u * s * _silu(w.in_proj(xn, COL_ZA))

    x_b = w.in_proj(xn, COL_XB)
    xbs_ref[...] = x_b
    newest.start()
    groups = []
    for g, win_len in enumerate(POOL_WINDOWS):
        lo, hi = g * POOL_GROUP, (g + 1) * POOL_GROUP
        cur = x_b[:, lo:hi]
        win = cur
        for k in range(1, win_len):
            win = win + hist_ref[POOL_BUF - k, :, lo:hi]
        groups.append(win / float(min(pos0 + 1, win_len)) - cur)
    pooled = jnp.concatenate(groups, axis=1)
    y_b = _pool_mix(pooled, w) * _silu(w.in_proj(xn, COL_ZB))

    _scatter_rows(y_ref, _merge_and_embed(x, _gather_rows(p_ref), xn, y_a, y_b, w))


def _pack_kernel(*refs, tril_periods):
    n = len(tril_periods)
    n_vecs = len(VEC_ROWS)
    bias_ref, vec_refs = refs[n], refs[n + 1:n + 1 + n_vecs]
    outs = refs[n + 1 + n_vecs:]
    bias_t_ref, vecs_ref = outs[n], outs[n + 1]
    bias_t_ref[...] = bias_ref[...].T
    for row, vec_ref in zip(VEC_ROWS, vec_refs):
        vecs_ref[row:row + 1, :] = vec_ref[...]
    vecs_ref[n_vecs:, :] = jnp.zeros((N_VEC_ROWS - n_vecs, D_MODEL), _F32)
    for src, dst, period in zip(refs[:n], outs[:n], tril_periods):
        m = src[...]
        if period is not None:
            row = lax.broadcasted_iota(jnp.int32, m.shape, 0) % period
            m = jnp.where(row >= lax.broadcasted_iota(jnp.int32, m.shape, 1), m, 0.0)
        dst[...] = pltpu.bitcast(m.astype(_BF16), jnp.uint32)


def _pack_params(mats, tril_periods, bias, vecs, n_steps):
    blocks = [m.shape[0] // n_steps for m in mats]
    small = lambda shape: pl.BlockSpec(shape, lambda i: (0, 0))
    for m, rows, period in zip(mats, blocks, tril_periods):
        assert rows * n_steps == m.shape[0] and rows % (2 * BF16_ROWS) == 0
        assert period is None or rows % period == 0
    return pl.pallas_call(
        functools.partial(_pack_kernel, tril_periods=tril_periods),
        grid=(n_steps,),
        in_specs=[pl.BlockSpec((rows, m.shape[1]), lambda i: (i, 0))
                  for m, rows in zip(mats, blocks)]
                 + [small(bias.shape)] + [small(v.shape) for v in vecs],
        out_specs=[pl.BlockSpec((rows // 2, m.shape[1]), lambda i: (i, 0))
                   for m, rows in zip(mats, blocks)]
                  + [small(bias.shape[::-1]), small((N_VEC_ROWS, D_MODEL))],
        out_shape=[jax.ShapeDtypeStruct((m.shape[0] // 2, m.shape[1]), jnp.uint32) for m in mats]
                  + [jax.ShapeDtypeStruct(bias.shape[::-1], _F32),
                     jax.ShapeDtypeStruct((N_VEC_ROWS, D_MODEL), _F32)],
        compiler_params=pltpu.CompilerParams(dimension_semantics=("arbitrary",),
                                             vmem_limit_bytes=VMEM_LIMIT_BYTES),
        name="pack_params",
    )(*mats, bias, *vecs)


def _resident(shape):
    zeros = (0,) * len(shape)
    return pl.BlockSpec(shape, lambda *_: zeros, pipeline_mode=pl.Buffered(1))


def _layer(h, p, hs, ps, state, taps, weights, tile_m, pos0):
    batch, seq, _ = h.shape
    rows = state.shape[1]
    assert seq % tile_m == 0 and tile_m % CHUNK == 0 and seq >= POOL_BUF and rows == SAMPLE_ROWS
    tiles_per_prompt = seq // tile_m
    n_tiles = batch * tiles_per_prompt
    assert n_tiles >= 2

    def front_tile(s):
        t = jnp.minimum(s, n_tiles - 1)
        return t // tiles_per_prompt, t % tiles_per_prompt, 0

    def back_tile(s):
        t = jnp.maximum(s - 1, 0)
        return t // tiles_per_prompt, t % tiles_per_prompt, 0

    per_prompt = lambda s: (jnp.minimum(s, n_tiles - 1) // tiles_per_prompt, 0, 0)
    hbm = pl.BlockSpec(memory_space=pl.ANY)
    whole = lambda a: pl.BlockSpec(a.shape, lambda s: (0,) * a.ndim)
    return pl.pallas_call(
        functools.partial(_layer_kernel, tile_m=tile_m, tiles_per_prompt=tiles_per_prompt,
                          n_tiles=n_tiles, pos0=pos0),
        grid=(n_tiles + 1,),
        in_specs=[pl.BlockSpec((1, tile_m, D_MODEL), front_tile),
                  pl.BlockSpec((1, tile_m, D_MODEL), back_tile),
                  pl.BlockSpec((1, tile_m, PLE_DIM), back_tile),
                  _resident(hs.shape), _resident(ps.shape), _resident(taps.shape), hbm]
                 + [_resident(a.shape) for a in weights],
        out_specs=[pl.BlockSpec((1, tile_m, D_MODEL), back_tile),
                   pl.BlockSpec((1, CHUNK, D_MODEL), per_prompt),
                   pl.BlockSpec((POOL_BUF, batch, D_MODEL), lambda s: (0, 0, 0)),
                   whole(hs), whole(hs), hbm],
        out_shape=[jax.ShapeDtypeStruct((batch, seq, D_MODEL), _F32),
                   jax.ShapeDtypeStruct((batch, CHUNK, D_MODEL), _F32),
                   jax.ShapeDtypeStruct((POOL_BUF, batch, D_MODEL), _F32),
                   jax.ShapeDtypeStruct(hs.shape, _F32),
                   jax.ShapeDtypeStruct(hs.shape, _F32),
                   jax.ShapeDtypeStruct(state.shape, _F32)],
        scratch_shapes=[pltpu.VMEM((N_LANE_BLOCKS, TAIL + tile_m, LANES), _F32),
                        pltpu.VMEM((N_LANE_BLOCKS, 2, TAIL + tile_m, LANES), _F32),
                        pltpu.VMEM((tile_m, D_MODEL), _BF16),
                        pltpu.VMEM((tile_m, D_MODEL), _BF16),
                        pltpu.VMEM((tile_m, D_MODEL), _BF16),
                        pltpu.VMEM(state.shape, _F32),
                        pltpu.VMEM((rows, D_MODEL), _F32),
                        pltpu.SemaphoreType.DMA((3,))],
        compiler_params=pltpu.CompilerParams(
            dimension_semantics=("arbitrary",),
            vmem_limit_bytes=VMEM_LIMIT_BYTES),
        name="layer",
    )(h, h, p, hs, ps, taps, state, *weights)


def kernel(x_prompt, x_sample, state_pool, p_prompt, p_sample, pre_g, w_in, ln_g, ln_b, w_s, b_s,
           w_pool, pool_scale, w_pa, w_pb, w_out, post_g, w_ple, w_pg, ple_in_g, ple_out_g):
    depth = w_in.shape[0]
    dec_batch, dec_seq, _ = x_sample.shape
    assert dec_seq == 1 and x_prompt.shape[2] == D_MODEL
    hp = x_prompt
    hs = x_sample.reshape(dec_batch * N_LANE_BLOCKS, LANES)
    pv, pp, sv, sp = [], [], [], []
    for i in range(depth):
        w_s_rows = w_s[i].reshape(D_MODEL, CHUNK)
        params = _pack_params(
            [w_in[i], w_s_rows, w_pool[i].reshape(D_MODEL, POOL_GROUP), w_pa[i], w_pb[i],
             w_out[i], w_ple[i], w_pg[i]],
            (None, CHUNK) + (None,) * 6, b_s[i],
            [v[i:i + 1] for v in (pre_g, ln_g, ln_b, pool_scale, post_g, ple_in_g, ple_out_g)],
            n_steps=PACK_STEPS)
        hp, v, pool, hs, v_s, pool_s = _layer(
            hp, p_prompt[i], hs, p_sample[i].reshape(dec_batch * PLE_DIM // LANES, LANES),
            jnp.swapaxes(state_pool[i], 0, 1), w_s_rows, params, TILE_M, PAST_LEN)
        pv.append(v)
        pp.append(jnp.swapaxes(pool, 0, 1))
        sv.append(v_s.reshape(dec_batch, 1, D_MODEL))
        sp.append(jnp.swapaxes(pool_s, 0, 1))
    return (hp, hs.reshape(dec_batch, 1, D_MODEL), jnp.stack(pv), jnp.stack(pp), jnp.stack(sv),
            jnp.stack(sp))
```

```python
import functools
from typing import Any, NamedTuple

import jax
import jax.numpy as jnp
from jax import lax
from jax.experimental import pallas as pl
from jax.experimental.pallas import tpu as pltpu

D_MODEL = 1024
CHUNK = 128
N_HEADS = D_MODEL // CHUNK
POOL_WINDOWS = (2, 4, 8, 16)
POOL_GROUP = D_MODEL // len(POOL_WINDOWS)
POOL_BUF = max(POOL_WINDOWS) - 1
PLE_DIM = 256
PAST_LEN = 16384
SAMPLE_ROWS = 128
EPS = 1e-6
COL_U, COL_V, COL_ZA, COL_XB, COL_ZB, COL_GA, COL_GB = range(7)
VEC_ROWS = tuple(range(7))
VEC_PRE_G, VEC_LN_G, VEC_LN_B, VEC_POOL_SCALE, VEC_POST_G, VEC_PLE_IN_G, VEC_PLE_OUT_G = VEC_ROWS
N_VEC_ROWS = 8

LANES = 128
N_LANE_BLOCKS = D_MODEL // LANES
BF16_ROWS = 16
SUBLANES = 8
TAIL = 32
DOUBLING_MIN_WINDOW = 8
PACK_STEPS = 8
TILE_M = 256
VMEM_LIMIT_BYTES = 56 * 1024 * 1024

_BF16 = jnp.bfloat16
_F32 = jnp.float32
_GELU_C = 0.7978845608028654


def _dot(a, b):
    return jnp.dot(a, b, preferred_element_type=_F32)


def _rmsnorm(x, g):
    return x * lax.rsqrt(jnp.mean(x * x, axis=-1, keepdims=True) + EPS) * g


def _layernorm(x, g, b):
    xc = x - jnp.mean(x, axis=-1, keepdims=True)
    return xc * lax.rsqrt(jnp.mean(xc * xc, axis=-1, keepdims=True) + EPS) * g + b


def _gelu_tanh(x):
    return 0.5 * x * (1.0 + jnp.tanh(_GELU_C * (x + 0.044715 * (x * x * x))))


def _sigmoid(x):
    return 0.5 * jnp.tanh(0.5 * x) + 0.5


def _silu(x):
    return x * _sigmoid(x)


def _unpack(ref, rows=slice(None), cols=slice(None)):
    return pltpu.bitcast(ref[rows, cols], _BF16)


class _Weights(NamedTuple):
    in_proj_w: Any
    spatial: Any
    pool: Any
    pa: Any
    pb: Any
    out: Any
    ple: Any
    pg: Any
    bias_t: Any
    vecs: Any

    def vec(self, row):
        return self.vecs[row:row + 1, :]

    def in_proj(self, xn, col):
        return _dot(xn, _unpack(self.in_proj_w, cols=slice(col * D_MODEL, (col + 1) * D_MODEL)))


def _pool_mix(pooled, w):
    parts = []
    for g in range(len(POOL_WINDOWS)):
        q = pooled[:, g * POOL_GROUP:(g + 1) * POOL_GROUP].astype(_BF16)
        rows = slice(g * POOL_GROUP // 2, (g + 1) * POOL_GROUP // 2)
        parts.append(_dot(q, _unpack(w.pool, rows=rows)))
    return jnp.concatenate(parts, axis=1) * w.vec(VEC_POOL_SCALE)


def _merge_and_embed(h, p, xn, y_a, y_b, w):
    g_a = w.in_proj(xn, COL_GA)
    g_b = w.in_proj(xn, COL_GB)
    m = (_sigmoid(g_a) * _dot(y_a.astype(_BF16), _unpack(w.pa))
         + _sigmoid(g_b) * _dot(y_b.astype(_BF16), _unpack(w.pb)))
    h = h + _rmsnorm(_dot(m.astype(_BF16), _unpack(w.out)), w.vec(VEC_POST_G))
    e = _dot(p.astype(_BF16), _unpack(w.ple))
    hn = _rmsnorm(h, w.vec(VEC_PLE_IN_G)).astype(_BF16)
    gate = _sigmoid(_dot(hn, _unpack(w.pg)))
    return h + _rmsnorm(gate * e, w.vec(VEC_PLE_OUT_G))


def _spatial_mix(vn_b, w, tile_m):
    w_tril = [_unpack(w.spatial, rows=slice(hd * CHUNK // 2, (hd + 1) * CHUNK // 2))
              for hd in range(N_HEADS)]
    bias = [w.bias_t[:, hd:hd + 1] for hd in range(N_HEADS)]
    s_rows = []
    for c in range(tile_m // CHUNK):
        heads = []
        for hd in range(N_HEADS):
            blk = vn_b[c * CHUNK:(c + 1) * CHUNK, hd * CHUNK:(hd + 1) * CHUNK]
            heads.append(_dot(w_tril[hd], blk) + bias[hd])
        s_rows.append(jnp.concatenate(heads, axis=1))
    return jnp.concatenate(s_rows, axis=0)


def _window_sum(src, lvl, w, rows):
    n_levels = w.bit_length() - 1
    assert w == 1 << n_levels and SUBLANES * (n_levels - 1) + 1 <= TAIL
    for l in range(1, n_levels):
        start = TAIL - SUBLANES * (n_levels - l)
        half = 1 << (l - 1)
        dst = lvl.at[(l - 1) % 2]
        dst[start:rows, :] = src[start:rows, :] + src[start - half:rows - half, :]
        src = dst
    half = w // 2
    return src[TAIL:rows, :] + src[TAIL - half:rows - half, :]


def _window_pool(x_b, ext_ref, lvl_ref, pool_ref, b, j, tile_m):
    rows = TAIL + tile_m
    for c in range(N_LANE_BLOCKS):
        ext_ref[c, TAIL:rows, :] = x_b[:, c * LANES:(c + 1) * LANES]
    pos = j * tile_m + lax.broadcasted_iota(jnp.int32, (tile_m, 1), 0)
    blocks = []
    for c in range(N_LANE_BLOCKS):
        w = POOL_WINDOWS[c * LANES // POOL_GROUP]
        cur = x_b[:, c * LANES:(c + 1) * LANES]
        if w >= DOUBLING_MIN_WINDOW:
            win = _window_sum(ext_ref.at[c], lvl_ref.at[c], w, rows)
        else:
            win = cur
            for k in range(1, w):
                win = win + ext_ref[c, TAIL - k:TAIL - k + tile_m, :]
        cnt = jnp.minimum(pos + 1, w).astype(_F32)
        blocks.append(win / cnt - cur)
    latest = jnp.concatenate([ext_ref[c, TAIL + tile_m - POOL_BUF:TAIL + tile_m, :]
                              for c in range(N_LANE_BLOCKS)], axis=1)
    pool_ref[:, pl.ds(b, 1), :] = latest[:, None, :]
    for c in range(N_LANE_BLOCKS):
        ext_ref[c, 0:TAIL, :] = ext_ref[c, tile_m:tile_m + TAIL, :]
    return jnp.concatenate(blocks, axis=1)


def _layer_kernel(x_ref, x_lag_ref, p_lag_ref, xs_ref, ps_ref, taps_ref, state_hbm,
                  w_in_ref, w_s_ref, w_pool_ref, w_pa_ref, w_pb_ref, w_out_ref, w_ple_ref,
                  w_pg_ref, bs_t_ref, vecs_ref,
                  y_ref, vn_ref, pool_ref, ys_ref, vns_ref, spool_hbm,
                  ext_ref, lvl_ref, xn_c, ya_c, yb_c, hist_ref, xbs_ref, sems,
                  *, tile_m, tiles_per_prompt, n_tiles, pos0):
    s_id = pl.program_id(0)
    front = jnp.minimum(s_id, n_tiles - 1)
    b = front // tiles_per_prompt
    j = lax.rem(front, tiles_per_prompt)
    w = _Weights(in_proj_w=w_in_ref, spatial=w_s_ref, pool=w_pool_ref, pa=w_pa_ref, pb=w_pb_ref,
                 out=w_out_ref, ple=w_ple_ref, pg=w_pg_ref, bias_t=bs_t_ref, vecs=vecs_ref)

    @pl.when(j == 0)
    def _():
        ext_ref[:, 0:TAIL, :] = jnp.zeros((N_LANE_BLOCKS, TAIL, LANES), _F32)

    @pl.when(s_id == 0)
    def _front_only():
        xn = _rmsnorm(x_ref[0], w.vec(VEC_PRE_G)).astype(_BF16)
        xn_c[...] = xn
        vn = _layernorm(_gelu_tanh(w.in_proj(xn, COL_V)), w.vec(VEC_LN_G), w.vec(VEC_LN_B))
        vn_ref[0] = vn[tile_m - CHUNK:, :]
        s = _spatial_mix(vn.astype(_BF16), w, tile_m)
        ya_c[...] = (_gelu_tanh(w.in_proj(xn, COL_U)) * s
                     * _silu(w.in_proj(xn, COL_ZA))).astype(_BF16)
        pooled = _window_pool(w.in_proj(xn, COL_XB), ext_ref, lvl_ref, pool_ref, b, j, tile_m)
        yb_c[...] = (_pool_mix(pooled, w) * _silu(w.in_proj(xn, COL_ZB))).astype(_BF16)

    @pl.when(s_id == n_tiles)
    def _back_only_and_sample():
        history, shift, newest = _sample_copies(state_hbm, spool_hbm, hist_ref, xbs_ref, sems)
        history.start()
        y_ref[0] = _merge_and_embed(x_lag_ref[0], p_lag_ref[0], xn_c[...], ya_c[...], yb_c[...], w)
        history.wait()
        shift.start()
        _sample_rows(xs_ref, ps_ref, taps_ref, hist_ref, newest, xbs_ref, w, ys_ref, vns_ref, pos0)
        shift.wait()
        newest.wait()

    @pl.when(jnp.logical_and(s_id > 0, s_id < n_tiles))
    def _both():
        xn_prev = xn_c[...]
        g_a = w.in_proj(xn_prev, COL_GA)
        xn = _rmsnorm(x_ref[0], w.vec(VEC_PRE_G)).astype(_BF16)
        xn_c[...] = xn
        v_raw = w.in_proj(xn, COL_V)
        e = _dot(p_lag_ref[0].astype(_BF16), _unpack(w.ple))
        vn = _layernorm(_gelu_tanh(v_raw), w.vec(VEC_LN_G), w.vec(VEC_LN_B))
        vn_ref[0] = vn[tile_m - CHUNK:, :]
        u_raw = w.in_proj(xn, COL_U)
        s = _spatial_mix(vn.astype(_BF16), w, tile_m)
        g_b = w.in_proj(xn_prev, COL_GB)
        proj_a = _dot(ya_c[...], _unpack(w.pa))
        proj_b = _dot(yb_c[...], _unpack(w.pb))
        x_b = w.in_proj(xn, COL_XB)
        m = (_sigmoid(g_a) * proj_a + _sigmoid(g_b) * proj_b).astype(_BF16)
        o = _dot(m, _unpack(w.out))
        zb_raw = w.in_proj(xn, COL_ZB)
        za_raw = w.in_proj(xn, COL_ZA)
        pooled = _window_pool(x_b, ext_ref, lvl_ref, pool_ref, b, j, tile_m)
        ya_c[...] = (_gelu_tanh(u_raw) * s * _silu(za_raw)).astype(_BF16)
        mixed = _pool_mix(pooled, w)
        h = x_lag_ref[0] + _rmsnorm(o, w.vec(VEC_POST_G))
        hn = _rmsnorm(h, w.vec(VEC_PLE_IN_G)).astype(_BF16)
        gate_raw = _dot(hn, _unpack(w.pg))
        yb_c[...] = (mixed * _silu(zb_raw)).astype(_BF16)
        y_ref[0] = h + _rmsnorm(_sigmoid(gate_raw) * e, w.vec(VEC_PLE_OUT_G))


def _gather_rows(ref):
    n = ref.shape[0] // SAMPLE_ROWS
    return jnp.concatenate([ref[pl.ds(c, SAMPLE_ROWS, stride=n), :] for c in range(n)], axis=1)


def _scatter_rows(ref, value):
    n = ref.shape[0] // SAMPLE_ROWS
    for c in range(n):
        ref[pl.ds(c, SAMPLE_ROWS, stride=n), :] = value[:, c * LANES:(c + 1) * LANES]


def _sample_copies(state_hbm, spool_hbm, hist_ref, xbs_ref, sems):
    history = pltpu.make_async_copy(state_hbm, hist_ref, sems.at[0])
    shift = pltpu.make_async_copy(hist_ref.at[pl.ds(1, POOL_BUF - 1)],
                                  spool_hbm.at[pl.ds(0, POOL_BUF - 1)], sems.at[1])
    newest = pltpu.make_async_copy(xbs_ref, spool_hbm.at[POOL_BUF - 1], sems.at[2])
    return history, shift, newest


def _sample_rows(x_ref, p_ref, taps_ref, hist_ref, newest, xbs_ref, w, y_ref, vn_ref, pos0):
    x = _gather_rows(x_ref)
    xn = _rmsnorm(x, w.vec(VEC_PRE_G)).astype(_BF16)

    u = _gelu_tanh(w.in_proj(xn, COL_U))
    vn = _layernorm(_gelu_tanh(w.in_proj(xn, COL_V)), w.vec(VEC_LN_G), w.vec(VEC_LN_B))
    _scatter_rows(vn_ref, vn)
    heads = []
    for hd in range(N_HEADS):
        tap = taps_ref[hd * CHUNK:hd * CHUNK + 1, 0:1]
        heads.append(vn[:, hd * CHUNK:(hd + 1) * CHUNK] * tap + w.bias_t[0:1, hd:hd + 1])
    s = jnp.concatenate(heads, axis=1)
    y_a = u * s * _silu(w.in_proj(xn, COL_ZA))

    x_b = w.in_proj(xn, COL_XB)
    xbs_ref[...] = x_b
    newest.start()
    groups = []
    for g, win_len in enumerate(POOL_WINDOWS):
        lo, hi = g * POOL_GROUP, (g + 1) * POOL_GROUP
        cur = x_b[:, lo:hi]
        win = cur
        for k in range(1, win_len):
            win = win + hist_ref[POOL_BUF - k, :, lo:hi]
        groups.append(win / float(min(pos0 + 1, win_len)) - cur)
    pooled = jnp.concatenate(groups, axis=1)
    y_b = _pool_mix(pooled, w) * _silu(w.in_proj(xn, COL_ZB))

    _scatter_rows(y_ref, _merge_and_embed(x, _gather_rows(p_ref), xn, y_a, y_b, w))


def _pack_kernel(*refs, tril_periods):
    n = len(tril_periods)
    n_vecs = len(VEC_ROWS)
    bias_ref, vec_refs = refs[n], refs[n + 1:n + 1 + n_vecs]
    outs = refs[n + 1 + n_vecs:]
    bias_t_ref, vecs_ref = outs[n], outs[n + 1]
    bias_t_ref[...] = bias_ref[...].T
    for row, vec_ref in zip(VEC_ROWS, vec_refs):
        vecs_ref[row:row + 1, :] = vec_ref[...]
    vecs_ref[n_vecs:, :] = jnp.zeros((N_VEC_ROWS - n_vecs, D_MODEL), _F32)
    for src, dst, period in zip(refs[:n], outs[:n], tril_periods):
        m = src[...]
        if period is not None:
            row = lax.broadcasted_iota(jnp.int32, m.shape, 0) % period
            m = jnp.where(row >= lax.broadcasted_iota(jnp.int32, m.shape, 1), m, 0.0)
        dst[...] = pltpu.bitcast(m.astype(_BF16), jnp.uint32)


def _pack_params(mats, tril_periods, bias, vecs, n_steps):
    blocks = [m.shape[0] // n_steps for m in mats]
    small = lambda shape: pl.BlockSpec(shape, lambda i: (0, 0))
    for m, rows, period in zip(mats, blocks, tril_periods):
        assert rows * n_steps == m.shape[0] and rows % (2 * BF16_ROWS) == 0
        assert period is None or rows % period == 0
    return pl.pallas_call(
        functools.partial(_pack_kernel, tril_periods=tril_periods),
        grid=(n_steps,),
        in_specs=[pl.BlockSpec((rows, m.shape[1]), lambda i: (i, 0))
                  for m, rows in zip(mats, blocks)]
                 + [small(bias.shape)] + [small(v.shape) for v in vecs],
        out_specs=[pl.BlockSpec((rows // 2, m.shape[1]), lambda i: (i, 0))
                   for m, rows in zip(mats, blocks)]
                  + [small(bias.shape[::-1]), small((N_VEC_ROWS, D_MODEL))],
        out_shape=[jax.ShapeDtypeStruct((m.shape[0] // 2, m.shape[1]), jnp.uint32) for m in mats]
                  + [jax.ShapeDtypeStruct(bias.shape[::-1], _F32),
                     jax.ShapeDtypeStruct((N_VEC_ROWS, D_MODEL), _F32)],
        compiler_params=pltpu.CompilerParams(dimension_semantics=("arbitrary",),
                                             vmem_limit_bytes=VMEM_LIMIT_BYTES),
        name="pack_params",
    )(*mats, bias, *vecs)


def _resident(shape):
    zeros = (0,) * len(shape)
    return pl.BlockSpec(shape, lambda *_: zeros, pipeline_mode=pl.Buffered(1))


def _layer(h, p, hs, ps, state, taps, weights, tile_m, pos0):
    batch, seq, _ = h.shape
    rows = state.shape[1]
    assert seq % tile_m == 0 and tile_m % CHUNK == 0 and seq >= POOL_BUF and rows == SAMPLE_ROWS
    tiles_per_prompt = seq // tile_m
    n_tiles = batch * tiles_per_prompt
    assert n_tiles >= 2

    def front_tile(s):
        t = jnp.minimum(s, n_tiles - 1)
        return t // tiles_per_prompt, t % tiles_per_prompt, 0

    def back_tile(s):
        t = jnp.maximum(s - 1, 0)
        return t // tiles_per_prompt, t % tiles_per_prompt, 0

    per_prompt = lambda s: (jnp.minimum(s, n_tiles - 1) // tiles_per_prompt, 0, 0)
    hbm = pl.BlockSpec(memory_space=pl.ANY)
    whole = lambda a: pl.BlockSpec(a.shape, lambda s: (0,) * a.ndim)
    return pl.pallas_call(
        functools.partial(_layer_kernel, tile_m=tile_m, tiles_per_prompt=tiles_per_prompt,
                          n_tiles=n_tiles, pos0=pos0),
        grid=(n_tiles + 1,),
        in_specs=[pl.BlockSpec((1, tile_m, D_MODEL), front_tile),
                  pl.BlockSpec((1, tile_m, D_MODEL), back_tile),
                  pl.BlockSpec((1, tile_m, PLE_DIM), back_tile),
                  _resident(hs.shape), _resident(ps.shape), _resident(taps.shape), hbm]
                 + [_resident(a.shape) for a in weights],
        out_specs=[pl.BlockSpec((1, tile_m, D_MODEL), back_tile),
                   pl.BlockSpec((1, CHUNK, D_MODEL), per_prompt),
                   pl.BlockSpec((POOL_BUF, batch, D_MODEL), lambda s: (0, 0, 0)),
                   whole(hs), whole(hs), hbm],
        out_shape=[jax.ShapeDtypeStruct((batch, seq, D_MODEL), _F32),
                   jax.ShapeDtypeStruct((batch, CHUNK, D_MODEL), _F32),
                   jax.ShapeDtypeStruct((POOL_BUF, batch, D_MODEL), _F32),
                   jax.ShapeDtypeStruct(hs.shape, _F32),
                   jax.ShapeDtypeStruct(hs.shape, _F32),
                   jax.ShapeDtypeStruct(state.shape, _F32)],
        scratch_shapes=[pltpu.VMEM((N_LANE_BLOCKS, TAIL + tile_m, LANES), _F32),
                        pltpu.VMEM((N_LANE_BLOCKS, 2, TAIL + tile_m, LANES), _F32),
                        pltpu.VMEM((tile_m, D_MODEL), _BF16),
                        pltpu.VMEM((tile_m, D_MODEL), _BF16),
                        pltpu.VMEM((tile_m, D_MODEL), _BF16),
                        pltpu.VMEM(state.shape, _F32),
                        pltpu.VMEM((rows, D_MODEL), _F32),
                        pltpu.SemaphoreType.DMA((3,))],
        compiler_params=pltpu.CompilerParams(
            dimension_semantics=("arbitrary",),
            vmem_limit_bytes=VMEM_LIMIT_BYTES),
        name="layer",
    )(h, h, p, hs, ps, taps, state, *weights)


def kernel(x_prompt, x_sample, state_pool, p_prompt, p_sample, pre_g, w_in, ln_g, ln_b, w_s, b_s,
           w_pool, pool_scale, w_pa, w_pb, w_out, post_g, w_ple, w_pg, ple_in_g, ple_out_g):
    depth = w_in.shape[0]
    dec_batch, dec_seq, _ = x_sample.shape
    assert dec_seq == 1 and x_prompt.shape[2] == D_MODEL
    hp = x_prompt
    hs = x_sample.reshape(dec_batch * N_LANE_BLOCKS, LANES)
    pv, pp, sv, sp = [], [], [], []
    for i in range(depth):
        w_s_rows = w_s[i].reshape(D_MODEL, CHUNK)
        params = _pack_params(
            [w_in[i], w_s_rows, w_pool[i].reshape(D_MODEL, POOL_GROUP), w_pa[i], w_pb[i],
             w_out[i], w_ple[i], w_pg[i]],
            (None, CHUNK) + (None,) * 6, b_s[i],
            [v[i:i + 1] for v in (pre_g, ln_g, ln_b, pool_scale, post_g, ple_in_g, ple_out_g)],
            n_steps=PACK_STEPS)
        hp, v, pool, hs, v_s, pool_s = _layer(
            hp, p_prompt[i], hs, p_sample[i].reshape(dec_batch * PLE_DIM // LANES, LANES),
            jnp.swapaxes(state_pool[i], 0, 1), w_s_rows, params, TILE_M, PAST_LEN)
        pv.append(v)
        pp.append(jnp.swapaxes(pool, 0, 1))
        sv.append(v_s.reshape(dec_batch, 1, D_MODEL))
        sp.append(jnp.swapaxes(pool_s, 0, 1))
    return (hp, hs.reshape(dec_batch, 1, D_MODEL), jnp.stack(pv), jnp.stack(pp), jnp.stack(sv),
            jnp.stack(sp))
```

```python
import functools
from typing import Any, NamedTuple

import jax
import jax.numpy as jnp
from jax import lax
from jax.experimental import pallas as pl
from jax.experimental.pallas import tpu as pltpu

D_MODEL = 1024
CHUNK = 128
N_HEADS = D_MODEL // CHUNK
POOL_WINDOWS = (2, 4, 8, 16)
POOL_GROUP = D_MODEL // len(POOL_WINDOWS)
POOL_BUF = max(POOL_WINDOWS) - 1
PLE_DIM = 256
PAST_LEN = 16384
SAMPLE_ROWS = 128
EPS = 1e-6
COL_U, COL_V, COL_ZA, COL_XB, COL_ZB, COL_GA, COL_GB = range(7)
VEC_ROWS = tuple(range(7))
VEC_PRE_G, VEC_LN_G, VEC_LN_B, VEC_POOL_SCALE, VEC_POST_G, VEC_PLE_IN_G, VEC_PLE_OUT_G = VEC_ROWS
N_VEC_ROWS = 8

LANES = 128
N_LANE_BLOCKS = D_MODEL // LANES
BF16_ROWS = 16
SUBLANES = 8
TAIL = 32
DOUBLING_MIN_WINDOW = 8
PACK_STEPS = 8
TILE_M = 256
VMEM_LIMIT_BYTES = 56 * 1024 * 1024

_BF16 = jnp.bfloat16
_F32 = jnp.float32
_GELU_C = 0.7978845608028654


def _dot(a, b):
    return jnp.dot(a, b, preferred_element_type=_F32)


def _rmsnorm(x, g):
    return x * lax.rsqrt(jnp.mean(x * x, axis=-1, keepdims=True) + EPS) * g


def _layernorm(x, g, b):
    xc = x - jnp.mean(x, axis=-1, keepdims=True)
    return xc * lax.rsqrt(jnp.mean(xc * xc, axis=-1, keepdims=True) + EPS) * g + b


def _gelu_tanh(x):
    return 0.5 * x * (1.0 + jnp.tanh(_GELU_C * (x + 0.044715 * (x * x * x))))


def _sigmoid(x):
    return 0.5 * jnp.tanh(0.5 * x) + 0.5


def _silu(x):
    return x * _sigmoid(x)


def _unpack(ref, rows=slice(None), cols=slice(None)):
    return pltpu.bitcast(ref[rows, cols], _BF16)


class _Weights(NamedTuple):
    in_proj_w: Any
    spatial: Any
    pool: Any
    pa: Any
    pb: Any
    out: Any
    ple: Any
    pg: Any
    bias_t: Any
    vecs: Any

    def vec(self, row):
        return self.vecs[row:row + 1, :]

    def in_proj(self, xn, col):
        return _dot(xn, _unpack(self.in_proj_w, cols=slice(col * D_MODEL, (col + 1) * D_MODEL)))


def _pool_mix(pooled, w):
    parts = []
    for g in range(len(POOL_WINDOWS)):
        q = pooled[:, g * POOL_GROUP:(g + 1) * POOL_GROUP].astype(_BF16)
        rows = slice(g * POOL_GROUP // 2, (g + 1) * POOL_GROUP // 2)
        parts.append(_dot(q, _unpack(w.pool, rows=rows)))
    return jnp.concatenate(parts, axis=1) * w.vec(VEC_POOL_SCALE)


def _merge_and_embed(h, p, xn, y_a, y_b, w):
    g_a = w.in_proj(xn, COL_GA)
    g_b = w.in_proj(xn, COL_GB)
    m = (_sigmoid(g_a) * _dot(y_a.astype(_BF16), _unpack(w.pa))
         + _sigmoid(g_b) * _dot(y_b.astype(_BF16), _unpack(w.pb)))
    h = h + _rmsnorm(_dot(m.astype(_BF16), _unpack(w.out)), w.vec(VEC_POST_G))
    e = _dot(p.astype(_BF16), _unpack(w.ple))
    hn = _rmsnorm(h, w.vec(VEC_PLE_IN_G)).astype(_BF16)
    gate = _sigmoid(_dot(hn, _unpack(w.pg)))
    return h + _rmsnorm(gate * e, w.vec(VEC_PLE_OUT_G))


def _spatial_mix(vn_b, w, tile_m):
    w_tril = [_unpack(w.spatial, rows=slice(hd * CHUNK // 2, (hd + 1) * CHUNK // 2))
              for hd in range(N_HEADS)]
    bias = [w.bias_t[:, hd:hd + 1] for hd in range(N_HEADS)]
    s_rows = []
    for c in range(tile_m // CHUNK):
        heads = []
        for hd in range(N_HEADS):
            blk = vn_b[c * CHUNK:(c + 1) * CHUNK, hd * CHUNK:(hd + 1) * CHUNK]
            heads.append(_dot(w_tril[hd], blk) + bias[hd])
        s_rows.append(jnp.concatenate(heads, axis=1))
    return jnp.concatenate(s_rows, axis=0)


def _window_sum(src, lvl, w, rows):
    n_levels = w.bit_length() - 1
    assert w == 1 << n_levels and SUBLANES * (n_levels - 1) + 1 <= TAIL
    for l in range(1, n_levels):
        start = TAIL - SUBLANES * (n_levels - l)
        half = 1 << (l - 1)
        dst = lvl.at[(l - 1) % 2]
        dst[start:rows, :] = src[start:rows, :] + src[start - half:rows - half, :]
        src = dst
    half = w // 2
    return src[TAIL:rows, :] + src[TAIL - half:rows - half, :]


def _window_pool(x_b, ext_ref, lvl_ref, pool_ref, b, j, tile_m):
    rows = TAIL + tile_m
    for c in range(N_LANE_BLOCKS):
        ext_ref[c, TAIL:rows, :] = x_b[:, c * LANES:(c + 1) * LANES]
    pos = j * tile_m + lax.broadcasted_iota(jnp.int32, (tile_m, 1), 0)
    blocks = []
    for c in range(N_LANE_BLOCKS):
        w = POOL_WINDOWS[c * LANES // POOL_GROUP]
        cur = x_b[:, c * LANES:(c + 1) * LANES]
        if w >= DOUBLING_MIN_WINDOW:
            win = _window_sum(ext_ref.at[c], lvl_ref.at[c], w, rows)
        else:
            win = cur
            for k in range(1, w):
                win = win + ext_ref[c, TAIL - k:TAIL - k + tile_m, :]
        cnt = jnp.minimum(pos + 1, w).astype(_F32)
        blocks.append(win / cnt - cur)
    latest = jnp.concatenate([ext_ref[c, TAIL + tile_m - POOL_BUF:TAIL + tile_m, :]
                              for c in range(N_LANE_BLOCKS)], axis=1)
    pool_ref[:, pl.ds(b, 1), :] = latest[:, None, :]
    for c in range(N_LANE_BLOCKS):
        ext_ref[c, 0:TAIL, :] = ext_ref[c, tile_m:tile_m + TAIL, :]
    return jnp.concatenate(blocks, axis=1)


def _layer_kernel(x_ref, x_lag_ref, p_lag_ref, xs_ref, ps_ref, taps_ref, state_hbm,
                  w_in_ref, w_s_ref, w_pool_ref, w_pa_ref, w_pb_ref, w_out_ref, w_ple_ref,
                  w_pg_ref, bs_t_ref, vecs_ref,
                  y_ref, vn_ref, pool_ref, ys_ref, vns_ref, spool_hbm,
                  ext_ref, lvl_ref, xn_c, ya_c, yb_c, hist_ref, xbs_ref, sems,
                  *, tile_m, tiles_per_prompt, n_tiles, pos0):
    s_id = pl.program_id(0)
    front = jnp.minimum(s_id, n_tiles - 1)
    b = front // tiles_per_prompt
    j = lax.rem(front, tiles_per_prompt)
    w = _Weights(in_proj_w=w_in_ref, spatial=w_s_ref, pool=w_pool_ref, pa=w_pa_ref, pb=w_pb_ref,
                 out=w_out_ref, ple=w_ple_ref, pg=w_pg_ref, bias_t=bs_t_ref, vecs=vecs_ref)

    @pl.when(j == 0)
    def _():
        ext_ref[:, 0:TAIL, :] = jnp.zeros((N_LANE_BLOCKS, TAIL, LANES), _F32)

    @pl.when(s_id == 0)
    def _front_only():
        xn = _rmsnorm(x_ref[0], w.vec(VEC_PRE_G)).astype(_BF16)
        xn_c[...] = xn
        vn = _layernorm(_gelu_tanh(w.in_proj(xn, COL_V)), w.vec(VEC_LN_G), w.vec(VEC_LN_B))
        vn_ref[0] = vn[tile_m - CHUNK:, :]
        s = _spatial_mix(vn.astype(_BF16), w, tile_m)
        ya_c[...] = (_gelu_tanh(w.in_proj(xn, COL_U)) * s
                     * _silu(w.in_proj(xn, COL_ZA))).astype(_BF16)
        pooled = _window_pool(w.in_proj(xn, COL_XB), ext_ref, lvl_ref, pool_ref, b, j, tile_m)
        yb_c[...] = (_pool_mix(pooled, w) * _silu(w.in_proj(xn, COL_ZB))).astype(_BF16)

    @pl.when(s_id == n_tiles)
    def _back_only_and_sample():
        history, shift, newest = _sample_copies(state_hbm, spool_hbm, hist_ref, xbs_ref, sems)
        history.start()
        y_ref[0] = _merge_and_embed(x_lag_ref[0], p_lag_ref[0], xn_c[...], ya_c[...], yb_c[...], w)
        history.wait()
        shift.start()
        _sample_rows(xs_ref, ps_ref, taps_ref, hist_ref, newest, xbs_ref, w, ys_ref, vns_ref, pos0)
        shift.wait()
        newest.wait()

    @pl.when(jnp.logical_and(s_id > 0, s_id < n_tiles))
    def _both():
        xn_prev = xn_c[...]
        g_a = w.in_proj(xn_prev, COL_GA)
        e = _dot(p_lag_ref[0].astype(_BF16), _unpack(w.ple))
        xn = _rmsnorm(x_ref[0], w.vec(VEC_PRE_G)).astype(_BF16)
        xn_c[...] = xn
        v_raw = w.in_proj(xn, COL_V)
        g_b = w.in_proj(xn_prev, COL_GB)
        vn = _layernorm(_gelu_tanh(v_raw), w.vec(VEC_LN_G), w.vec(VEC_LN_B))
        vn_ref[0] = vn[tile_m - CHUNK:, :]
        u_raw = w.in_proj(xn, COL_U)
        s = _spatial_mix(vn.astype(_BF16), w, tile_m)
        proj_a = _dot(ya_c[...], _unpack(w.pa))
        proj_b = _dot(yb_c[...], _unpack(w.pb))
        x_b = w.in_proj(xn, COL_XB)
        m = (_sigmoid(g_a) * proj_a + _sigmoid(g_b) * proj_b).astype(_BF16)
        o = _dot(m, _unpack(w.out))
        zb_raw = w.in_proj(xn, COL_ZB)
        za_raw = w.in_proj(xn, COL_ZA)
        pooled = _window_pool(x_b, ext_ref, lvl_ref, pool_ref, b, j, tile_m)
        ya_c[...] = (_gelu_tanh(u_raw) * s * _silu(za_raw)).astype(_BF16)
        mixed = _pool_mix(pooled, w)
        h = x_lag_ref[0] + _rmsnorm(o, w.vec(VEC_POST_G))
        hn = _rmsnorm(h, w.vec(VEC_PLE_IN_G)).astype(_BF16)
        gate_raw = _dot(hn, _unpack(w.pg))
        yb_c[...] = (mixed * _silu(zb_raw)).astype(_BF16)
        y_ref[0] = h + _rmsnorm(_sigmoid(gate_raw) * e, w.vec(VEC_PLE_OUT_G))


def _gather_rows(ref):
    n = ref.shape[0] // SAMPLE_ROWS
    return jnp.concatenate([ref[pl.ds(c, SAMPLE_ROWS, stride=n), :] for c in range(n)], axis=1)


def _scatter_rows(ref, value):
    n = ref.shape[0] // SAMPLE_ROWS
    for c in range(n):
        ref[pl.ds(c, SAMPLE_ROWS, stride=n), :] = value[:, c * LANES:(c + 1) * LANES]


def _sample_copies(state_hbm, spool_hbm, hist_ref, xbs_ref, sems):
    history = pltpu.make_async_copy(state_hbm, hist_ref, sems.at[0])
    shift = pltpu.make_async_copy(hist_ref.at[pl.ds(1, POOL_BUF - 1)],
                                  spool_hbm.at[pl.ds(0, POOL_BUF - 1)], sems.at[1])
    newest = pltpu.make_async_copy(xbs_ref, spool_hbm.at[POOL_BUF - 1], sems.at[2])
    return history, shift, newest


def _sample_rows(x_ref, p_ref, taps_ref, hist_ref, newest, xbs_ref, w, y_ref, vn_ref, pos0):
    x = _gather_rows(x_ref)
    xn = _rmsnorm(x, w.vec(VEC_PRE_G)).astype(_BF16)

    u = _gelu_tanh(w.in_proj(xn, COL_U))
    vn = _layernorm(_gelu_tanh(w.in_proj(xn, COL_V)), w.vec(VEC_LN_G), w.vec(VEC_LN_B))
    _scatter_rows(vn_ref, vn)
    heads = []
    for hd in range(N_HEADS):
        tap = taps_ref[hd * CHUNK:hd * CHUNK + 1, 0:1]
        heads.append(vn[:, hd * CHUNK:(hd + 1) * CHUNK] * tap + w.bias_t[0:1, hd:hd + 1])
    s = jnp.concatenate(heads, axis=1)
    y_a = u * s * _silu(w.in_proj(xn, COL_ZA))

    x_b = w.in_proj(xn, COL_XB)
    xbs_ref[...] = x_b
    newest.start()
    groups = []
    for g, win_len in enumerate(POOL_WINDOWS):
        lo, hi = g * POOL_GROUP, (g + 1) * POOL_GROUP
        cur = x_b[:, lo:hi]
        win = cur
        for k in range(1, win_len):
            win = win + hist_ref[POOL_BUF - k, :, lo:hi]
        groups.append(win / float(min(pos0 + 1, win_len)) - cur)
    pooled = jnp.concatenate(groups, axis=1)
    y_b = _pool_mix(pooled, w) * _silu(w.in_proj(xn, COL_ZB))

    _scatter_rows(y_ref, _merge_and_embed(x, _gather_rows(p_ref), xn, y_a, y_b, w))


def _pack_kernel(*refs, tril_periods):
    n = len(tril_periods)
    n_vecs = len(VEC_ROWS)
    bias_ref, vec_refs = refs[n], refs[n + 1:n + 1 + n_vecs]
    outs = refs[n + 1 + n_vecs:]
    bias_t_ref, vecs_ref = outs[n], outs[n + 1]
    bias_t_ref[...] = bias_ref[...].T
    for row, vec_ref in zip(VEC_ROWS, vec_refs):
        vecs_ref[row:row + 1, :] = vec_ref[...]
    vecs_ref[n_vecs:, :] = jnp.zeros((N_VEC_ROWS - n_vecs, D_MODEL), _F32)
    for src, dst, period in zip(refs[:n], outs[:n], tril_periods):
        m = src[...]
        if period is not None:
            row = lax.broadcasted_iota(jnp.int32, m.shape, 0) % period
            m = jnp.where(row >= lax.broadcasted_iota(jnp.int32, m.shape, 1), m, 0.0)
        dst[...] = pltpu.bitcast(m.astype(_BF16), jnp.uint32)


def _pack_params(mats, tril_periods, bias, vecs, n_steps):
    blocks = [m.shape[0] // n_steps for m in mats]
    small = lambda shape: pl.BlockSpec(shape, lambda i: (0, 0))
    for m, rows, period in zip(mats, blocks, tril_periods):
        assert rows * n_steps == m.shape[0] and rows % (2 * BF16_ROWS) == 0
        assert period is None or rows % period == 0
    return pl.pallas_call(
        functools.partial(_pack_kernel, tril_periods=tril_periods),
        grid=(n_steps,),
        in_specs=[pl.BlockSpec((rows, m.shape[1]), lambda i: (i, 0))
                  for m, rows in zip(mats, blocks)]
                 + [small(bias.shape)] + [small(v.shape) for v in vecs],
        out_specs=[pl.BlockSpec((rows // 2, m.shape[1]), lambda i: (i, 0))
                   for m, rows in zip(mats, blocks)]
                  + [small(bias.shape[::-1]), small((N_VEC_ROWS, D_MODEL))],
        out_shape=[jax.ShapeDtypeStruct((m.shape[0] // 2, m.shape[1]), jnp.uint32) for m in mats]
                  + [jax.ShapeDtypeStruct(bias.shape[::-1], _F32),
                     jax.ShapeDtypeStruct((N_VEC_ROWS, D_MODEL), _F32)],
        compiler_params=pltpu.CompilerParams(dimension_semantics=("arbitrary",),
                                             vmem_limit_bytes=VMEM_LIMIT_BYTES),
        name="pack_params",
    )(*mats, bias, *vecs)


def _resident(shape):
    zeros = (0,) * len(shape)
    return pl.BlockSpec(shape, lambda *_: zeros, pipeline_mode=pl.Buffered(1))


def _layer(h, p, hs, ps, state, taps, weights, tile_m, pos0):
    batch, seq, _ = h.shape
    rows = state.shape[1]
    assert seq % tile_m == 0 and tile_m % CHUNK == 0 and seq >= POOL_BUF and rows == SAMPLE_ROWS
    tiles_per_prompt = seq // tile_m
    n_tiles = batch * tiles_per_prompt
    assert n_tiles >= 2

    def front_tile(s):
        t = jnp.minimum(s, n_tiles - 1)
        return t // tiles_per_prompt, t % tiles_per_prompt, 0

    def back_tile(s):
        t = jnp.maximum(s - 1, 0)
        return t // tiles_per_prompt, t % tiles_per_prompt, 0

    per_prompt = lambda s: (jnp.minimum(s, n_tiles - 1) // tiles_per_prompt, 0, 0)
    hbm = pl.BlockSpec(memory_space=pl.ANY)
    whole = lambda a: pl.BlockSpec(a.shape, lambda s: (0,) * a.ndim)
    return pl.pallas_call(
        functools.partial(_layer_kernel, tile_m=tile_m, tiles_per_prompt=tiles_per_prompt,
                          n_tiles=n_tiles, pos0=pos0),
        grid=(n_tiles + 1,),
        in_specs=[pl.BlockSpec((1, tile_m, D_MODEL), front_tile),
                  pl.BlockSpec((1, tile_m, D_MODEL), back_tile),
                  pl.BlockSpec((1, tile_m, PLE_DIM), back_tile),
                  _resident(hs.shape), _resident(ps.shape), _resident(taps.shape), hbm]
                 + [_resident(a.shape) for a in weights],
        out_specs=[pl.BlockSpec((1, tile_m, D_MODEL), back_tile),
                   pl.BlockSpec((1, CHUNK, D_MODEL), per_prompt),
                   pl.BlockSpec((POOL_BUF, batch, D_MODEL), lambda s: (0, 0, 0)),
                   whole(hs), whole(hs), hbm],
        out_shape=[jax.ShapeDtypeStruct((batch, seq, D_MODEL), _F32),
                   jax.ShapeDtypeStruct((batch, CHUNK, D_MODEL), _F32),
                   jax.ShapeDtypeStruct((POOL_BUF, batch, D_MODEL), _F32),
                   jax.ShapeDtypeStruct(hs.shape, _F32),
                   jax.ShapeDtypeStruct(hs.shape, _F32),
                   jax.ShapeDtypeStruct(state.shape, _F32)],
        scratch_shapes=[pltpu.VMEM((N_LANE_BLOCKS, TAIL + tile_m, LANES), _F32),
                        pltpu.VMEM((N_LANE_BLOCKS, 2, TAIL + tile_m, LANES), _F32),
                        pltpu.VMEM((tile_m, D_MODEL), _BF16),
                        pltpu.VMEM((tile_m, D_MODEL), _BF16),
                        pltpu.VMEM((tile_m, D_MODEL), _BF16),
                        pltpu.VMEM(state.shape, _F32),
                        pltpu.VMEM((rows, D_MODEL), _F32),
                        pltpu.SemaphoreType.DMA((3,))],
        compiler_params=pltpu.CompilerParams(
            dimension_semantics=("arbitrary",),
            vmem_limit_bytes=VMEM_LIMIT_BYTES),
        name="layer",
    )(h, h, p, hs, ps, taps, state, *weights)


def kernel(x_prompt, x_sample, state_pool, p_prompt, p_sample, pre_g, w_in, ln_g, ln_b, w_s, b_s,
           w_pool, pool_scale, w_pa, w_pb, w_out, post_g, w_ple, w_pg, ple_in_g, ple_out_g):
    depth = w_in.shape[0]
    dec_batch, dec_seq, _ = x_sample.shape
    assert dec_seq == 1 and x_prompt.shape[2] == D_MODEL
    hp = x_prompt
    hs = x_sample.reshape(dec_batch * N_LANE_BLOCKS, LANES)
    pv, pp, sv, sp = [], [], [], []
    for i in range(depth):
        w_s_rows = w_s[i].reshape(D_MODEL, CHUNK)
        params = _pack_params(
            [w_in[i], w_s_rows, w_pool[i].reshape(D_MODEL, POOL_GROUP), w_pa[i], w_pb[i],
             w_out[i], w_ple[i], w_pg[i]],
            (None, CHUNK) + (None,) * 6, b_s[i],
            [v[i:i + 1] for v in (pre_g, ln_g, ln_b, pool_scale, post_g, ple_in_g, ple_out_g)],
            n_steps=PACK_STEPS)
        hp, v, pool, hs, v_s, pool_s = _layer(
            hp, p_prompt[i], hs, p_sample[i].reshape(dec_batch * PLE_DIM // LANES, LANES),
            jnp.swapaxes(state_pool[i], 0, 1), w_s_rows, params, TILE_M, PAST_LEN)
        pv.append(v)
        pp.append(jnp.swapaxes(pool, 0, 1))
        sv.append(v_s.reshape(dec_batch, 1, D_MODEL))
        sp.append(jnp.swapaxes(pool_s, 0, 1))
    return (hp, hs.reshape(dec_batch, 1, D_MODEL), jnp.stack(pv), jnp.stack(pp), jnp.stack(sv),
            jnp.stack(sp))
```

```python
import functools
from typing import Any, NamedTuple

import jax
import jax.numpy as jnp
from jax import lax
from jax.experimental import pallas as pl
from jax.experimental.pallas import tpu as pltpu

D_MODEL = 1024
CHUNK = 128
N_HEADS = D_MODEL // CHUNK
POOL_WINDOWS = (2, 4, 8, 16)
POOL_GROUP = D_MODEL // len(POOL_WINDOWS)
POOL_BUF = max(POOL_WINDOWS) - 1
PLE_DIM = 256
PAST_LEN = 16384
SAMPLE_ROWS = 128
EPS = 1e-6
COL_U, COL_V, COL_ZA, COL_XB, COL_ZB, COL_GA, COL_GB = range(7)
VEC_ROWS = tuple(range(7))
VEC_PRE_G, VEC_LN_G, VEC_LN_B, VEC_POOL_SCALE, VEC_POST_G, VEC_PLE_IN_G, VEC_PLE_OUT_G = VEC_ROWS
N_VEC_ROWS = 8

LANES = 128
N_LANE_BLOCKS = D_MODEL // LANES
BF16_ROWS = 16
SUBLANES = 8
TAIL = 32
DOUBLING_MIN_WINDOW = 8
PACK_STEPS = 8
TILE_M = 256
VMEM_LIMIT_BYTES = 56 * 1024 * 1024

_BF16 = jnp.bfloat16
_F32 = jnp.float32
_GELU_C = 0.7978845608028654


def _dot(a, b):
    return jnp.dot(a, b, preferred_element_type=_F32)


def _rmsnorm(x, g):
    return x * lax.rsqrt(jnp.mean(x * x, axis=-1, keepdims=True) + EPS) * g


def _layernorm(x, g, b):
    xc = x - jnp.mean(x, axis=-1, keepdims=True)
    return xc * lax.rsqrt(jnp.mean(xc * xc, axis=-1, keepdims=True) + EPS) * g + b


def _gelu_tanh(x):
    return 0.5 * x * (1.0 + jnp.tanh(_GELU_C * (x + 0.044715 * (x * x * x))))


def _sigmoid(x):
    return 0.5 * jnp.tanh(0.5 * x) + 0.5


def _silu(x):
    return x * _sigmoid(x)


def _unpack(ref, rows=slice(None), cols=slice(None)):
    return pltpu.bitcast(ref[rows, cols], _BF16)


class _Weights(NamedTuple):
    in_proj_w: Any
    spatial: Any
    pool: Any
    pa: Any
    pb: Any
    out: Any
    ple: Any
    pg: Any
    bias_t: Any
    vecs: Any

    def vec(self, row):
        return self.vecs[row:row + 1, :]

    def in_proj(self, xn, col):
        return _dot(xn, _unpack(self.in_proj_w, cols=slice(col * D_MODEL, (col + 1) * D_MODEL)))


def _pool_mix(pooled, w):
    parts = []
    for g in range(len(POOL_WINDOWS)):
        q = pooled[:, g * POOL_GROUP:(g + 1) * POOL_GROUP].astype(_BF16)
        rows = slice(g * POOL_GROUP // 2, (g + 1) * POOL_GROUP // 2)
        parts.append(_dot(q, _unpack(w.pool, rows=rows)))
    return jnp.concatenate(parts, axis=1) * w.vec(VEC_POOL_SCALE)


def _merge_and_embed(h, p, xn, y_a, y_b, w):
    g_a = w.in_proj(xn, COL_GA)
    g_b = w.in_proj(xn, COL_GB)
    m = (_sigmoid(g_a) * _dot(y_a.astype(_BF16), _unpack(w.pa))
         + _sigmoid(g_b) * _dot(y_b.astype(_BF16), _unpack(w.pb)))
    h = h + _rmsnorm(_dot(m.astype(_BF16), _unpack(w.out)), w.vec(VEC_POST_G))
    e = _dot(p.astype(_BF16), _unpack(w.ple))
    hn = _rmsnorm(h, w.vec(VEC_PLE_IN_G)).astype(_BF16)
    gate = _sigmoid(_dot(hn, _unpack(w.pg)))
    return h + _rmsnorm(gate * e, w.vec(VEC_PLE_OUT_G))


def _spatial_mix(vn_b, w, tile_m):
    w_tril = [_unpack(w.spatial, rows=slice(hd * CHUNK // 2, (hd + 1) * CHUNK // 2))
              for hd in range(N_HEADS)]
    bias = [w.bias_t[:, hd:hd + 1] for hd in range(N_HEADS)]
    s_rows = []
    for c in range(tile_m // CHUNK):
        heads = []
        for hd in range(N_HEADS):
            blk = vn_b[c * CHUNK:(c + 1) * CHUNK, hd * CHUNK:(hd + 1) * CHUNK]
            heads.append(_dot(w_tril[hd], blk) + bias[hd])
        s_rows.append(jnp.concatenate(heads, axis=1))
    return jnp.concatenate(s_rows, axis=0)


def _window_sum(src, lvl, w, rows):
    n_levels = w.bit_length() - 1
    assert w == 1 << n_levels and SUBLANES * (n_levels - 1) + 1 <= TAIL
    for l in range(1, n_levels):
        start = TAIL - SUBLANES * (n_levels - l)
        half = 1 << (l - 1)
        dst = lvl.at[(l - 1) % 2]
        dst[start:rows, :] = src[start:rows, :] + src[start - half:rows - half, :]
        src = dst
    half = w // 2
    return src[TAIL:rows, :] + src[TAIL - half:rows - half, :]


def _window_pool(x_b, ext_ref, lvl_ref, pool_ref, b, j, tile_m):
    rows = TAIL + tile_m
    for c in range(N_LANE_BLOCKS):
        ext_ref[c, TAIL:rows, :] = x_b[:, c * LANES:(c + 1) * LANES]
    pos = j * tile_m + lax.broadcasted_iota(jnp.int32, (tile_m, 1), 0)
    blocks = []
    for c in range(N_LANE_BLOCKS):
        w = POOL_WINDOWS[c * LANES // POOL_GROUP]
        cur = x_b[:, c * LANES:(c + 1) * LANES]
        if w >= DOUBLING_MIN_WINDOW:
            win = _window_sum(ext_ref.at[c], lvl_ref.at[c], w, rows)
        else:
            win = cur
            for k in range(1, w):
                win = win + ext_ref[c, TAIL - k:TAIL - k + tile_m, :]
        cnt = jnp.minimum(pos + 1, w).astype(_F32)
        blocks.append(win / cnt - cur)
    latest = jnp.concatenate([ext_ref[c, TAIL + tile_m - POOL_BUF:TAIL + tile_m, :]
                              for c in range(N_LANE_BLOCKS)], axis=1)
    pool_ref[:, pl.ds(b, 1), :] = latest[:, None, :]
    for c in range(N_LANE_BLOCKS):
        ext_ref[c, 0:TAIL, :] = ext_ref[c, tile_m:tile_m + TAIL, :]
    return jnp.concatenate(blocks, axis=1)


def _layer_kernel(x_ref, x_lag_ref, p_lag_ref, xs_ref, ps_ref, taps_ref, state_hbm,
                  w_in_ref, w_s_ref, w_pool_ref, w_pa_ref, w_pb_ref, w_out_ref, w_ple_ref,
                  w_pg_ref, bs_t_ref, vecs_ref,
                  y_ref, vn_ref, pool_ref, ys_ref, vns_ref, spool_hbm,
                  ext_ref, lvl_ref, xn_c, ya_c, yb_c, hist_ref, xbs_ref, sems,
                  *, tile_m, tiles_per_prompt, n_tiles, pos0):
    s_id = pl.program_id(0)
    front = jnp.minimum(s_id, n_tiles - 1)
    b = front // tiles_per_prompt
    j = lax.rem(front, tiles_per_prompt)
    w = _Weights(in_proj_w=w_in_ref, spatial=w_s_ref, pool=w_pool_ref, pa=w_pa_ref, pb=w_pb_ref,
                 out=w_out_ref, ple=w_ple_ref, pg=w_pg_ref, bias_t=bs_t_ref, vecs=vecs_ref)

    @pl.when(j == 0)
    def _():
        ext_ref[:, 0:TAIL, :] = jnp.zeros((N_LANE_BLOCKS, TAIL, LANES), _F32)

    @pl.when(s_id == 0)
    def _front_only():
        xn = _rmsnorm(x_ref[0], w.vec(VEC_PRE_G)).astype(_BF16)
        xn_c[...] = xn
        vn = _layernorm(_gelu_tanh(w.in_proj(xn, COL_V)), w.vec(VEC_LN_G), w.vec(VEC_LN_B))
        vn_ref[0] = vn[tile_m - CHUNK:, :]
        s = _spatial_mix(vn.astype(_BF16), w, tile_m)
        ya_c[...] = (_gelu_tanh(w.in_proj(xn, COL_U)) * s
                     * _silu(w.in_proj(xn, COL_ZA))).astype(_BF16)
        pooled = _window_pool(w.in_proj(xn, COL_XB), ext_ref, lvl_ref, pool_ref, b, j, tile_m)
        yb_c[...] = (_pool_mix(pooled, w) * _silu(w.in_proj(xn, COL_ZB))).astype(_BF16)

    @pl.when(s_id == n_tiles)
    def _back_only_and_sample():
        history, shift, newest = _sample_copies(state_hbm, spool_hbm, hist_ref, xbs_ref, sems)
        history.start()
        y_ref[0] = _merge_and_embed(x_lag_ref[0], p_lag_ref[0], xn_c[...], ya_c[...], yb_c[...], w)
        history.wait()
        shift.start()
        _sample_rows(xs_ref, ps_ref, taps_ref, hist_ref, newest, xbs_ref, w, ys_ref, vns_ref, pos0)
        shift.wait()
        newest.wait()

    @pl.when(jnp.logical_and(s_id > 0, s_id < n_tiles))
    def _both():
        xn_prev = xn_c[...]
        g_a = w.in_proj(xn_prev, COL_GA)
        xn = _rmsnorm(x_ref[0], w.vec(VEC_PRE_G)).astype(_BF16)
        xn_c[...] = xn
        v_raw = w.in_proj(xn, COL_V)
        g_b = w.in_proj(xn_prev, COL_GB)
        e = _dot(p_lag_ref[0].astype(_BF16), _unpack(w.ple))
        vn = _layernorm(_gelu_tanh(v_raw), w.vec(VEC_LN_G), w.vec(VEC_LN_B))
        vn_ref[0] = vn[tile_m - CHUNK:, :]
        u_raw = w.in_proj(xn, COL_U)
        s = _spatial_mix(vn.astype(_BF16), w, tile_m)
        proj_a = _dot(ya_c[...], _unpack(w.pa))
        proj_b = _dot(yb_c[...], _unpack(w.pb))
        x_b = w.in_proj(xn, COL_XB)
        m = (_sigmoid(g_a) * proj_a + _sigmoid(g_b) * proj_b).astype(_BF16)
        o = _dot(m, _unpack(w.out))
        zb_raw = w.in_proj(xn, COL_ZB)
        pooled = _window_pool(x_b, ext_ref, lvl_ref, pool_ref, b, j, tile_m)
        mixed = _pool_mix(pooled, w)
        za_raw = w.in_proj(xn, COL_ZA)
        ya_c[...] = (_gelu_tanh(u_raw) * s * _silu(za_raw)).astype(_BF16)
        h = x_lag_ref[0] + _rmsnorm(o, w.vec(VEC_POST_G))
        hn = _rmsnorm(h, w.vec(VEC_PLE_IN_G)).astype(_BF16)
        gate_raw = _dot(hn, _unpack(w.pg))
        yb_c[...] = (mixed * _silu(zb_raw)).astype(_BF16)
        y_ref[0] = h + _rmsnorm(_sigmoid(gate_raw) * e, w.vec(VEC_PLE_OUT_G))


def _gather_rows(ref):
    n = ref.shape[0] // SAMPLE_ROWS
    return jnp.concatenate([ref[pl.ds(c, SAMPLE_ROWS, stride=n), :] for c in range(n)], axis=1)


def _scatter_rows(ref, value):
    n = ref.shape[0] // SAMPLE_ROWS
    for c in range(n):
        ref[pl.ds(c, SAMPLE_ROWS, stride=n), :] = value[:, c * LANES:(c + 1) * LANES]


def _sample_copies(state_hbm, spool_hbm, hist_ref, xbs_ref, sems):
    history = pltpu.make_async_copy(state_hbm, hist_ref, sems.at[0])
    shift = pltpu.make_async_copy(hist_ref.at[pl.ds(1, POOL_BUF - 1)],
                                  spool_hbm.at[pl.ds(0, POOL_BUF - 1)], sems.at[1])
    newest = pltpu.make_async_copy(xbs_ref, spool_hbm.at[POOL_BUF - 1], sems.at[2])
    return history, shift, newest


def _sample_rows(x_ref, p_ref, taps_ref, hist_ref, newest, xbs_ref, w, y_ref, vn_ref, pos0):
    x = _gather_rows(x_ref)
    xn = _rmsnorm(x, w.vec(VEC_PRE_G)).astype(_BF16)

    u = _gelu_tanh(w.in_proj(xn, COL_U))
    vn = _layernorm(_gelu_tanh(w.in_proj(xn, COL_V)), w.vec(VEC_LN_G), w.vec(VEC_LN_B))
    _scatter_rows(vn_ref, vn)
    heads = []
    for hd in range(N_HEADS):
        tap = taps_ref[hd * CHUNK:hd * CHUNK + 1, 0:1]
        heads.append(vn[:, hd * CHUNK:(hd + 1) * CHUNK] * tap + w.bias_t[0:1, hd:hd + 1])
    s = jnp.concatenate(heads, axis=1)
    y_a = u * s * _silu(w.in_proj(xn, COL_ZA))

    x_b = w.in_proj(xn, COL_XB)
    xbs_ref[...] = x_b
    newest.start()
    groups = []
    for g, win_len in enumerate(POOL_WINDOWS):
        lo, hi = g * POOL_GROUP, (g + 1) * POOL_GROUP
        cur = x_b[:, lo:hi]
        win = cur
        for k in range(1, win_len):
            win = win + hist_ref[POOL_BUF - k, :, lo:hi]
        groups.append(win / float(min(pos0 + 1, win_len)) - cur)
    pooled = jnp.concatenate(groups, axis=1)
    y_b = _pool_mix(pooled, w) * _silu(w.in_proj(xn, COL_ZB))

    _scatter_rows(y_ref, _merge_and_embed(x, _gather_rows(p_ref), xn, y_a, y_b, w))


def _pack_kernel(*refs, tril_periods):
    n = len(tril_periods)
    n_vecs = len(VEC_ROWS)
    bias_ref, vec_refs = refs[n], refs[n + 1:n + 1 + n_vecs]
    outs = refs[n + 1 + n_vecs:]
    bias_t_ref, vecs_ref = outs[n], outs[n + 1]
    bias_t_ref[...] = bias_ref[...].T
    for row, vec_ref in zip(VEC_ROWS, vec_refs):
        vecs_ref[row:row + 1, :] = vec_ref[...]
    vecs_ref[n_vecs:, :] = jnp.zeros((N_VEC_ROWS - n_vecs, D_MODEL), _F32)
    for src, dst, period in zip(refs[:n], outs[:n], tril_periods):
        m = src[...]
        if period is not None:
            row = lax.broadcasted_iota(jnp.int32, m.shape, 0) % period
            m = jnp.where(row >= lax.broadcasted_iota(jnp.int32, m.shape, 1), m, 0.0)
        dst[...] = pltpu.bitcast(m.astype(_BF16), jnp.uint32)


def _pack_params(mats, tril_periods, bias, vecs, n_steps):
    blocks = [m.shape[0] // n_steps for m in mats]
    small = lambda shape: pl.BlockSpec(shape, lambda i: (0, 0))
    for m, rows, period in zip(mats, blocks, tril_periods):
        assert rows * n_steps == m.shape[0] and rows % (2 * BF16_ROWS) == 0
        assert period is None or rows % period == 0
    return pl.pallas_call(
        functools.partial(_pack_kernel, tril_periods=tril_periods),
        grid=(n_steps,),
        in_specs=[pl.BlockSpec((rows, m.shape[1]), lambda i: (i, 0))
                  for m, rows in zip(mats, blocks)]
                 + [small(bias.shape)] + [small(v.shape) for v in vecs],
        out_specs=[pl.BlockSpec((rows // 2, m.shape[1]), lambda i: (i, 0))
                   for m, rows in zip(mats, blocks)]
                  + [small(bias.shape[::-1]), small((N_VEC_ROWS, D_MODEL))],
        out_shape=[jax.ShapeDtypeStruct((m.shape[0] // 2, m.shape[1]), jnp.uint32) for m in mats]
                  + [jax.ShapeDtypeStruct(bias.shape[::-1], _F32),
                     jax.ShapeDtypeStruct((N_VEC_ROWS, D_MODEL), _F32)],
        compiler_params=pltpu.CompilerParams(dimension_semantics=("arbitrary",),
                                             vmem_limit_bytes=VMEM_LIMIT_BYTES),
        name="pack_params",
    )(*mats, bias, *vecs)


def _resident(shape):
    zeros = (0,) * len(shape)
    return pl.BlockSpec(shape, lambda *_: zeros, pipeline_mode=pl.Buffered(1))


def _layer(h, p, hs, ps, state, taps, weights, tile_m, pos0):
    batch, seq, _ = h.shape
    rows = state.shape[1]
    assert seq % tile_m == 0 and tile_m % CHUNK == 0 and seq >= POOL_BUF and rows == SAMPLE_ROWS
    tiles_per_prompt = seq // tile_m
    n_tiles = batch * tiles_per_prompt
    assert n_tiles >= 2

    def front_tile(s):
        t = jnp.minimum(s, n_tiles - 1)
        return t // tiles_per_prompt, t % tiles_per_prompt, 0

    def back_tile(s):
        t = jnp.maximum(s - 1, 0)
        return t // tiles_per_prompt, t % tiles_per_prompt, 0

    per_prompt = lambda s: (jnp.minimum(s, n_tiles - 1) // tiles_per_prompt, 0, 0)
    hbm = pl.BlockSpec(memory_space=pl.ANY)
    whole = lambda a: pl.BlockSpec(a.shape, lambda s: (0,) * a.ndim)
    return pl.pallas_call(
        functools.partial(_layer_kernel, tile_m=tile_m, tiles_per_prompt=tiles_per_prompt,
                          n_tiles=n_tiles, pos0=pos0),
        grid=(n_tiles + 1,),
        in_specs=[pl.BlockSpec((1, tile_m, D_MODEL), front_tile),
                  pl.BlockSpec((1, tile_m, D_MODEL), back_tile),
                  pl.BlockSpec((1, tile_m, PLE_DIM), back_tile),
                  _resident(hs.shape), _resident(ps.shape), _resident(taps.shape), hbm]
                 + [_resident(a.shape) for a in weights],
        out_specs=[pl.BlockSpec((1, tile_m, D_MODEL), back_tile),
                   pl.BlockSpec((1, CHUNK, D_MODEL), per_prompt),
                   pl.BlockSpec((POOL_BUF, batch, D_MODEL), lambda s: (0, 0, 0)),
                   whole(hs), whole(hs), hbm],
        out_shape=[jax.ShapeDtypeStruct((batch, seq, D_MODEL), _F32),
                   jax.ShapeDtypeStruct((batch, CHUNK, D_MODEL), _F32),
                   jax.ShapeDtypeStruct((POOL_BUF, batch, D_MODEL), _F32),
                   jax.ShapeDtypeStruct(hs.shape, _F32),
                   jax.ShapeDtypeStruct(hs.shape, _F32),
                   jax.ShapeDtypeStruct(state.shape, _F32)],
        scratch_shapes=[pltpu.VMEM((N_LANE_BLOCKS, TAIL + tile_m, LANES), _F32),
                        pltpu.VMEM((N_LANE_BLOCKS, 2, TAIL + tile_m, LANES), _F32),
                        pltpu.VMEM((tile_m, D_MODEL), _BF16),
                        pltpu.VMEM((tile_m, D_MODEL), _BF16),
                        pltpu.VMEM((tile_m, D_MODEL), _BF16),
                        pltpu.VMEM(state.shape, _F32),
                        pltpu.VMEM((rows, D_MODEL), _F32),
                        pltpu.SemaphoreType.DMA((3,))],
        compiler_params=pltpu.CompilerParams(
            dimension_semantics=("arbitrary",),
            vmem_limit_bytes=VMEM_LIMIT_BYTES),
        name="layer",
    )(h, h, p, hs, ps, taps, state, *weights)


def kernel(x_prompt, x_sample, state_pool, p_prompt, p_sample, pre_g, w_in, ln_g, ln_b, w_s, b_s,
           w_pool, pool_scale, w_pa, w_pb, w_out, post_g, w_ple, w_pg, ple_in_g, ple_out_g):
    depth = w_in.shape[0]
    dec_batch, dec_seq, _ = x_sample.shape
    assert dec_seq == 1 and x_prompt.shape[2] == D_MODEL
    hp = x_prompt
    hs = x_sample.reshape(dec_batch * N_LANE_BLOCKS, LANES)
    pv, pp, sv, sp = [], [], [], []
    for i in range(depth):
        w_s_rows = w_s[i].reshape(D_MODEL, CHUNK)
        params = _pack_params(
            [w_in[i], w_s_rows, w_pool[i].reshape(D_MODEL, POOL_GROUP), w_pa[i], w_pb[i],
             w_out[i], w_ple[i], w_pg[i]],
            (None, CHUNK) + (None,) * 6, b_s[i],
            [v[i:i + 1] for v in (pre_g, ln_g, ln_b, pool_scale, post_g, ple_in_g, ple_out_g)],
            n_steps=PACK_STEPS)
        hp, v, pool, hs, v_s, pool_s = _layer(
            hp, p_prompt[i], hs, p_sample[i].reshape(dec_batch * PLE_DIM // LANES, LANES),
            jnp.swapaxes(state_pool[i], 0, 1), w_s_rows, params, TILE_M, PAST_LEN)
        pv.append(v)
        pp.append(jnp.swapaxes(pool, 0, 1))
        sv.append(v_s.reshape(dec_batch, 1, D_MODEL))
        sp.append(jnp.swapaxes(pool_s, 0, 1))
    return (hp, hs.reshape(dec_batch, 1, D_MODEL), jnp.stack(pv), jnp.stack(pp), jnp.stack(sv),
            jnp.stack(sp))
```

```python
import functools
from typing import Any, NamedTuple

import jax
import jax.numpy as jnp
from jax import lax
from jax.experimental import pallas as pl
from jax.experimental.pallas import tpu as pltpu

D_MODEL = 1024
CHUNK = 128
N_HEADS = D_MODEL // CHUNK
POOL_WINDOWS = (2, 4, 8, 16)
POOL_GROUP = D_MODEL // len(POOL_WINDOWS)
POOL_BUF = max(POOL_WINDOWS) - 1
PLE_DIM = 256
PAST_LEN = 16384
SAMPLE_ROWS = 128
EPS = 1e-6
COL_U, COL_V, COL_ZA, COL_XB, COL_ZB, COL_GA, COL_GB = range(7)
VEC_ROWS = tuple(range(7))
VEC_PRE_G, VEC_LN_G, VEC_LN_B, VEC_POOL_SCALE, VEC_POST_G, VEC_PLE_IN_G, VEC_PLE_OUT_G = VEC_ROWS
N_VEC_ROWS = 8

LANES = 128
N_LANE_BLOCKS = D_MODEL // LANES
BF16_ROWS = 16
SUBLANES = 8
TAIL = 32
DOUBLING_MIN_WINDOW = 8
PACK_STEPS = 8
TILE_M = 256
VMEM_LIMIT_BYTES = 56 * 1024 * 1024

_BF16 = jnp.bfloat16
_F32 = jnp.float32
_GELU_C = 0.7978845608028654


def _dot(a, b):
    return jnp.dot(a, b, preferred_element_type=_F32)


def _rmsnorm(x, g):
    return x * lax.rsqrt(jnp.mean(x * x, axis=-1, keepdims=True) + EPS) * g


def _layernorm(x, g, b):
    xc = x - jnp.mean(x, axis=-1, keepdims=True)
    return xc * lax.rsqrt(jnp.mean(xc * xc, axis=-1, keepdims=True) + EPS) * g + b


def _gelu_tanh(x):
    return 0.5 * x * (1.0 + jnp.tanh(_GELU_C * (x + 0.044715 * (x * x * x))))


def _sigmoid(x):
    return 0.5 * jnp.tanh(0.5 * x) + 0.5


def _silu(x):
    return x * _sigmoid(x)


def _unpack(ref, rows=slice(None), cols=slice(None)):
    return pltpu.bitcast(ref[rows, cols], _BF16)


class _Weights(NamedTuple):
    in_proj_w: Any
    spatial: Any
    pool: Any
    pa: Any
    pb: Any
    out: Any
    ple: Any
    pg: Any
    bias_t: Any
    vecs: Any

    def vec(self, row):
        return self.vecs[row:row + 1, :]

    def in_proj(self, xn, col):
        return _dot(xn, _unpack(self.in_proj_w, cols=slice(col * D_MODEL, (col + 1) * D_MODEL)))


def _pool_mix(pooled, w):
    parts = []
    for g in range(len(POOL_WINDOWS)):
        q = pooled[:, g * POOL_GROUP:(g + 1) * POOL_GROUP].astype(_BF16)
        rows = slice(g * POOL_GROUP // 2, (g + 1) * POOL_GROUP // 2)
        parts.append(_dot(q, _unpack(w.pool, rows=rows)))
    return jnp.concatenate(parts, axis=1) * w.vec(VEC_POOL_SCALE)


def _merge_and_embed(h, p, xn, y_a, y_b, w):
    g_a = w.in_proj(xn, COL_GA)
    g_b = w.in_proj(xn, COL_GB)
    m = (_sigmoid(g_a) * _dot(y_a.astype(_BF16), _unpack(w.pa))
         + _sigmoid(g_b) * _dot(y_b.astype(_BF16), _unpack(w.pb)))
    h = h + _rmsnorm(_dot(m.astype(_BF16), _unpack(w.out)), w.vec(VEC_POST_G))
    e = _dot(p.astype(_BF16), _unpack(w.ple))
    hn = _rmsnorm(h, w.vec(VEC_PLE_IN_G)).astype(_BF16)
    gate = _sigmoid(_dot(hn, _unpack(w.pg)))
    return h + _rmsnorm(gate * e, w.vec(VEC_PLE_OUT_G))


def _spatial_mix(vn_b, w, tile_m):
    w_tril = [_unpack(w.spatial, rows=slice(hd * CHUNK // 2, (hd + 1) * CHUNK // 2))
              for hd in range(N_HEADS)]
    bias = [w.bias_t[:, hd:hd + 1] for hd in range(N_HEADS)]
    s_rows = []
    for c in range(tile_m // CHUNK):
        heads = []
        for hd in range(N_HEADS):
            blk = vn_b[c * CHUNK:(c + 1) * CHUNK, hd * CHUNK:(hd + 1) * CHUNK]
            heads.append(_dot(w_tril[hd], blk) + bias[hd])
        s_rows.append(jnp.concatenate(heads, axis=1))
    return jnp.concatenate(s_rows, axis=0)


def _window_sum(src, lvl, w, rows):
    n_levels = w.bit_length() - 1
    assert w == 1 << n_levels and SUBLANES * (n_levels - 1) + 1 <= TAIL
    for l in range(1, n_levels):
        start = TAIL - SUBLANES * (n_levels - l)
        half = 1 << (l - 1)
        dst = lvl.at[(l - 1) % 2]
        dst[start:rows, :] = src[start:rows, :] + src[start - half:rows - half, :]
        src = dst
    half = w // 2
    return src[TAIL:rows, :] + src[TAIL - half:rows - half, :]


def _window_pool(x_b, ext_ref, lvl_ref, pool_ref, b, j, tile_m):
    rows = TAIL + tile_m
    for c in range(N_LANE_BLOCKS):
        ext_ref[c, TAIL:rows, :] = x_b[:, c * LANES:(c + 1) * LANES]
    pos = j * tile_m + lax.broadcasted_iota(jnp.int32, (tile_m, 1), 0)
    blocks = []
    for c in range(N_LANE_BLOCKS):
        w = POOL_WINDOWS[c * LANES // POOL_GROUP]
        cur = x_b[:, c * LANES:(c + 1) * LANES]
        if w >= DOUBLING_MIN_WINDOW:
            win = _window_sum(ext_ref.at[c], lvl_ref.at[c], w, rows)
        else:
            win = cur
            for k in range(1, w):
                win = win + ext_ref[c, TAIL - k:TAIL - k + tile_m, :]
        cnt = jnp.minimum(pos + 1, w).astype(_F32)
        blocks.append(win / cnt - cur)
    latest = jnp.concatenate([ext_ref[c, TAIL + tile_m - POOL_BUF:TAIL + tile_m, :]
                              for c in range(N_LANE_BLOCKS)], axis=1)
    pool_ref[:, pl.ds(b, 1), :] = latest[:, None, :]
    for c in range(N_LANE_BLOCKS):
        ext_ref[c, 0:TAIL, :] = ext_ref[c, tile_m:tile_m + TAIL, :]
    return jnp.concatenate(blocks, axis=1)


def _layer_kernel(x_ref, x_lag_ref, p_lag_ref, xs_ref, ps_ref, taps_ref, state_hbm,
                  w_in_ref, w_s_ref, w_pool_ref, w_pa_ref, w_pb_ref, w_out_ref, w_ple_ref,
                  w_pg_ref, bs_t_ref, vecs_ref,
                  y_ref, vn_ref, pool_ref, ys_ref, vns_ref, spool_hbm,
                  ext_ref, lvl_ref, xn_c, ya_c, yb_c, hist_ref, xbs_ref, sems,
                  *, tile_m, tiles_per_prompt, n_tiles, pos0):
    s_id = pl.program_id(0)
    front = jnp.minimum(s_id, n_tiles - 1)
    b = front // tiles_per_prompt
    j = lax.rem(front, tiles_per_prompt)
    w = _Weights(in_proj_w=w_in_ref, spatial=w_s_ref, pool=w_pool_ref, pa=w_pa_ref, pb=w_pb_ref,
                 out=w_out_ref, ple=w_ple_ref, pg=w_pg_ref, bias_t=bs_t_ref, vecs=vecs_ref)

    @pl.when(j == 0)
    def _():
        ext_ref[:, 0:TAIL, :] = jnp.zeros((N_LANE_BLOCKS, TAIL, LANES), _F32)

    @pl.when(s_id == 0)
    def _front_only():
        xn = _rmsnorm(x_ref[0], w.vec(VEC_PRE_G)).astype(_BF16)
        xn_c[...] = xn
        vn = _layernorm(_gelu_tanh(w.in_proj(xn, COL_V)), w.vec(VEC_LN_G), w.vec(VEC_LN_B))
        vn_ref[0] = vn[tile_m - CHUNK:, :]
        s = _spatial_mix(vn.astype(_BF16), w, tile_m)
        ya_c[...] = (_gelu_tanh(w.in_proj(xn, COL_U)) * s
                     * _silu(w.in_proj(xn, COL_ZA))).astype(_BF16)
        pooled = _window_pool(w.in_proj(xn, COL_XB), ext_ref, lvl_ref, pool_ref, b, j, tile_m)
        yb_c[...] = (_pool_mix(pooled, w) * _silu(w.in_proj(xn, COL_ZB))).astype(_BF16)

    @pl.when(s_id == n_tiles)
    def _back_only_and_sample():
        history, shift, newest = _sample_copies(state_hbm, spool_hbm, hist_ref, xbs_ref, sems)
        history.start()
        xn_prev = xn_c[...]
        g_a = w.in_proj(xn_prev, COL_GA)
        g_b = w.in_proj(xn_prev, COL_GB)
        xs, xns, ya_s, xb_s = _sample_front(xs_ref, taps_ref, xbs_ref, w, vns_ref)
        newest.start()
        proj_a = _dot(ya_c[...], _unpack(w.pa))
        proj_b = _dot(yb_c[...], _unpack(w.pb))
        zb_s = w.in_proj(xns, COL_ZB)
        m = (_sigmoid(g_a) * proj_a + _sigmoid(g_b) * proj_b).astype(_BF16)
        o = _dot(m, _unpack(w.out))
        e = _dot(p_lag_ref[0].astype(_BF16), _unpack(w.ple))
        history.wait()
        shift.start()
        yb_s = _sample_pool(xb_s, hist_ref, w, pos0) * _silu(zb_s)
        h = x_lag_ref[0] + _rmsnorm(o, w.vec(VEC_POST_G))
        hn = _rmsnorm(h, w.vec(VEC_PLE_IN_G)).astype(_BF16)
        gate_raw = _dot(hn, _unpack(w.pg))
        _scatter_rows(ys_ref, _merge_and_embed(xs, _gather_rows(ps_ref), xns, ya_s, yb_s, w))
        y_ref[0] = h + _rmsnorm(_sigmoid(gate_raw) * e, w.vec(VEC_PLE_OUT_G))
        shift.wait()
        newest.wait()

    @pl.when(jnp.logical_and(s_id > 0, s_id < n_tiles))
    def _both():
        xn_prev = xn_c[...]
        g_a = w.in_proj(xn_prev, COL_GA)
        xn = _rmsnorm(x_ref[0], w.vec(VEC_PRE_G)).astype(_BF16)
        xn_c[...] = xn
        v_raw = w.in_proj(xn, COL_V)
        g_b = w.in_proj(xn_prev, COL_GB)
        e = _dot(p_lag_ref[0].astype(_BF16), _unpack(w.ple))
        vn = _layernorm(_gelu_tanh(v_raw), w.vec(VEC_LN_G), w.vec(VEC_LN_B))
        vn_ref[0] = vn[tile_m - CHUNK:, :]
        u_raw = w.in_proj(xn, COL_U)
        s = _spatial_mix(vn.astype(_BF16), w, tile_m)
        proj_a = _dot(ya_c[...], _unpack(w.pa))
        proj_b = _dot(yb_c[...], _unpack(w.pb))
        x_b = w.in_proj(xn, COL_XB)
        m = (_sigmoid(g_a) * proj_a + _sigmoid(g_b) * proj_b).astype(_BF16)
        o = _dot(m, _unpack(w.out))
        zb_raw = w.in_proj(xn, COL_ZB)
        pooled = _window_pool(x_b, ext_ref, lvl_ref, pool_ref, b, j, tile_m)
        mixed = _pool_mix(pooled, w)
        za_raw = w.in_proj(xn, COL_ZA)
        ya_c[...] = (_gelu_tanh(u_raw) * s * _silu(za_raw)).astype(_BF16)
        h = x_lag_ref[0] + _rmsnorm(o, w.vec(VEC_POST_G))
        hn = _rmsnorm(h, w.vec(VEC_PLE_IN_G)).astype(_BF16)
        gate_raw = _dot(hn, _unpack(w.pg))
        yb_c[...] = (mixed * _silu(zb_raw)).astype(_BF16)
        y_ref[0] = h + _rmsnorm(_sigmoid(gate_raw) * e, w.vec(VEC_PLE_OUT_G))


def _gather_rows(ref):
    n = ref.shape[0] // SAMPLE_ROWS
    return jnp.concatenate([ref[pl.ds(c, SAMPLE_ROWS, stride=n), :] for c in range(n)], axis=1)


def _scatter_rows(ref, value):
    n = ref.shape[0] // SAMPLE_ROWS
    for c in range(n):
        ref[pl.ds(c, SAMPLE_ROWS, stride=n), :] = value[:, c * LANES:(c + 1) * LANES]


def _sample_copies(state_hbm, spool_hbm, hist_ref, xbs_ref, sems):
    history = pltpu.make_async_copy(state_hbm, hist_ref, sems.at[0])
    shift = pltpu.make_async_copy(hist_ref.at[pl.ds(1, POOL_BUF - 1)],
                                  spool_hbm.at[pl.ds(0, POOL_BUF - 1)], sems.at[1])
    newest = pltpu.make_async_copy(xbs_ref, spool_hbm.at[POOL_BUF - 1], sems.at[2])
    return history, shift, newest


def _sample_front(x_ref, taps_ref, xbs_ref, w, vn_ref):
    x = _gather_rows(x_ref)
    xn = _rmsnorm(x, w.vec(VEC_PRE_G)).astype(_BF16)

    u = _gelu_tanh(w.in_proj(xn, COL_U))
    vn = _layernorm(_gelu_tanh(w.in_proj(xn, COL_V)), w.vec(VEC_LN_G), w.vec(VEC_LN_B))
    _scatter_rows(vn_ref, vn)
    heads = []
    for hd in range(N_HEADS):
        tap = taps_ref[hd * CHUNK:hd * CHUNK + 1, 0:1]
        heads.append(vn[:, hd * CHUNK:(hd + 1) * CHUNK] * tap + w.bias_t[0:1, hd:hd + 1])
    s = jnp.concatenate(heads, axis=1)
    y_a = u * s * _silu(w.in_proj(xn, COL_ZA))

    x_b = w.in_proj(xn, COL_XB)
    xbs_ref[...] = x_b
    return x, xn, y_a, x_b


def _sample_pool(x_b, hist_ref, w, pos0):
    groups = []
    for g, win_len in enumerate(POOL_WINDOWS):
        lo, hi = g * POOL_GROUP, (g + 1) * POOL_GROUP
        cur = x_b[:, lo:hi]
        win = cur
        for k in range(1, win_len):
            win = win + hist_ref[POOL_BUF - k, :, lo:hi]
        groups.append(win / float(min(pos0 + 1, win_len)) - cur)
    return _pool_mix(jnp.concatenate(groups, axis=1), w)


def _pack_kernel(*refs, tril_periods):
    n = len(tril_periods)
    n_vecs = len(VEC_ROWS)
    bias_ref, vec_refs = refs[n], refs[n + 1:n + 1 + n_vecs]
    outs = refs[n + 1 + n_vecs:]
    bias_t_ref, vecs_ref = outs[n], outs[n + 1]
    bias_t_ref[...] = bias_ref[...].T
    for row, vec_ref in zip(VEC_ROWS, vec_refs):
        vecs_ref[row:row + 1, :] = vec_ref[...]
    vecs_ref[n_vecs:, :] = jnp.zeros((N_VEC_ROWS - n_vecs, D_MODEL), _F32)
    for src, dst, period in zip(refs[:n], outs[:n], tril_periods):
        m = src[...]
        if period is not None:
            row = lax.broadcasted_iota(jnp.int32, m.shape, 0) % period
            m = jnp.where(row >= lax.broadcasted_iota(jnp.int32, m.shape, 1), m, 0.0)
        dst[...] = pltpu.bitcast(m.astype(_BF16), jnp.uint32)


def _pack_params(mats, tril_periods, bias, vecs, n_steps):
    blocks = [m.shape[0] // n_steps for m in mats]
    small = lambda shape: pl.BlockSpec(shape, lambda i: (0, 0))
    for m, rows, period in zip(mats, blocks, tril_periods):
        assert rows * n_steps == m.shape[0] and rows % (2 * BF16_ROWS) == 0
        assert period is None or rows % period == 0
    return pl.pallas_call(
        functools.partial(_pack_kernel, tril_periods=tril_periods),
        grid=(n_steps,),
        in_specs=[pl.BlockSpec((rows, m.shape[1]), lambda i: (i, 0))
                  for m, rows in zip(mats, blocks)]
                 + [small(bias.shape)] + [small(v.shape) for v in vecs],
        out_specs=[pl.BlockSpec((rows // 2, m.shape[1]), lambda i: (i, 0))
                   for m, rows in zip(mats, blocks)]
                  + [small(bias.shape[::-1]), small((N_VEC_ROWS, D_MODEL))],
        out_shape=[jax.ShapeDtypeStruct((m.shape[0] // 2, m.shape[1]), jnp.uint32) for m in mats]
                  + [jax.ShapeDtypeStruct(bias.shape[::-1], _F32),
                     jax.ShapeDtypeStruct((N_VEC_ROWS, D_MODEL), _F32)],
        compiler_params=pltpu.CompilerParams(dimension_semantics=("arbitrary",),
                                             vmem_limit_bytes=VMEM_LIMIT_BYTES),
        name="pack_params",
    )(*mats, bias, *vecs)


def _resident(shape):
    zeros = (0,) * len(shape)
    return pl.BlockSpec(shape, lambda *_: zeros, pipeline_mode=pl.Buffered(1))


def _layer(h, p, hs, ps, state, taps, weights, tile_m, pos0):
    batch, seq, _ = h.shape
    rows = state.shape[1]
    assert seq % tile_m == 0 and tile_m % CHUNK == 0 and seq >= POOL_BUF and rows == SAMPLE_ROWS
    tiles_per_prompt = seq // tile_m
    n_tiles = batch * tiles_per_prompt
    assert n_tiles >= 2

    def front_tile(s):
        t = jnp.minimum(s, n_tiles - 1)
        return t // tiles_per_prompt, t % tiles_per_prompt, 0

    def back_tile(s):
        t = jnp.maximum(s - 1, 0)
        return t // tiles_per_prompt, t % tiles_per_prompt, 0

    per_prompt = lambda s: (jnp.minimum(s, n_tiles - 1) // tiles_per_prompt, 0, 0)
    hbm = pl.BlockSpec(memory_space=pl.ANY)
    whole = lambda a: pl.BlockSpec(a.shape, lambda s: (0,) * a.ndim)
    return pl.pallas_call(
        functools.partial(_layer_kernel, tile_m=tile_m, tiles_per_prompt=tiles_per_prompt,
                          n_tiles=n_tiles, pos0=pos0),
        grid=(n_tiles + 1,),
        in_specs=[pl.BlockSpec((1, tile_m, D_MODEL), front_tile),
                  pl.BlockSpec((1, tile_m, D_MODEL), back_tile),
                  pl.BlockSpec((1, tile_m, PLE_DIM), back_tile),
                  _resident(hs.shape), _resident(ps.shape), _resident(taps.shape), hbm]
                 + [_resident(a.shape) for a in weights],
        out_specs=[pl.BlockSpec((1, tile_m, D_MODEL), back_tile),
                   pl.BlockSpec((1, CHUNK, D_MODEL), per_prompt),
                   pl.BlockSpec((POOL_BUF, batch, D_MODEL), lambda s: (0, 0, 0)),
                   whole(hs), whole(hs), hbm],
        out_shape=[jax.ShapeDtypeStruct((batch, seq, D_MODEL), _F32),
                   jax.ShapeDtypeStruct((batch, CHUNK, D_MODEL), _F32),
                   jax.ShapeDtypeStruct((POOL_BUF, batch, D_MODEL), _F32),
                   jax.ShapeDtypeStruct(hs.shape, _F32),
                   jax.ShapeDtypeStruct(hs.shape, _F32),
                   jax.ShapeDtypeStruct(state.shape, _F32)],
        scratch_shapes=[pltpu.VMEM((N_LANE_BLOCKS, TAIL + tile_m, LANES), _F32),
                        pltpu.VMEM((N_LANE_BLOCKS, 2, TAIL + tile_m, LANES), _F32),
                        pltpu.VMEM((tile_m, D_MODEL), _BF16),
                        pltpu.VMEM((tile_m, D_MODEL), _BF16),
                        pltpu.VMEM((tile_m, D_MODEL), _BF16),
                        pltpu.VMEM(state.shape, _F32),
                        pltpu.VMEM((rows, D_MODEL), _F32),
                        pltpu.SemaphoreType.DMA((3,))],
        compiler_params=pltpu.CompilerParams(
            dimension_semantics=("arbitrary",),
            vmem_limit_bytes=VMEM_LIMIT_BYTES),
        name="layer",
    )(h, h, p, hs, ps, taps, state, *weights)


def kernel(x_prompt, x_sample, state_pool, p_prompt, p_sample, pre_g, w_in, ln_g, ln_b, w_s, b_s,
           w_pool, pool_scale, w_pa, w_pb, w_out, post_g, w_ple, w_pg, ple_in_g, ple_out_g):
    depth = w_in.shape[0]
    dec_batch, dec_seq, _ = x_sample.shape
    assert dec_seq == 1 and x_prompt.shape[2] == D_MODEL
    hp = x_prompt
    hs = x_sample.reshape(dec_batch * N_LANE_BLOCKS, LANES)
    pv, pp, sv, sp = [], [], [], []
    for i in range(depth):
        w_s_rows = w_s[i].reshape(D_MODEL, CHUNK)
        params = _pack_params(
            [w_in[i], w_s_rows, w_pool[i].reshape(D_MODEL, POOL_GROUP), w_pa[i], w_pb[i],
             w_out[i], w_ple[i], w_pg[i]],
            (None, CHUNK) + (None,) * 6, b_s[i],
            [v[i:i + 1] for v in (pre_g, ln_g, ln_b, pool_scale, post_g, ple_in_g, ple_out_g)],
            n_steps=PACK_STEPS)
        hp, v, pool, hs, v_s, pool_s = _layer(
            hp, p_prompt[i], hs, p_sample[i].reshape(dec_batch * PLE_DIM // LANES, LANES),
            jnp.swapaxes(state_pool[i], 0, 1), w_s_rows, params, TILE_M, PAST_LEN)
        pv.append(v)
        pp.append(jnp.swapaxes(pool, 0, 1))
        sv.append(v_s.reshape(dec_batch, 1, D_MODEL))
        sp.append(jnp.swapaxes(pool_s, 0, 1))
    return (hp, hs.reshape(dec_batch, 1, D_MODEL), jnp.stack(pv), jnp.stack(pp), jnp.stack(sv),
            jnp.stack(sp))
```

```python
import functools
from typing import Any, NamedTuple

import jax
import jax.numpy as jnp
from jax import lax
from jax.experimental import pallas as pl
from jax.experimental.pallas import tpu as pltpu

D_MODEL = 1024
CHUNK = 128
N_HEADS = D_MODEL // CHUNK
POOL_WINDOWS = (2, 4, 8, 16)
POOL_GROUP = D_MODEL // len(POOL_WINDOWS)
POOL_BUF = max(POOL_WINDOWS) - 1
PLE_DIM = 256
PAST_LEN = 16384
SAMPLE_ROWS = 128
EPS = 1e-6
COL_U, COL_V, COL_ZA, COL_XB, COL_ZB, COL_GA, COL_GB = range(7)
VEC_ROWS = tuple(range(7))
VEC_PRE_G, VEC_LN_G, VEC_LN_B, VEC_POOL_SCALE, VEC_POST_G, VEC_PLE_IN_G, VEC_PLE_OUT_G = VEC_ROWS
N_VEC_ROWS = 8

LANES = 128
N_LANE_BLOCKS = D_MODEL // LANES
BF16_ROWS = 16
SUBLANES = 8
TAIL = 32
DOUBLING_MIN_WINDOW = 4
PACK_STEPS = 8
TILE_M = 256
VMEM_LIMIT_BYTES = 56 * 1024 * 1024

_BF16 = jnp.bfloat16
_F32 = jnp.float32
_GELU_C = 0.7978845608028654


def _dot(a, b):
    return jnp.dot(a, b, preferred_element_type=_F32)


def _rmsnorm(x, g):
    return x * lax.rsqrt(jnp.mean(x * x, axis=-1, keepdims=True) + EPS) * g


def _layernorm(x, g, b):
    xc = x - jnp.mean(x, axis=-1, keepdims=True)
    return xc * lax.rsqrt(jnp.mean(xc * xc, axis=-1, keepdims=True) + EPS) * g + b


def _gelu_tanh(x):
    return 0.5 * x * (1.0 + jnp.tanh(_GELU_C * (x + 0.044715 * (x * x * x))))


def _sigmoid(x):
    return 0.5 * jnp.tanh(0.5 * x) + 0.5


def _silu(x):
    return x * _sigmoid(x)


def _unpack(ref, rows=slice(None), cols=slice(None)):
    return pltpu.bitcast(ref[rows, cols], _BF16)


class _Weights(NamedTuple):
    in_proj_w: Any
    spatial: Any
    pool: Any
    pa: Any
    pb: Any
    out: Any
    ple: Any
    pg: Any
    bias_t: Any
    vecs: Any

    def vec(self, row):
        return self.vecs[row:row + 1, :]

    def in_proj(self, xn, col):
        return _dot(xn, _unpack(self.in_proj_w, cols=slice(col * D_MODEL, (col + 1) * D_MODEL)))


def _pool_mix(pooled, w):
    parts = []
    for g in range(len(POOL_WINDOWS)):
        q = pooled[:, g * POOL_GROUP:(g + 1) * POOL_GROUP].astype(_BF16)
        rows = slice(g * POOL_GROUP // 2, (g + 1) * POOL_GROUP // 2)
        parts.append(_dot(q, _unpack(w.pool, rows=rows)))
    return jnp.concatenate(parts, axis=1) * w.vec(VEC_POOL_SCALE)


def _merge_and_embed(h, p, xn, y_a, y_b, w):
    g_a = w.in_proj(xn, COL_GA)
    g_b = w.in_proj(xn, COL_GB)
    m = (_sigmoid(g_a) * _dot(y_a.astype(_BF16), _unpack(w.pa))
         + _sigmoid(g_b) * _dot(y_b.astype(_BF16), _unpack(w.pb)))
    h = h + _rmsnorm(_dot(m.astype(_BF16), _unpack(w.out)), w.vec(VEC_POST_G))
    e = _dot(p.astype(_BF16), _unpack(w.ple))
    hn = _rmsnorm(h, w.vec(VEC_PLE_IN_G)).astype(_BF16)
    gate = _sigmoid(_dot(hn, _unpack(w.pg)))
    return h + _rmsnorm(gate * e, w.vec(VEC_PLE_OUT_G))


def _spatial_mix(vn_b, w, tile_m):
    w_tril = [_unpack(w.spatial, rows=slice(hd * CHUNK // 2, (hd + 1) * CHUNK // 2))
              for hd in range(N_HEADS)]
    bias = [w.bias_t[:, hd:hd + 1] for hd in range(N_HEADS)]
    s_rows = []
    for c in range(tile_m // CHUNK):
        heads = []
        for hd in range(N_HEADS):
            blk = vn_b[c * CHUNK:(c + 1) * CHUNK, hd * CHUNK:(hd + 1) * CHUNK]
            heads.append(_dot(w_tril[hd], blk) + bias[hd])
        s_rows.append(jnp.concatenate(heads, axis=1))
    return jnp.concatenate(s_rows, axis=0)


def _window_sum(src, lvl, w, rows):
    n_levels = w.bit_length() - 1
    assert w == 1 << n_levels and SUBLANES * (n_levels - 1) + 1 <= TAIL
    for l in range(1, n_levels):
        start = TAIL - SUBLANES * (n_levels - l)
        half = 1 << (l - 1)
        dst = lvl.at[(l - 1) % 2]
        dst[start:rows, :] = src[start:rows, :] + src[start - half:rows - half, :]
        src = dst
    half = w // 2
    return src[TAIL:rows, :] + src[TAIL - half:rows - half, :]


def _window_pool(x_b, ext_ref, lvl_ref, pool_ref, b, j, tile_m):
    rows = TAIL + tile_m
    for c in range(N_LANE_BLOCKS):
        ext_ref[c, TAIL:rows, :] = x_b[:, c * LANES:(c + 1) * LANES]
    pos = j * tile_m + lax.broadcasted_iota(jnp.int32, (tile_m, 1), 0)
    blocks = []
    for c in range(N_LANE_BLOCKS):
        w = POOL_WINDOWS[c * LANES // POOL_GROUP]
        cur = x_b[:, c * LANES:(c + 1) * LANES]
        if w >= DOUBLING_MIN_WINDOW:
            win = _window_sum(ext_ref.at[c], lvl_ref.at[c], w, rows)
        else:
            win = cur
            for k in range(1, w):
                win = win + ext_ref[c, TAIL - k:TAIL - k + tile_m, :]
        cnt = jnp.minimum(pos + 1, w).astype(_F32)
        blocks.append(win / cnt - cur)
    latest = jnp.concatenate([ext_ref[c, TAIL + tile_m - POOL_BUF:TAIL + tile_m, :]
                              for c in range(N_LANE_BLOCKS)], axis=1)
    pool_ref[:, pl.ds(b, 1), :] = latest[:, None, :]
    for c in range(N_LANE_BLOCKS):
        ext_ref[c, 0:TAIL, :] = ext_ref[c, tile_m:tile_m + TAIL, :]
    return jnp.concatenate(blocks, axis=1)


def _layer_kernel(x_ref, x_lag_ref, p_lag_ref, xs_ref, ps_ref, taps_ref, state_hbm,
                  w_in_ref, w_s_ref, w_pool_ref, w_pa_ref, w_pb_ref, w_out_ref, w_ple_ref,
                  w_pg_ref, bs_t_ref, vecs_ref,
                  y_ref, vn_ref, pool_ref, ys_ref, vns_ref, spool_hbm,
                  ext_ref, lvl_ref, xn_c, ya_c, yb_c, hist_ref, xbs_ref, sems,
                  *, tile_m, tiles_per_prompt, n_tiles, pos0):
    s_id = pl.program_id(0)
    front = jnp.minimum(s_id, n_tiles - 1)
    b = front // tiles_per_prompt
    j = lax.rem(front, tiles_per_prompt)
    w = _Weights(in_proj_w=w_in_ref, spatial=w_s_ref, pool=w_pool_ref, pa=w_pa_ref, pb=w_pb_ref,
                 out=w_out_ref, ple=w_ple_ref, pg=w_pg_ref, bias_t=bs_t_ref, vecs=vecs_ref)

    @pl.when(j == 0)
    def _():
        ext_ref[:, 0:TAIL, :] = jnp.zeros((N_LANE_BLOCKS, TAIL, LANES), _F32)

    @pl.when(s_id == 0)
    def _front_only():
        xn = _rmsnorm(x_ref[0], w.vec(VEC_PRE_G)).astype(_BF16)
        xn_c[...] = xn
        vn = _layernorm(_gelu_tanh(w.in_proj(xn, COL_V)), w.vec(VEC_LN_G), w.vec(VEC_LN_B))
        vn_ref[0] = vn[tile_m - CHUNK:, :]
        s = _spatial_mix(vn.astype(_BF16), w, tile_m)
        ya_c[...] = (_gelu_tanh(w.in_proj(xn, COL_U)) * s
                     * _silu(w.in_proj(xn, COL_ZA))).astype(_BF16)
        pooled = _window_pool(w.in_proj(xn, COL_XB), ext_ref, lvl_ref, pool_ref, b, j, tile_m)
        yb_c[...] = (_pool_mix(pooled, w) * _silu(w.in_proj(xn, COL_ZB))).astype(_BF16)

    @pl.when(s_id == n_tiles)
    def _back_only_and_sample():
        history, shift, newest = _sample_copies(state_hbm, spool_hbm, hist_ref, xbs_ref, sems)
        history.start()
        y_ref[0] = _merge_and_embed(x_lag_ref[0], p_lag_ref[0], xn_c[...], ya_c[...], yb_c[...], w)
        history.wait()
        shift.start()
        _sample_rows(xs_ref, ps_ref, taps_ref, hist_ref, newest, xbs_ref, w, ys_ref, vns_ref, pos0)
        shift.wait()
        newest.wait()

    @pl.when(jnp.logical_and(s_id > 0, s_id < n_tiles))
    def _both():
        xn_prev = xn_c[...]
        g_a = w.in_proj(xn_prev, COL_GA)
        xn = _rmsnorm(x_ref[0], w.vec(VEC_PRE_G)).astype(_BF16)
        xn_c[...] = xn
        v_raw = w.in_proj(xn, COL_V)
        g_b = w.in_proj(xn_prev, COL_GB)
        e = _dot(p_lag_ref[0].astype(_BF16), _unpack(w.ple))
        vn = _layernorm(_gelu_tanh(v_raw), w.vec(VEC_LN_G), w.vec(VEC_LN_B))
        vn_ref[0] = vn[tile_m - CHUNK:, :]
        u_raw = w.in_proj(xn, COL_U)
        s = _spatial_mix(vn.astype(_BF16), w, tile_m)
        proj_a = _dot(ya_c[...], _unpack(w.pa))
        proj_b = _dot(yb_c[...], _unpack(w.pb))
        x_b = w.in_proj(xn, COL_XB)
        m = (_sigmoid(g_a) * proj_a + _sigmoid(g_b) * proj_b).astype(_BF16)
        o = _dot(m, _unpack(w.out))
        zb_raw = w.in_proj(xn, COL_ZB)
        pooled = _window_pool(x_b, ext_ref, lvl_ref, pool_ref, b, j, tile_m)
        mixed = _pool_mix(pooled, w)
        za_raw = w.in_proj(xn, COL_ZA)
        ya_c[...] = (_gelu_tanh(u_raw) * s * _silu(za_raw)).astype(_BF16)
        h = x_lag_ref[0] + _rmsnorm(o, w.vec(VEC_POST_G))
        hn = _rmsnorm(h, w.vec(VEC_PLE_IN_G)).astype(_BF16)
        gate_raw = _dot(hn, _unpack(w.pg))
        yb_c[...] = (mixed * _silu(zb_raw)).astype(_BF16)
        y_ref[0] = h + _rmsnorm(_sigmoid(gate_raw) * e, w.vec(VEC_PLE_OUT_G))


def _gather_rows(ref):
    n = ref.shape[0] // SAMPLE_ROWS
    return jnp.concatenate([ref[pl.ds(c, SAMPLE_ROWS, stride=n), :] for c in range(n)], axis=1)


def _scatter_rows(ref, value):
    n = ref.shape[0] // SAMPLE_ROWS
    for c in range(n):
        ref[pl.ds(c, SAMPLE_ROWS, stride=n), :] = value[:, c * LANES:(c + 1) * LANES]


def _sample_copies(state_hbm, spool_hbm, hist_ref, xbs_ref, sems):
    history = pltpu.make_async_copy(state_hbm, hist_ref, sems.at[0])
    shift = pltpu.make_async_copy(hist_ref.at[pl.ds(1, POOL_BUF - 1)],
                                  spool_hbm.at[pl.ds(0, POOL_BUF - 1)], sems.at[1])
    newest = pltpu.make_async_copy(xbs_ref, spool_hbm.at[POOL_BUF - 1], sems.at[2])
    return history, shift, newest


def _sample_rows(x_ref, p_ref, taps_ref, hist_ref, newest, xbs_ref, w, y_ref, vn_ref, pos0):
    x = _gather_rows(x_ref)
    xn = _rmsnorm(x, w.vec(VEC_PRE_G)).astype(_BF16)

    u = _gelu_tanh(w.in_proj(xn, COL_U))
    vn = _layernorm(_gelu_tanh(w.in_proj(xn, COL_V)), w.vec(VEC_LN_G), w.vec(VEC_LN_B))
    _scatter_rows(vn_ref, vn)
    heads = []
    for hd in range(N_HEADS):
        tap = taps_ref[hd * CHUNK:hd * CHUNK + 1, 0:1]
        heads.append(vn[:, hd * CHUNK:(hd + 1) * CHUNK] * tap + w.bias_t[0:1, hd:hd + 1])
    s = jnp.concatenate(heads, axis=1)
    y_a = u * s * _silu(w.in_proj(xn, COL_ZA))

    x_b = w.in_proj(xn, COL_XB)
    xbs_ref[...] = x_b
    newest.start()
    groups = []
    for g, win_len in enumerate(POOL_WINDOWS):
        lo, hi = g * POOL_GROUP, (g + 1) * POOL_GROUP
        cur = x_b[:, lo:hi]
        win = cur
        for k in range(1, win_len):
            win = win + hist_ref[POOL_BUF - k, :, lo:hi]
        groups.append(win / float(min(pos0 + 1, win_len)) - cur)
    pooled = jnp.concatenate(groups, axis=1)
    y_b = _pool_mix(pooled, w) * _silu(w.in_proj(xn, COL_ZB))

    _scatter_rows(y_ref, _merge_and_embed(x, _gather_rows(p_ref), xn, y_a, y_b, w))


def _pack_kernel(*refs, tril_periods):
    n = len(tril_periods)
    n_vecs = len(VEC_ROWS)
    bias_ref, vec_refs = refs[n], refs[n + 1:n + 1 + n_vecs]
    outs = refs[n + 1 + n_vecs:]
    bias_t_ref, vecs_ref = outs[n], outs[n + 1]
    bias_t_ref[...] = bias_ref[...].T
    for row, vec_ref in zip(VEC_ROWS, vec_refs):
        vecs_ref[row:row + 1, :] = vec_ref[...]
    vecs_ref[n_vecs:, :] = jnp.zeros((N_VEC_ROWS - n_vecs, D_MODEL), _F32)
    for src, dst, period in zip(refs[:n], outs[:n], tril_periods):
        m = src[...]
        if period is not None:
            row = lax.broadcasted_iota(jnp.int32, m.shape, 0) % period
            m = jnp.where(row >= lax.broadcasted_iota(jnp.int32, m.shape, 1), m, 0.0)
        dst[...] = pltpu.bitcast(m.astype(_BF16), jnp.uint32)


def _pack_params(mats, tril_periods, bias, vecs, n_steps):
    blocks = [m.shape[0] // n_steps for m in mats]
    small = lambda shape: pl.BlockSpec(shape, lambda i: (0, 0))
    for m, rows, period in zip(mats, blocks, tril_periods):
        assert rows * n_steps == m.shape[0] and rows % (2 * BF16_ROWS) == 0
        assert period is None or rows % period == 0
    return pl.pallas_call(
        functools.partial(_pack_kernel, tril_periods=tril_periods),
        grid=(n_steps,),
        in_specs=[pl.BlockSpec((rows, m.shape[1]), lambda i: (i, 0))
                  for m, rows in zip(mats, blocks)]
                 + [small(bias.shape)] + [small(v.shape) for v in vecs],
        out_specs=[pl.BlockSpec((rows // 2, m.shape[1]), lambda i: (i, 0))
                   for m, rows in zip(mats, blocks)]
                  + [small(bias.shape[::-1]), small((N_VEC_ROWS, D_MODEL))],
        out_shape=[jax.ShapeDtypeStruct((m.shape[0] // 2, m.shape[1]), jnp.uint32) for m in mats]
                  + [jax.ShapeDtypeStruct(bias.shape[::-1], _F32),
                     jax.ShapeDtypeStruct((N_VEC_ROWS, D_MODEL), _F32)],
        compiler_params=pltpu.CompilerParams(dimension_semantics=("arbitrary",),
                                             vmem_limit_bytes=VMEM_LIMIT_BYTES),
        name="pack_params",
    )(*mats, bias, *vecs)


def _resident(shape):
    zeros = (0,) * len(shape)
    return pl.BlockSpec(shape, lambda *_: zeros, pipeline_mode=pl.Buffered(1))


def _layer(h, p, hs, ps, state, taps, weights, tile_m, pos0):
    batch, seq, _ = h.shape
    rows = state.shape[1]
    assert seq % tile_m == 0 and tile_m % CHUNK == 0 and seq >= POOL_BUF and rows == SAMPLE_ROWS
    tiles_per_prompt = seq // tile_m
    n_tiles = batch * tiles_per_prompt
    assert n_tiles >= 2

    def front_tile(s):
        t = jnp.minimum(s, n_tiles - 1)
        return t // tiles_per_prompt, t % tiles_per_prompt, 0

    def back_tile(s):
        t = jnp.maximum(s - 1, 0)
        return t // tiles_per_prompt, t % tiles_per_prompt, 0

    per_prompt = lambda s: (jnp.minimum(s, n_tiles - 1) // tiles_per_prompt, 0, 0)
    hbm = pl.BlockSpec(memory_space=pl.ANY)
    whole = lambda a: pl.BlockSpec(a.shape, lambda s: (0,) * a.ndim)
    return pl.pallas_call(
        functools.partial(_layer_kernel, tile_m=tile_m, tiles_per_prompt=tiles_per_prompt,
                          n_tiles=n_tiles, pos0=pos0),
        grid=(n_tiles + 1,),
        in_specs=[pl.BlockSpec((1, tile_m, D_MODEL), front_tile),
                  pl.BlockSpec((1, tile_m, D_MODEL), back_tile),
                  pl.BlockSpec((1, tile_m, PLE_DIM), back_tile),
                  _resident(hs.shape), _resident(ps.shape), _resident(taps.shape), hbm]
                 + [_resident(a.shape) for a in weights],
        out_specs=[pl.BlockSpec((1, tile_m, D_MODEL), back_tile),
                   pl.BlockSpec((1, CHUNK, D_MODEL), per_prompt),
                   pl.BlockSpec((POOL_BUF, batch, D_MODEL), lambda s: (0, 0, 0)),
                   whole(hs), whole(hs), hbm],
        out_shape=[jax.ShapeDtypeStruct((batch, seq, D_MODEL), _F32),
                   jax.ShapeDtypeStruct((batch, CHUNK, D_MODEL), _F32),
                   jax.ShapeDtypeStruct((POOL_BUF, batch, D_MODEL), _F32),
                   jax.ShapeDtypeStruct(hs.shape, _F32),
                   jax.ShapeDtypeStruct(hs.shape, _F32),
                   jax.ShapeDtypeStruct(state.shape, _F32)],
        scratch_shapes=[pltpu.VMEM((N_LANE_BLOCKS, TAIL + tile_m, LANES), _F32),
                        pltpu.VMEM((N_LANE_BLOCKS, 2, TAIL + tile_m, LANES), _F32),
                        pltpu.VMEM((tile_m, D_MODEL), _BF16),
                        pltpu.VMEM((tile_m, D_MODEL), _BF16),
                        pltpu.VMEM((tile_m, D_MODEL), _BF16),
                        pltpu.VMEM(state.shape, _F32),
                        pltpu.VMEM((rows, D_MODEL), _F32),
                        pltpu.SemaphoreType.DMA((3,))],
        compiler_params=pltpu.CompilerParams(
            dimension_semantics=("arbitrary",),
            vmem_limit_bytes=VMEM_LIMIT_BYTES),
        name="layer",
    )(h, h, p, hs, ps, taps, state, *weights)


def kernel(x_prompt, x_sample, state_pool, p_prompt, p_sample, pre_g, w_in, ln_g, ln_b, w_s, b_s,
           w_pool, pool_scale, w_pa, w_pb, w_out, post_g, w_ple, w_pg, ple_in_g, ple_out_g):
    depth = w_in.shape[0]
    dec_batch, dec_seq, _ = x_sample.shape
    assert dec_seq == 1 and x_prompt.shape[2] == D_MODEL
    hp = x_prompt
    hs = x_sample.reshape(dec_batch * N_LANE_BLOCKS, LANES)
    pv, pp, sv, sp = [], [], [], []
    for i in range(depth):
        w_s_rows = w_s[i].reshape(D_MODEL, CHUNK)
        params = _pack_params(
            [w_in[i], w_s_rows, w_pool[i].reshape(D_MODEL, POOL_GROUP), w_pa[i], w_pb[i],
             w_out[i], w_ple[i], w_pg[i]],
            (None, CHUNK) + (None,) * 6, b_s[i],
            [v[i:i + 1] for v in (pre_g, ln_g, ln_b, pool_scale, post_g, ple_in_g, ple_out_g)],
            n_steps=PACK_STEPS)
        hp, v, pool, hs, v_s, pool_s = _layer(
            hp, p_prompt[i], hs, p_sample[i].reshape(dec_batch * PLE_DIM // LANES, LANES),
            jnp.swapaxes(state_pool[i], 0, 1), w_s_rows, params, TILE_M, PAST_LEN)
        pv.append(v)
        pp.append(jnp.swapaxes(pool, 0, 1))
        sv.append(v_s.reshape(dec_batch, 1, D_MODEL))
        sp.append(jnp.swapaxes(pool_s, 0, 1))
    return (hp, hs.reshape(dec_batch, 1, D_MODEL), jnp.stack(pv), jnp.stack(pp), jnp.stack(sv),
            jnp.stack(sp))
```
